```python
import math
import jax, jax.numpy as jnp
from jax import lax
import numpy as np

D_MODEL = 1024
BATCH = 8
SEQ = 2048
DEPTH = 2
DEC_BATCH = 128
DEC_SEQ = 4
PAST_LEN = 16384
PAGE_SIZE = 128

N_MIXERS = 2
N_MOD = 6
CHUNK_A = 128
GMLP_WIDTH = D_MODEL
GMLP_GROUPS = 4
GMLP_GROUP_W = GMLP_WIDTH // GMLP_GROUPS
GLA_HEADS = 4
GLA_DK = D_MODEL // 2
GLA_DV = D_MODEL
GLA_DK_HEAD = GLA_DK // GLA_HEADS
GLA_DV_HEAD = GLA_DV // GLA_HEADS
GLA_GATE_RANK = 16
GLA_TAU = 16.0
GLA_CHUNK = 64
GLA_IN = 2 * GLA_DK + 2 * GLA_DV + GLA_GATE_RANK
D_FF = 4 * D_MODEL
EPS = 1e-6

kernel_name = "hybrid_gmlp_gla_adaln_step"


def rms_norm(x, g):
    xf = x.astype(jnp.float32)
    y = xf * lax.rsqrt(jnp.mean(xf * xf, axis=-1, keepdims=True) + EPS)
    return (y * g.astype(jnp.float32)).astype(x.dtype)


def layer_norm(x, g, b):
    xf = x.astype(jnp.float32)
    mu = jnp.mean(xf, axis=-1, keepdims=True)
    var = jnp.mean(jnp.square(xf - mu), axis=-1, keepdims=True)
    y = (xf - mu) * lax.rsqrt(var + EPS)
    return (y * g.astype(jnp.float32) + b.astype(jnp.float32)).astype(x.dtype)


def gmlp_mixer(h, w_in, b_in, ln_g, ln_b, w_s, b_s, w_out, b_out):
    B, T, _ = h.shape
    z = jax.nn.gelu(h @ w_in + b_in)
    u, v = jnp.split(z, 2, axis=-1)
    v = layer_norm(v, ln_g, ln_b)
    n_chunks = -(-T // CHUNK_A)
    pad = n_chunks * CHUNK_A - T
    vp = jnp.pad(v, ((0, 0), (0, pad), (0, 0))).reshape(B, n_chunks, CHUNK_A, GMLP_GROUPS, GMLP_GROUP_W)
    causal = jnp.tril(jnp.ones((CHUNK_A, CHUNK_A), dtype=bool))
    ws = jnp.where(causal[None], w_s, jnp.zeros_like(w_s))
    s = jnp.einsum('gij,bnjgc->bnigc', ws, vp) + b_s.T[None, None, :, :, None]
    s = s.reshape(B, n_chunks * CHUNK_A, GMLP_WIDTH)[:, :T]
    y = (u * s) @ w_out + b_out
    return y, v


def gla_recurrence(q, k, v, log_a, s0):
    B, T, H, dk = q.shape
    dv = v.shape[-1]
    C = math.gcd(T, GLA_CHUNK)
    N = T // C
    f32 = jnp.float32

    def blocks(t):
        return t.astype(f32).reshape(B, N, C, H, t.shape[-1]).transpose(1, 0, 3, 2, 4)

    causal = jnp.tril(jnp.ones((C, C), dtype=bool))[..., None]

    def step(S, inp):
        qc, kc, vc, lac = inp
        b = jnp.cumsum(lac, axis=-2)
        o_inter = jnp.einsum('bhik,bhkv->bhiv', qc * jnp.exp(b), S)
        diff = b[:, :, :, None, :] - b[:, :, None, :, :]
        decay = jnp.exp(jnp.where(causal, diff, -jnp.inf))
        attn = jnp.einsum('bhik,bhjk,bhijk->bhij', qc, kc, decay)
        o_intra = jnp.einsum('bhij,bhjv->bhiv', attn, vc)
        b_last = b[:, :, -1:, :]
        S_new = jnp.exp(b_last[:, :, 0, :])[..., None] * S + jnp.einsum(
            'bhjk,bhjv->bhkv', kc * jnp.exp(b_last - b), vc)
        return S_new, o_inter + o_intra

    S_fin, o = lax.scan(step, s0.astype(f32), (blocks(q), blocks(k), blocks(v), blocks(log_a)))
    o = o.transpose(1, 0, 3, 2, 4).reshape(B, T, H, dv)
    return o, S_fin


def gla_mixer(h, s0, w_in, w_gate2, b_gate, norm_g, w_out):
    B, T, _ = h.shape
    proj = h @ w_in
    q, k, v, g, a_lr = jnp.split(
        proj, [GLA_DK, 2 * GLA_DK, 2 * GLA_DK + GLA_DV, 2 * GLA_DK + 2 * GLA_DV], axis=-1)
    log_a = jax.nn.log_sigmoid((a_lr @ w_gate2 + b_gate).astype(jnp.float32)) / GLA_TAU
    q = q.reshape(B, T, GLA_HEADS, GLA_DK_HEAD) * (GLA_DK_HEAD ** -0.5)
    k = k.reshape(B, T, GLA_HEADS, GLA_DK_HEAD)
    v = v.reshape(B, T, GLA_HEADS, GLA_DV_HEAD)
    log_a = log_a.reshape(B, T, GLA_HEADS, GLA_DK_HEAD)
    o, S_fin = gla_recurrence(q, k, v, log_a, s0)
    o = rms_norm(o.astype(h.dtype), norm_g).reshape(B, T, GLA_DV)
    y = (o * jax.nn.silu(g)) @ w_out
    return y, S_fin.astype(s0.dtype)


def sqrelu_mlp(h, w1, b1, w2, b2):
    return jnp.square(jax.nn.relu(h @ w1 + b1)) @ w2 + b2


def trunk(x, c, s0, ada_w, ada_b, norm_mix_g, norm_ffn_g, ffn_w1, ffn_b1, ffn_w2, ffn_b2,
          gmlp_w_in, gmlp_b_in, gmlp_ln_g, gmlp_ln_b, gmlp_w_s, gmlp_b_s, gmlp_w_out, gmlp_b_out,
          gla_w_in, gla_w_gate2, gla_b_gate, gla_norm_g, gla_w_out, final_norm_g):
    chunk_v = None
    s_fin = None
    sc = jax.nn.silu(c)
    for i in range(DEPTH):
        mod = sc @ ada_w[i] + ada_b[i]
        sh1, sc1, g1, sh2, sc2, g2 = [m[:, None, :] for m in jnp.split(mod, N_MOD, axis=-1)]
        h = rms_norm(x, norm_mix_g[i]) * (1 + sc1) + sh1
        if i % N_MIXERS == 0:
            y, chunk_v = gmlp_mixer(h, gmlp_w_in, gmlp_b_in, gmlp_ln_g, gmlp_ln_b,
                                    gmlp_w_s, gmlp_b_s, gmlp_w_out, gmlp_b_out)
        else:
            y, s_fin = gla_mixer(h, s0, gla_w_in, gla_w_gate2, gla_b_gate, gla_norm_g, gla_w_out)
        x = x + g1 * y
        h = rms_norm(x, norm_ffn_g[i]) * (1 + sc2) + sh2
        x = x + g2 * sqrelu_mlp(h, ffn_w1[i], ffn_b1[i], ffn_w2[i], ffn_b2[i])
    return rms_norm(x, final_norm_g), chunk_v, s_fin


def setup_inputs(seed: int = 0) -> dict:
    key = jax.random.key(seed)
    ks = jax.random.split(key, 30)
    f32 = jnp.float32

    def nrm(k, shape, scale):
        return jax.random.normal(k, shape, f32) * scale

    D = D_MODEL
    return {
        "x_prompt": nrm(ks[0], (BATCH, SEQ, D), 1.0),
        "x_sample": nrm(ks[1], (DEC_BATCH, DEC_SEQ, D), 1.0),
        "c_prompt": nrm(ks[2], (BATCH, D), 1.0),
        "c_sample": nrm(ks[3], (DEC_BATCH, D), 1.0),
        "state_gla": nrm(ks[4], (DEC_BATCH, GLA_HEADS, GLA_DK_HEAD, GLA_DV_HEAD), 0.5),
        "ada_w": nrm(ks[5], (DEPTH, D, N_MOD * D), 0.5 * D ** -0.5),
        "ada_b": nrm(ks[6], (DEPTH, N_MOD * D), 0.02),
        "norm_mix_g": 1.0 + nrm(ks[7], (DEPTH, D), 0.02),
        "norm_ffn_g": 1.0 + nrm(ks[8], (DEPTH, D), 0.02),
        "ffn_w1": nrm(ks[9], (DEPTH, D, D_FF), D ** -0.5),
        "ffn_b1": nrm(ks[10], (DEPTH, D_FF), 0.02),
        "ffn_w2": nrm(ks[11], (DEPTH, D_FF, D), D_FF ** -0.5),
        "ffn_b2": nrm(ks[12], (DEPTH, D), 0.02),
        "gmlp_w_in": nrm(ks[13], (D, 2 * GMLP_WIDTH), D ** -0.5),
        "gmlp_b_in": nrm(ks[14], (2 * GMLP_WIDTH,), 0.02),
        "gmlp_ln_g": 1.0 + nrm(ks[15], (GMLP_WIDTH,), 0.02),
        "gmlp_ln_b": nrm(ks[16], (GMLP_WIDTH,), 0.02),
        "gmlp_w_s": nrm(ks[17], (GMLP_GROUPS, CHUNK_A, CHUNK_A), CHUNK_A ** -0.5),
        "gmlp_b_s": 1.0 + nrm(ks[18], (GMLP_GROUPS, CHUNK_A), 0.02),
        "gmlp_w_out": nrm(ks[19], (GMLP_WIDTH, D), GMLP_WIDTH ** -0.5),
        "gmlp_b_out": nrm(ks[20], (D,), 0.02),
        "gla_w_in": nrm(ks[21], (D, GLA_IN), D ** -0.5),
        "gla_w_gate2": nrm(ks[22], (GLA_GATE_RANK, GLA_DK), GLA_GATE_RANK ** -0.5),
        "gla_b_gate": nrm(ks[23], (GLA_DK,), 0.1),
        "gla_norm_g": 1.0 + nrm(ks[24], (GLA_DV_HEAD,), 0.02),
        "gla_w_out": nrm(ks[25], (GLA_DV, D), GLA_DV ** -0.5),
        "final_norm_g": 1.0 + nrm(ks[26], (D,), 0.02),
    }


def reference(x_prompt, x_sample, c_prompt, c_sample, state_gla, ada_w, ada_b, norm_mix_g, norm_ffn_g,
              ffn_w1, ffn_b1, ffn_w2, ffn_b2, gmlp_w_in, gmlp_b_in, gmlp_ln_g, gmlp_ln_b, gmlp_w_s,
              gmlp_b_s, gmlp_w_out, gmlp_b_out, gla_w_in, gla_w_gate2, gla_b_gate, gla_norm_g,
              gla_w_out, final_norm_g):
    s0_prompt = jnp.zeros((x_prompt.shape[0], GLA_HEADS, GLA_DK_HEAD, GLA_DV_HEAD), dtype=state_gla.dtype)
    y_prompt, _, state_gla_prompt = trunk(
        x_prompt, c_prompt, s0_prompt, ada_w, ada_b, norm_mix_g, norm_ffn_g, ffn_w1, ffn_b1, ffn_w2, ffn_b2,
        gmlp_w_in, gmlp_b_in, gmlp_ln_g, gmlp_ln_b, gmlp_w_s, gmlp_b_s, gmlp_w_out, gmlp_b_out,
        gla_w_in, gla_w_gate2, gla_b_gate, gla_norm_g, gla_w_out, final_norm_g)
    y_sample, state_chunk_v_sample, state_gla_sample = trunk(
        x_sample, c_sample, state_gla, ada_w, ada_b, norm_mix_g, norm_ffn_g, ffn_w1, ffn_b1, ffn_w2, ffn_b2,
        gmlp_w_in, gmlp_b_in, gmlp_ln_g, gmlp_ln_b, gmlp_w_s, gmlp_b_s, gmlp_w_out, gmlp_b_out,
        gla_w_in, gla_w_gate2, gla_b_gate, gla_norm_g, gla_w_out, final_norm_g)
    return (y_prompt, y_sample, state_gla_prompt, state_gla_sample, state_chunk_v_sample)
```

```python
import functools

import jax
import jax.numpy as jnp
from jax import lax
from jax.experimental import pallas as pl
from jax.experimental.pallas import tpu as pltpu

F32 = jnp.float32
BF16 = jnp.bfloat16

D_MODEL = 1024
N_MOD = 6
CHUNK_A = 128
GMLP_WIDTH = D_MODEL
GMLP_GROUPS = 4
GMLP_GROUP_W = GMLP_WIDTH // GMLP_GROUPS
GLA_HEADS = 4
GLA_DK = D_MODEL // 2
GLA_DV = D_MODEL
GLA_DK_HEAD = GLA_DK // GLA_HEADS
GLA_DV_HEAD = GLA_DV // GLA_HEADS
GLA_GATE_RANK = 16
GLA_TAU = 16.0
GLA_CHUNK = 64
GLA_QKVG = 2 * GLA_DK + 2 * GLA_DV
GLA_GATE_PAD = 128
GLA_IN_PAD = GLA_QKVG + GLA_GATE_PAD
D_FF = 4 * D_MODEL
EPS = 1e-6

SUBLANES = 8
FFN_CHUNK = 1024
VMEM_LIMIT = 56 * 1024 * 1024


def _rms(x, g):
    return x * lax.rsqrt(jnp.mean(x * x, axis=-1, keepdims=True) + EPS) * g


def _modulated_norm(x, g, scale, shift):
    return _rms(x, g) * (1.0 + scale) + shift


def _dot(a, b):
    return jnp.dot(a.astype(BF16), b.astype(BF16), preferred_element_type=F32)


def _dot_nt(a, b):
    return lax.dot_general(a.astype(BF16), b.astype(BF16), (((1,), (1,)), ((), ())),
                           preferred_element_type=F32)


def _dot_tn(a, b):
    return lax.dot_general(a.astype(BF16), b.astype(BF16), (((0,), (0,)), ((), ())),
                           preferred_element_type=F32)


def _gelu_tanh(x):
    return 0.5 * x * (1.0 + jnp.tanh(0.7978845608028654 * (x + 0.044715 * (x * x * x))))


def _silu(x):
    return x * jax.nn.sigmoid(x)


def _log_sigmoid(x):
    return -(jnp.maximum(-x, 0.0) + jnp.log1p(jnp.exp(-jnp.abs(x))))


def _ada_kernel(c_ref, w_ref, b_ref, op_ref, os_ref, *, n_prompt):
    s = _silu(c_ref[...])
    y = _dot(s, w_ref[...]) + b_ref[...]
    op_ref[...] = y[:n_prompt]
    os_ref[...] = y[n_prompt:]


def _ada_call(c_all, ada_w, ada_b, n_prompt):
    depth = ada_w.shape[0]
    rows = c_all.shape[0]
    d = D_MODEL
    return pl.pallas_call(
        functools.partial(_ada_kernel, n_prompt=n_prompt),
        grid=(depth, N_MOD),
        in_specs=[
            pl.BlockSpec((rows, d), lambda l, j: (0, 0)),
            pl.BlockSpec((None, d, d), lambda l, j: (l, 0, j)),
            pl.BlockSpec((None, 1, d), lambda l, j: (l, 0, j)),
        ],
        out_specs=[
            pl.BlockSpec((None, n_prompt, d), lambda l, j: (l, 0, j)),
            pl.BlockSpec((None, rows - n_prompt, d), lambda l, j: (l, 0, j)),
        ],
        out_shape=[
            jax.ShapeDtypeStruct((depth, n_prompt, N_MOD * d), F32),
            jax.ShapeDtypeStruct((depth, rows - n_prompt, N_MOD * d), F32),
        ],
        compiler_params=pltpu.CompilerParams(
            dimension_semantics=("arbitrary", "arbitrary"), vmem_limit_bytes=VMEM_LIMIT),
        name="adaln_mod",
    )(c_all, ada_w, ada_b.reshape(depth, 1, N_MOD * d))


def _const_spec(shape):
    zeros = (0,) * len(shape)
    return pl.BlockSpec(shape, lambda i: zeros, pipeline_mode=pl.Buffered(1))


def _mod_specs(mod, layer, first_chunk, tm, rows_per_seq):
    d = D_MODEL
    specs = []
    for j in range(first_chunk, first_chunk + 3):
        if rows_per_seq is None:
            specs.append(pl.BlockSpec((None, tm, d), lambda i, j=j: (layer, i, j)))
        else:
            tiles = rows_per_seq // tm
            specs.append(pl.BlockSpec((None, None, 1, d),
                                      lambda i, j=j, tiles=tiles: (layer, i // tiles, 0, j)))
    return specs


def _row_spec(tm, width):
    return pl.BlockSpec((tm, width), lambda i: (i, 0))


def _gmlp_kernel(x_ref, sh_ref, sc_ref, gt_ref, ng_ref, win_ref, bin_ref, lng_ref, lnb_ref,
                 mix_ref, bs_ref, wout_ref, bout_ref, xo_ref, *v_refs, tm):
    x = x_ref[...]
    h = _modulated_norm(x, ng_ref[...], sc_ref[...], sh_ref[...])
    z = _gelu_tanh(_dot(h, win_ref[...]) + bin_ref[...])
    u = z[:, :GMLP_WIDTH]
    v = z[:, GMLP_WIDTH:]
    mu = jnp.mean(v, axis=-1, keepdims=True)
    vc = v - mu
    var = jnp.mean(vc * vc, axis=-1, keepdims=True)
    v = vc * lax.rsqrt(var + EPS) * lng_ref[...] + lnb_ref[...]
    if v_refs:
        v_refs[0][...] = v
    vb = v.astype(BF16)
    rows = []
    for c in range(tm // CHUNK_A):
        cols = []
        for g in range(GMLP_GROUPS):
            blk = vb[c * CHUNK_A:(c + 1) * CHUNK_A, g * GMLP_GROUP_W:(g + 1) * GMLP_GROUP_W]
            cols.append(jnp.dot(mix_ref[g], blk, preferred_element_type=F32))
        rows.append(jnp.concatenate(cols, axis=1) + bs_ref[...])
    s = jnp.concatenate(rows, axis=0)
    y = _dot(u * s, wout_ref[...]) + bout_ref[...]
    xo_ref[...] = x + gt_ref[...] * y


def _gmlp_call(x, mod, layer, rows_per_seq, tm, w, want_v):
    n, d = x.shape
    out_shape = [jax.ShapeDtypeStruct((n, d), F32)]
    out_specs = [_row_spec(tm, d)]
    if want_v:
        out_shape.append(jax.ShapeDtypeStruct((n, GMLP_WIDTH), F32))
        out_specs.append(_row_spec(tm, GMLP_WIDTH))
    in_specs = [_row_spec(tm, d)] + _mod_specs(mod, layer, 0, tm, rows_per_seq) + [
        _const_spec((1, d)),
        _const_spec((d, 2 * GMLP_WIDTH)),
        _const_spec((1, 2 * GMLP_WIDTH)),
        _const_spec((1, GMLP_WIDTH)),
        _const_spec((1, GMLP_WIDTH)),
        _const_spec((GMLP_GROUPS, CHUNK_A, CHUNK_A)),
        _const_spec((CHUNK_A, GMLP_WIDTH)),
        _const_spec((GMLP_WIDTH, d)),
        _const_spec((1, d)),
    ]
    return pl.pallas_call(
        functools.partial(_gmlp_kernel, tm=tm),
        grid=(n // tm,),
        in_specs=in_specs,
        out_specs=out_specs,
        out_shape=out_shape,
        compiler_params=pltpu.CompilerParams(
            dimension_semantics=("arbitrary",), vmem_limit_bytes=VMEM_LIMIT),
        name="gmlp_mixer",
    )(x, mod, mod, mod, w["norm_g"], w["w_in"], w["b_in"], w["ln_g"], w["ln_b"],
      w["mix"], w["bias_s"], w["w_out"], w["b_out"])


def _ffn_kernel(x_ref, sh_ref, sc_ref, gt_ref, ng_ref, w1_ref, b1_ref, w2_ref, b2_ref,
                fg_ref, o_ref, *, final_norm):
    x = x_ref[...]
    h = _modulated_norm(x, ng_ref[...], sc_ref[...], sh_ref[...]).astype(BF16)
    acc = jnp.zeros(x.shape, F32)
    for j in range(D_FF // FFN_CHUNK):
        cols = slice(j * FFN_CHUNK, (j + 1) * FFN_CHUNK)
        a = jnp.dot(h, w1_ref[:, cols], preferred_element_type=F32) + b1_ref[:, cols]
        r = jnp.square(jnp.maximum(a, 0.0)).astype(BF16)
        acc = acc + jnp.dot(r, w2_ref[cols, :], preferred_element_type=F32)
    y = x + gt_ref[...] * (acc + b2_ref[...])
    if final_norm:
        y = _rms(y, fg_ref[...])
    o_ref[...] = y


def _ffn_call(x, mod, layer, rows_per_seq, tm, w, final_norm):
    n, d = x.shape
    in_specs = [_row_spec(tm, d)] + _mod_specs(mod, layer, 3, tm, rows_per_seq) + [
        _const_spec((1, d)),
        _const_spec((d, D_FF)),
        _const_spec((1, D_FF)),
        _const_spec((D_FF, d)),
        _const_spec((1, d)),
        _const_spec((1, d)),
    ]
    return pl.pallas_call(
        functools.partial(_ffn_kernel, final_norm=final_norm),
        grid=(n // tm,),
        in_specs=in_specs,
        out_specs=_row_spec(tm, d),
        out_shape=jax.ShapeDtypeStruct((n, d), F32),
        compiler_params=pltpu.CompilerParams(
            dimension_semantics=("arbitrary",), vmem_limit_bytes=VMEM_LIMIT),
        name="sqrelu_mlp",
    )(x, mod, mod, mod, w["norm_g"], w["w1"], w["b1"], w["w2"], w["b2"], w["final_g"])


def _gla_project(x, ng, scale, shift, win, wg2, bg):
    h = _modulated_norm(x, ng, scale, shift)
    proj = _dot(h, win)
    gate = _dot(proj[:, GLA_QKVG:], wg2) + bg
    log_a = _log_sigmoid(gate) * (1.0 / GLA_TAU)
    return proj[:, :GLA_QKVG], log_a


def _gla_output(o, gate_in, x, gt, og, wout):
    parts = []
    for h in range(GLA_HEADS):
        cols = slice(h * GLA_DV_HEAD, (h + 1) * GLA_DV_HEAD)
        parts.append(_rms(o[:, cols], og))
    on = jnp.concatenate(parts, axis=1)
    y = _dot(on * _silu(gate_in), wout)
    return x + gt * y


def _split3(x):
    p1 = x.astype(BF16)
    r1 = x - p1.astype(F32)
    p2 = r1.astype(BF16)
    r2 = r1 - p2.astype(F32)
    return p1, p2, r2.astype(BF16)


def _chunk_cumsum(tril, log_a):
    p1, p2, p3 = _split3(log_a)
    return (jnp.dot(tril, p1, preferred_element_type=F32)
            + jnp.dot(tril, p2, preferred_element_type=F32)
            + jnp.dot(tril, p3, preferred_element_type=F32))


def _level_masks(c):
    rr = lax.broadcasted_iota(jnp.int32, (c, c), 0)
    cc = lax.broadcasted_iota(jnp.int32, (c, c), 1)
    masks = []
    s = c
    while s > SUBLANES:
        half = s // 2
        same = (rr ^ cc) < s
        masks.append(same & ((rr & half) != 0) & ((cc & half) == 0))
        s = half
    return masks, rr, cc


def _block_rows(x, s, r):
    c, w = x.shape
    return jnp.concatenate(
        [jnp.broadcast_to(x[p * s + r:p * s + r + 1, :], (s, w)) for p in range(c // s)], axis=0)


def _diag_terms(q, k, a):
    c, w = q.shape
    ii = lax.broadcasted_iota(jnp.int32, (c, w), 0) & (SUBLANES - 1)
    decay = jnp.zeros((c, w), F32)
    terms = [None] * SUBLANES
    for j in range(SUBLANES - 1, -1, -1):
        if j < SUBLANES - 1:
            decay = decay * _block_rows(a, SUBLANES, j + 1)
        decay = jnp.where(ii == j, 1.0, decay)
        terms[j] = q * _block_rows(k, SUBLANES, j) * decay
    return jnp.concatenate(terms, axis=0)


def _gla_chunk_pair(q2, k2, v2, la2, b2, st_refs, ones_bd, masks, rr, cc):
    c = q2[0].shape[0]
    zs = []
    for q, k, la in zip(q2, k2, la2):
        zs.append(_diag_terms(q, k, jnp.exp(la)))
    sums = jnp.dot(jnp.concatenate(zs, axis=1).astype(BF16), ones_bd,
                   preferred_element_type=F32)
    outs = []
    for hh, (q, k, v, b, st_ref) in enumerate(zip(q2, k2, v2, b2, st_refs)):
        dk = q.shape[1]
        attn = jnp.zeros((c, c), F32)
        base = rr & ~(SUBLANES - 1)
        for j in range(SUBLANES):
            rj = sums[j * c:(j + 1) * c, hh * dk:hh * dk + c]
            attn = jnp.where(cc == base + j, rj, attn)
        s = c
        for mask in masks:
            half = s // 2
            e = jnp.exp(-jnp.abs(b - _block_rows(b, s, half)))
            attn = jnp.where(mask, _dot_nt(q * e, k * e), attn)
            s = half
        b_last = b[c - 1:c, :]
        st = st_ref[...]
        o = _dot_nt(q * jnp.exp(b), st) + _dot(attn, v)
        kd = k * jnp.exp(b_last - b)
        st_ref[...] = jnp.exp(b_last) * st + _dot_tn(v, kd)
        outs.append(o)
    return outs


def _gla_prompt_kernel(x_ref, sh_ref, sc_ref, gt_ref, ng_ref, win_ref, wg2_ref, bg_ref,
                       og_ref, wout_ref, tril_ref, ones_ref, xo_ref, s_ref,
                       proj_scr, la_scr, o_scr, st_scr, *, tm, tiles_per_seq):
    tile = pl.program_id(0) % tiles_per_seq

    @pl.when(tile == 0)
    def _():
        st_scr[...] = jnp.zeros(st_scr.shape, F32)

    x = x_ref[...]
    proj, log_a = _gla_project(x, ng_ref[...], sc_ref[...], sh_ref[...],
                               win_ref[...], wg2_ref[...], bg_ref[...])
    proj_scr[...] = proj
    la_scr[...] = log_a

    c = GLA_CHUNK
    dk, dv = GLA_DK_HEAD, GLA_DV_HEAD
    qscale = GLA_DK_HEAD ** -0.5

    def chunk_body(ci, carry):
        r0 = pl.multiple_of(ci * c, c)
        rows = pl.ds(r0, c)
        la_c = la_scr[rows, :]
        b_c = _chunk_cumsum(tril_ref[...], la_c)
        masks, rr, cc = _level_masks(c)
        outs = []
        for h0 in range(0, GLA_HEADS, 2):
            heads = (h0, h0 + 1)
            q2 = [proj_scr[rows, h * dk:(h + 1) * dk] * qscale for h in heads]
            k2 = [proj_scr[rows, GLA_DK + h * dk:GLA_DK + (h + 1) * dk] for h in heads]
            v2 = [proj_scr[rows, 2 * GLA_DK + h * dv:2 * GLA_DK + (h + 1) * dv] for h in heads]
            la2 = [la_c[:, h * dk:(h + 1) * dk] for h in heads]
            b2 = [b_c[:, h * dk:(h + 1) * dk] for h in heads]
            st_refs = [st_scr.at[h] for h in heads]
            outs += _gla_chunk_pair(q2, k2, v2, la2, b2, st_refs, ones_ref[...], masks, rr, cc)
        o_scr[rows, :] = jnp.concatenate(outs, axis=1)
        return carry

    lax.fori_loop(0, tm // c, chunk_body, 0)

    gate_in = proj_scr[:, 2 * GLA_DK + GLA_DV:]
    xo_ref[...] = _gla_output(o_scr[...], gate_in, x, gt_ref[...], og_ref[...], wout_ref[...])

    @pl.when(tile == tiles_per_seq - 1)
    def _():
        for h in range(GLA_HEADS):
            s_ref[h] = st_scr[h].T


def _gla_prompt_call(x, mod, layer, n_seq, seq_len, tm, w):
    n, d = x.shape
    tiles_per_seq = seq_len // tm
    dk, dv = GLA_DK_HEAD, GLA_DV_HEAD
    in_specs = [_row_spec(tm, d)] + _mod_specs(mod, layer, 0, tm, seq_len) + [
        _const_spec((1, d)),
        _const_spec((d, GLA_IN_PAD)),
        _const_spec((GLA_GATE_PAD, GLA_DK)),
        _const_spec((1, GLA_DK)),
        _const_spec((1, dv)),
        _const_spec((GLA_DV, d)),
        _const_spec((GLA_CHUNK, GLA_CHUNK)),
        _const_spec((2 * dk, 2 * dk)),
    ]
    out_specs = [
        _row_spec(tm, d),
        pl.BlockSpec((None, GLA_HEADS, dk, dv), lambda i: (i // tiles_per_seq, 0, 0, 0)),
    ]
    out_shape = [
        jax.ShapeDtypeStruct((n, d), F32),
        jax.ShapeDtypeStruct((n_seq, GLA_HEADS, dk, dv), F32),
    ]
    return pl.pallas_call(
        functools.partial(_gla_prompt_kernel, tm=tm, tiles_per_seq=tiles_per_seq),
        grid=(n // tm,),
        in_specs=in_specs,
        out_specs=out_specs,
        out_shape=out_shape,
        scratch_shapes=[
            pltpu.VMEM((tm, GLA_QKVG), F32),
            pltpu.VMEM((tm, GLA_DK), F32),
            pltpu.VMEM((tm, GLA_DV), F32),
            pltpu.VMEM((GLA_HEADS, dv, dk), F32),
        ],
        compiler_params=pltpu.CompilerParams(
            dimension_semantics=("arbitrary",), vmem_limit_bytes=VMEM_LIMIT),
        name="gla_mixer_prompt",
    )(x, mod, mod, mod, w["norm_g"], w["w_in"], w["w_gate2"], w["b_gate"], w["out_g"],
      w["w_out"], w["tril"], w["ones_bd"])


def _gla_proj_kernel(x_ref, sh_ref, sc_ref, ng_ref, win_ref, wg2_ref, bg_ref, p_ref, la_ref):
    proj, log_a = _gla_project(x_ref[...], ng_ref[...], sc_ref[...], sh_ref[...],
                               win_ref[...], wg2_ref[...], bg_ref[...])
    p_ref[...] = proj
    la_ref[...] = log_a


def _gla_proj_call(x, mod, layer, tm, w):
    n, d = x.shape
    specs = _mod_specs(mod, layer, 0, tm, None)
    in_specs = [_row_spec(tm, d), specs[0], specs[1],
                _const_spec((1, d)),
                _const_spec((d, GLA_IN_PAD)),
                _const_spec((GLA_GATE_PAD, GLA_DK)),
                _const_spec((1, GLA_DK))]
    return pl.pallas_call(
        _gla_proj_kernel,
        grid=(n // tm,),
        in_specs=in_specs,
        out_specs=[_row_spec(tm, GLA_QKVG), _row_spec(tm, GLA_DK)],
        out_shape=[jax.ShapeDtypeStruct((n, GLA_QKVG), F32),
                   jax.ShapeDtypeStruct((n, GLA_DK), F32)],
        compiler_params=pltpu.CompilerParams(
            dimension_semantics=("arbitrary",), vmem_limit_bytes=VMEM_LIMIT),
        name="gla_proj_sample",
    )(x, mod, mod, w["norm_g"], w["w_in"], w["w_gate2"], w["b_gate"])


def _gla_step_kernel(p_ref, la_ref, s0_ref, o_ref, s1_ref, p_scr, la_scr, *, seqs, steps):
    dk, dv = GLA_DK_HEAD, GLA_DV_HEAD
    qscale = GLA_DK_HEAD ** -0.5
    row = lax.broadcasted_iota(jnp.int32, (SUBLANES, dk), 0)
    p_scr[...] = jnp.zeros(p_scr.shape, F32)
    la_scr[...] = jnp.zeros(la_scr.shape, F32)

    def seq_body(bi, carry):
        p_scr[0:steps, :] = p_ref[bi]
        la_scr[0:steps, :] = la_ref[bi]
        outs = []
        for h in range(GLA_HEADS):
            q = p_scr[:, h * dk:(h + 1) * dk] * qscale
            k = p_scr[:, GLA_DK + h * dk:GLA_DK + (h + 1) * dk]
            v = p_scr[:, 2 * GLA_DK + h * dv:2 * GLA_DK + (h + 1) * dv]
            la = la_scr[:, h * dk:(h + 1) * dk]
            b = jnp.zeros((SUBLANES, dk), F32)
            for t in range(steps):
                b = b + jnp.where(row >= t, la[t:t + 1, :], 0.0)
            b_last = b[steps - 1:steps, :]
            s0 = s0_ref[bi, h]
            o = _dot(q * jnp.exp(b), s0)
            for j in range(steps):
                e = jnp.exp(jnp.minimum(b - b[j:j + 1, :], 0.0))
                z = jnp.where(row >= j, q * k[j:j + 1, :] * e, 0.0)
                o = o + jnp.sum(z, axis=-1, keepdims=True) * v[j:j + 1, :]
            outs.append(o)
            m = jnp.where(row == steps, jnp.exp(b_last), k * jnp.exp(b_last - b))
            m_t = m.T
            s1_ref[bi, h] = m_t[:, steps:steps + 1] * s0 + _dot(m_t, v)
        o_ref[bi] = jnp.concatenate(outs, axis=1)[0:steps, :]
        return carry

    lax.fori_loop(0, seqs, seq_body, 0)


def _gla_step_call(proj, log_a, state, seqs):
    n_seq, steps = proj.shape[0], proj.shape[1]
    dk, dv = GLA_DK_HEAD, GLA_DV_HEAD
    blk = lambda *tail: pl.BlockSpec((seqs,) + tail, lambda i: (i,) + (0,) * len(tail))
    return pl.pallas_call(
        functools.partial(_gla_step_kernel, seqs=seqs, steps=steps),
        grid=(n_seq // seqs,),
        in_specs=[blk(steps, GLA_QKVG), blk(steps, GLA_DK), blk(GLA_HEADS, dk, dv)],
        out_specs=[blk(steps, GLA_DV), blk(GLA_HEADS, dk, dv)],
        out_shape=[jax.ShapeDtypeStruct((n_seq, steps, GLA_DV), F32),
                   jax.ShapeDtypeStruct(state.shape, F32)],
        scratch_shapes=[pltpu.VMEM((SUBLANES, GLA_QKVG), F32),
                        pltpu.VMEM((SUBLANES, GLA_DK), F32)],
        compiler_params=pltpu.CompilerParams(
            dimension_semantics=("arbitrary",), vmem_limit_bytes=VMEM_LIMIT),
        name="gla_step_sample",
    )(proj, log_a, state)


def _gla_out_kernel(o_ref, p_ref, x_ref, gt_ref, og_ref, wout_ref, xo_ref):
    xo_ref[...] = _gla_output(o_ref[...], p_ref[...], x_ref[...], gt_ref[...],
                              og_ref[...], wout_ref[...])


def _gla_out_call(o, proj, x, mod, layer, tm, w):
    n, d = x.shape
    gate_spec = _mod_specs(mod, layer, 0, tm, None)[2]
    g_block = (2 * GLA_DK + GLA_DV) // GLA_DV
    in_specs = [_row_spec(tm, GLA_DV),
                pl.BlockSpec((tm, GLA_DV), lambda i: (i, g_block)),
                _row_spec(tm, d), gate_spec,
                _const_spec((1, GLA_DV_HEAD)),
                _const_spec((GLA_DV, d))]
    return pl.pallas_call(
        _gla_out_kernel,
        grid=(n // tm,),
        in_specs=in_specs,
        out_specs=_row_spec(tm, d),
        out_shape=jax.ShapeDtypeStruct((n, d), F32),
        compiler_params=pltpu.CompilerParams(
            dimension_semantics=("arbitrary",), vmem_limit_bytes=VMEM_LIMIT),
        name="gla_out_sample",
    )(o, proj, x, mod, w["out_g"], w["w_out"])


PROMPT_TILE = 512
SAMPLE_TILE = 512
SAMPLE_SEQS_PER_STEP = 8


def kernel(x_prompt, x_sample, c_prompt, c_sample, state_gla, ada_w, ada_b, norm_mix_g, norm_ffn_g,
           ffn_w1, ffn_b1, ffn_w2, ffn_b2, gmlp_w_in, gmlp_b_in, gmlp_ln_g, gmlp_ln_b, gmlp_w_s,
           gmlp_b_s, gmlp_w_out, gmlp_b_out, gla_w_in, gla_w_gate2, gla_b_gate, gla_norm_g,
           gla_w_out, final_norm_g):
    n_seq_p, seq_p, d = x_prompt.shape
    n_seq_s, seq_s, _ = x_sample.shape
    assert d == D_MODEL and seq_p % PROMPT_TILE == 0 and PROMPT_TILE % CHUNK_A == 0
    assert seq_s < SUBLANES and (n_seq_s * seq_s) % SAMPLE_TILE == 0 and CHUNK_A % seq_s == 0
    row = lambda a: a.reshape(1, -1)

    c_all = jnp.concatenate([c_prompt, jnp.repeat(c_sample, seq_s, axis=0)], axis=0)
    mod_p, mod_s = _ada_call(c_all, ada_w, ada_b, n_seq_p)
    mod_p = mod_p.reshape(mod_p.shape[0], n_seq_p, 1, N_MOD * d)

    causal = jnp.tril(jnp.ones((CHUNK_A, CHUNK_A), dtype=bool))
    ws = jnp.where(causal[None], gmlp_w_s, jnp.zeros_like(gmlp_w_s))
    reps = CHUNK_A // seq_s
    eye = jnp.eye(reps, dtype=F32)
    ws_s = jnp.einsum("ab,gij->gaibj", eye, ws[:, :seq_s, :seq_s]).reshape(
        GMLP_GROUPS, CHUNK_A, CHUNK_A)
    bias_p = jnp.repeat(gmlp_b_s.T, GMLP_GROUP_W, axis=1)
    bias_s = jnp.tile(bias_p[:seq_s], (reps, 1))

    gmlp_w = dict(norm_g=row(norm_mix_g[0]), w_in=gmlp_w_in.astype(BF16), b_in=row(gmlp_b_in),
                  ln_g=row(gmlp_ln_g), ln_b=row(gmlp_ln_b), w_out=gmlp_w_out.astype(BF16),
                  b_out=row(gmlp_b_out))
    gmlp_w_p = dict(gmlp_w, mix=ws.astype(BF16), bias_s=bias_p)
    gmlp_w_s = dict(gmlp_w, mix=ws_s.astype(BF16), bias_s=bias_s)

    ffn_w = [dict(norm_g=row(norm_ffn_g[i]), w1=ffn_w1[i].astype(BF16), b1=row(ffn_b1[i]),
                  w2=ffn_w2[i].astype(BF16), b2=row(ffn_b2[i]), final_g=row(final_norm_g))
             for i in range(2)]

    w_in_pad = jnp.pad(gla_w_in, ((0, 0), (0, GLA_IN_PAD - gla_w_in.shape[1]))).astype(BF16)
    wg2_pad = jnp.pad(gla_w_gate2, ((0, GLA_GATE_PAD - GLA_GATE_RANK), (0, 0))).astype(BF16)
    blk_ones = jnp.kron(jnp.eye(2, dtype=F32), jnp.ones((GLA_DK_HEAD, GLA_DK_HEAD), F32))
    gla_w = dict(norm_g=row(norm_mix_g[1]), w_in=w_in_pad, w_gate2=wg2_pad, b_gate=row(gla_b_gate),
                 out_g=row(gla_norm_g), w_out=gla_w_out.astype(BF16),
                 tril=jnp.tril(jnp.ones((GLA_CHUNK, GLA_CHUNK), F32)).astype(BF16),
                 ones_bd=blk_ones.astype(BF16))

    xp = x_prompt.reshape(n_seq_p * seq_p, d)
    xp, = _gmlp_call(xp, mod_p, 0, seq_p, PROMPT_TILE, gmlp_w_p, want_v=False)
    xp = _ffn_call(xp, mod_p, 0, seq_p, PROMPT_TILE, ffn_w[0], final_norm=False)
    xp, state_p = _gla_prompt_call(xp, mod_p, 1, n_seq_p, seq_p, PROMPT_TILE, gla_w)
    xp = _ffn_call(xp, mod_p, 1, seq_p, PROMPT_TILE, ffn_w[1], final_norm=True)

    xs = x_sample.reshape(n_seq_s * seq_s, d)
    xs, chunk_v = _gmlp_call(xs, mod_s, 0, None, SAMPLE_TILE, gmlp_w_s, want_v=True)
    xs = _ffn_call(xs, mod_s, 0, None, SAMPLE_TILE, ffn_w[0], final_norm=False)
    proj, log_a = _gla_proj_call(xs, mod_s, 1, SAMPLE_TILE, gla_w)
    o, state_s = _gla_step_call(proj.reshape(n_seq_s, seq_s, GLA_QKVG),
                                log_a.reshape(n_seq_s, seq_s, GLA_DK), state_gla,
                                SAMPLE_SEQS_PER_STEP)
    xs = _gla_out_call(o.reshape(n_seq_s * seq_s, GLA_DV), proj, xs, mod_s, 1, SAMPLE_TILE, gla_w)
    xs = _ffn_call(xs, mod_s, 1, None, SAMPLE_TILE, ffn_w[1], final_norm=True)

    return (xp.reshape(x_prompt.shape), xs.reshape(x_sample.shape), state_p, state_s,
            chunk_v.reshape(n_seq_s, seq_s, GMLP_WIDTH))
```

```python
import functools

import jax
import jax.numpy as jnp
from jax import lax
from jax.experimental import pallas as pl
from jax.experimental.pallas import tpu as pltpu

F32 = jnp.float32
BF16 = jnp.bfloat16

D_MODEL = 1024
N_MOD = 6
CHUNK_A = 128
GMLP_WIDTH = D_MODEL
GMLP_GROUPS = 4
GMLP_GROUP_W = GMLP_WIDTH // GMLP_GROUPS
GLA_HEADS = 4
GLA_DK = D_MODEL // 2
GLA_DV = D_MODEL
GLA_DK_HEAD = GLA_DK // GLA_HEADS
GLA_DV_HEAD = GLA_DV // GLA_HEADS
GLA_GATE_RANK = 16
GLA_TAU = 16.0
GLA_CHUNK = 64
GLA_FAST_CHUNK = 128
GLA_TRIL = max(GLA_CHUNK, GLA_FAST_CHUNK)
GLA_SAFE_EXPONENT = 80.0
GLA_QKVG = 2 * GLA_DK + 2 * GLA_DV
GLA_GATE_PAD = 128
GLA_IN_PAD = GLA_QKVG + GLA_GATE_PAD
D_FF = 4 * D_MODEL
EPS = 1e-6

SUBLANES = 8
FFN_CHUNK = 1024
VMEM_LIMIT = 56 * 1024 * 1024


def _rms(x, g):
    return x * lax.rsqrt(jnp.mean(x * x, axis=-1, keepdims=True) + EPS) * g


def _modulated_norm(x, g, scale, shift):
    return _rms(x, g) * (1.0 + scale) + shift


def _dot(a, b):
    return jnp.dot(a.astype(BF16), b.astype(BF16), preferred_element_type=F32)


def _dot_nt(a, b):
    return lax.dot_general(a.astype(BF16), b.astype(BF16), (((1,), (1,)), ((), ())),
                           preferred_element_type=F32)


def _dot_tn(a, b):
    return lax.dot_general(a.astype(BF16), b.astype(BF16), (((0,), (0,)), ((), ())),
                           preferred_element_type=F32)


def _gelu_tanh(x):
    return 0.5 * x * (1.0 + jnp.tanh(0.7978845608028654 * (x + 0.044715 * (x * x * x))))


def _silu(x):
    return x * jax.nn.sigmoid(x)


def _log_sigmoid(x):
    return -(jnp.maximum(-x, 0.0) + jnp.log1p(jnp.exp(-jnp.abs(x))))


def _ada_kernel(c_ref, w_ref, b_ref, op_ref, os_ref, *, n_prompt):
    s = _silu(c_ref[...])
    y = _dot(s, w_ref[...]) + b_ref[...]
    op_ref[...] = y[:n_prompt]
    os_ref[...] = y[n_prompt:]


def _ada_call(c_all, ada_w, ada_b, n_prompt):
    depth = ada_w.shape[0]
    rows = c_all.shape[0]
    d = D_MODEL
    return pl.pallas_call(
        functools.partial(_ada_kernel, n_prompt=n_prompt),
        grid=(depth, N_MOD),
        in_specs=[
            pl.BlockSpec((rows, d), lambda l, j: (0, 0)),
            pl.BlockSpec((None, d, d), lambda l, j: (l, 0, j)),
            pl.BlockSpec((None, 1, d), lambda l, j: (l, 0, j)),
        ],
        out_specs=[
            pl.BlockSpec((None, n_prompt, d), lambda l, j: (l, 0, j)),
            pl.BlockSpec((None, rows - n_prompt, d), lambda l, j: (l, 0, j)),
        ],
        out_shape=[
            jax.ShapeDtypeStruct((depth, n_prompt, N_MOD * d), F32),
            jax.ShapeDtypeStruct((depth, rows - n_prompt, N_MOD * d), F32),
        ],
        compiler_params=pltpu.CompilerParams(
            dimension_semantics=("arbitrary", "arbitrary"), vmem_limit_bytes=VMEM_LIMIT),
        name="adaln_mod",
    )(c_all, ada_w, ada_b.reshape(depth, 1, N_MOD * d))


def _const_spec(shape):
    zeros = (0,) * len(shape)
    return pl.BlockSpec(shape, lambda i: zeros, pipeline_mode=pl.Buffered(1))


def _layer_spec(shape, layer):
    zeros = (0,) * len(shape)
    return pl.BlockSpec((None,) + shape, lambda i: (layer,) + zeros,
                        pipeline_mode=pl.Buffered(1))


def _mod_specs(mod, layer, first_chunk, tm, rows_per_seq):
    d = D_MODEL
    specs = []
    for j in range(first_chunk, first_chunk + 3):
        if rows_per_seq is None:
            specs.append(pl.BlockSpec((None, tm, d), lambda i, j=j: (layer, i, j)))
        else:
            tiles = rows_per_seq // tm
            specs.append(pl.BlockSpec((None, None, 1, d),
                                      lambda i, j=j, tiles=tiles: (layer, i // tiles, 0, j)))
    return specs


def _row_spec(tm, width):
    return pl.BlockSpec((tm, width), lambda i: (i, 0))


def _gmlp_kernel(x_ref, sh_ref, sc_ref, gt_ref, ng_ref, win_ref, bin_ref, lng_ref, lnb_ref,
                 mix_ref, bs_ref, wout_ref, bout_ref, xo_ref, *v_refs, tm):
    x = x_ref[...]
    h = _modulated_norm(x, ng_ref[...], sc_ref[...], sh_ref[...])
    z = _gelu_tanh(_dot(h, win_ref[...]) + bin_ref[...])
    u = z[:, :GMLP_WIDTH]
    v = z[:, GMLP_WIDTH:]
    mu = jnp.mean(v, axis=-1, keepdims=True)
    vc = v - mu
    var = jnp.mean(vc * vc, axis=-1, keepdims=True)
    v = vc * lax.rsqrt(var + EPS) * lng_ref[...] + lnb_ref[...]
    if v_refs:
        v_refs[0][...] = v
    vb = v.astype(BF16)
    rows = []
    for c in range(tm // CHUNK_A):
        cols = []
        for g in range(GMLP_GROUPS):
            blk = vb[c * CHUNK_A:(c + 1) * CHUNK_A, g * GMLP_GROUP_W:(g + 1) * GMLP_GROUP_W]
            cols.append(jnp.dot(mix_ref[g], blk, preferred_element_type=F32))
        rows.append(jnp.concatenate(cols, axis=1) + bs_ref[...])
    s = jnp.concatenate(rows, axis=0)
    y = _dot(u * s, wout_ref[...]) + bout_ref[...]
    xo_ref[...] = x + gt_ref[...] * y


def _gmlp_call(x, mod, layer, rows_per_seq, tm, w, want_v):
    n, d = x.shape
    out_shape = [jax.ShapeDtypeStruct((n, d), F32)]
    out_specs = [_row_spec(tm, d)]
    if want_v:
        out_shape.append(jax.ShapeDtypeStruct((n, GMLP_WIDTH), F32))
        out_specs.append(_row_spec(tm, GMLP_WIDTH))
    in_specs = [_row_spec(tm, d)] + _mod_specs(mod, layer, 0, tm, rows_per_seq) + [
        _const_spec((1, d)),
        _const_spec((d, 2 * GMLP_WIDTH)),
        _const_spec((1, 2 * GMLP_WIDTH)),
        _const_spec((1, GMLP_WIDTH)),
        _const_spec((1, GMLP_WIDTH)),
        _const_spec((GMLP_GROUPS, CHUNK_A, CHUNK_A)),
        _const_spec((CHUNK_A, GMLP_WIDTH)),
        _const_spec((GMLP_WIDTH, d)),
        _const_spec((1, d)),
    ]
    return pl.pallas_call(
        functools.partial(_gmlp_kernel, tm=tm),
        grid=(n // tm,),
        in_specs=in_specs,
        out_specs=out_specs,
        out_shape=out_shape,
        compiler_params=pltpu.CompilerParams(
            dimension_semantics=("arbitrary",), vmem_limit_bytes=VMEM_LIMIT),
        name="gmlp_mixer",
    )(x, mod, mod, mod, w["norm_g"], w["w_in"], w["b_in"], w["ln_g"], w["ln_b"],
      w["mix"], w["bias_s"], w["w_out"], w["b_out"])


def _ffn_kernel(x_ref, sh_ref, sc_ref, gt_ref, ng_ref, w1_ref, b1_ref, w2_ref, b2_ref,
                fg_ref, o_ref, *, final_norm):
    x = x_ref[...]
    h = _modulated_norm(x, ng_ref[...], sc_ref[...], sh_ref[...]).astype(BF16)
    acc = jnp.zeros(x.shape, F32)
    for j in range(D_FF // FFN_CHUNK):
        cols = slice(j * FFN_CHUNK, (j + 1) * FFN_CHUNK)
        a = jnp.dot(h, w1_ref[:, cols], preferred_element_type=F32) + b1_ref[:, cols]
        r = jnp.square(jnp.maximum(a, 0.0)).astype(BF16)
        acc = acc + jnp.dot(r, w2_ref[cols, :], preferred_element_type=F32)
    y = x + gt_ref[...] * (acc + b2_ref[...])
    if final_norm:
        y = _rms(y, fg_ref[...])
    o_ref[...] = y


def _ffn_call(x, mod, layer, rows_per_seq, tm, w, final_norm):
    n, d = x.shape
    in_specs = [_row_spec(tm, d)] + _mod_specs(mod, layer, 3, tm, rows_per_seq) + [
        _layer_spec((1, d), layer),
        _layer_spec((d, D_FF), layer),
        _layer_spec((1, D_FF), layer),
        _layer_spec((D_FF, d), layer),
        _layer_spec((1, d), layer),
        _const_spec((1, d)),
    ]
    return pl.pallas_call(
        functools.partial(_ffn_kernel, final_norm=final_norm),
        grid=(n // tm,),
        in_specs=in_specs,
        out_specs=_row_spec(tm, d),
        out_shape=jax.ShapeDtypeStruct((n, d), F32),
        compiler_params=pltpu.CompilerParams(
            dimension_semantics=("arbitrary",), vmem_limit_bytes=VMEM_LIMIT),
        name="sqrelu_mlp",
    )(x, mod, mod, mod, w["norm_g"], w["w1"], w["b1"], w["w2"], w["b2"], w["final_g"])


def _gla_project(x, ng, scale, shift, win, wg2, bg):
    h = _modulated_norm(x, ng, scale, shift)
    proj = _dot(h, win)
    gate = _dot(proj[:, GLA_QKVG:], wg2) + bg
    log_a = _log_sigmoid(gate) * (1.0 / GLA_TAU)
    return proj[:, :GLA_QKVG], log_a


def _gla_output(o, gate_in, x, gt, og, wout):
    parts = []
    for h in range(GLA_HEADS):
        cols = slice(h * GLA_DV_HEAD, (h + 1) * GLA_DV_HEAD)
        parts.append(_rms(o[:, cols], og))
    on = jnp.concatenate(parts, axis=1)
    y = _dot(on * _silu(gate_in), wout)
    return x + gt * y


def _split3(x):
    p1 = x.astype(BF16)
    r1 = x - p1.astype(F32)
    p2 = r1.astype(BF16)
    r2 = r1 - p2.astype(F32)
    return p1, p2, r2.astype(BF16)


def _chunk_cumsum(tril, log_a):
    p1, p2, p3 = _split3(log_a)
    return (jnp.dot(tril, p1, preferred_element_type=F32)
            + jnp.dot(tril, p2, preferred_element_type=F32)
            + jnp.dot(tril, p3, preferred_element_type=F32))


def _level_masks(c):
    rr = lax.broadcasted_iota(jnp.int32, (c, c), 0)
    cc = lax.broadcasted_iota(jnp.int32, (c, c), 1)
    masks = []
    s = c
    while s > SUBLANES:
        half = s // 2
        same = (rr ^ cc) < s
        masks.append(same & ((rr & half) != 0) & ((cc & half) == 0))
        s = half
    return masks, rr, cc


def _block_rows(x, s, r):
    c, w = x.shape
    return jnp.concatenate(
        [jnp.broadcast_to(x[p * s + r:p * s + r + 1, :], (s, w)) for p in range(c // s)], axis=0)


def _diag_terms(q, k, a):
    c, w = q.shape
    ii = lax.broadcasted_iota(jnp.int32, (c, w), 0) & (SUBLANES - 1)
    decay = jnp.zeros((c, w), F32)
    terms = [None] * SUBLANES
    for j in range(SUBLANES - 1, -1, -1):
        if j < SUBLANES - 1:
            decay = decay * _block_rows(a, SUBLANES, j + 1)
        decay = jnp.where(ii == j, 1.0, decay)
        terms[j] = q * _block_rows(k, SUBLANES, j) * decay
    return jnp.concatenate(terms, axis=0)


def _gla_chunk_pair(q2, k2, v2, la2, b2, st_refs, ones_bd, masks, rr, cc):
    c = q2[0].shape[0]
    zs = []
    for q, k, la in zip(q2, k2, la2):
        zs.append(_diag_terms(q, k, jnp.exp(la)))
    sums = jnp.dot(jnp.concatenate(zs, axis=1).astype(BF16), ones_bd,
                   preferred_element_type=F32)
    outs = []
    for hh, (q, k, v, b, st_ref) in enumerate(zip(q2, k2, v2, b2, st_refs)):
        dk = q.shape[1]
        attn = jnp.zeros((c, c), F32)
        base = rr & ~(SUBLANES - 1)
        for j in range(SUBLANES):
            rj = sums[j * c:(j + 1) * c, hh * dk:hh * dk + c]
            attn = jnp.where(cc == base + j, rj, attn)
        s = c
        for mask in masks:
            half = s // 2
            e = jnp.exp(-jnp.abs(b - _block_rows(b, s, half)))
            attn = jnp.where(mask, _dot_nt(q * e, k * e), attn)
            s = half
        b_last = b[c - 1:c, :]
        st = st_ref[...]
        o = _dot_nt(q * jnp.exp(b), st) + _dot(attn, v)
        kd = k * jnp.exp(b_last - b)
        st_ref[...] = jnp.exp(b_last) * st + _dot_tn(v, kd)
        outs.append(o)
    return outs


def _gla_prompt_kernel(x_ref, sh_ref, sc_ref, gt_ref, ng_ref, win_ref, wg2_ref, bg_ref,
                       og_ref, wout_ref, tril_ref, ones_ref, xo_ref, s_ref,
                       proj_scr, la_scr, b_scr, o_scr, st_scr, *, tm, tiles_per_seq):
    tile = pl.program_id(0) % tiles_per_seq

    @pl.when(tile == 0)
    def _():
        st_scr[...] = jnp.zeros(st_scr.shape, F32)

    x = x_ref[...]
    proj, log_a = _gla_project(x, ng_ref[...], sc_ref[...], sh_ref[...],
                               win_ref[...], wg2_ref[...], bg_ref[...])
    proj_scr[...] = proj
    la_scr[...] = log_a

    dk, dv = GLA_DK_HEAD, GLA_DV_HEAD
    qscale = GLA_DK_HEAD ** -0.5
    cf = GLA_FAST_CHUNK
    for ci in range(tm // cf):
        rows = slice(ci * cf, (ci + 1) * cf)
        b_scr[rows, :] = _chunk_cumsum(tril_ref[0:cf, 0:cf], log_a[rows, :])

    def head_slices(rows, h):
        q = proj_scr[rows, h * dk:(h + 1) * dk] * qscale
        k = proj_scr[rows, GLA_DK + h * dk:GLA_DK + (h + 1) * dk]
        v = proj_scr[rows, 2 * GLA_DK + h * dv:2 * GLA_DK + (h + 1) * dv]
        return q, k, v

    def single_ref_tile():
        rr = lax.broadcasted_iota(jnp.int32, (cf, cf), 0)
        cc = lax.broadcasted_iota(jnp.int32, (cf, cf), 1)
        causal = rr >= cc
        chunks = [slice(ci * cf, (ci + 1) * cf) for ci in range(tm // cf)]
        for h in range(GLA_HEADS):
            qes, vbs, scores, kvs, decays = [], [], [], [], []
            for rows in chunks:
                q, k, v = head_slices(rows, h)
                b = b_scr[rows, h * dk:(h + 1) * dk]
                decay_last = jnp.exp(b[cf - 1:cf, :])
                qe = (q * jnp.exp(b)).astype(BF16)
                kt = k * jnp.exp(-b)
                vb = v.astype(BF16)
                scores.append(_dot_nt(qe, kt))
                kvs.append(_dot_tn(vb, kt * decay_last))
                qes.append(qe)
                vbs.append(vb)
                decays.append(decay_last)
            st = st_scr[h]
            states = []
            for kv, decay_last in zip(kvs, decays):
                states.append(st)
                st = decay_last * st + kv
            st_scr[h] = st
            for rows, qe, vb, sc, st_in in zip(chunks, qes, vbs, scores, states):
                attn = jnp.where(causal, sc, 0.0)
                o_scr[rows, h * dv:(h + 1) * dv] = _dot_nt(qe, st_in) + _dot(attn, vb)

    c = GLA_CHUNK

    def robust_body(ci, carry):
        rows = pl.ds(pl.multiple_of(ci * c, c), c)
        masks, rr, cc = _level_masks(c)
        b_c = _chunk_cumsum(tril_ref[0:c, 0:c], la_scr[rows, :])
        outs = []
        for h0 in range(0, GLA_HEADS, 2):
            heads = (h0, h0 + 1)
            qkv = [head_slices(rows, h) for h in heads]
            la2 = [la_scr[rows, h * dk:(h + 1) * dk] for h in heads]
            b2 = [b_c[:, h * dk:(h + 1) * dk] for h in heads]
            st_refs = [st_scr.at[h] for h in heads]
            outs += _gla_chunk_pair([t[0] for t in qkv], [t[1] for t in qkv], [t[2] for t in qkv],
                                    la2, b2, st_refs, ones_ref[...], masks, rr, cc)
        o_scr[rows, :] = jnp.concatenate(outs, axis=1)
        return carry

    single_ref_safe = jnp.min(b_scr[...]) >= -GLA_SAFE_EXPONENT

    @pl.when(single_ref_safe)
    def _():
        single_ref_tile()

    @pl.when(jnp.logical_not(single_ref_safe))
    def _():
        lax.fori_loop(0, tm // c, robust_body, 0)

    gate_in = proj_scr[:, 2 * GLA_DK + GLA_DV:]
    xo_ref[...] = _gla_output(o_scr[...], gate_in, x, gt_ref[...], og_ref[...], wout_ref[...])

    @pl.when(tile == tiles_per_seq - 1)
    def _():
        for h in range(GLA_HEADS):
            s_ref[h] = st_scr[h].T


def _gla_prompt_call(x, mod, layer, n_seq, seq_len, tm, w):
    n, d = x.shape
    tiles_per_seq = seq_len // tm
    dk, dv = GLA_DK_HEAD, GLA_DV_HEAD
    in_specs = [_row_spec(tm, d)] + _mod_specs(mod, layer, 0, tm, seq_len) + [
        _const_spec((1, d)),
        _const_spec((d, GLA_IN_PAD)),
        _const_spec((GLA_GATE_PAD, GLA_DK)),
        _const_spec((1, GLA_DK)),
        _const_spec((1, dv)),
        _const_spec((GLA_DV, d)),
        _const_spec((GLA_TRIL, GLA_TRIL)),
        _const_spec((2 * dk, 2 * dk)),
    ]
    out_specs = [
        _row_spec(tm, d),
        pl.BlockSpec((None, GLA_HEADS, dk, dv), lambda i: (i // tiles_per_seq, 0, 0, 0)),
    ]
    out_shape = [
        jax.ShapeDtypeStruct((n, d), F32),
        jax.ShapeDtypeStruct((n_seq, GLA_HEADS, dk, dv), F32),
    ]
    return pl.pallas_call(
        functools.partial(_gla_prompt_kernel, tm=tm, tiles_per_seq=tiles_per_seq),
        grid=(n // tm,),
        in_specs=in_specs,
        out_specs=out_specs,
        out_shape=out_shape,
        scratch_shapes=[
            pltpu.VMEM((tm, GLA_QKVG), F32),
            pltpu.VMEM((tm, GLA_DK), F32),
            pltpu.VMEM((tm, GLA_DK), F32),
            pltpu.VMEM((tm, GLA_DV), F32),
            pltpu.VMEM((GLA_HEADS, dv, dk), F32),
        ],
        compiler_params=pltpu.CompilerParams(
            dimension_semantics=("arbitrary",), vmem_limit_bytes=VMEM_LIMIT),
        name="gla_mixer_prompt",
    )(x, mod, mod, mod, w["norm_g"], w["w_in"], w["w_gate2"], w["b_gate"], w["out_g"],
      w["w_out"], w["tril"], w["ones_bd"])


def _gla_proj_kernel(x_ref, sh_ref, sc_ref, ng_ref, win_ref, wg2_ref, bg_ref, p_ref, la_ref):
    proj, log_a = _gla_project(x_ref[...], ng_ref[...], sc_ref[...], sh_ref[...],
                               win_ref[...], wg2_ref[...], bg_ref[...])
    p_ref[...] = proj
    la_ref[...] = log_a


def _gla_proj_call(x, mod, layer, tm, w):
    n, d = x.shape
    specs = _mod_specs(mod, layer, 0, tm, None)
    in_specs = [_row_spec(tm, d), specs[0], specs[1],
                _const_spec((1, d)),
                _const_spec((d, GLA_IN_PAD)),
                _const_spec((GLA_GATE_PAD, GLA_DK)),
                _const_spec((1, GLA_DK))]
    return pl.pallas_call(
        _gla_proj_kernel,
        grid=(n // tm,),
        in_specs=in_specs,
        out_specs=[_row_spec(tm, GLA_QKVG), _row_spec(tm, GLA_DK)],
        out_shape=[jax.ShapeDtypeStruct((n, GLA_QKVG), F32),
                   jax.ShapeDtypeStruct((n, GLA_DK), F32)],
        compiler_params=pltpu.CompilerParams(
            dimension_semantics=("arbitrary",), vmem_limit_bytes=VMEM_LIMIT),
        name="gla_proj_sample",
    )(x, mod, mod, w["norm_g"], w["w_in"], w["w_gate2"], w["b_gate"])


def _gla_step_kernel(p_ref, la_ref, s0_ref, o_ref, s1_ref, *, seqs, steps):
    dk, dv = GLA_DK_HEAD, GLA_DV_HEAD
    qscale = GLA_DK_HEAD ** -0.5
    per_tile = SUBLANES // steps
    row = lax.broadcasted_iota(jnp.int32, (SUBLANES, dk), 0)

    def seq_rows(tile, s):
        if s:
            tile = pltpu.roll(tile, SUBLANES - s * steps, 0)
        keep = lax.broadcasted_iota(jnp.int32, tile.shape, 0) < steps
        return jnp.where(keep, tile, 0.0)

    def tile_body(ti, carry):
        rows = pl.ds(pl.multiple_of(ti * SUBLANES, SUBLANES), SUBLANES)
        p_tile = p_ref[rows, :]
        la_tile = la_ref[rows, :]
        o_tile = jnp.zeros((SUBLANES, GLA_DV), F32)
        for s in range(per_tile):
            bi = ti * per_tile + s
            p = seq_rows(p_tile, s)
            la_all = seq_rows(la_tile, s)
            outs = []
            for h in range(GLA_HEADS):
                q = p[:, h * dk:(h + 1) * dk] * qscale
                k = p[:, GLA_DK + h * dk:GLA_DK + (h + 1) * dk]
                v = p[:, 2 * GLA_DK + h * dv:2 * GLA_DK + (h + 1) * dv]
                la = la_all[:, h * dk:(h + 1) * dk]
                b = jnp.zeros((SUBLANES, dk), F32)
                for t in range(steps):
                    b = b + jnp.where(row >= t, la[t:t + 1, :], 0.0)
                b_last = b[steps - 1:steps, :]
                s0 = s0_ref[bi, h]
                o = _dot(q * jnp.exp(b), s0)
                for j in range(steps):
                    e = jnp.exp(jnp.minimum(b - b[j:j + 1, :], 0.0))
                    z = jnp.where(row >= j, q * k[j:j + 1, :] * e, 0.0)
                    o = o + jnp.sum(z, axis=-1, keepdims=True) * v[j:j + 1, :]
                outs.append(o)
                m = jnp.where(row == steps, jnp.exp(b_last), k * jnp.exp(b_last - b))
                m_t = m.T
                s1_ref[bi, h] = m_t[:, steps:steps + 1] * s0 + _dot(m_t, v)
            o_seq = jnp.concatenate(outs, axis=1)
            o_tile = o_tile + (pltpu.roll(o_seq, s * steps, 0) if s else o_seq)
        o_ref[rows, :] = o_tile
        return carry

    lax.fori_loop(0, seqs // per_tile, tile_body, 0)


def _gla_step_call(proj, log_a, state, steps, seqs):
    n_seq = state.shape[0]
    dk, dv = GLA_DK_HEAD, GLA_DV_HEAD
    rows = lambda width: pl.BlockSpec((seqs * steps, width), lambda i: (i, 0))
    st_spec = pl.BlockSpec((seqs, GLA_HEADS, dk, dv), lambda i: (i, 0, 0, 0))
    return pl.pallas_call(
        functools.partial(_gla_step_kernel, seqs=seqs, steps=steps),
        grid=(n_seq // seqs,),
        in_specs=[rows(GLA_QKVG), rows(GLA_DK), st_spec],
        out_specs=[rows(GLA_DV), st_spec],
        out_shape=[jax.ShapeDtypeStruct((n_seq * steps, GLA_DV), F32),
                   jax.ShapeDtypeStruct(state.shape, F32)],
        compiler_params=pltpu.CompilerParams(
            dimension_semantics=("arbitrary",), vmem_limit_bytes=VMEM_LIMIT),
        name="gla_step_sample",
    )(proj, log_a, state)


def _gla_out_kernel(o_ref, p_ref, x_ref, gt_ref, og_ref, wout_ref, xo_ref):
    xo_ref[...] = _gla_output(o_ref[...], p_ref[...], x_ref[...], gt_ref[...],
                              og_ref[...], wout_ref[...])


def _gla_out_call(o, proj, x, mod, layer, tm, w):
    n, d = x.shape
    gate_spec = _mod_specs(mod, layer, 0, tm, None)[2]
    g_block = (2 * GLA_DK + GLA_DV) // GLA_DV
    in_specs = [_row_spec(tm, GLA_DV),
                pl.BlockSpec((tm, GLA_DV), lambda i: (i, g_block)),
                _row_spec(tm, d), gate_spec,
                _const_spec((1, GLA_DV_HEAD)),
                _const_spec((GLA_DV, d))]
    return pl.pallas_call(
        _gla_out_kernel,
        grid=(n // tm,),
        in_specs=in_specs,
        out_specs=_row_spec(tm, d),
        out_shape=jax.ShapeDtypeStruct((n, d), F32),
        compiler_params=pltpu.CompilerParams(
            dimension_semantics=("arbitrary",), vmem_limit_bytes=VMEM_LIMIT),
        name="gla_out_sample",
    )(o, proj, x, mod, w["out_g"], w["w_out"])


PROMPT_TILE = 512
SAMPLE_TILE = 512
SAMPLE_SEQS_PER_STEP = 8


def kernel(x_prompt, x_sample, c_prompt, c_sample, state_gla, ada_w, ada_b, norm_mix_g, norm_ffn_g,
           ffn_w1, ffn_b1, ffn_w2, ffn_b2, gmlp_w_in, gmlp_b_in, gmlp_ln_g, gmlp_ln_b, gmlp_w_s,
           gmlp_b_s, gmlp_w_out, gmlp_b_out, gla_w_in, gla_w_gate2, gla_b_gate, gla_norm_g,
           gla_w_out, final_norm_g):
    n_seq_p, seq_p, d = x_prompt.shape
    n_seq_s, seq_s, _ = x_sample.shape
    assert d == D_MODEL and seq_p % PROMPT_TILE == 0 and PROMPT_TILE % CHUNK_A == 0
    assert seq_s < SUBLANES and SUBLANES % seq_s == 0 and (n_seq_s * seq_s) % SAMPLE_TILE == 0
    row = lambda a: a.reshape(1, -1)

    c_all = jnp.concatenate([c_prompt, jnp.repeat(c_sample, seq_s, axis=0)], axis=0)
    mod_p, mod_s = _ada_call(c_all, ada_w, ada_b, n_seq_p)
    mod_p = mod_p.reshape(mod_p.shape[0], n_seq_p, 1, N_MOD * d)

    causal = jnp.tril(jnp.ones((CHUNK_A, CHUNK_A), dtype=bool))
    ws = jnp.where(causal[None], gmlp_w_s, jnp.zeros_like(gmlp_w_s))
    reps = CHUNK_A // seq_s
    eye = jnp.eye(reps, dtype=F32)
    ws_s = jnp.einsum("ab,gij->gaibj", eye, ws[:, :seq_s, :seq_s]).reshape(
        GMLP_GROUPS, CHUNK_A, CHUNK_A)
    bias_p = jnp.repeat(gmlp_b_s.T, GMLP_GROUP_W, axis=1)
    bias_s = jnp.tile(bias_p[:seq_s], (reps, 1))

    gmlp_w = dict(norm_g=row(norm_mix_g[0]), w_in=gmlp_w_in.astype(BF16), b_in=row(gmlp_b_in),
                  ln_g=row(gmlp_ln_g), ln_b=row(gmlp_ln_b), w_out=gmlp_w_out.astype(BF16),
                  b_out=row(gmlp_b_out))
    gmlp_w_p = dict(gmlp_w, mix=ws.astype(BF16), bias_s=bias_p)
    gmlp_w_s = dict(gmlp_w, mix=ws_s.astype(BF16), bias_s=bias_s)

    depth = ffn_w1.shape[0]
    ffn_w = dict(norm_g=norm_ffn_g.reshape(depth, 1, d), w1=ffn_w1.astype(BF16),
                 b1=ffn_b1.reshape(depth, 1, D_FF), w2=ffn_w2.astype(BF16),
                 b2=ffn_b2.reshape(depth, 1, d), final_g=row(final_norm_g))

    w_in_pad = jnp.pad(gla_w_in, ((0, 0), (0, GLA_IN_PAD - gla_w_in.shape[1]))).astype(BF16)
    wg2_pad = jnp.pad(gla_w_gate2, ((0, GLA_GATE_PAD - GLA_GATE_RANK), (0, 0))).astype(BF16)
    blk_ones = jnp.kron(jnp.eye(2, dtype=F32), jnp.ones((GLA_DK_HEAD, GLA_DK_HEAD), F32))
    gla_w = dict(norm_g=row(norm_mix_g[1]), w_in=w_in_pad, w_gate2=wg2_pad, b_gate=row(gla_b_gate),
                 out_g=row(gla_norm_g), w_out=gla_w_out.astype(BF16),
                 tril=jnp.tril(jnp.ones((GLA_TRIL, GLA_TRIL), F32)).astype(BF16),
                 ones_bd=blk_ones.astype(BF16))

    xp = x_prompt.reshape(n_seq_p * seq_p, d)
    xp, = _gmlp_call(xp, mod_p, 0, seq_p, PROMPT_TILE, gmlp_w_p, want_v=False)
    xp = _ffn_call(xp, mod_p, 0, seq_p, PROMPT_TILE, ffn_w, final_norm=False)
    xp, state_p = _gla_prompt_call(xp, mod_p, 1, n_seq_p, seq_p, PROMPT_TILE, gla_w)
    xp = _ffn_call(xp, mod_p, 1, seq_p, PROMPT_TILE, ffn_w, final_norm=True)

    xs = x_sample.reshape(n_seq_s * seq_s, d)
    xs, chunk_v = _gmlp_call(xs, mod_s, 0, None, SAMPLE_TILE, gmlp_w_s, want_v=True)
    xs = _ffn_call(xs, mod_s, 0, None, SAMPLE_TILE, ffn_w, final_norm=False)
    proj, log_a = _gla_proj_call(xs, mod_s, 1, SAMPLE_TILE, gla_w)
    o, state_s = _gla_step_call(proj, log_a, state_gla, seq_s, SAMPLE_SEQS_PER_STEP)
    xs = _gla_out_call(o, proj, xs, mod_s, 1, SAMPLE_TILE, gla_w)
    xs = _ffn_call(xs, mod_s, 1, None, SAMPLE_TILE, ffn_w, final_norm=True)

    return (xp.reshape(x_prompt.shape), xs.reshape(x_sample.shape), state_p, state_s,
            chunk_v.reshape(n_seq_s, seq_s, GMLP_WIDTH))
```

```python
import functools

import jax
import jax.numpy as jnp
from jax import lax
from jax.experimental import pallas as pl
from jax.experimental.pallas import tpu as pltpu

F32 = jnp.float32
BF16 = jnp.bfloat16

D_MODEL = 1024
N_MOD = 6
CHUNK_A = 128
GMLP_WIDTH = D_MODEL
GMLP_GROUPS = 4
GMLP_GROUP_W = GMLP_WIDTH // GMLP_GROUPS
GLA_HEADS = 4
GLA_DK = D_MODEL // 2
GLA_DV = D_MODEL
GLA_DK_HEAD = GLA_DK // GLA_HEADS
GLA_DV_HEAD = GLA_DV // GLA_HEADS
GLA_GATE_RANK = 16
GLA_TAU = 16.0
GLA_CHUNK = 64
GLA_FAST_CHUNK = 128
GLA_TRIL = max(GLA_CHUNK, GLA_FAST_CHUNK)
GLA_SAFE_EXPONENT = 80.0
GLA_QKVG = 2 * GLA_DK + 2 * GLA_DV
GLA_GATE_PAD = 128
GLA_IN_PAD = GLA_QKVG + GLA_GATE_PAD
D_FF = 4 * D_MODEL
EPS = 1e-6

SUBLANES = 8
FFN_CHUNK = 1024
GMLP_SUBTILE = 256
GLA_SUBTILE = 512
VMEM_LIMIT = 56 * 1024 * 1024


def _rms(x, g):
    return x * lax.rsqrt(jnp.mean(x * x, axis=-1, keepdims=True) + EPS) * g


def _modulated_norm(x, g, scale, shift):
    inv = lax.rsqrt(jnp.mean(x * x, axis=-1, keepdims=True) + EPS)
    return (x * inv) * (g * (1.0 + scale)) + shift


def _dot(a, b):
    return jnp.dot(a.astype(BF16), b.astype(BF16), preferred_element_type=F32)


def _dot_nt(a, b):
    return lax.dot_general(a.astype(BF16), b.astype(BF16), (((1,), (1,)), ((), ())),
                           preferred_element_type=F32)


def _dot_tn(a, b):
    return lax.dot_general(a.astype(BF16), b.astype(BF16), (((0,), (0,)), ((), ())),
                           preferred_element_type=F32)


def _gelu_tanh(x):
    c1 = -2.0 * 0.7978845608028654 * 1.4426950408889634
    c2 = c1 * 0.044715
    return x / (1.0 + jnp.exp2(x * (c1 + c2 * (x * x))))


def _silu(x):
    return x * jax.nn.sigmoid(x)


def _log_sigmoid(x):
    return -(jnp.maximum(-x, 0.0) + jnp.log1p(jnp.exp(-jnp.abs(x))))


def _ada_kernel(c_ref, w_ref, b_ref, op_ref, os_ref, *, n_prompt):
    s = _silu(c_ref[...])
    y = _dot(s, w_ref[...]) + b_ref[...]
    op_ref[...] = y[:n_prompt]
    os_ref[...] = y[n_prompt:]


def _ada_call(c_all, ada_w, ada_b, n_prompt):
    depth = ada_w.shape[0]
    rows = c_all.shape[0]
    d = D_MODEL
    return pl.pallas_call(
        functools.partial(_ada_kernel, n_prompt=n_prompt),
        grid=(depth, N_MOD),
        in_specs=[
            pl.BlockSpec((rows, d), lambda l, j: (0, 0)),
            pl.BlockSpec((None, d, d), lambda l, j: (l, 0, j)),
            pl.BlockSpec((None, 1, d), lambda l, j: (l, 0, j)),
        ],
        out_specs=[
            pl.BlockSpec((None, n_prompt, d), lambda l, j: (l, 0, j)),
            pl.BlockSpec((None, rows - n_prompt, d), lambda l, j: (l, 0, j)),
        ],
        out_shape=[
            jax.ShapeDtypeStruct((depth, n_prompt, N_MOD * d), F32),
            jax.ShapeDtypeStruct((depth, rows - n_prompt, N_MOD * d), F32),
        ],
        compiler_params=pltpu.CompilerParams(
            dimension_semantics=("arbitrary", "arbitrary"), vmem_limit_bytes=VMEM_LIMIT),
        name="adaln_mod",
    )(c_all, ada_w, ada_b.reshape(depth, 1, N_MOD * d))


def _const_spec(shape):
    zeros = (0,) * len(shape)
    return pl.BlockSpec(shape, lambda i: zeros, pipeline_mode=pl.Buffered(1))


def _layer_spec(shape, layer):
    zeros = (0,) * len(shape)
    return pl.BlockSpec((None,) + shape, lambda i: (layer,) + zeros,
                        pipeline_mode=pl.Buffered(1))


def _mod_specs(mod, layer, first_chunk, tm, rows_per_seq):
    d = D_MODEL
    specs = []
    for j in range(first_chunk, first_chunk + 3):
        if rows_per_seq is None:
            specs.append(pl.BlockSpec((None, tm, d), lambda i, j=j: (layer, i, j)))
        else:
            tiles = rows_per_seq // tm
            specs.append(pl.BlockSpec((None, None, 1, d),
                                      lambda i, j=j, tiles=tiles: (layer, i // tiles, 0, j)))
    return specs


def _mod_rows(ref, rows):
    return ref[...] if ref.shape[0] == 1 else ref[rows, :]


def _row_spec(tm, width):
    return pl.BlockSpec((tm, width), lambda i: (i, 0))


def _gmlp_kernel(x_ref, sh_ref, sc_ref, gt_ref, ng_ref, win_ref, bin_ref, lng_ref, lnb_ref,
                 mix_ref, bs_ref, wout_ref, bout_ref, xo_ref, *rest, tm, sub):
    *v_refs, z_scr = rest
    groups = [slice(r0, r0 + sub) for r0 in range(0, tm, sub)]

    def project(gi):
        rows = groups[gi]
        h = _modulated_norm(x_ref[rows, :], ng_ref[...], _mod_rows(sc_ref, rows),
                            _mod_rows(sh_ref, rows))
        z_scr[gi % 2] = _dot(h, win_ref[...])

    project(0)
    for gi, rows in enumerate(groups):
        if gi + 1 < len(groups):
            project(gi + 1)
        z = _gelu_tanh(z_scr[gi % 2] + bin_ref[...])
        u = z[:, :GMLP_WIDTH]
        v = z[:, GMLP_WIDTH:]
        mu = jnp.mean(v, axis=-1, keepdims=True)
        vc = v - mu
        var = jnp.mean(vc * vc, axis=-1, keepdims=True)
        v = vc * lax.rsqrt(var + EPS) * lng_ref[...] + lnb_ref[...]
        if v_refs:
            v_refs[0][rows, :] = v
        vb = v.astype(BF16)
        mixed = []
        for c in range(sub // CHUNK_A):
            cols = []
            for g in range(GMLP_GROUPS):
                blk = vb[c * CHUNK_A:(c + 1) * CHUNK_A, g * GMLP_GROUP_W:(g + 1) * GMLP_GROUP_W]
                cols.append(jnp.dot(mix_ref[g], blk, preferred_element_type=F32))
            mixed.append(jnp.concatenate(cols, axis=1) + bs_ref[...])
        s = jnp.concatenate(mixed, axis=0)
        y = _dot(u * s, wout_ref[...]) + bout_ref[...]
        xo_ref[rows, :] = x_ref[rows, :] + _mod_rows(gt_ref, rows) * y


def _gmlp_call(x, mod, layer, rows_per_seq, tm, w, want_v):
    n, d = x.shape
    out_shape = [jax.ShapeDtypeStruct((n, d), F32)]
    out_specs = [_row_spec(tm, d)]
    if want_v:
        out_shape.append(jax.ShapeDtypeStruct((n, GMLP_WIDTH), F32))
        out_specs.append(_row_spec(tm, GMLP_WIDTH))
    in_specs = [_row_spec(tm, d)] + _mod_specs(mod, layer, 0, tm, rows_per_seq) + [
        _const_spec((1, d)),
        _const_spec((d, 2 * GMLP_WIDTH)),
        _const_spec((1, 2 * GMLP_WIDTH)),
        _const_spec((1, GMLP_WIDTH)),
        _const_spec((1, GMLP_WIDTH)),
        _const_spec((GMLP_GROUPS, CHUNK_A, CHUNK_A)),
        _const_spec((CHUNK_A, GMLP_WIDTH)),
        _const_spec((GMLP_WIDTH, d)),
        _const_spec((1, d)),
    ]
    return pl.pallas_call(
        functools.partial(_gmlp_kernel, tm=tm, sub=GMLP_SUBTILE),
        grid=(n // tm,),
        in_specs=in_specs,
        out_specs=out_specs,
        out_shape=out_shape,
        scratch_shapes=[pltpu.VMEM((2, GMLP_SUBTILE, 2 * GMLP_WIDTH), F32)],
        compiler_params=pltpu.CompilerParams(
            dimension_semantics=("arbitrary",), vmem_limit_bytes=VMEM_LIMIT),
        name="gmlp_mixer",
    )(x, mod, mod, mod, w["norm_g"], w["w_in"], w["b_in"], w["ln_g"], w["ln_b"],
      w["mix"], w["bias_s"], w["w_out"], w["b_out"])


def _ffn_kernel(x_ref, sh_ref, sc_ref, gt_ref, ng_ref, w1_ref, b1_ref, w2_ref, b2_ref,
                fg_ref, o_ref, *, final_norm):
    x = x_ref[...]
    h = _modulated_norm(x, ng_ref[...], sc_ref[...], sh_ref[...]).astype(BF16)
    acc = jnp.zeros(x.shape, F32)
    for j in range(D_FF // FFN_CHUNK):
        cols = slice(j * FFN_CHUNK, (j + 1) * FFN_CHUNK)
        a = jnp.dot(h, w1_ref[:, cols], preferred_element_type=F32) + b1_ref[:, cols]
        r = jnp.square(jnp.maximum(a, 0.0)).astype(BF16)
        acc = acc + jnp.dot(r, w2_ref[cols, :], preferred_element_type=F32)
    y = x + gt_ref[...] * (acc + b2_ref[...])
    if final_norm:
        y = _rms(y, fg_ref[...])
    o_ref[...] = y


def _ffn_call(x, mod, layer, rows_per_seq, tm, w, final_norm):
    n, d = x.shape
    in_specs = [_row_spec(tm, d)] + _mod_specs(mod, layer, 3, tm, rows_per_seq) + [
        _layer_spec((1, d), layer),
        _layer_spec((d, D_FF), layer),
        _layer_spec((1, D_FF), layer),
        _layer_spec((D_FF, d), layer),
        _layer_spec((1, d), layer),
        _const_spec((1, d)),
    ]
    return pl.pallas_call(
        functools.partial(_ffn_kernel, final_norm=final_norm),
        grid=(n // tm,),
        in_specs=in_specs,
        out_specs=_row_spec(tm, d),
        out_shape=jax.ShapeDtypeStruct((n, d), F32),
        compiler_params=pltpu.CompilerParams(
            dimension_semantics=("arbitrary",), vmem_limit_bytes=VMEM_LIMIT),
        name="sqrelu_mlp",
    )(x, mod, mod, mod, w["norm_g"], w["w1"], w["b1"], w["w2"], w["b2"], w["final_g"])


def _gla_project(x, ng, scale, shift, win, wg2, bg):
    h = _modulated_norm(x, ng, scale, shift)
    proj = _dot(h, win)
    gate = _dot(proj[:, GLA_QKVG:], wg2) + bg
    log_a = _log_sigmoid(gate) * (1.0 / GLA_TAU)
    return proj[:, :GLA_QKVG], log_a


def _gla_output(o, gate_in, x, gt, og, wout):
    parts = []
    for h in range(GLA_HEADS):
        cols = slice(h * GLA_DV_HEAD, (h + 1) * GLA_DV_HEAD)
        parts.append(_rms(o[:, cols], og))
    on = jnp.concatenate(parts, axis=1)
    y = _dot(on * _silu(gate_in), wout)
    return x + gt * y


def _as_column(row):
    return jnp.broadcast_to(row, (SUBLANES, row.shape[1])).T[:, 0:1]


def _chunk_cumsum(tril, log_a):
    head = log_a.astype(BF16)
    rest = (log_a - head.astype(F32)).astype(BF16)
    return (jnp.dot(tril, head, preferred_element_type=F32)
            + jnp.dot(tril, rest, preferred_element_type=F32))


def _level_masks(c):
    rr = lax.broadcasted_iota(jnp.int32, (c, c), 0)
    cc = lax.broadcasted_iota(jnp.int32, (c, c), 1)
    masks = []
    s = c
    while s > SUBLANES:
        half = s // 2
        same = (rr ^ cc) < s
        masks.append(same & ((rr & half) != 0) & ((cc & half) == 0))
        s = half
    return masks, rr, cc


def _block_rows(x, s, r):
    c, w = x.shape
    return jnp.concatenate(
        [jnp.broadcast_to(x[p * s + r:p * s + r + 1, :], (s, w)) for p in range(c // s)], axis=0)


def _diag_terms(q, k, a):
    c, w = q.shape
    ii = lax.broadcasted_iota(jnp.int32, (c, w), 0) & (SUBLANES - 1)
    decay = jnp.zeros((c, w), F32)
    terms = [None] * SUBLANES
    for j in range(SUBLANES - 1, -1, -1):
        if j < SUBLANES - 1:
            decay = decay * _block_rows(a, SUBLANES, j + 1)
        decay = jnp.where(ii == j, 1.0, decay)
        terms[j] = q * _block_rows(k, SUBLANES, j) * decay
    return jnp.concatenate(terms, axis=0)


def _gla_chunk_pair(q2, k2, v2, la2, b2, st_refs, ones_bd, masks, rr, cc):
    c = q2[0].shape[0]
    zs = []
    for q, k, la in zip(q2, k2, la2):
        zs.append(_diag_terms(q, k, jnp.exp(la)))
    sums = jnp.dot(jnp.concatenate(zs, axis=1).astype(BF16), ones_bd,
                   preferred_element_type=F32)
    outs = []
    for hh, (q, k, v, b, st_ref) in enumerate(zip(q2, k2, v2, b2, st_refs)):
        dk = q.shape[1]
        attn = jnp.zeros((c, c), F32)
        base = rr & ~(SUBLANES - 1)
        for j in range(SUBLANES):
            rj = sums[j * c:(j + 1) * c, hh * dk:hh * dk + c]
            attn = jnp.where(cc == base + j, rj, attn)
        s = c
        for mask in masks:
            half = s // 2
            e = jnp.exp(-jnp.abs(b - _block_rows(b, s, half)))
            attn = jnp.where(mask, _dot_nt(q * e, k * e), attn)
            s = half
        b_last = b[c - 1:c, :]
        st = st_ref[...]
        o = _dot(q * jnp.exp(b), st) + _dot(attn, v)
        kd = k * jnp.exp(b_last - b)
        st_ref[...] = _as_column(jnp.exp(b_last)) * st + _dot_tn(kd, v)
        outs.append(o)
    return outs


def _gla_prompt_kernel(x_ref, sh_ref, sc_ref, gt_ref, ng_ref, win_ref, wg2_ref, bg_ref,
                       og_ref, wout_ref, tril_ref, ones_ref, xo_ref, s_ref,
                       proj_scr, ga_scr, la_scr, b_scr, o_scr, st_scr, *, tm, sub, tiles_per_seq):
    tile = pl.program_id(0) % tiles_per_seq

    @pl.when(tile == 0)
    def _():
        st_scr[...] = jnp.zeros(st_scr.shape, F32)

    dk, dv = GLA_DK_HEAD, GLA_DV_HEAD
    qscale = GLA_DK_HEAD ** -0.5
    cf = GLA_FAST_CHUNK
    c = GLA_CHUNK
    groups = [slice(r0, r0 + sub) for r0 in range(0, tm, sub)]

    def project(rows):
        h = _modulated_norm(x_ref[rows, :], ng_ref[...], _mod_rows(sc_ref, rows),
                            _mod_rows(sh_ref, rows))
        proj = _dot(h, win_ref[...])
        proj_scr[rows, :] = proj[:, :GLA_QKVG]
        ga_scr[rows, :] = proj[:, GLA_QKVG:]

    def decays(rows):
        gate = _dot(ga_scr[rows, :], wg2_ref[...]) + bg_ref[...]
        log_a = _log_sigmoid(gate) * (1.0 / GLA_TAU)
        la_scr[rows, :] = log_a
        for ci in range(sub // cf):
            b_scr[rows.start + ci * cf:rows.start + (ci + 1) * cf, :] = _chunk_cumsum(
                tril_ref[0:cf, 0:cf], log_a[ci * cf:(ci + 1) * cf, :])
        return jnp.min(b_scr[rows, :]) >= -GLA_SAFE_EXPONENT

    def head_slices(rows, h):
        q = proj_scr[rows, h * dk:(h + 1) * dk] * qscale
        k = proj_scr[rows, GLA_DK + h * dk:GLA_DK + (h + 1) * dk]
        v = proj_scr[rows, 2 * GLA_DK + h * dv:2 * GLA_DK + (h + 1) * dv]
        return q, k, v

    def single_ref_rows(group):
        rr = lax.broadcasted_iota(jnp.int32, (cf, cf), 0)
        cc = lax.broadcasted_iota(jnp.int32, (cf, cf), 1)
        causal = rr >= cc
        chunks = [slice(r0, r0 + cf) for r0 in range(group.start, group.stop, cf)]
        for h in range(GLA_HEADS):
            qes, vbs, scores, kvs, decays = [], [], [], [], []
            for rows in chunks:
                q, k, v = head_slices(rows, h)
                b = b_scr[rows, h * dk:(h + 1) * dk]
                decay_last = jnp.exp(b[cf - 1:cf, :])
                qe = (q * jnp.exp(b)).astype(BF16)
                kt = k * jnp.exp(-b)
                vb = v.astype(BF16)
                scores.append(_dot_nt(qe, kt))
                kvs.append(_dot_tn(kt * decay_last, vb))
                qes.append(qe)
                vbs.append(vb)
                decays.append(_as_column(decay_last))
            st = st_scr[h]
            states = []
            for kv, decay_col in zip(kvs, decays):
                states.append(st)
                st = decay_col * st + kv
            st_scr[h] = st
            for rows, qe, vb, sc, st_in in zip(chunks, qes, vbs, scores, states):
                attn = jnp.where(causal, sc, 0.0).astype(BF16)
                lhs = jnp.concatenate([qe, attn], axis=1)
                rhs = jnp.concatenate([st_in.astype(BF16), vb], axis=0)
                o_scr[rows, h * dv:(h + 1) * dv] = jnp.dot(lhs, rhs, preferred_element_type=F32)

    def robust_rows(group):
        def robust_body(ci, carry):
            rows = pl.ds(pl.multiple_of(group.start + ci * c, c), c)
            masks, rr, cc = _level_masks(c)
            b_c = _chunk_cumsum(tril_ref[0:c, 0:c], la_scr[rows, :])
            outs = []
            for h0 in range(0, GLA_HEADS, 2):
                heads = (h0, h0 + 1)
                qkv = [head_slices(rows, h) for h in heads]
                la2 = [la_scr[rows, h * dk:(h + 1) * dk] for h in heads]
                b2 = [b_c[:, h * dk:(h + 1) * dk] for h in heads]
                st_refs = [st_scr.at[h] for h in heads]
                outs += _gla_chunk_pair([t[0] for t in qkv], [t[1] for t in qkv],
                                        [t[2] for t in qkv], la2, b2, st_refs, ones_ref[...],
                                        masks, rr, cc)
            o_scr[rows, :] = jnp.concatenate(outs, axis=1)
            return carry

        lax.fori_loop(0, sub // c, robust_body, 0)

    def output(rows):
        gate_in = proj_scr[rows, 2 * GLA_DK + GLA_DV:]
        xo_ref[rows, :] = _gla_output(o_scr[rows, :], gate_in, x_ref[rows, :],
                                      _mod_rows(gt_ref, rows), og_ref[...], wout_ref[...])

    project(groups[0])
    for gi, rows in enumerate(groups):
        single_ref_safe = decays(rows)
        following = groups[gi + 1] if gi + 1 < len(groups) else None

        @pl.when(single_ref_safe)
        def _():
            if following is not None:
                project(following)
            single_ref_rows(rows)
            output(rows)

        @pl.when(jnp.logical_not(single_ref_safe))
        def _():
            if following is not None:
                project(following)
            robust_rows(rows)
            output(rows)

    @pl.when(tile == tiles_per_seq - 1)
    def _():
        s_ref[...] = st_scr[...]


def _gla_prompt_call(x, mod, layer, n_seq, seq_len, tm, w):
    n, d = x.shape
    tiles_per_seq = seq_len // tm
    dk, dv = GLA_DK_HEAD, GLA_DV_HEAD
    in_specs = [_row_spec(tm, d)] + _mod_specs(mod, layer, 0, tm, seq_len) + [
        _const_spec((1, d)),
        _const_spec((d, GLA_IN_PAD)),
        _const_spec((GLA_GATE_PAD, GLA_DK)),
        _const_spec((1, GLA_DK)),
        _const_spec((1, dv)),
        _const_spec((GLA_DV, d)),
        _const_spec((GLA_TRIL, GLA_TRIL)),
        _const_spec((2 * dk, 2 * dk)),
    ]
    out_specs = [
        _row_spec(tm, d),
        pl.BlockSpec((None, GLA_HEADS, dk, dv), lambda i: (i // tiles_per_seq, 0, 0, 0)),
    ]
    out_shape = [
        jax.ShapeDtypeStruct((n, d), F32),
        jax.ShapeDtypeStruct((n_seq, GLA_HEADS, dk, dv), F32),
    ]
    return pl.pallas_call(
        functools.partial(_gla_prompt_kernel, tm=tm, sub=GLA_SUBTILE,
                          tiles_per_seq=tiles_per_seq),
        grid=(n // tm,),
        in_specs=in_specs,
        out_specs=out_specs,
        out_shape=out_shape,
        scratch_shapes=[
            pltpu.VMEM((tm, GLA_QKVG), F32),
            pltpu.VMEM((tm, GLA_GATE_PAD), F32),
            pltpu.VMEM((tm, GLA_DK), F32),
            pltpu.VMEM((tm, GLA_DK), F32),
            pltpu.VMEM((tm, GLA_DV), F32),
            pltpu.VMEM((GLA_HEADS, dk, dv), F32),
        ],
        compiler_params=pltpu.CompilerParams(
            dimension_semantics=("arbitrary",), vmem_limit_bytes=VMEM_LIMIT),
        name="gla_mixer_prompt",
    )(x, mod, mod, mod, w["norm_g"], w["w_in"], w["w_gate2"], w["b_gate"], w["out_g"],
      w["w_out"], w["tril"], w["ones_bd"])


def _gla_proj_kernel(x_ref, sh_ref, sc_ref, ng_ref, win_ref, wg2_ref, bg_ref, p_ref, la_ref):
    proj, log_a = _gla_project(x_ref[...], ng_ref[...], sc_ref[...], sh_ref[...],
                               win_ref[...], wg2_ref[...], bg_ref[...])
    p_ref[...] = proj
    la_ref[...] = log_a


def _gla_proj_call(x, mod, layer, tm, w):
    n, d = x.shape
    specs = _mod_specs(mod, layer, 0, tm, None)
    in_specs = [_row_spec(tm, d), specs[0], specs[1],
                _const_spec((1, d)),
                _const_spec((d, GLA_IN_PAD)),
                _const_spec((GLA_GATE_PAD, GLA_DK)),
                _const_spec((1, GLA_DK))]
    return pl.pallas_call(
        _gla_proj_kernel,
        grid=(n // tm,),
        in_specs=in_specs,
        out_specs=[_row_spec(tm, GLA_QKVG), _row_spec(tm, GLA_DK)],
        out_shape=[jax.ShapeDtypeStruct((n, GLA_QKVG), F32),
                   jax.ShapeDtypeStruct((n, GLA_DK), F32)],
        compiler_params=pltpu.CompilerParams(
            dimension_semantics=("arbitrary",), vmem_limit_bytes=VMEM_LIMIT),
        name="gla_proj_sample",
    )(x, mod, mod, w["norm_g"], w["w_in"], w["w_gate2"], w["b_gate"])


def _gla_step_kernel(p_ref, la_ref, s0_ref, o_ref, s1_ref, *, seqs, steps):
    dk, dv = GLA_DK_HEAD, GLA_DV_HEAD
    qscale = GLA_DK_HEAD ** -0.5
    per_tile = SUBLANES // steps
    row = lax.broadcasted_iota(jnp.int32, (SUBLANES, dk), 0)

    def seq_rows(tile, s):
        if s:
            tile = pltpu.roll(tile, SUBLANES - s * steps, 0)
        keep = lax.broadcasted_iota(jnp.int32, tile.shape, 0) < steps
        return jnp.where(keep, tile, 0.0)

    def tile_body(ti, carry):
        rows = pl.ds(pl.multiple_of(ti * SUBLANES, SUBLANES), SUBLANES)
        p_tile = p_ref[rows, :]
        la_tile = la_ref[rows, :]
        o_tile = jnp.zeros((SUBLANES, GLA_DV), F32)
        for s in range(per_tile):
            bi = ti * per_tile + s
            p = seq_rows(p_tile, s)
            la_all = seq_rows(la_tile, s)
            outs = []
            for h in range(GLA_HEADS):
                q = p[:, h * dk:(h + 1) * dk] * qscale
                k = p[:, GLA_DK + h * dk:GLA_DK + (h + 1) * dk]
                v = p[:, 2 * GLA_DK + h * dv:2 * GLA_DK + (h + 1) * dv]
                la = la_all[:, h * dk:(h + 1) * dk]
                b = jnp.zeros((SUBLANES, dk), F32)
                for t in range(steps):
                    b = b + jnp.where(row >= t, la[t:t + 1, :], 0.0)
                b_last = b[steps - 1:steps, :]
                s0 = s0_ref[bi, h]
                o = _dot(q * jnp.exp(b), s0)
                for j in range(steps):
                    e = jnp.exp(jnp.minimum(b - b[j:j + 1, :], 0.0))
                    z = jnp.where(row >= j, q * k[j:j + 1, :] * e, 0.0)
                    o = o + jnp.sum(z, axis=-1, keepdims=True) * v[j:j + 1, :]
                outs.append(o)
                m = jnp.where(row == steps, jnp.exp(b_last), k * jnp.exp(b_last - b))
                m_t = m.T
                s1_ref[bi, h] = m_t[:, steps:steps + 1] * s0 + _dot(m_t, v)
            o_seq = jnp.concatenate(outs, axis=1)
            o_tile = o_tile + (pltpu.roll(o_seq, s * steps, 0) if s else o_seq)
        o_ref[rows, :] = o_tile
        return carry

    lax.fori_loop(0, seqs // per_tile, tile_body, 0)


def _gla_step_call(proj, log_a, state, steps, seqs):
    n_seq = state.shape[0]
    dk, dv = GLA_DK_HEAD, GLA_DV_HEAD
    rows = lambda width: pl.BlockSpec((seqs * steps, width), lambda i: (i, 0))
    st_spec = pl.BlockSpec((seqs, GLA_HEADS, dk, dv), lambda i: (i, 0, 0, 0))
    return pl.pallas_call(
        functools.partial(_gla_step_kernel, seqs=seqs, steps=steps),
        grid=(n_seq // seqs,),
        in_specs=[rows(GLA_QKVG), rows(GLA_DK), st_spec],
        out_specs=[rows(GLA_DV), st_spec],
        out_shape=[jax.ShapeDtypeStruct((n_seq * steps, GLA_DV), F32),
                   jax.ShapeDtypeStruct(state.shape, F32)],
        compiler_params=pltpu.CompilerParams(
            dimension_semantics=("arbitrary",), vmem_limit_bytes=VMEM_LIMIT),
        name="gla_step_sample",
    )(proj, log_a, state)


def _gla_out_kernel(o_ref, p_ref, x_ref, gt_ref, og_ref, wout_ref, xo_ref):
    xo_ref[...] = _gla_output(o_ref[...], p_ref[...], x_ref[...], gt_ref[...],
                              og_ref[...], wout_ref[...])


def _gla_out_call(o, proj, x, mod, layer, tm, w):
    n, d = x.shape
    gate_spec = _mod_specs(mod, layer, 0, tm, None)[2]
    g_block = (2 * GLA_DK + GLA_DV) // GLA_DV
    in_specs = [_row_spec(tm, GLA_DV),
                pl.BlockSpec((tm, GLA_DV), lambda i: (i, g_block)),
                _row_spec(tm, d), gate_spec,
                _const_spec((1, GLA_DV_HEAD)),
                _const_spec((GLA_DV, d))]
    return pl.pallas_call(
        _gla_out_kernel,
        grid=(n // tm,),
        in_specs=in_specs,
        out_specs=_row_spec(tm, d),
        out_shape=jax.ShapeDtypeStruct((n, d), F32),
        compiler_params=pltpu.CompilerParams(
            dimension_semantics=("arbitrary",), vmem_limit_bytes=VMEM_LIMIT),
        name="gla_out_sample",
    )(o, proj, x, mod, w["out_g"], w["w_out"])


PROMPT_TILE = 512
PROMPT_GMLP_TILE = 1024
PROMPT_GLA_TILE = 1024
SAMPLE_TILE = 512
SAMPLE_SEQS_PER_STEP = 8


def kernel(x_prompt, x_sample, c_prompt, c_sample, state_gla, ada_w, ada_b, norm_mix_g, norm_ffn_g,
           ffn_w1, ffn_b1, ffn_w2, ffn_b2, gmlp_w_in, gmlp_b_in, gmlp_ln_g, gmlp_ln_b, gmlp_w_s,
           gmlp_b_s, gmlp_w_out, gmlp_b_out, gla_w_in, gla_w_gate2, gla_b_gate, gla_norm_g,
           gla_w_out, final_norm_g):
    n_seq_p, seq_p, d = x_prompt.shape
    n_seq_s, seq_s, _ = x_sample.shape
    assert d == D_MODEL and seq_p % PROMPT_TILE == 0 and PROMPT_TILE % CHUNK_A == 0
    assert seq_s < SUBLANES and SUBLANES % seq_s == 0 and (n_seq_s * seq_s) % SAMPLE_TILE == 0
    row = lambda a: a.reshape(1, -1)

    c_all = jnp.concatenate([c_prompt, jnp.repeat(c_sample, seq_s, axis=0)], axis=0)
    mod_p, mod_s = _ada_call(c_all, ada_w, ada_b, n_seq_p)
    mod_p = mod_p.reshape(mod_p.shape[0], n_seq_p, 1, N_MOD * d)

    causal = jnp.tril(jnp.ones((CHUNK_A, CHUNK_A), dtype=bool))
    ws = jnp.where(causal[None], gmlp_w_s, jnp.zeros_like(gmlp_w_s))
    reps = CHUNK_A // seq_s
    eye = jnp.eye(reps, dtype=F32)
    ws_s = jnp.einsum("ab,gij->gaibj", eye, ws[:, :seq_s, :seq_s]).reshape(
        GMLP_GROUPS, CHUNK_A, CHUNK_A)
    bias_p = jnp.repeat(gmlp_b_s.T, GMLP_GROUP_W, axis=1)
    bias_s = jnp.tile(bias_p[:seq_s], (reps, 1))

    gmlp_w = dict(norm_g=row(norm_mix_g[0]), w_in=gmlp_w_in.astype(BF16), b_in=row(gmlp_b_in),
                  ln_g=row(gmlp_ln_g), ln_b=row(gmlp_ln_b), w_out=gmlp_w_out.astype(BF16),
                  b_out=row(gmlp_b_out))
    gmlp_w_p = dict(gmlp_w, mix=ws.astype(BF16), bias_s=bias_p)
    gmlp_w_s = dict(gmlp_w, mix=ws_s.astype(BF16), bias_s=bias_s)

    depth = ffn_w1.shape[0]
    ffn_w = dict(norm_g=norm_ffn_g.reshape(depth, 1, d), w1=ffn_w1.astype(BF16),
                 b1=ffn_b1.reshape(depth, 1, D_FF), w2=ffn_w2.astype(BF16),
                 b2=ffn_b2.reshape(depth, 1, d), final_g=row(final_norm_g))

    w_in_pad = jnp.pad(gla_w_in, ((0, 0), (0, GLA_IN_PAD - gla_w_in.shape[1]))).astype(BF16)
    wg2_pad = jnp.pad(gla_w_gate2, ((0, GLA_GATE_PAD - GLA_GATE_RANK), (0, 0))).astype(BF16)
    blk_ones = jnp.kron(jnp.eye(2, dtype=F32), jnp.ones((GLA_DK_HEAD, GLA_DK_HEAD), F32))
    gla_w = dict(norm_g=row(norm_mix_g[1]), w_in=w_in_pad, w_gate2=wg2_pad, b_gate=row(gla_b_gate),
                 out_g=row(gla_norm_g), w_out=gla_w_out.astype(BF16),
                 tril=jnp.tril(jnp.ones((GLA_TRIL, GLA_TRIL), F32)).astype(BF16),
                 ones_bd=blk_ones.astype(BF16))

    xp = x_prompt.reshape(n_seq_p * seq_p, d)
    xp, = _gmlp_call(xp, mod_p, 0, seq_p, PROMPT_GMLP_TILE, gmlp_w_p, want_v=False)
    xp = _ffn_call(xp, mod_p, 0, seq_p, PROMPT_TILE, ffn_w, final_norm=False)
    xp, state_p = _gla_prompt_call(xp, mod_p, 1, n_seq_p, seq_p, PROMPT_GLA_TILE, gla_w)
    xp = _ffn_call(xp, mod_p, 1, seq_p, PROMPT_TILE, ffn_w, final_norm=True)

    xs = x_sample.reshape(n_seq_s * seq_s, d)
    xs, chunk_v = _gmlp_call(xs, mod_s, 0, None, SAMPLE_TILE, gmlp_w_s, want_v=True)
    xs = _ffn_call(xs, mod_s, 0, None, SAMPLE_TILE, ffn_w, final_norm=False)
    proj, log_a = _gla_proj_call(xs, mod_s, 1, SAMPLE_TILE, gla_w)
    o, state_s = _gla_step_call(proj, log_a, state_gla, seq_s, SAMPLE_SEQS_PER_STEP)
    xs = _gla_out_call(o, proj, xs, mod_s, 1, SAMPLE_TILE, gla_w)
    xs = _ffn_call(xs, mod_s, 1, None, SAMPLE_TILE, ffn_w, final_norm=True)

    return (xp.reshape(x_prompt.shape), xs.reshape(x_sample.shape), state_p, state_s,
            chunk_v.reshape(n_seq_s, seq_s, GMLP_WIDTH))
```

```python
import functools

import jax
import jax.numpy as jnp
from jax import lax
from jax.experimental import pallas as pl
from jax.experimental.pallas import tpu as pltpu

F32 = jnp.float32
BF16 = jnp.bfloat16

D_MODEL = 1024
N_MOD = 6
CHUNK_A = 128
GMLP_WIDTH = D_MODEL
GMLP_GROUPS = 4
GMLP_GROUP_W = GMLP_WIDTH // GMLP_GROUPS
GLA_HEADS = 4
GLA_DK = D_MODEL // 2
GLA_DV = D_MODEL
GLA_DK_HEAD = GLA_DK // GLA_HEADS
GLA_DV_HEAD = GLA_DV // GLA_HEADS
GLA_GATE_RANK = 16
GLA_TAU = 16.0
GLA_CHUNK = 64
GLA_FAST_CHUNK = 128
GLA_TRIL = max(GLA_CHUNK, GLA_FAST_CHUNK)
GLA_SAFE_EXPONENT = 80.0
GLA_QKVG = 2 * GLA_DK + 2 * GLA_DV
GLA_GATE_PAD = 128
GLA_IN_PAD = GLA_QKVG + GLA_GATE_PAD
D_FF = 4 * D_MODEL
EPS = 1e-6

SUBLANES = 8
BF16_SUBLANES = 16
FFN_CHUNK = 1024
FFN_SUBTILE = 512
GMLP_SUBTILE = 256
GLA_SUBTILE = 512
VMEM_LIMIT = 56 * 1024 * 1024


def _rms(x, g):
    return x * lax.rsqrt(jnp.mean(x * x, axis=-1, keepdims=True) + EPS) * g


def _modulated_norm(x, g, scale, shift):
    inv = lax.rsqrt(jnp.mean(x * x, axis=-1, keepdims=True) + EPS)
    return (x * inv) * (g * (1.0 + scale)) + shift


def _dot(a, b):
    return jnp.dot(a.astype(BF16), b.astype(BF16), preferred_element_type=F32)


def _dot_nt(a, b):
    return lax.dot_general(a.astype(BF16), b.astype(BF16), (((1,), (1,)), ((), ())),
                           preferred_element_type=F32)


def _dot_tn(a, b):
    return lax.dot_general(a.astype(BF16), b.astype(BF16), (((0,), (0,)), ((), ())),
                           preferred_element_type=F32)


def _gelu_tanh(x):
    c1 = -2.0 * 0.7978845608028654 * 1.4426950408889634
    c2 = c1 * 0.044715
    return x / (1.0 + jnp.exp2(x * (c1 + c2 * (x * x))))


def _silu(x):
    return x * jax.nn.sigmoid(x)


def _log_sigmoid(x):
    return -(jnp.maximum(-x, 0.0) + jnp.log1p(jnp.exp(-jnp.abs(x))))


def _ada_kernel(c_ref, w_ref, b_ref, op_ref, os_ref, s_scr, *, n_prompt):
    @pl.when((pl.program_id(0) == 0) & (pl.program_id(1) == 0))
    def _():
        s_scr[...] = _silu(c_ref[...]).astype(BF16)

    y = _dot(s_scr[...], w_ref[...]) + b_ref[...]
    op_ref[...] = y[:n_prompt]
    os_ref[...] = y[n_prompt:n_prompt + os_ref.shape[0]]


def _ada_call(c_prompt, c_sample_rows, ada_w, ada_b):
    depth = ada_w.shape[0]
    n_prompt, n_sample = c_prompt.shape[0], c_sample_rows.shape[0]
    d = D_MODEL
    pad = -(n_prompt + n_sample) % BF16_SUBLANES
    c_all = jnp.concatenate([c_prompt, c_sample_rows, jnp.zeros((pad, d), F32)], axis=0)
    rows = c_all.shape[0]
    return pl.pallas_call(
        functools.partial(_ada_kernel, n_prompt=n_prompt),
        grid=(depth, N_MOD),
        in_specs=[
            pl.BlockSpec((rows, d), lambda l, j: (0, 0)),
            pl.BlockSpec((None, d, d), lambda l, j: (l, 0, j)),
            pl.BlockSpec((None, 1, d), lambda l, j: (l, 0, j)),
        ],
        out_specs=[
            pl.BlockSpec((None, n_prompt, d), lambda l, j: (l, 0, j)),
            pl.BlockSpec((None, n_sample, d), lambda l, j: (l, 0, j)),
        ],
        out_shape=[
            jax.ShapeDtypeStruct((depth, n_prompt, N_MOD * d), F32),
            jax.ShapeDtypeStruct((depth, n_sample, N_MOD * d), F32),
        ],
        scratch_shapes=[pltpu.VMEM((rows, d), BF16)],
        compiler_params=pltpu.CompilerParams(
            dimension_semantics=("arbitrary", "arbitrary"), vmem_limit_bytes=VMEM_LIMIT),
        name="adaln_mod",
    )(c_all, ada_w, ada_b.reshape(depth, 1, N_MOD * d))


def _const_spec(shape):
    zeros = (0,) * len(shape)
    return pl.BlockSpec(shape, lambda i: zeros, pipeline_mode=pl.Buffered(1))


def _layer_spec(shape, layer):
    zeros = (0,) * len(shape)
    return pl.BlockSpec((None,) + shape, lambda i: (layer,) + zeros,
                        pipeline_mode=pl.Buffered(1))


def _mod_specs(mod, layer, first_chunk, tm, rows_per_seq):
    d = D_MODEL
    specs = []
    for j in range(first_chunk, first_chunk + 3):
        if rows_per_seq is None:
            specs.append(pl.BlockSpec((None, tm, d), lambda i, j=j: (layer, i, j)))
        else:
            tiles = rows_per_seq // tm
            specs.append(pl.BlockSpec((None, None, 1, d),
                                      lambda i, j=j, tiles=tiles: (layer, i // tiles, 0, j)))
    return specs


def _mod_rows(ref, rows):
    return ref[...] if ref.shape[0] == 1 else ref[rows, :]


def _row_spec(tm, width):
    return pl.BlockSpec((tm, width), lambda i: (i, 0))


class _TwoGroupGrid:
    def __init__(self, n_prompt_rows, n_sample_rows, tm_p, tm_s, seq_len):
        self.tm_p, self.tm_s = tm_p, tm_s
        self.p_steps = n_prompt_rows // tm_p
        self.s_steps = n_sample_rows // tm_s
        self.tiles_per_seq = seq_len // tm_p
        self.steps = self.p_steps + self.s_steps

    def p_idx(self, i):
        return jnp.minimum(i, self.p_steps - 1)

    def s_idx(self, i):
        return jnp.maximum(i - self.p_steps, 0)

    def rows_p(self, width):
        return pl.BlockSpec((self.tm_p, width), lambda i: (self.p_idx(i), 0))

    def rows_s(self, width, col_block=0):
        return pl.BlockSpec((self.tm_s, width), lambda i: (self.s_idx(i), col_block))

    def mods_p(self, layer, first_chunk):
        return [pl.BlockSpec((None, None, 1, D_MODEL),
                             lambda i, j=j: (layer, self.p_idx(i) // self.tiles_per_seq, 0, j))
                for j in range(first_chunk, first_chunk + 3)]

    def mods_s(self, layer, first_chunk):
        return [pl.BlockSpec((None, self.tm_s, D_MODEL),
                             lambda i, j=j: (layer, self.s_idx(i), j))
                for j in range(first_chunk, first_chunk + 3)]


def _gmlp_rows(x_ref, sh_ref, sc_ref, gt_ref, mix_ref, bs_ref, xo_ref, v_ref, z_scr, w, tm, sub):
    ng_ref, win_ref, bin_ref, lng_ref, lnb_ref, wout_ref, bout_ref = w
    groups = [slice(r0, r0 + sub) for r0 in range(0, tm, sub)]

    def project(gi):
        rows = groups[gi]
        h = _modulated_norm(x_ref[rows, :], ng_ref[...], _mod_rows(sc_ref, rows),
                            _mod_rows(sh_ref, rows))
        z_scr[gi % 2] = _dot(h, win_ref[...])

    project(0)
    for gi, rows in enumerate(groups):
        if gi + 1 < len(groups):
            project(gi + 1)
        z = _gelu_tanh(z_scr[gi % 2] + bin_ref[...])
        u = z[:, :GMLP_WIDTH]
        v = z[:, GMLP_WIDTH:]
        mu = jnp.mean(v, axis=-1, keepdims=True)
        vc = v - mu
        var = jnp.mean(vc * vc, axis=-1, keepdims=True)
        v = vc * lax.rsqrt(var + EPS) * lng_ref[...] + lnb_ref[...]
        if v_ref is not None:
            v_ref[rows, :] = v
        vb = v.astype(BF16)
        mixed = []
        for c in range(sub // CHUNK_A):
            cols = []
            for g in range(GMLP_GROUPS):
                blk = vb[c * CHUNK_A:(c + 1) * CHUNK_A, g * GMLP_GROUP_W:(g + 1) * GMLP_GROUP_W]
                cols.append(jnp.dot(mix_ref[g], blk, preferred_element_type=F32))
            mixed.append(jnp.concatenate(cols, axis=1) + bs_ref[...])
        s = jnp.concatenate(mixed, axis=0)
        y = _dot(u * s, wout_ref[...]) + bout_ref[...]
        xo_ref[rows, :] = x_ref[rows, :] + _mod_rows(gt_ref, rows) * y


def _gmlp_kernel(xp_ref, shp_ref, scp_ref, gtp_ref, xs_ref, shs_ref, scs_ref, gts_ref,
                 ng_ref, win_ref, bin_ref, lng_ref, lnb_ref, wout_ref, bout_ref,
                 mixp_ref, bsp_ref, mixs_ref, bss_ref, op_ref, os_ref, vs_ref, z_scr, *, grid):
    w = (ng_ref, win_ref, bin_ref, lng_ref, lnb_ref, wout_ref, bout_ref)
    step = pl.program_id(0)

    @pl.when(step < grid.p_steps)
    def _():
        _gmlp_rows(xp_ref, shp_ref, scp_ref, gtp_ref, mixp_ref, bsp_ref, op_ref, None, z_scr, w,
                   grid.tm_p, GMLP_SUBTILE)

    @pl.when(step >= grid.p_steps)
    def _():
        _gmlp_rows(xs_ref, shs_ref, scs_ref, gts_ref, mixs_ref, bss_ref, os_ref, vs_ref, z_scr, w,
                   grid.tm_s, GMLP_SUBTILE)


def _gmlp_call(xp, xs, mod_p, mod_s, grid, w):
    d = D_MODEL
    mix_spec = _const_spec((GMLP_GROUPS, CHUNK_A, CHUNK_A))
    bias_spec = _const_spec((CHUNK_A, GMLP_WIDTH))
    in_specs = ([grid.rows_p(d)] + grid.mods_p(0, 0) + [grid.rows_s(d)] + grid.mods_s(0, 0) + [
        _const_spec((1, d)),
        _const_spec((d, 2 * GMLP_WIDTH)),
        _const_spec((1, 2 * GMLP_WIDTH)),
        _const_spec((1, GMLP_WIDTH)),
        _const_spec((1, GMLP_WIDTH)),
        _const_spec((GMLP_WIDTH, d)),
        _const_spec((1, d)),
        mix_spec, bias_spec, mix_spec, bias_spec,
    ])
    return pl.pallas_call(
        functools.partial(_gmlp_kernel, grid=grid),
        grid=(grid.steps,),
        in_specs=in_specs,
        out_specs=[grid.rows_p(d), grid.rows_s(d), grid.rows_s(GMLP_WIDTH)],
        out_shape=[jax.ShapeDtypeStruct(xp.shape, F32), jax.ShapeDtypeStruct(xs.shape, F32),
                   jax.ShapeDtypeStruct((xs.shape[0], GMLP_WIDTH), F32)],
        scratch_shapes=[pltpu.VMEM((2, GMLP_SUBTILE, 2 * GMLP_WIDTH), F32)],
        compiler_params=pltpu.CompilerParams(
            dimension_semantics=("arbitrary",), vmem_limit_bytes=VMEM_LIMIT),
        name="gmlp_mixer",
    )(xp, mod_p, mod_p, mod_p, xs, mod_s, mod_s, mod_s, w["norm_g"], w["w_in"], w["b_in"],
      w["ln_g"], w["ln_b"], w["w_out"], w["b_out"], w["mix_p"], w["bias_p"], w["mix_s"],
      w["bias_s"])


def _ffn_rows(x_ref, sh_ref, sc_ref, gt_ref, o_ref, h_scr, w, final_norm, tm):
    ng_ref, w1_ref, b1_ref, w2_ref, b2_ref, fg_ref = w
    sub = min(FFN_SUBTILE, tm)
    groups = [slice(r0, r0 + sub) for r0 in range(0, tm, sub)]

    def normalise(gi):
        rows = groups[gi]
        h_scr[gi % 2, 0:sub, :] = _modulated_norm(
            x_ref[rows, :], ng_ref[...], _mod_rows(sc_ref, rows),
            _mod_rows(sh_ref, rows)).astype(BF16)

    normalise(0)
    for gi, rows in enumerate(groups):
        if gi + 1 < len(groups):
            normalise(gi + 1)
        h = h_scr[gi % 2, 0:sub, :]
        acc = jnp.zeros((sub, D_MODEL), F32)
        for j in range(D_FF // FFN_CHUNK):
            cols = slice(j * FFN_CHUNK, (j + 1) * FFN_CHUNK)
            a = jnp.dot(h, w1_ref[:, cols], preferred_element_type=F32) + b1_ref[:, cols]
            r = jnp.square(jnp.maximum(a, 0.0)).astype(BF16)
            acc = acc + jnp.dot(r, w2_ref[cols, :], preferred_element_type=F32)
        y = x_ref[rows, :] + _mod_rows(gt_ref, rows) * (acc + b2_ref[...])
        if final_norm:
            y = _rms(y, fg_ref[...])
        o_ref[rows, :] = y


def _ffn_kernel(xp_ref, shp_ref, scp_ref, gtp_ref, xs_ref, shs_ref, scs_ref, gts_ref,
                ng_ref, w1_ref, b1_ref, w2_ref, b2_ref, fg_ref, op_ref, os_ref, h_scr,
                *, final_norm, grid):
    w = (ng_ref, w1_ref, b1_ref, w2_ref, b2_ref, fg_ref)
    step = pl.program_id(0)

    @pl.when(step < grid.p_steps)
    def _():
        _ffn_rows(xp_ref, shp_ref, scp_ref, gtp_ref, op_ref, h_scr, w, final_norm, grid.tm_p)

    @pl.when(step >= grid.p_steps)
    def _():
        _ffn_rows(xs_ref, shs_ref, scs_ref, gts_ref, os_ref, h_scr, w, final_norm, grid.tm_s)


def _ffn_call(xp, xs, mod_p, mod_s, layer, grid, w, final_norm):
    d = D_MODEL
    in_specs = ([grid.rows_p(d)] + grid.mods_p(layer, 3) + [grid.rows_s(d)]
                + grid.mods_s(layer, 3) + [
        _layer_spec((1, d), layer),
        _layer_spec((d, D_FF), layer),
        _layer_spec((1, D_FF), layer),
        _layer_spec((D_FF, d), layer),
        _layer_spec((1, d), layer),
        _const_spec((1, d)),
    ])
    return pl.pallas_call(
        functools.partial(_ffn_kernel, final_norm=final_norm, grid=grid),
        grid=(grid.steps,),
        in_specs=in_specs,
        out_specs=[grid.rows_p(d), grid.rows_s(d)],
        out_shape=[jax.ShapeDtypeStruct(xp.shape, F32), jax.ShapeDtypeStruct(xs.shape, F32)],
        scratch_shapes=[pltpu.VMEM((2, FFN_SUBTILE, d), BF16)],
        compiler_params=pltpu.CompilerParams(
            dimension_semantics=("arbitrary",), vmem_limit_bytes=VMEM_LIMIT),
        name="sqrelu_mlp",
    )(xp, mod_p, mod_p, mod_p, xs, mod_s, mod_s, mod_s, w["norm_g"], w["w1"], w["b1"], w["w2"],
      w["b2"], w["final_g"])


def _gla_project(x, ng, scale, shift, win, wg2, bg):
    h = _modulated_norm(x, ng, scale, shift)
    proj = _dot(h, win)
    gate = _dot(proj[:, GLA_QKVG:], wg2) + bg
    log_a = _log_sigmoid(gate) * (1.0 / GLA_TAU)
    return proj[:, :GLA_QKVG], log_a


def _gla_output(o, gate_in, x, gt, og, wout):
    parts = []
    for h in range(GLA_HEADS):
        cols = slice(h * GLA_DV_HEAD, (h + 1) * GLA_DV_HEAD)
        parts.append(_rms(o[:, cols], og))
    on = jnp.concatenate(parts, axis=1)
    y = _dot(on * _silu(gate_in), wout)
    return x + gt * y


def _as_column(row):
    return jnp.broadcast_to(row, (SUBLANES, row.shape[1])).T[:, 0:1]


def _chunk_cumsum(tril, log_a):
    head = log_a.astype(BF16)
    rest = (log_a - head.astype(F32)).astype(BF16)
    return (jnp.dot(tril, head, preferred_element_type=F32)
            + jnp.dot(tril, rest, preferred_element_type=F32))


def _level_masks(c):
    rr = lax.broadcasted_iota(jnp.int32, (c, c), 0)
    cc = lax.broadcasted_iota(jnp.int32, (c, c), 1)
    masks = []
    s = c
    while s > SUBLANES:
        half = s // 2
        same = (rr ^ cc) < s
        masks.append(same & ((rr & half) != 0) & ((cc & half) == 0))
        s = half
    return masks, rr, cc


def _block_rows(x, s, r):
    c, w = x.shape
    return jnp.concatenate(
        [jnp.broadcast_to(x[p * s + r:p * s + r + 1, :], (s, w)) for p in range(c // s)], axis=0)


def _diag_terms(q, k, a):
    c, w = q.shape
    ii = lax.broadcasted_iota(jnp.int32, (c, w), 0) & (SUBLANES - 1)
    decay = jnp.zeros((c, w), F32)
    terms = [None] * SUBLANES
    for j in range(SUBLANES - 1, -1, -1):
        if j < SUBLANES - 1:
            decay = decay * _block_rows(a, SUBLANES, j + 1)
        decay = jnp.where(ii == j, 1.0, decay)
        terms[j] = q * _block_rows(k, SUBLANES, j) * decay
    return jnp.concatenate(terms, axis=0)


def _gla_chunk_pair(q2, k2, v2, la2, b2, st_refs, ones_bd, masks, rr, cc):
    c = q2[0].shape[0]
    zs = []
    for q, k, la in zip(q2, k2, la2):
        zs.append(_diag_terms(q, k, jnp.exp(la)))
    sums = jnp.dot(jnp.concatenate(zs, axis=1).astype(BF16), ones_bd,
                   preferred_element_type=F32)
    outs = []
    for hh, (q, k, v, b, st_ref) in enumerate(zip(q2, k2, v2, b2, st_refs)):
        dk = q.shape[1]
        attn = jnp.zeros((c, c), F32)
        base = rr & ~(SUBLANES - 1)
        for j in range(SUBLANES):
            rj = sums[j * c:(j + 1) * c, hh * dk:hh * dk + c]
            attn = jnp.where(cc == base + j, rj, attn)
        s = c
        for mask in masks:
            half = s // 2
            e = jnp.exp(-jnp.abs(b - _block_rows(b, s, half)))
            attn = jnp.where(mask, _dot_nt(q * e, k * e), attn)
            s = half
        b_last = b[c - 1:c, :]
        st = st_ref[...]
        o = _dot(q * jnp.exp(b), st) + _dot(attn, v)
        kd = k * jnp.exp(b_last - b)
        st_ref[...] = _as_column(jnp.exp(b_last)) * st + _dot_tn(kd, v)
        outs.append(o)
    return outs


def _gla_kernel(xp_ref, shp_ref, scp_ref, gtp_ref, xs_ref, shs_ref, scs_ref, ng_ref, win_ref,
                wg2_ref, bg_ref, og_ref, wout_ref, tril_ref, ones_ref, xo_ref, s_ref, ps_ref,
                las_ref, proj_scr, ga_scr, la_scr, b_scr, o_scr, st_scr, *, grid, sub):
    step = pl.program_id(0)

    @pl.when(step < grid.p_steps)
    def _():
        _gla_prompt_rows(xp_ref, shp_ref, scp_ref, gtp_ref, ng_ref, win_ref, wg2_ref, bg_ref,
                         og_ref, wout_ref, tril_ref, ones_ref, xo_ref, s_ref, proj_scr, ga_scr,
                         la_scr, b_scr, o_scr, st_scr, tm=grid.tm_p, sub=sub,
                         tiles_per_seq=grid.tiles_per_seq)

    @pl.when(step >= grid.p_steps)
    def _():
        proj, log_a = _gla_project(xs_ref[...], ng_ref[...], scs_ref[...], shs_ref[...],
                                   win_ref[...], wg2_ref[...], bg_ref[...])
        ps_ref[...] = proj
        las_ref[...] = log_a


def _gla_prompt_rows(x_ref, sh_ref, sc_ref, gt_ref, ng_ref, win_ref, wg2_ref, bg_ref,
                     og_ref, wout_ref, tril_ref, ones_ref, xo_ref, s_ref,
                     proj_scr, ga_scr, la_scr, b_scr, o_scr, st_scr, *, tm, sub, tiles_per_seq):
    tile = pl.program_id(0) % tiles_per_seq

    @pl.when(tile == 0)
    def _():
        st_scr[...] = jnp.zeros(st_scr.shape, F32)

    dk, dv = GLA_DK_HEAD, GLA_DV_HEAD
    qscale = GLA_DK_HEAD ** -0.5
    cf = GLA_FAST_CHUNK
    c = GLA_CHUNK
    groups = [slice(r0, r0 + sub) for r0 in range(0, tm, sub)]

    def project(rows):
        h = _modulated_norm(x_ref[rows, :], ng_ref[...], _mod_rows(sc_ref, rows),
                            _mod_rows(sh_ref, rows))
        proj = _dot(h, win_ref[...])
        proj_scr[rows, :] = proj[:, :GLA_QKVG]
        ga_scr[rows, :] = proj[:, GLA_QKVG:]

    def decays(rows):
        gate = _dot(ga_scr[rows, :], wg2_ref[...]) + bg_ref[...]
        log_a = _log_sigmoid(gate) * (1.0 / GLA_TAU)
        la_scr[...] = log_a
        for ci in range(sub // cf):
            chunk = slice(ci * cf, (ci + 1) * cf)
            b_scr[chunk, :] = _chunk_cumsum(tril_ref[0:cf, 0:cf], log_a[chunk, :])
        return jnp.min(b_scr[...]) >= -GLA_SAFE_EXPONENT

    def head_slices(rows, h):
        q = proj_scr[rows, h * dk:(h + 1) * dk] * qscale
        k = proj_scr[rows, GLA_DK + h * dk:GLA_DK + (h + 1) * dk]
        v = proj_scr[rows, 2 * GLA_DK + h * dv:2 * GLA_DK + (h + 1) * dv]
        return q, k, v

    def single_ref_rows(group):
        rr = lax.broadcasted_iota(jnp.int32, (cf, cf), 0)
        cc = lax.broadcasted_iota(jnp.int32, (cf, cf), 1)
        causal = rr >= cc
        local = [slice(r0, r0 + cf) for r0 in range(0, sub, cf)]
        chunks = [slice(group.start + r.start, group.start + r.stop) for r in local]
        for h in range(GLA_HEADS):
            qes, vbs, scores, kvs, decays = [], [], [], [], []
            for rows, lrows in zip(chunks, local):
                q, k, v = head_slices(rows, h)
                b = b_scr[lrows, h * dk:(h + 1) * dk]
                decay_last = jnp.exp(b[cf - 1:cf, :])
                qe = (q * jnp.exp(b)).astype(BF16)
                kt = k * jnp.exp(-b)
                vb = v.astype(BF16)
                scores.append(_dot_nt(qe, kt))
                kvs.append(_dot_tn(kt * decay_last, vb))
                qes.append(qe)
                vbs.append(vb)
                decays.append(_as_column(decay_last))
            st = st_scr[h]
            states = []
            for kv, decay_col in zip(kvs, decays):
                states.append(st)
                st = decay_col * st + kv
            st_scr[h] = st
            for lrows, qe, vb, sc, st_in in zip(local, qes, vbs, scores, states):
                attn = jnp.where(causal, sc, 0.0).astype(BF16)
                lhs = jnp.concatenate([qe, attn], axis=1)
                rhs = jnp.concatenate([st_in.astype(BF16), vb], axis=0)
                o_scr[lrows, h * dv:(h + 1) * dv] = jnp.dot(lhs, rhs, preferred_element_type=F32)

    def robust_rows(group):
        def robust_body(ci, carry):
            lrows = pl.ds(pl.multiple_of(ci * c, c), c)
            rows = pl.ds(pl.multiple_of(group.start + ci * c, c), c)
            masks, rr, cc = _level_masks(c)
            b_c = _chunk_cumsum(tril_ref[0:c, 0:c], la_scr[lrows, :])
            outs = []
            for h0 in range(0, GLA_HEADS, 2):
                heads = (h0, h0 + 1)
                qkv = [head_slices(rows, h) for h in heads]
                la2 = [la_scr[lrows, h * dk:(h + 1) * dk] for h in heads]
                b2 = [b_c[:, h * dk:(h + 1) * dk] for h in heads]
                st_refs = [st_scr.at[h] for h in heads]
                outs += _gla_chunk_pair([t[0] for t in qkv], [t[1] for t in qkv],
                                        [t[2] for t in qkv], la2, b2, st_refs, ones_ref[...],
                                        masks, rr, cc)
            o_scr[lrows, :] = jnp.concatenate(outs, axis=1)
            return carry

        lax.fori_loop(0, sub // c, robust_body, 0)

    def output(rows):
        gate_in = proj_scr[rows, 2 * GLA_DK + GLA_DV:]
        xo_ref[rows, :] = _gla_output(o_scr[...], gate_in, x_ref[rows, :],
                                      _mod_rows(gt_ref, rows), og_ref[...], wout_ref[...])

    project(groups[0])
    for gi, rows in enumerate(groups):
        single_ref_safe = decays(rows)
        following = groups[gi + 1] if gi + 1 < len(groups) else None

        @pl.when(single_ref_safe)
        def _():
            if following is not None:
                project(following)
            single_ref_rows(rows)
            output(rows)

        @pl.when(jnp.logical_not(single_ref_safe))
        def _():
            if following is not None:
                project(following)
            robust_rows(rows)
            output(rows)

    @pl.when(tile == tiles_per_seq - 1)
    def _():
        s_ref[...] = st_scr[...]


def _gla_call(xp, xs, mod_p, mod_s, layer, n_seq, grid, w):
    n, d = xp.shape
    tm = grid.tm_p
    dk, dv = GLA_DK_HEAD, GLA_DV_HEAD
    in_specs = [grid.rows_p(d)] + grid.mods_p(layer, 0) + [grid.rows_s(d)] + grid.mods_s(
        layer, 0)[:2] + [
        _const_spec((1, d)),
        _const_spec((d, GLA_IN_PAD)),
        _const_spec((GLA_GATE_PAD, GLA_DK)),
        _const_spec((1, GLA_DK)),
        _const_spec((1, dv)),
        _const_spec((GLA_DV, d)),
        _const_spec((GLA_TRIL, GLA_TRIL)),
        _const_spec((2 * dk, 2 * dk)),
    ]
    out_specs = [
        grid.rows_p(d),
        pl.BlockSpec((None, GLA_HEADS, dk, dv),
                     lambda i: (grid.p_idx(i) // grid.tiles_per_seq, 0, 0, 0)),
        grid.rows_s(GLA_QKVG),
        grid.rows_s(GLA_DK),
    ]
    out_shape = [
        jax.ShapeDtypeStruct((n, d), F32),
        jax.ShapeDtypeStruct((n_seq, GLA_HEADS, dk, dv), F32),
        jax.ShapeDtypeStruct((xs.shape[0], GLA_QKVG), F32),
        jax.ShapeDtypeStruct((xs.shape[0], GLA_DK), F32),
    ]
    return pl.pallas_call(
        functools.partial(_gla_kernel, grid=grid, sub=GLA_SUBTILE),
        grid=(grid.steps,),
        in_specs=in_specs,
        out_specs=out_specs,
        out_shape=out_shape,
        scratch_shapes=[
            pltpu.VMEM((tm, GLA_QKVG), F32),
            pltpu.VMEM((tm, GLA_GATE_PAD), F32),
            pltpu.VMEM((GLA_SUBTILE, GLA_DK), F32),
            pltpu.VMEM((GLA_SUBTILE, GLA_DK), F32),
            pltpu.VMEM((GLA_SUBTILE, GLA_DV), F32),
            pltpu.VMEM((GLA_HEADS, dk, dv), F32),
        ],
        compiler_params=pltpu.CompilerParams(
            dimension_semantics=("arbitrary",), vmem_limit_bytes=VMEM_LIMIT),
        name="gla_mixer",
    )(xp, mod_p, mod_p, mod_p, xs, mod_s, mod_s, w["norm_g"], w["w_in"], w["w_gate2"],
      w["b_gate"], w["out_g"], w["w_out"], w["tril"], w["ones_bd"])


def _gla_step_kernel(p_ref, la_ref, s0_ref, o_ref, s1_ref, *, seqs, steps):
    dk, dv = GLA_DK_HEAD, GLA_DV_HEAD
    qscale = GLA_DK_HEAD ** -0.5
    per_tile = SUBLANES // steps
    row = lax.broadcasted_iota(jnp.int32, (SUBLANES, dk), 0)

    def seq_rows(tile, s):
        if s:
            tile = pltpu.roll(tile, SUBLANES - s * steps, 0)
        keep = lax.broadcasted_iota(jnp.int32, tile.shape, 0) < steps
        return jnp.where(keep, tile, 0.0)

    def tile_body(ti, carry):
        rows = pl.ds(pl.multiple_of(ti * SUBLANES, SUBLANES), SUBLANES)
        p_tile = p_ref[rows, :]
        la_tile = la_ref[rows, :]
        o_tile = jnp.zeros((SUBLANES, GLA_DV), F32)
        for s in range(per_tile):
            bi = ti * per_tile + s
            p = seq_rows(p_tile, s)
            la_all = seq_rows(la_tile, s)
            outs = []
            for h in range(GLA_HEADS):
                q = p[:, h * dk:(h + 1) * dk] * qscale
                k = p[:, GLA_DK + h * dk:GLA_DK + (h + 1) * dk]
                v = p[:, 2 * GLA_DK + h * dv:2 * GLA_DK + (h + 1) * dv]
                la = la_all[:, h * dk:(h + 1) * dk]
                b = jnp.zeros((SUBLANES, dk), F32)
                for t in range(steps):
                    b = b + jnp.where(row >= t, la[t:t + 1, :], 0.0)
                b_last = b[steps - 1:steps, :]
                s0 = s0_ref[bi, h]
                o = _dot(q * jnp.exp(b), s0)
                for j in range(steps):
                    e = jnp.exp(jnp.minimum(b - b[j:j + 1, :], 0.0))
                    z = jnp.where(row >= j, q * k[j:j + 1, :] * e, 0.0)
                    o = o + jnp.sum(z, axis=-1, keepdims=True) * v[j:j + 1, :]
                outs.append(o)
                m = jnp.where(row == steps, jnp.exp(b_last), k * jnp.exp(b_last - b))
                m_t = m.T
                s1_ref[bi, h] = m_t[:, steps:steps + 1] * s0 + _dot(m_t, v)
            o_seq = jnp.concatenate(outs, axis=1)
            o_tile = o_tile + (pltpu.roll(o_seq, s * steps, 0) if s else o_seq)
        o_ref[rows, :] = o_tile
        return carry

    lax.fori_loop(0, seqs // per_tile, tile_body, 0)


def _gla_step_call(proj, log_a, state, steps, seqs):
    n_seq = state.shape[0]
    dk, dv = GLA_DK_HEAD, GLA_DV_HEAD
    rows = lambda width: pl.BlockSpec((seqs * steps, width), lambda i: (i, 0))
    st_spec = pl.BlockSpec((seqs, GLA_HEADS, dk, dv), lambda i: (i, 0, 0, 0))
    return pl.pallas_call(
        functools.partial(_gla_step_kernel, seqs=seqs, steps=steps),
        grid=(n_seq // seqs,),
        in_specs=[rows(GLA_QKVG), rows(GLA_DK), st_spec],
        out_specs=[rows(GLA_DV), st_spec],
        out_shape=[jax.ShapeDtypeStruct((n_seq * steps, GLA_DV), F32),
                   jax.ShapeDtypeStruct(state.shape, F32)],
        compiler_params=pltpu.CompilerParams(
            dimension_semantics=("arbitrary",), vmem_limit_bytes=VMEM_LIMIT),
        name="gla_step_sample",
    )(proj, log_a, state)


def _gla_out_kernel(o_ref, p_ref, x_ref, gt_ref, og_ref, wout_ref, xo_ref):
    xo_ref[...] = _gla_output(o_ref[...], p_ref[...], x_ref[...], gt_ref[...],
                              og_ref[...], wout_ref[...])


def _gla_out_call(o, proj, x, mod, layer, tm, w):
    n, d = x.shape
    gate_spec = _mod_specs(mod, layer, 0, tm, None)[2]
    g_block = (2 * GLA_DK + GLA_DV) // GLA_DV
    in_specs = [_row_spec(tm, GLA_DV),
                pl.BlockSpec((tm, GLA_DV), lambda i: (i, g_block)),
                _row_spec(tm, d), gate_spec,
                _const_spec((1, GLA_DV_HEAD)),
                _const_spec((GLA_DV, d))]
    return pl.pallas_call(
        _gla_out_kernel,
        grid=(n // tm,),
        in_specs=in_specs,
        out_specs=_row_spec(tm, d),
        out_shape=jax.ShapeDtypeStruct((n, d), F32),
        compiler_params=pltpu.CompilerParams(
            dimension_semantics=("arbitrary",), vmem_limit_bytes=VMEM_LIMIT),
        name="gla_out_sample",
    )(o, proj, x, mod, w["out_g"], w["w_out"])


PROMPT_FFN_TILE = 1024
PROMPT_GMLP_TILE = 1024
PROMPT_GLA_TILE = 1024
SAMPLE_TILE = 256
SAMPLE_GLA_TILE = 128
SAMPLE_SEQS_PER_STEP = 8


def kernel(x_prompt, x_sample, c_prompt, c_sample, state_gla, ada_w, ada_b, norm_mix_g, norm_ffn_g,
           ffn_w1, ffn_b1, ffn_w2, ffn_b2, gmlp_w_in, gmlp_b_in, gmlp_ln_g, gmlp_ln_b, gmlp_w_s,
           gmlp_b_s, gmlp_w_out, gmlp_b_out, gla_w_in, gla_w_gate2, gla_b_gate, gla_norm_g,
           gla_w_out, final_norm_g):
    n_seq_p, seq_p, d = x_prompt.shape
    n_seq_s, seq_s, _ = x_sample.shape
    assert d == D_MODEL and GMLP_SUBTILE % CHUNK_A == 0 and GLA_SUBTILE % GLA_FAST_CHUNK == 0
    assert all(seq_p % t == 0 for t in (PROMPT_FFN_TILE, PROMPT_GMLP_TILE, PROMPT_GLA_TILE))
    assert seq_s < SUBLANES and SUBLANES % seq_s == 0 and (n_seq_s * seq_s) % SAMPLE_TILE == 0
    row = lambda a: a.reshape(1, -1)

    mod_p, mod_s = _ada_call(c_prompt, jnp.repeat(c_sample, seq_s, axis=0), ada_w, ada_b)
    mod_p = mod_p.reshape(mod_p.shape[0], n_seq_p, 1, N_MOD * d)

    causal = jnp.tril(jnp.ones((CHUNK_A, CHUNK_A), dtype=bool))
    ws = jnp.where(causal[None], gmlp_w_s, jnp.zeros_like(gmlp_w_s))
    reps = CHUNK_A // seq_s
    eye = jnp.eye(reps, dtype=F32)
    ws_s = jnp.einsum("ab,gij->gaibj", eye, ws[:, :seq_s, :seq_s]).reshape(
        GMLP_GROUPS, CHUNK_A, CHUNK_A)
    bias_p = jnp.repeat(gmlp_b_s.T, GMLP_GROUP_W, axis=1)
    bias_s = jnp.tile(bias_p[:seq_s], (reps, 1))

    gmlp_w = dict(norm_g=row(norm_mix_g[0]), w_in=gmlp_w_in.astype(BF16), b_in=row(gmlp_b_in),
                  ln_g=row(gmlp_ln_g), ln_b=row(gmlp_ln_b), w_out=gmlp_w_out.astype(BF16),
                  b_out=row(gmlp_b_out))
    gmlp_w = dict(gmlp_w, mix_p=ws.astype(BF16), bias_p=bias_p, mix_s=ws_s.astype(BF16),
                  bias_s=bias_s)

    depth = ffn_w1.shape[0]
    ffn_w = dict(norm_g=norm_ffn_g.reshape(depth, 1, d), w1=ffn_w1.astype(BF16),
                 b1=ffn_b1.reshape(depth, 1, D_FF), w2=ffn_w2.astype(BF16),
                 b2=ffn_b2.reshape(depth, 1, d), final_g=row(final_norm_g))

    w_in_pad = jnp.pad(gla_w_in, ((0, 0), (0, GLA_IN_PAD - gla_w_in.shape[1]))).astype(BF16)
    wg2_pad = jnp.pad(gla_w_gate2, ((0, GLA_GATE_PAD - GLA_GATE_RANK), (0, 0))).astype(BF16)
    blk_ones = jnp.kron(jnp.eye(2, dtype=F32), jnp.ones((GLA_DK_HEAD, GLA_DK_HEAD), F32))
    gla_w = dict(norm_g=row(norm_mix_g[1]), w_in=w_in_pad, w_gate2=wg2_pad, b_gate=row(gla_b_gate),
                 out_g=row(gla_norm_g), w_out=gla_w_out.astype(BF16),
                 tril=jnp.tril(jnp.ones((GLA_TRIL, GLA_TRIL), F32)).astype(BF16),
                 ones_bd=blk_ones.astype(BF16))

    xp = x_prompt.reshape(n_seq_p * seq_p, d)
    xs = x_sample.reshape(n_seq_s * seq_s, d)
    grid_for = lambda tm_p, tm_s=SAMPLE_TILE: _TwoGroupGrid(xp.shape[0], xs.shape[0], tm_p, tm_s,
                                                            seq_p)
    xp, xs, chunk_v = _gmlp_call(xp, xs, mod_p, mod_s, grid_for(PROMPT_GMLP_TILE), gmlp_w)
    xp, xs = _ffn_call(xp, xs, mod_p, mod_s, 0, grid_for(PROMPT_FFN_TILE), ffn_w, final_norm=False)
    xp, state_p, proj, log_a = _gla_call(xp, xs, mod_p, mod_s, 1, n_seq_p,
                                         grid_for(PROMPT_GLA_TILE, SAMPLE_GLA_TILE), gla_w)
    o, state_s = _gla_step_call(proj, log_a, state_gla, seq_s, SAMPLE_SEQS_PER_STEP)
    xs = _gla_out_call(o, proj, xs, mod_s, 1, SAMPLE_TILE, gla_w)
    xp, xs = _ffn_call(xp, xs, mod_p, mod_s, 1, grid_for(PROMPT_FFN_TILE), ffn_w, final_norm=True)

    return (xp.reshape(x_prompt.shape), xs.reshape(x_sample.shape), state_p, state_s,
            chunk_v.reshape(n_seq_s, seq_s, GMLP_WIDTH))
```

```python
import functools

import jax
import jax.numpy as jnp
from jax import lax
from jax.experimental import pallas as pl
from jax.experimental.pallas import tpu as pltpu

F32 = jnp.float32
BF16 = jnp.bfloat16

D_MODEL = 1024
N_MOD = 6
CHUNK_A = 128
GMLP_WIDTH = D_MODEL
GMLP_GROUPS = 4
GMLP_GROUP_W = GMLP_WIDTH // GMLP_GROUPS
GLA_HEADS = 4
GLA_DK = D_MODEL // 2
GLA_DV = D_MODEL
GLA_DK_HEAD = GLA_DK // GLA_HEADS
GLA_DV_HEAD = GLA_DV // GLA_HEADS
GLA_GATE_RANK = 16
GLA_TAU = 16.0
GLA_CHUNK = 64
GLA_FAST_CHUNK = 128
GLA_TRIL = max(GLA_CHUNK, GLA_FAST_CHUNK)
GLA_SAFE_EXPONENT = 80.0
GLA_QKVG = 2 * GLA_DK + 2 * GLA_DV
GLA_GATE_PAD = 128
GLA_IN_PAD = GLA_QKVG + GLA_GATE_PAD
D_FF = 4 * D_MODEL
EPS = 1e-6

SUBLANES = 8
BF16_SUBLANES = 16
ADA_COLS = 2048
FFN_CHUNK = 1024
FFN_SUBTILE = 512
GMLP_SUBTILE = 256
GLA_SUBTILE = 512
VMEM_LIMIT = 56 * 1024 * 1024


def _rms(x, g):
    return x * lax.rsqrt(jnp.mean(x * x, axis=-1, keepdims=True) + EPS) * g


def _modulated_norm(x, g, scale, shift):
    inv = lax.rsqrt(jnp.mean(x * x, axis=-1, keepdims=True) + EPS)
    return (x * inv) * (g * (1.0 + scale)) + shift


def _dot(a, b):
    return jnp.dot(a.astype(BF16), b.astype(BF16), preferred_element_type=F32)


def _dot_nt(a, b):
    return lax.dot_general(a.astype(BF16), b.astype(BF16), (((1,), (1,)), ((), ())),
                           preferred_element_type=F32)


def _dot_tn(a, b):
    return lax.dot_general(a.astype(BF16), b.astype(BF16), (((0,), (0,)), ((), ())),
                           preferred_element_type=F32)


def _gelu_tanh(x):
    c1 = -2.0 * 0.7978845608028654 * 1.4426950408889634
    c2 = c1 * 0.044715
    return x / (1.0 + jnp.exp2(x * (c1 + c2 * (x * x))))


def _silu(x):
    return x * jax.nn.sigmoid(x)


def _log_sigmoid(x):
    return -(jnp.maximum(-x, 0.0) + jnp.log1p(jnp.exp(-jnp.abs(x))))


def _ada_kernel(c_ref, w_ref, b_ref, op_ref, os_ref, s_scr, *, n_prompt):
    @pl.when((pl.program_id(0) == 0) & (pl.program_id(1) == 0))
    def _():
        s_scr[...] = _silu(c_ref[...]).astype(BF16)

    y = _dot(s_scr[...], w_ref[...]) + b_ref[...]
    op_ref[...] = y[:n_prompt]
    os_ref[...] = y[n_prompt:n_prompt + os_ref.shape[0]]


def _ada_call(c_prompt, c_sample_rows, ada_w, ada_b):
    depth = ada_w.shape[0]
    n_prompt, n_sample = c_prompt.shape[0], c_sample_rows.shape[0]
    d = D_MODEL
    pad = -(n_prompt + n_sample) % BF16_SUBLANES
    c_all = jnp.concatenate([c_prompt, c_sample_rows, jnp.zeros((pad, d), F32)], axis=0)
    rows = c_all.shape[0]
    tn = ADA_COLS
    return pl.pallas_call(
        functools.partial(_ada_kernel, n_prompt=n_prompt),
        grid=(depth, N_MOD * d // tn),
        in_specs=[
            pl.BlockSpec((rows, d), lambda l, j: (0, 0)),
            pl.BlockSpec((None, d, tn), lambda l, j: (l, 0, j)),
            pl.BlockSpec((None, 1, tn), lambda l, j: (l, 0, j)),
        ],
        out_specs=[
            pl.BlockSpec((None, n_prompt, tn), lambda l, j: (l, 0, j)),
            pl.BlockSpec((None, n_sample, tn), lambda l, j: (l, 0, j)),
        ],
        out_shape=[
            jax.ShapeDtypeStruct((depth, n_prompt, N_MOD * d), F32),
            jax.ShapeDtypeStruct((depth, n_sample, N_MOD * d), F32),
        ],
        scratch_shapes=[pltpu.VMEM((rows, d), BF16)],
        compiler_params=pltpu.CompilerParams(
            dimension_semantics=("arbitrary", "arbitrary"), vmem_limit_bytes=VMEM_LIMIT),
        name="adaln_mod",
    )(c_all, ada_w, ada_b.reshape(depth, 1, N_MOD * d))


def _const_spec(shape):
    zeros = (0,) * len(shape)
    return pl.BlockSpec(shape, lambda i: zeros, pipeline_mode=pl.Buffered(1))


def _layer_spec(shape, layer):
    zeros = (0,) * len(shape)
    return pl.BlockSpec((None,) + shape, lambda i: (layer,) + zeros,
                        pipeline_mode=pl.Buffered(1))


def _mod_specs(mod, layer, first_chunk, tm, rows_per_seq):
    d = D_MODEL
    specs = []
    for j in range(first_chunk, first_chunk + 3):
        if rows_per_seq is None:
            specs.append(pl.BlockSpec((None, tm, d), lambda i, j=j: (layer, i, j)))
        else:
            tiles = rows_per_seq // tm
            specs.append(pl.BlockSpec((None, None, 1, d),
                                      lambda i, j=j, tiles=tiles: (layer, i // tiles, 0, j)))
    return specs


def _mod_rows(ref, rows):
    return ref[...] if ref.shape[0] == 1 else ref[rows, :]


def _row_spec(tm, width):
    return pl.BlockSpec((tm, width), lambda i: (i, 0))


class _TwoGroupGrid:
    def __init__(self, n_prompt_rows, n_sample_rows, tm_p, tm_s, seq_len):
        self.tm_p, self.tm_s = tm_p, tm_s
        self.p_steps = n_prompt_rows // tm_p
        self.s_steps = n_sample_rows // tm_s
        self.tiles_per_seq = seq_len // tm_p
        self.steps = self.p_steps + self.s_steps

    def p_idx(self, i):
        return jnp.minimum(i, self.p_steps - 1)

    def s_idx(self, i):
        return jnp.maximum(i - self.p_steps, 0)

    def rows_p(self, width):
        return pl.BlockSpec((self.tm_p, width), lambda i: (self.p_idx(i), 0))

    def rows_s(self, width, col_block=0):
        return pl.BlockSpec((self.tm_s, width), lambda i: (self.s_idx(i), col_block))

    def mods_p(self, layer, first_chunk):
        return [pl.BlockSpec((None, None, 1, D_MODEL),
                             lambda i, j=j: (layer, self.p_idx(i) // self.tiles_per_seq, 0, j))
                for j in range(first_chunk, first_chunk + 3)]

    def mods_s(self, layer, first_chunk):
        return [pl.BlockSpec((None, self.tm_s, D_MODEL),
                             lambda i, j=j: (layer, self.s_idx(i), j))
                for j in range(first_chunk, first_chunk + 3)]


def _gmlp_rows(x_ref, sh_ref, sc_ref, gt_ref, mix_ref, bs_ref, xo_ref, v_ref, z_scr, w, tm, sub):
    ng_ref, win_ref, bin_ref, lng_ref, lnb_ref, wout_ref, bout_ref = w
    groups = [slice(r0, r0 + sub) for r0 in range(0, tm, sub)]

    def project(gi):
        rows = groups[gi]
        h = _modulated_norm(x_ref[rows, :], ng_ref[...], _mod_rows(sc_ref, rows),
                            _mod_rows(sh_ref, rows))
        z_scr[gi % 2] = _dot(h, win_ref[...])

    project(0)
    for gi, rows in enumerate(groups):
        if gi + 1 < len(groups):
            project(gi + 1)
        z = _gelu_tanh(z_scr[gi % 2] + bin_ref[...])
        u = z[:, :GMLP_WIDTH]
        v = z[:, GMLP_WIDTH:]
        mu = jnp.mean(v, axis=-1, keepdims=True)
        vc = v - mu
        var = jnp.mean(vc * vc, axis=-1, keepdims=True)
        v = vc * lax.rsqrt(var + EPS) * lng_ref[...] + lnb_ref[...]
        if v_ref is not None:
            v_ref[rows, :] = v
        vb = v.astype(BF16)
        mixed = []
        for c in range(sub // CHUNK_A):
            cols = []
            for g in range(GMLP_GROUPS):
                blk = vb[c * CHUNK_A:(c + 1) * CHUNK_A, g * GMLP_GROUP_W:(g + 1) * GMLP_GROUP_W]
                cols.append(jnp.dot(mix_ref[g], blk, preferred_element_type=F32))
            mixed.append(jnp.concatenate(cols, axis=1) + bs_ref[...])
        s = jnp.concatenate(mixed, axis=0)
        y = _dot(u * s, wout_ref[...]) + bout_ref[...]
        xo_ref[rows, :] = x_ref[rows, :] + _mod_rows(gt_ref, rows) * y


def _gmlp_kernel(xp_ref, shp_ref, scp_ref, gtp_ref, xs_ref, shs_ref, scs_ref, gts_ref,
                 ng_ref, win_ref, bin_ref, lng_ref, lnb_ref, wout_ref, bout_ref,
                 mixp_ref, bsp_ref, mixs_ref, bss_ref, op_ref, os_ref, vs_ref, z_scr, *, grid):
    w = (ng_ref, win_ref, bin_ref, lng_ref, lnb_ref, wout_ref, bout_ref)
    step = pl.program_id(0)

    @pl.when(step < grid.p_steps)
    def _():
        _gmlp_rows(xp_ref, shp_ref, scp_ref, gtp_ref, mixp_ref, bsp_ref, op_ref, None, z_scr, w,
                   grid.tm_p, GMLP_SUBTILE)

    @pl.when(step >= grid.p_steps)
    def _():
        _gmlp_rows(xs_ref, shs_ref, scs_ref, gts_ref, mixs_ref, bss_ref, os_ref, vs_ref, z_scr, w,
                   grid.tm_s, GMLP_SUBTILE)


def _gmlp_call(xp, xs, mod_p, mod_s, grid, w):
    d = D_MODEL
    mix_spec = _const_spec((GMLP_GROUPS, CHUNK_A, CHUNK_A))
    bias_spec = _const_spec((CHUNK_A, GMLP_WIDTH))
    in_specs = ([grid.rows_p(d)] + grid.mods_p(0, 0) + [grid.rows_s(d)] + grid.mods_s(0, 0) + [
        _const_spec((1, d)),
        _const_spec((d, 2 * GMLP_WIDTH)),
        _const_spec((1, 2 * GMLP_WIDTH)),
        _const_spec((1, GMLP_WIDTH)),
        _const_spec((1, GMLP_WIDTH)),
        _const_spec((GMLP_WIDTH, d)),
        _const_spec((1, d)),
        mix_spec, bias_spec, mix_spec, bias_spec,
    ])
    return pl.pallas_call(
        functools.partial(_gmlp_kernel, grid=grid),
        grid=(grid.steps,),
        in_specs=in_specs,
        out_specs=[grid.rows_p(d), grid.rows_s(d), grid.rows_s(GMLP_WIDTH)],
        out_shape=[jax.ShapeDtypeStruct(xp.shape, F32), jax.ShapeDtypeStruct(xs.shape, F32),
                   jax.ShapeDtypeStruct((xs.shape[0], GMLP_WIDTH), F32)],
        scratch_shapes=[pltpu.VMEM((2, GMLP_SUBTILE, 2 * GMLP_WIDTH), F32)],
        compiler_params=pltpu.CompilerParams(
            dimension_semantics=("arbitrary",), vmem_limit_bytes=VMEM_LIMIT),
        name="gmlp_mixer",
    )(xp, mod_p, mod_p, mod_p, xs, mod_s, mod_s, mod_s, w["norm_g"], w["w_in"], w["b_in"],
      w["ln_g"], w["ln_b"], w["w_out"], w["b_out"], w["mix_p"], w["bias_p"], w["mix_s"],
      w["bias_s"])


def _ffn_rows(x_ref, sh_ref, sc_ref, gt_ref, o_ref, h_scr, w, final_norm, tm):
    ng_ref, w1_ref, b1_ref, w2_ref, b2_ref, fg_ref = w
    sub = min(FFN_SUBTILE, tm)
    groups = [slice(r0, r0 + sub) for r0 in range(0, tm, sub)]

    def normalise(gi):
        rows = groups[gi]
        h_scr[gi % 2, 0:sub, :] = _modulated_norm(
            x_ref[rows, :], ng_ref[...], _mod_rows(sc_ref, rows),
            _mod_rows(sh_ref, rows)).astype(BF16)

    normalise(0)
    for gi, rows in enumerate(groups):
        if gi + 1 < len(groups):
            normalise(gi + 1)
        h = h_scr[gi % 2, 0:sub, :]
        acc = jnp.zeros((sub, D_MODEL), F32)
        for j in range(D_FF // FFN_CHUNK):
            cols = slice(j * FFN_CHUNK, (j + 1) * FFN_CHUNK)
            a = jnp.dot(h, w1_ref[:, cols], preferred_element_type=F32) + b1_ref[:, cols]
            r = jnp.square(jnp.maximum(a, 0.0)).astype(BF16)
            acc = acc + jnp.dot(r, w2_ref[cols, :], preferred_element_type=F32)
        y = x_ref[rows, :] + _mod_rows(gt_ref, rows) * (acc + b2_ref[...])
        if final_norm:
            y = _rms(y, fg_ref[...])
        o_ref[rows, :] = y


def _ffn_kernel(xp_ref, shp_ref, scp_ref, gtp_ref, xs_ref, shs_ref, scs_ref, gts_ref,
                ng_ref, w1_ref, b1_ref, w2_ref, b2_ref, fg_ref, op_ref, os_ref, h_scr,
                *, final_norm, grid):
    w = (ng_ref, w1_ref, b1_ref, w2_ref, b2_ref, fg_ref)
    step = pl.program_id(0)

    @pl.when(step < grid.p_steps)
    def _():
        _ffn_rows(xp_ref, shp_ref, scp_ref, gtp_ref, op_ref, h_scr, w, final_norm, grid.tm_p)

    @pl.when(step >= grid.p_steps)
    def _():
        _ffn_rows(xs_ref, shs_ref, scs_ref, gts_ref, os_ref, h_scr, w, final_norm, grid.tm_s)


def _ffn_call(xp, xs, mod_p, mod_s, layer, grid, w, final_norm):
    d = D_MODEL
    in_specs = ([grid.rows_p(d)] + grid.mods_p(layer, 3) + [grid.rows_s(d)]
                + grid.mods_s(layer, 3) + [
        _layer_spec((1, d), layer),
        _layer_spec((d, D_FF), layer),
        _layer_spec((1, D_FF), layer),
        _layer_spec((D_FF, d), layer),
        _layer_spec((1, d), layer),
        _const_spec((1, d)),
    ])
    return pl.pallas_call(
        functools.partial(_ffn_kernel, final_norm=final_norm, grid=grid),
        grid=(grid.steps,),
        in_specs=in_specs,
        out_specs=[grid.rows_p(d), grid.rows_s(d)],
        out_shape=[jax.ShapeDtypeStruct(xp.shape, F32), jax.ShapeDtypeStruct(xs.shape, F32)],
        scratch_shapes=[pltpu.VMEM((2, FFN_SUBTILE, d), BF16)],
        compiler_params=pltpu.CompilerParams(
            dimension_semantics=("arbitrary",), vmem_limit_bytes=VMEM_LIMIT),
        name="sqrelu_mlp",
    )(xp, mod_p, mod_p, mod_p, xs, mod_s, mod_s, mod_s, w["norm_g"], w["w1"], w["b1"], w["w2"],
      w["b2"], w["final_g"])


def _gla_project(x, ng, scale, shift, win, wg2, bg):
    h = _modulated_norm(x, ng, scale, shift)
    proj = _dot(h, win)
    gate = _dot(proj[:, GLA_QKVG:], wg2) + bg
    log_a = _log_sigmoid(gate) * (1.0 / GLA_TAU)
    return proj[:, :GLA_QKVG], log_a


def _gla_output(o, gate_in, x, gt, og, wout):
    parts = []
    for h in range(GLA_HEADS):
        cols = slice(h * GLA_DV_HEAD, (h + 1) * GLA_DV_HEAD)
        parts.append(_rms(o[:, cols], og))
    on = jnp.concatenate(parts, axis=1)
    y = _dot(on * _silu(gate_in), wout)
    return x + gt * y


def _as_column(row):
    return jnp.broadcast_to(row, (SUBLANES, row.shape[1])).T[:, 0:1]


def _chunk_cumsum(tril, log_a):
    head = log_a.astype(BF16)
    rest = (log_a - head.astype(F32)).astype(BF16)
    return (jnp.dot(tril, head, preferred_element_type=F32)
            + jnp.dot(tril, rest, preferred_element_type=F32))


def _level_masks(c):
    rr = lax.broadcasted_iota(jnp.int32, (c, c), 0)
    cc = lax.broadcasted_iota(jnp.int32, (c, c), 1)
    masks = []
    s = c
    while s > SUBLANES:
        half = s // 2
        same = (rr ^ cc) < s
        masks.append(same & ((rr & half) != 0) & ((cc & half) == 0))
        s = half
    return masks, rr, cc


def _block_rows(x, s, r):
    c, w = x.shape
    return jnp.concatenate(
        [jnp.broadcast_to(x[p * s + r:p * s + r + 1, :], (s, w)) for p in range(c // s)], axis=0)


def _diag_terms(q, k, a):
    c, w = q.shape
    ii = lax.broadcasted_iota(jnp.int32, (c, w), 0) & (SUBLANES - 1)
    decay = jnp.zeros((c, w), F32)
    terms = [None] * SUBLANES
    for j in range(SUBLANES - 1, -1, -1):
        if j < SUBLANES - 1:
            decay = decay * _block_rows(a, SUBLANES, j + 1)
        decay = jnp.where(ii == j, 1.0, decay)
        terms[j] = q * _block_rows(k, SUBLANES, j) * decay
    return jnp.concatenate(terms, axis=0)


def _gla_chunk_pair(q2, k2, v2, la2, b2, st_refs, ones_bd, masks, rr, cc):
    c = q2[0].shape[0]
    zs = []
    for q, k, la in zip(q2, k2, la2):
        zs.append(_diag_terms(q, k, jnp.exp(la)))
    sums = jnp.dot(jnp.concatenate(zs, axis=1).astype(BF16), ones_bd,
                   preferred_element_type=F32)
    outs = []
    for hh, (q, k, v, b, st_ref) in enumerate(zip(q2, k2, v2, b2, st_refs)):
        dk = q.shape[1]
        attn = jnp.zeros((c, c), F32)
        base = rr & ~(SUBLANES - 1)
        for j in range(SUBLANES):
            rj = sums[j * c:(j + 1) * c, hh * dk:hh * dk + c]
            attn = jnp.where(cc == base + j, rj, attn)
        s = c
        for mask in masks:
            half = s // 2
            e = jnp.exp(-jnp.abs(b - _block_rows(b, s, half)))
            attn = jnp.where(mask, _dot_nt(q * e, k * e), attn)
            s = half
        b_last = b[c - 1:c, :]
        st = st_ref[...]
        o = _dot(q * jnp.exp(b), st) + _dot(attn, v)
        kd = k * jnp.exp(b_last - b)
        st_ref[...] = _as_column(jnp.exp(b_last)) * st + _dot_tn(kd, v)
        outs.append(o)
    return outs


def _gla_prompt_kernel(x_ref, sh_ref, sc_ref, gt_ref, ng_ref, win_ref, wg2_ref, bg_ref,
                       og_ref, wout_ref, tril_ref, ones_ref, xo_ref, s_ref,
                       proj_scr, ga_scr, la_scr, b_scr, o_scr, st_scr, *, tm, sub, tiles_per_seq):
    tile = pl.program_id(0) % tiles_per_seq

    @pl.when(tile == 0)
    def _():
        st_scr[...] = jnp.zeros(st_scr.shape, F32)

    dk, dv = GLA_DK_HEAD, GLA_DV_HEAD
    qscale = GLA_DK_HEAD ** -0.5
    cf = GLA_FAST_CHUNK
    c = GLA_CHUNK
    groups = [slice(r0, r0 + sub) for r0 in range(0, tm, sub)]

    def project(rows):
        h = _modulated_norm(x_ref[rows, :], ng_ref[...], _mod_rows(sc_ref, rows),
                            _mod_rows(sh_ref, rows))
        proj = _dot(h, win_ref[...])
        proj_scr[rows, :] = proj[:, :GLA_QKVG]
        ga_scr[rows, :] = proj[:, GLA_QKVG:]

    def decays(rows):
        gate = _dot(ga_scr[rows, :], wg2_ref[...]) + bg_ref[...]
        log_a = _log_sigmoid(gate) * (1.0 / GLA_TAU)
        la_scr[...] = log_a
        for ci in range(sub // cf):
            chunk = slice(ci * cf, (ci + 1) * cf)
            b_scr[chunk, :] = _chunk_cumsum(tril_ref[0:cf, 0:cf], log_a[chunk, :])
        return jnp.min(b_scr[...]) >= -GLA_SAFE_EXPONENT

    def head_slices(rows, h):
        q = proj_scr[rows, h * dk:(h + 1) * dk] * qscale
        k = proj_scr[rows, GLA_DK + h * dk:GLA_DK + (h + 1) * dk]
        v = proj_scr[rows, 2 * GLA_DK + h * dv:2 * GLA_DK + (h + 1) * dv]
        return q, k, v

    def single_ref_rows(group):
        rr = lax.broadcasted_iota(jnp.int32, (cf, cf), 0)
        cc = lax.broadcasted_iota(jnp.int32, (cf, cf), 1)
        causal = rr >= cc
        local = [slice(r0, r0 + cf) for r0 in range(0, sub, cf)]
        chunks = [slice(group.start + r.start, group.start + r.stop) for r in local]
        for h in range(GLA_HEADS):
            qes, vbs, scores, kvs, decays = [], [], [], [], []
            for rows, lrows in zip(chunks, local):
                q, k, v = head_slices(rows, h)
                b = b_scr[lrows, h * dk:(h + 1) * dk]
                decay_last = jnp.exp(b[cf - 1:cf, :])
                qe = (q * jnp.exp(b)).astype(BF16)
                kt = k * jnp.exp(-b)
                vb = v.astype(BF16)
                scores.append(_dot_nt(qe, kt))
                kvs.append(_dot_tn(kt * decay_last, vb))
                qes.append(qe)
                vbs.append(vb)
                decays.append(_as_column(decay_last))
            st = st_scr[h]
            states = []
            for kv, decay_col in zip(kvs, decays):
                states.append(st)
                st = decay_col * st + kv
            st_scr[h] = st
            for lrows, qe, vb, sc, st_in in zip(local, qes, vbs, scores, states):
                attn = jnp.where(causal, sc, 0.0).astype(BF16)
                lhs = jnp.concatenate([qe, attn], axis=1)
                rhs = jnp.concatenate([st_in.astype(BF16), vb], axis=0)
                o_scr[lrows, h * dv:(h + 1) * dv] = jnp.dot(lhs, rhs, preferred_element_type=F32)

    def robust_rows(group):
        def robust_body(ci, carry):
            lrows = pl.ds(pl.multiple_of(ci * c, c), c)
            rows = pl.ds(pl.multiple_of(group.start + ci * c, c), c)
            masks, rr, cc = _level_masks(c)
            b_c = _chunk_cumsum(tril_ref[0:c, 0:c], la_scr[lrows, :])
            outs = []
            for h0 in range(0, GLA_HEADS, 2):
                heads = (h0, h0 + 1)
                qkv = [head_slices(rows, h) for h in heads]
                la2 = [la_scr[lrows, h * dk:(h + 1) * dk] for h in heads]
                b2 = [b_c[:, h * dk:(h + 1) * dk] for h in heads]
                st_refs = [st_scr.at[h] for h in heads]
                outs += _gla_chunk_pair([t[0] for t in qkv], [t[1] for t in qkv],
                                        [t[2] for t in qkv], la2, b2, st_refs, ones_ref[...],
                                        masks, rr, cc)
            o_scr[lrows, :] = jnp.concatenate(outs, axis=1)
            return carry

        lax.fori_loop(0, sub // c, robust_body, 0)

    def output(rows):
        gate_in = proj_scr[rows, 2 * GLA_DK + GLA_DV:]
        xo_ref[rows, :] = _gla_output(o_scr[...], gate_in, x_ref[rows, :],
                                      _mod_rows(gt_ref, rows), og_ref[...], wout_ref[...])

    project(groups[0])
    for gi, rows in enumerate(groups):
        single_ref_safe = decays(rows)
        following = groups[gi + 1] if gi + 1 < len(groups) else None

        @pl.when(single_ref_safe)
        def _():
            if following is not None:
                project(following)
            single_ref_rows(rows)
            output(rows)

        @pl.when(jnp.logical_not(single_ref_safe))
        def _():
            if following is not None:
                project(following)
            robust_rows(rows)
            output(rows)

    @pl.when(tile == tiles_per_seq - 1)
    def _():
        s_ref[...] = st_scr[...]


def _gla_prompt_call(x, mod, layer, n_seq, seq_len, tm, w):
    n, d = x.shape
    tiles_per_seq = seq_len // tm
    dk, dv = GLA_DK_HEAD, GLA_DV_HEAD
    in_specs = [_row_spec(tm, d)] + _mod_specs(mod, layer, 0, tm, seq_len) + [
        _const_spec((1, d)),
        _const_spec((d, GLA_IN_PAD)),
        _const_spec((GLA_GATE_PAD, GLA_DK)),
        _const_spec((1, GLA_DK)),
        _const_spec((1, dv)),
        _const_spec((GLA_DV, d)),
        _const_spec((GLA_TRIL, GLA_TRIL)),
        _const_spec((2 * dk, 2 * dk)),
    ]
    out_specs = [
        _row_spec(tm, d),
        pl.BlockSpec((None, GLA_HEADS, dk, dv), lambda i: (i // tiles_per_seq, 0, 0, 0)),
    ]
    out_shape = [
        jax.ShapeDtypeStruct((n, d), F32),
        jax.ShapeDtypeStruct((n_seq, GLA_HEADS, dk, dv), F32),
    ]
    return pl.pallas_call(
        functools.partial(_gla_prompt_kernel, tm=tm, sub=GLA_SUBTILE,
                          tiles_per_seq=tiles_per_seq),
        grid=(n // tm,),
        in_specs=in_specs,
        out_specs=out_specs,
        out_shape=out_shape,
        scratch_shapes=[
            pltpu.VMEM((tm, GLA_QKVG), F32),
            pltpu.VMEM((tm, GLA_GATE_PAD), F32),
            pltpu.VMEM((GLA_SUBTILE, GLA_DK), F32),
            pltpu.VMEM((GLA_SUBTILE, GLA_DK), F32),
            pltpu.VMEM((GLA_SUBTILE, GLA_DV), F32),
            pltpu.VMEM((GLA_HEADS, dk, dv), F32),
        ],
        compiler_params=pltpu.CompilerParams(
            dimension_semantics=("arbitrary",), vmem_limit_bytes=VMEM_LIMIT),
        name="gla_mixer_prompt",
    )(x, mod, mod, mod, w["norm_g"], w["w_in"], w["w_gate2"], w["b_gate"], w["out_g"],
      w["w_out"], w["tril"], w["ones_bd"])


def _gla_proj_kernel(x_ref, sh_ref, sc_ref, ng_ref, win_ref, wg2_ref, bg_ref, p_ref, la_ref):
    proj, log_a = _gla_project(x_ref[...], ng_ref[...], sc_ref[...], sh_ref[...],
                               win_ref[...], wg2_ref[...], bg_ref[...])
    p_ref[...] = proj
    la_ref[...] = log_a


def _gla_proj_call(x, mod, layer, tm, w):
    n, d = x.shape
    specs = _mod_specs(mod, layer, 0, tm, None)
    in_specs = [_row_spec(tm, d), specs[0], specs[1],
                _const_spec((1, d)),
                _const_spec((d, GLA_IN_PAD)),
                _const_spec((GLA_GATE_PAD, GLA_DK)),
                _const_spec((1, GLA_DK))]
    return pl.pallas_call(
        _gla_proj_kernel,
        grid=(n // tm,),
        in_specs=in_specs,
        out_specs=[_row_spec(tm, GLA_QKVG), _row_spec(tm, GLA_DK)],
        out_shape=[jax.ShapeDtypeStruct((n, GLA_QKVG), F32),
                   jax.ShapeDtypeStruct((n, GLA_DK), F32)],
        compiler_params=pltpu.CompilerParams(
            dimension_semantics=("arbitrary",), vmem_limit_bytes=VMEM_LIMIT),
        name="gla_proj_sample",
    )(x, mod, mod, w["norm_g"], w["w_in"], w["w_gate2"], w["b_gate"])


def _gla_step_kernel(p_ref, la_ref, s0_ref, o_ref, s1_ref, *, seqs, steps):
    dk, dv = GLA_DK_HEAD, GLA_DV_HEAD
    qscale = GLA_DK_HEAD ** -0.5
    per_tile = SUBLANES // steps
    row = lax.broadcasted_iota(jnp.int32, (SUBLANES, dk), 0)

    def seq_rows(tile, s):
        if s:
            tile = pltpu.roll(tile, SUBLANES - s * steps, 0)
        keep = lax.broadcasted_iota(jnp.int32, tile.shape, 0) < steps
        return jnp.where(keep, tile, 0.0)

    def tile_body(ti, carry):
        rows = pl.ds(pl.multiple_of(ti * SUBLANES, SUBLANES), SUBLANES)
        p_tile = p_ref[rows, :]
        la_tile = la_ref[rows, :]
        o_tile = jnp.zeros((SUBLANES, GLA_DV), F32)
        for s in range(per_tile):
            bi = ti * per_tile + s
            p = seq_rows(p_tile, s)
            la_all = seq_rows(la_tile, s)
            outs = []
            for h in range(GLA_HEADS):
                q = p[:, h * dk:(h + 1) * dk] * qscale
                k = p[:, GLA_DK + h * dk:GLA_DK + (h + 1) * dk]
                v = p[:, 2 * GLA_DK + h * dv:2 * GLA_DK + (h + 1) * dv]
                la = la_all[:, h * dk:(h + 1) * dk]
                b = jnp.zeros((SUBLANES, dk), F32)
                for t in range(steps):
                    b = b + jnp.where(row >= t, la[t:t + 1, :], 0.0)
                b_last = b[steps - 1:steps, :]
                s0 = s0_ref[bi, h]
                o = _dot(q * jnp.exp(b), s0)
                for j in range(steps):
                    e = jnp.exp(jnp.minimum(b - b[j:j + 1, :], 0.0))
                    z = jnp.where(row >= j, q * k[j:j + 1, :] * e, 0.0)
                    o = o + jnp.sum(z, axis=-1, keepdims=True) * v[j:j + 1, :]
                outs.append(o)
                m = jnp.where(row == steps, jnp.exp(b_last), k * jnp.exp(b_last - b))
                m_t = m.T
                s1_ref[bi, h] = m_t[:, steps:steps + 1] * s0 + _dot(m_t, v)
            o_seq = jnp.concatenate(outs, axis=1)
            o_tile = o_tile + (pltpu.roll(o_seq, s * steps, 0) if s else o_seq)
        o_ref[rows, :] = o_tile
        return carry

    lax.fori_loop(0, seqs // per_tile, tile_body, 0)


def _gla_step_call(proj, log_a, state, steps, seqs):
    n_seq = state.shape[0]
    dk, dv = GLA_DK_HEAD, GLA_DV_HEAD
    rows = lambda width: pl.BlockSpec((seqs * steps, width), lambda i: (i, 0))
    st_spec = pl.BlockSpec((seqs, GLA_HEADS, dk, dv), lambda i: (i, 0, 0, 0))
    return pl.pallas_call(
        functools.partial(_gla_step_kernel, seqs=seqs, steps=steps),
        grid=(n_seq // seqs,),
        in_specs=[rows(GLA_QKVG), rows(GLA_DK), st_spec],
        out_specs=[rows(GLA_DV), st_spec],
        out_shape=[jax.ShapeDtypeStruct((n_seq * steps, GLA_DV), F32),
                   jax.ShapeDtypeStruct(state.shape, F32)],
        compiler_params=pltpu.CompilerParams(
            dimension_semantics=("arbitrary",), vmem_limit_bytes=VMEM_LIMIT),
        name="gla_step_sample",
    )(proj, log_a, state)


def _gla_out_kernel(o_ref, p_ref, x_ref, gt_ref, og_ref, wout_ref, xo_ref):
    xo_ref[...] = _gla_output(o_ref[...], p_ref[...], x_ref[...], gt_ref[...],
                              og_ref[...], wout_ref[...])


def _gla_out_call(o, proj, x, mod, layer, tm, w):
    n, d = x.shape
    gate_spec = _mod_specs(mod, layer, 0, tm, None)[2]
    g_block = (2 * GLA_DK + GLA_DV) // GLA_DV
    in_specs = [_row_spec(tm, GLA_DV),
                pl.BlockSpec((tm, GLA_DV), lambda i: (i, g_block)),
                _row_spec(tm, d), gate_spec,
                _const_spec((1, GLA_DV_HEAD)),
                _const_spec((GLA_DV, d))]
    return pl.pallas_call(
        _gla_out_kernel,
        grid=(n // tm,),
        in_specs=in_specs,
        out_specs=_row_spec(tm, d),
        out_shape=jax.ShapeDtypeStruct((n, d), F32),
        compiler_params=pltpu.CompilerParams(
            dimension_semantics=("arbitrary",), vmem_limit_bytes=VMEM_LIMIT),
        name="gla_out_sample",
    )(o, proj, x, mod, w["out_g"], w["w_out"])


PROMPT_FFN_TILE = 1024
PROMPT_GMLP_TILE = 1024
PROMPT_GLA_TILE = 1024
SAMPLE_TILE = 256
SAMPLE_SEQS_PER_STEP = 8


def kernel(x_prompt, x_sample, c_prompt, c_sample, state_gla, ada_w, ada_b, norm_mix_g, norm_ffn_g,
           ffn_w1, ffn_b1, ffn_w2, ffn_b2, gmlp_w_in, gmlp_b_in, gmlp_ln_g, gmlp_ln_b, gmlp_w_s,
           gmlp_b_s, gmlp_w_out, gmlp_b_out, gla_w_in, gla_w_gate2, gla_b_gate, gla_norm_g,
           gla_w_out, final_norm_g):
    n_seq_p, seq_p, d = x_prompt.shape
    n_seq_s, seq_s, _ = x_sample.shape
    assert d == D_MODEL and GMLP_SUBTILE % CHUNK_A == 0 and GLA_SUBTILE % GLA_FAST_CHUNK == 0
    assert all(seq_p % t == 0 for t in (PROMPT_FFN_TILE, PROMPT_GMLP_TILE, PROMPT_GLA_TILE))
    assert seq_s < SUBLANES and SUBLANES % seq_s == 0 and (n_seq_s * seq_s) % SAMPLE_TILE == 0
    row = lambda a: a.reshape(1, -1)

    mod_p, mod_s = _ada_call(c_prompt, jnp.repeat(c_sample, seq_s, axis=0), ada_w, ada_b)
    mod_p = mod_p.reshape(mod_p.shape[0], n_seq_p, 1, N_MOD * d)

    causal = jnp.tril(jnp.ones((CHUNK_A, CHUNK_A), dtype=bool))
    ws = jnp.where(causal[None], gmlp_w_s, jnp.zeros_like(gmlp_w_s))
    reps = CHUNK_A // seq_s
    eye = jnp.eye(reps, dtype=F32)
    ws_s = jnp.einsum("ab,gij->gaibj", eye, ws[:, :seq_s, :seq_s]).reshape(
        GMLP_GROUPS, CHUNK_A, CHUNK_A)
    bias_p = jnp.repeat(gmlp_b_s.T, GMLP_GROUP_W, axis=1)
    bias_s = jnp.tile(bias_p[:seq_s], (reps, 1))

    gmlp_w = dict(norm_g=row(norm_mix_g[0]), w_in=gmlp_w_in.astype(BF16), b_in=row(gmlp_b_in),
                  ln_g=row(gmlp_ln_g), ln_b=row(gmlp_ln_b), w_out=gmlp_w_out.astype(BF16),
                  b_out=row(gmlp_b_out))
    gmlp_w = dict(gmlp_w, mix_p=ws.astype(BF16), bias_p=bias_p, mix_s=ws_s.astype(BF16),
                  bias_s=bias_s)

    depth = ffn_w1.shape[0]
    ffn_w = dict(norm_g=norm_ffn_g.reshape(depth, 1, d), w1=ffn_w1.astype(BF16),
                 b1=ffn_b1.reshape(depth, 1, D_FF), w2=ffn_w2.astype(BF16),
                 b2=ffn_b2.reshape(depth, 1, d), final_g=row(final_norm_g))

    w_in_pad = jnp.pad(gla_w_in, ((0, 0), (0, GLA_IN_PAD - gla_w_in.shape[1]))).astype(BF16)
    wg2_pad = jnp.pad(gla_w_gate2, ((0, GLA_GATE_PAD - GLA_GATE_RANK), (0, 0))).astype(BF16)
    blk_ones = jnp.kron(jnp.eye(2, dtype=F32), jnp.ones((GLA_DK_HEAD, GLA_DK_HEAD), F32))
    gla_w = dict(norm_g=row(norm_mix_g[1]), w_in=w_in_pad, w_gate2=wg2_pad, b_gate=row(gla_b_gate),
                 out_g=row(gla_norm_g), w_out=gla_w_out.astype(BF16),
                 tril=jnp.tril(jnp.ones((GLA_TRIL, GLA_TRIL), F32)).astype(BF16),
                 ones_bd=blk_ones.astype(BF16))

    xp = x_prompt.reshape(n_seq_p * seq_p, d)
    xs = x_sample.reshape(n_seq_s * seq_s, d)
    grid_for = lambda tm_p: _TwoGroupGrid(xp.shape[0], xs.shape[0], tm_p, SAMPLE_TILE, seq_p)
    xp, xs, chunk_v = _gmlp_call(xp, xs, mod_p, mod_s, grid_for(PROMPT_GMLP_TILE), gmlp_w)
    xp, xs = _ffn_call(xp, xs, mod_p, mod_s, 0, grid_for(PROMPT_FFN_TILE), ffn_w, final_norm=False)
    xp, state_p = _gla_prompt_call(xp, mod_p, 1, n_seq_p, seq_p, PROMPT_GLA_TILE, gla_w)
    proj, log_a = _gla_proj_call(xs, mod_s, 1, SAMPLE_TILE, gla_w)
    o, state_s = _gla_step_call(proj, log_a, state_gla, seq_s, SAMPLE_SEQS_PER_STEP)
    xs = _gla_out_call(o, proj, xs, mod_s, 1, SAMPLE_TILE, gla_w)
    xp, xs = _ffn_call(xp, xs, mod_p, mod_s, 1, grid_for(PROMPT_FFN_TILE), ffn_w, final_norm=True)

    return (xp.reshape(x_prompt.shape), xs.reshape(x_sample.shape), state_p, state_s,
            chunk_v.reshape(n_seq_s, seq_s, GMLP_WIDTH))
```

```python
import functools

import jax
import jax.numpy as jnp
import numpy as np
from jax import lax
from jax.experimental import pallas as pl
from jax.experimental.pallas import tpu as pltpu

F32 = jnp.float32
BF16 = jnp.bfloat16

D_MODEL = 1024
N_MOD = 6
CHUNK_A = 128
GMLP_WIDTH = D_MODEL
GMLP_GROUPS = 4
GMLP_GROUP_W = GMLP_WIDTH // GMLP_GROUPS
GLA_HEADS = 4
GLA_DK = D_MODEL // 2
GLA_DV = D_MODEL
GLA_DK_HEAD = GLA_DK // GLA_HEADS
GLA_DV_HEAD = GLA_DV // GLA_HEADS
GLA_GATE_RANK = 16
GLA_TAU = 16.0
GLA_CHUNK = 64
GLA_FAST_CHUNK = 128
GLA_TRIL = max(GLA_CHUNK, GLA_FAST_CHUNK)
GLA_SAFE_EXPONENT = 80.0
GLA_QKVG = 2 * GLA_DK + 2 * GLA_DV
GLA_GATE_PAD = 128
GLA_IN_PAD = GLA_QKVG + GLA_GATE_PAD
D_FF = 4 * D_MODEL
EPS = 1e-6

SUBLANES = 8
BF16_SUBLANES = 16
ADA_ROWS = 256
FFN_CHUNK = 1024
FFN_SUBTILE = 512
GMLP_SUBTILE = 256
GLA_SUBTILE = 512
VMEM_LIMIT = 56 * 1024 * 1024


def _rms(x, g):
    return x * lax.rsqrt(jnp.mean(x * x, axis=-1, keepdims=True) + EPS) * g


def _modulated_norm(x, g, scale, shift):
    inv = lax.rsqrt(jnp.mean(x * x, axis=-1, keepdims=True) + EPS)
    return (x * inv) * (g * (1.0 + scale)) + shift


def _dot(a, b):
    return jnp.dot(a.astype(BF16), b.astype(BF16), preferred_element_type=F32)


def _dot_nt(a, b):
    return lax.dot_general(a.astype(BF16), b.astype(BF16), (((1,), (1,)), ((), ())),
                           preferred_element_type=F32)


def _dot_tn(a, b):
    return lax.dot_general(a.astype(BF16), b.astype(BF16), (((0,), (0,)), ((), ())),
                           preferred_element_type=F32)


def _gelu_tanh(x):
    c1 = -2.0 * 0.7978845608028654 * 1.4426950408889634
    c2 = c1 * 0.044715
    return x / (1.0 + jnp.exp2(x * (c1 + c2 * (x * x))))


def _silu(x):
    return x * jax.nn.sigmoid(x)


def _log_sigmoid(x):
    return -(jnp.maximum(-x, 0.0) + jnp.log1p(jnp.exp(-jnp.abs(x))))


def _ada_kernel(c_ref, w_ref, b_ref, op_ref, os_ref, s_scr, *, n_prompt, tk):
    k = pl.program_id(1)
    n_sample = os_ref.shape[0]

    @pl.when((pl.program_id(0) == 0) & (k == 0))
    def _():
        for kk in range(s_scr.shape[0]):
            s_scr[kk] = _silu(c_ref[:, kk * tk:(kk + 1) * tk]).astype(BF16)

    @pl.when(k == 0)
    def _():
        op_ref[...] = jnp.broadcast_to(b_ref[...], op_ref.shape)
        os_ref[...] = jnp.broadcast_to(b_ref[...], os_ref.shape)

    s = s_scr[k]
    for j in range(N_MOD):
        cols = slice(j * D_MODEL, (j + 1) * D_MODEL)
        y = jnp.dot(s, w_ref[:, cols].astype(BF16), preferred_element_type=F32)
        op_ref[:, cols] += y[:n_prompt]
        os_ref[:, cols] += y[n_prompt:n_prompt + n_sample]


def _ada_call(c_prompt, c_sample_rows, ada_w, ada_b):
    depth = ada_w.shape[0]
    n_prompt, n_sample = c_prompt.shape[0], c_sample_rows.shape[0]
    d = D_MODEL
    pad = -(n_prompt + n_sample) % BF16_SUBLANES
    c_all = jnp.concatenate([c_prompt, c_sample_rows, jnp.zeros((pad, d), F32)], axis=0)
    rows = c_all.shape[0]
    tk = ADA_ROWS
    width = N_MOD * d
    return pl.pallas_call(
        functools.partial(_ada_kernel, n_prompt=n_prompt, tk=tk),
        grid=(depth, d // tk),
        in_specs=[
            pl.BlockSpec((rows, d), lambda l, k: (0, 0)),
            pl.BlockSpec((None, tk, width), lambda l, k: (l, k, 0)),
            pl.BlockSpec((None, 1, width), lambda l, k: (l, 0, 0)),
        ],
        out_specs=[
            pl.BlockSpec((None, n_prompt, width), lambda l, k: (l, 0, 0)),
            pl.BlockSpec((None, n_sample, width), lambda l, k: (l, 0, 0)),
        ],
        out_shape=[
            jax.ShapeDtypeStruct((depth, n_prompt, width), F32),
            jax.ShapeDtypeStruct((depth, n_sample, width), F32),
        ],
        scratch_shapes=[pltpu.VMEM((d // tk, rows, tk), BF16)],
        compiler_params=pltpu.CompilerParams(
            dimension_semantics=("arbitrary", "arbitrary"), vmem_limit_bytes=VMEM_LIMIT),
        name="adaln_mod",
    )(c_all, ada_w, ada_b.reshape(depth, 1, N_MOD * d))


def _const_spec(shape):
    zeros = (0,) * len(shape)
    return pl.BlockSpec(shape, lambda i: zeros, pipeline_mode=pl.Buffered(1))


def _layer_spec(shape, layer):
    zeros = (0,) * len(shape)
    return pl.BlockSpec((None,) + shape, lambda i: (layer,) + zeros,
                        pipeline_mode=pl.Buffered(1))


def _mod_specs(mod, layer, first_chunk, tm, rows_per_seq):
    d = D_MODEL
    specs = []
    for j in range(first_chunk, first_chunk + 3):
        if rows_per_seq is None:
            specs.append(pl.BlockSpec((None, tm, d), lambda i, j=j: (layer, i, j)))
        else:
            tiles = rows_per_seq // tm
            specs.append(pl.BlockSpec((None, None, 1, d),
                                      lambda i, j=j, tiles=tiles: (layer, i // tiles, 0, j)))
    return specs


def _mod_rows(ref, rows):
    return ref[...] if ref.shape[0] == 1 else ref[rows, :]


def _row_spec(tm, width):
    return pl.BlockSpec((tm, width), lambda i: (i, 0))


class _TwoGroupGrid:
    def __init__(self, n_prompt_rows, n_sample_rows, tm_p, tm_s, seq_len):
        self.tm_p, self.tm_s = tm_p, tm_s
        self.p_steps = n_prompt_rows // tm_p
        self.s_steps = n_sample_rows // tm_s
        self.tiles_per_seq = seq_len // tm_p
        self.steps = self.p_steps + self.s_steps

    def p_idx(self, i):
        return jnp.minimum(i, self.p_steps - 1)

    def s_idx(self, i):
        return jnp.maximum(i - self.p_steps, 0)

    def rows_p(self, width):
        return pl.BlockSpec((self.tm_p, width), lambda i: (self.p_idx(i), 0))

    def rows_s(self, width, col_block=0):
        return pl.BlockSpec((self.tm_s, width), lambda i: (self.s_idx(i), col_block))

    def mods_p(self, layer, first_chunk):
        return [pl.BlockSpec((None, None, 1, D_MODEL),
                             lambda i, j=j: (layer, self.p_idx(i) // self.tiles_per_seq, 0, j))
                for j in range(first_chunk, first_chunk + 3)]

    def mods_s(self, layer, first_chunk):
        return [pl.BlockSpec((None, self.tm_s, D_MODEL),
                             lambda i, j=j: (layer, self.s_idx(i), j))
                for j in range(first_chunk, first_chunk + 3)]


def _gmlp_rows(x_ref, sh_ref, sc_ref, gt_ref, mix_ref, bs_ref, xo_ref, v_ref, z_scr, w, tm, sub):
    ng_ref, win_ref, bin_ref, lng_ref, lnb_ref, wout_ref, bout_ref = w
    groups = [slice(r0, r0 + sub) for r0 in range(0, tm, sub)]

    def project(gi):
        rows = groups[gi]
        h = _modulated_norm(x_ref[rows, :], ng_ref[...], _mod_rows(sc_ref, rows),
                            _mod_rows(sh_ref, rows))
        z_scr[gi % 2] = _dot(h, win_ref[...])

    project(0)
    for gi, rows in enumerate(groups):
        if gi + 1 < len(groups):
            project(gi + 1)
        z = _gelu_tanh(z_scr[gi % 2] + bin_ref[...])
        u = z[:, :GMLP_WIDTH]
        v = z[:, GMLP_WIDTH:]
        mu = jnp.mean(v, axis=-1, keepdims=True)
        vc = v - mu
        var = jnp.mean(vc * vc, axis=-1, keepdims=True)
        v = vc * lax.rsqrt(var + EPS) * lng_ref[...] + lnb_ref[...]
        if v_ref is not None:
            v_ref[rows, :] = v
        vb = v.astype(BF16)
        mixed = []
        for c in range(sub // CHUNK_A):
            cols = []
            for g in range(GMLP_GROUPS):
                blk = vb[c * CHUNK_A:(c + 1) * CHUNK_A, g * GMLP_GROUP_W:(g + 1) * GMLP_GROUP_W]
                cols.append(jnp.dot(mix_ref[g], blk, preferred_element_type=F32))
            mixed.append(jnp.concatenate(cols, axis=1) + bs_ref[...])
        s = jnp.concatenate(mixed, axis=0)
        y = _dot(u * s, wout_ref[...]) + bout_ref[...]
        xo_ref[rows, :] = x_ref[rows, :] + _mod_rows(gt_ref, rows) * y


def _gmlp_kernel(xp_ref, shp_ref, scp_ref, gtp_ref, xs_ref, shs_ref, scs_ref, gts_ref,
                 ng_ref, win_ref, bin_ref, lng_ref, lnb_ref, wout_ref, bout_ref,
                 mixp_ref, bsp_ref, mixs_ref, bss_ref, op_ref, os_ref, vs_ref, z_scr, *, grid):
    w = (ng_ref, win_ref, bin_ref, lng_ref, lnb_ref, wout_ref, bout_ref)
    step = pl.program_id(0)

    @pl.when(step < grid.p_steps)
    def _():
        _gmlp_rows(xp_ref, shp_ref, scp_ref, gtp_ref, mixp_ref, bsp_ref, op_ref, None, z_scr, w,
                   grid.tm_p, GMLP_SUBTILE)

    @pl.when(step >= grid.p_steps)
    def _():
        _gmlp_rows(xs_ref, shs_ref, scs_ref, gts_ref, mixs_ref, bss_ref, os_ref, vs_ref, z_scr, w,
                   grid.tm_s, GMLP_SUBTILE)


def _gmlp_call(xp, xs, mod_p, mod_s, grid, w):
    d = D_MODEL
    mix_spec = _const_spec((GMLP_GROUPS, CHUNK_A, CHUNK_A))
    bias_spec = _const_spec((CHUNK_A, GMLP_WIDTH))
    in_specs = ([grid.rows_p(d)] + grid.mods_p(0, 0) + [grid.rows_s(d)] + grid.mods_s(0, 0) + [
        _const_spec((1, d)),
        _const_spec((d, 2 * GMLP_WIDTH)),
        _const_spec((1, 2 * GMLP_WIDTH)),
        _const_spec((1, GMLP_WIDTH)),
        _const_spec((1, GMLP_WIDTH)),
        _const_spec((GMLP_WIDTH, d)),
        _const_spec((1, d)),
        mix_spec, bias_spec, mix_spec, bias_spec,
    ])
    return pl.pallas_call(
        functools.partial(_gmlp_kernel, grid=grid),
        grid=(grid.steps,),
        in_specs=in_specs,
        out_specs=[grid.rows_p(d), grid.rows_s(d), grid.rows_s(GMLP_WIDTH)],
        out_shape=[jax.ShapeDtypeStruct(xp.shape, F32), jax.ShapeDtypeStruct(xs.shape, F32),
                   jax.ShapeDtypeStruct((xs.shape[0], GMLP_WIDTH), F32)],
        scratch_shapes=[pltpu.VMEM((2, GMLP_SUBTILE, 2 * GMLP_WIDTH), F32)],
        compiler_params=pltpu.CompilerParams(
            dimension_semantics=("arbitrary",), vmem_limit_bytes=VMEM_LIMIT),
        name="gmlp_mixer",
    )(xp, mod_p, mod_p, mod_p, xs, mod_s, mod_s, mod_s, w["norm_g"], w["w_in"], w["b_in"],
      w["ln_g"], w["ln_b"], w["w_out"], w["b_out"], w["mix_p"], w["bias_p"], w["mix_s"],
      w["bias_s"])


def _ffn_rows(x_ref, sh_ref, sc_ref, gt_ref, o_ref, h_scr, w, final_norm, tm):
    ng_ref, w1_ref, b1_ref, w2_ref, b2_ref, fg_ref = w
    sub = min(FFN_SUBTILE, tm)
    groups = [slice(r0, r0 + sub) for r0 in range(0, tm, sub)]

    def normalise(gi):
        rows = groups[gi]
        h_scr[gi % 2, 0:sub, :] = _modulated_norm(
            x_ref[rows, :], ng_ref[...], _mod_rows(sc_ref, rows),
            _mod_rows(sh_ref, rows)).astype(BF16)

    normalise(0)
    for gi, rows in enumerate(groups):
        if gi + 1 < len(groups):
            normalise(gi + 1)
        h = h_scr[gi % 2, 0:sub, :]
        acc = jnp.zeros((sub, D_MODEL), F32)
        for j in range(D_FF // FFN_CHUNK):
            cols = slice(j * FFN_CHUNK, (j + 1) * FFN_CHUNK)
            a = jnp.dot(h, w1_ref[:, cols], preferred_element_type=F32) + b1_ref[:, cols]
            r = jnp.square(jnp.maximum(a, 0.0)).astype(BF16)
            acc = acc + jnp.dot(r, w2_ref[cols, :], preferred_element_type=F32)
        y = x_ref[rows, :] + _mod_rows(gt_ref, rows) * (acc + b2_ref[...])
        if final_norm:
            y = _rms(y, fg_ref[...])
        o_ref[rows, :] = y


def _ffn_kernel(xp_ref, shp_ref, scp_ref, gtp_ref, xs_ref, shs_ref, scs_ref, gts_ref,
                ng_ref, w1_ref, b1_ref, w2_ref, b2_ref, fg_ref, op_ref, os_ref, h_scr,
                *, final_norm, grid):
    w = (ng_ref, w1_ref, b1_ref, w2_ref, b2_ref, fg_ref)
    step = pl.program_id(0)

    @pl.when(step < grid.p_steps)
    def _():
        _ffn_rows(xp_ref, shp_ref, scp_ref, gtp_ref, op_ref, h_scr, w, final_norm, grid.tm_p)

    @pl.when(step >= grid.p_steps)
    def _():
        _ffn_rows(xs_ref, shs_ref, scs_ref, gts_ref, os_ref, h_scr, w, final_norm, grid.tm_s)


def _ffn_call(xp, xs, mod_p, mod_s, layer, grid, w, final_norm):
    d = D_MODEL
    in_specs = ([grid.rows_p(d)] + grid.mods_p(layer, 3) + [grid.rows_s(d)]
                + grid.mods_s(layer, 3) + [
        _layer_spec((1, d), layer),
        _layer_spec((d, D_FF), layer),
        _layer_spec((1, D_FF), layer),
        _layer_spec((D_FF, d), layer),
        _layer_spec((1, d), layer),
        _const_spec((1, d)),
    ])
    return pl.pallas_call(
        functools.partial(_ffn_kernel, final_norm=final_norm, grid=grid),
        grid=(grid.steps,),
        in_specs=in_specs,
        out_specs=[grid.rows_p(d), grid.rows_s(d)],
        out_shape=[jax.ShapeDtypeStruct(xp.shape, F32), jax.ShapeDtypeStruct(xs.shape, F32)],
        scratch_shapes=[pltpu.VMEM((2, FFN_SUBTILE, d), BF16)],
        compiler_params=pltpu.CompilerParams(
            dimension_semantics=("arbitrary",), vmem_limit_bytes=VMEM_LIMIT),
        name="sqrelu_mlp",
    )(xp, mod_p, mod_p, mod_p, xs, mod_s, mod_s, mod_s, w["norm_g"], w["w1"], w["b1"], w["w2"],
      w["b2"], w["final_g"])


def _gla_project(x, ng, scale, shift, win, wg2, bg):
    h = _modulated_norm(x, ng, scale, shift)
    proj = _dot(h, win)
    gate = _dot(proj[:, GLA_QKVG:], wg2) + bg
    log_a = _log_sigmoid(gate) * (1.0 / GLA_TAU)
    return proj[:, :GLA_QKVG], log_a


def _gla_output(o, gate_in, x, gt, og, wout):
    parts = []
    for h in range(GLA_HEADS):
        cols = slice(h * GLA_DV_HEAD, (h + 1) * GLA_DV_HEAD)
        parts.append(_rms(o[:, cols], og))
    on = jnp.concatenate(parts, axis=1)
    y = _dot(on * _silu(gate_in), wout)
    return x + gt * y


def _as_column(row):
    return jnp.broadcast_to(row, (SUBLANES, row.shape[1])).T[:, 0:1]


def _chunk_cumsum(tril, log_a):
    head = log_a.astype(BF16)
    rest = (log_a - head.astype(F32)).astype(BF16)
    return (jnp.dot(tril, head, preferred_element_type=F32)
            + jnp.dot(tril, rest, preferred_element_type=F32))


def _level_masks(c):
    rr = lax.broadcasted_iota(jnp.int32, (c, c), 0)
    cc = lax.broadcasted_iota(jnp.int32, (c, c), 1)
    masks = []
    s = c
    while s > SUBLANES:
        half = s // 2
        same = (rr ^ cc) < s
        masks.append(same & ((rr & half) != 0) & ((cc & half) == 0))
        s = half
    return masks, rr, cc


def _block_rows(x, s, r):
    c, w = x.shape
    return jnp.concatenate(
        [jnp.broadcast_to(x[p * s + r:p * s + r + 1, :], (s, w)) for p in range(c // s)], axis=0)


def _diag_terms(q, k, a):
    c, w = q.shape
    ii = lax.broadcasted_iota(jnp.int32, (c, w), 0) & (SUBLANES - 1)
    decay = jnp.zeros((c, w), F32)
    terms = [None] * SUBLANES
    for j in range(SUBLANES - 1, -1, -1):
        if j < SUBLANES - 1:
            decay = decay * _block_rows(a, SUBLANES, j + 1)
        decay = jnp.where(ii == j, 1.0, decay)
        terms[j] = q * _block_rows(k, SUBLANES, j) * decay
    return jnp.concatenate(terms, axis=0)


def _gla_chunk_pair(q2, k2, v2, la2, b2, st_refs, ones_bd, masks, rr, cc):
    c = q2[0].shape[0]
    zs = []
    for q, k, la in zip(q2, k2, la2):
        zs.append(_diag_terms(q, k, jnp.exp(la)))
    sums = jnp.dot(jnp.concatenate(zs, axis=1).astype(BF16), ones_bd,
                   preferred_element_type=F32)
    outs = []
    for hh, (q, k, v, b, st_ref) in enumerate(zip(q2, k2, v2, b2, st_refs)):
        dk = q.shape[1]
        attn = jnp.zeros((c, c), F32)
        base = rr & ~(SUBLANES - 1)
        for j in range(SUBLANES):
            rj = sums[j * c:(j + 1) * c, hh * dk:hh * dk + c]
            attn = jnp.where(cc == base + j, rj, attn)
        s = c
        for mask in masks:
            half = s // 2
            e = jnp.exp(-jnp.abs(b - _block_rows(b, s, half)))
            attn = jnp.where(mask, _dot_nt(q * e, k * e), attn)
            s = half
        b_last = b[c - 1:c, :]
        st = st_ref[...]
        o = _dot(q * jnp.exp(b), st) + _dot(attn, v)
        kd = k * jnp.exp(b_last - b)
        st_ref[...] = _as_column(jnp.exp(b_last)) * st + _dot_tn(kd, v)
        outs.append(o)
    return outs


def _gla_prompt_kernel(x_ref, sh_ref, sc_ref, gt_ref, ng_ref, win_ref, wg2_ref, bg_ref,
                       og_ref, wout_ref, tril_ref, ones_ref, xo_ref, s_ref,
                       proj_scr, ga_scr, la_scr, b_scr, o_scr, st_scr, *, tm, sub, tiles_per_seq):
    tile = pl.program_id(0) % tiles_per_seq

    @pl.when(tile == 0)
    def _():
        st_scr[...] = jnp.zeros(st_scr.shape, F32)

    dk, dv = GLA_DK_HEAD, GLA_DV_HEAD
    qscale = GLA_DK_HEAD ** -0.5
    cf = GLA_FAST_CHUNK
    c = GLA_CHUNK
    groups = [slice(r0, r0 + sub) for r0 in range(0, tm, sub)]

    def project(rows):
        h = _modulated_norm(x_ref[rows, :], ng_ref[...], _mod_rows(sc_ref, rows),
                            _mod_rows(sh_ref, rows))
        proj = _dot(h, win_ref[...])
        proj_scr[rows, :] = proj[:, :GLA_QKVG]
        ga_scr[rows, :] = proj[:, GLA_QKVG:]

    def decays(rows):
        gate = _dot(ga_scr[rows, :], wg2_ref[...]) + bg_ref[...]
        log_a = _log_sigmoid(gate) * (1.0 / GLA_TAU)
        la_scr[rows, :] = log_a
        for ci in range(sub // cf):
            b_scr[rows.start + ci * cf:rows.start + (ci + 1) * cf, :] = _chunk_cumsum(
                tril_ref[0:cf, 0:cf], log_a[ci * cf:(ci + 1) * cf, :])
        return jnp.min(b_scr[rows, :]) >= -GLA_SAFE_EXPONENT

    def head_slices(rows, h):
        q = proj_scr[rows, h * dk:(h + 1) * dk] * qscale
        k = proj_scr[rows, GLA_DK + h * dk:GLA_DK + (h + 1) * dk]
        v = proj_scr[rows, 2 * GLA_DK + h * dv:2 * GLA_DK + (h + 1) * dv]
        return q, k, v

    def single_ref_rows(group):
        rr = lax.broadcasted_iota(jnp.int32, (cf, cf), 0)
        cc = lax.broadcasted_iota(jnp.int32, (cf, cf), 1)
        causal = rr >= cc
        chunks = [slice(r0, r0 + cf) for r0 in range(group.start, group.stop, cf)]
        for h in range(GLA_HEADS):
            qes, vbs, scores, kvs, decays = [], [], [], [], []
            for rows in chunks:
                q, k, v = head_slices(rows, h)
                b = b_scr[rows, h * dk:(h + 1) * dk]
                decay_last = jnp.exp(b[cf - 1:cf, :])
                qe = (q * jnp.exp(b)).astype(BF16)
                kt = k * jnp.exp(-b)
                vb = v.astype(BF16)
                scores.append(_dot_nt(qe, kt))
                kvs.append(_dot_tn(kt * decay_last, vb))
                qes.append(qe)
                vbs.append(vb)
                decays.append(_as_column(decay_last))
            st = st_scr[h]
            states = []
            for kv, decay_col in zip(kvs, decays):
                states.append(st)
                st = decay_col * st + kv
            st_scr[h] = st
            for rows, qe, vb, sc, st_in in zip(chunks, qes, vbs, scores, states):
                attn = jnp.where(causal, sc, 0.0).astype(BF16)
                lhs = jnp.concatenate([qe, attn], axis=1)
                rhs = jnp.concatenate([st_in.astype(BF16), vb], axis=0)
                o_scr[rows, h * dv:(h + 1) * dv] = jnp.dot(lhs, rhs, preferred_element_type=F32)

    def robust_rows(group):
        def robust_body(ci, carry):
            rows = pl.ds(pl.multiple_of(group.start + ci * c, c), c)
            masks, rr, cc = _level_masks(c)
            b_c = _chunk_cumsum(tril_ref[0:c, 0:c], la_scr[rows, :])
            outs = []
            for h0 in range(0, GLA_HEADS, 2):
                heads = (h0, h0 + 1)
                qkv = [head_slices(rows, h) for h in heads]
                la2 = [la_scr[rows, h * dk:(h + 1) * dk] for h in heads]
                b2 = [b_c[:, h * dk:(h + 1) * dk] for h in heads]
                st_refs = [st_scr.at[h] for h in heads]
                outs += _gla_chunk_pair([t[0] for t in qkv], [t[1] for t in qkv],
                                        [t[2] for t in qkv], la2, b2, st_refs, ones_ref[...],
                                        masks, rr, cc)
            o_scr[rows, :] = jnp.concatenate(outs, axis=1)
            return carry

        lax.fori_loop(0, sub // c, robust_body, 0)

    def output(rows):
        gate_in = proj_scr[rows, 2 * GLA_DK + GLA_DV:]
        xo_ref[rows, :] = _gla_output(o_scr[rows, :], gate_in, x_ref[rows, :],
                                      _mod_rows(gt_ref, rows), og_ref[...], wout_ref[...])

    project(groups[0])
    for gi, rows in enumerate(groups):
        single_ref_safe = decays(rows)
        following = groups[gi + 1] if gi + 1 < len(groups) else None

        @pl.when(single_ref_safe)
        def _():
            if following is not None:
                project(following)
            single_ref_rows(rows)
            output(rows)

        @pl.when(jnp.logical_not(single_ref_safe))
        def _():
            if following is not None:
                project(following)
            robust_rows(rows)
            output(rows)

    @pl.when(tile == tiles_per_seq - 1)
    def _():
        s_ref[...] = st_scr[...]


def _gla_prompt_call(x, mod, layer, n_seq, seq_len, tm, w):
    n, d = x.shape
    tiles_per_seq = seq_len // tm
    dk, dv = GLA_DK_HEAD, GLA_DV_HEAD
    in_specs = [_row_spec(tm, d)] + _mod_specs(mod, layer, 0, tm, seq_len) + [
        _const_spec((1, d)),
        _const_spec((d, GLA_IN_PAD)),
        _const_spec((GLA_GATE_PAD, GLA_DK)),
        _const_spec((1, GLA_DK)),
        _const_spec((1, dv)),
        _const_spec((GLA_DV, d)),
        _const_spec((GLA_TRIL, GLA_TRIL)),
        _const_spec((2 * dk, 2 * dk)),
    ]
    out_specs = [
        _row_spec(tm, d),
        pl.BlockSpec((None, GLA_HEADS, dk, dv), lambda i: (i // tiles_per_seq, 0, 0, 0)),
    ]
    out_shape = [
        jax.ShapeDtypeStruct((n, d), F32),
        jax.ShapeDtypeStruct((n_seq, GLA_HEADS, dk, dv), F32),
    ]
    return pl.pallas_call(
        functools.partial(_gla_prompt_kernel, tm=tm, sub=GLA_SUBTILE,
                          tiles_per_seq=tiles_per_seq),
        grid=(n // tm,),
        in_specs=in_specs,
        out_specs=out_specs,
        out_shape=out_shape,
        scratch_shapes=[
            pltpu.VMEM((tm, GLA_QKVG), F32),
            pltpu.VMEM((tm, GLA_GATE_PAD), F32),
            pltpu.VMEM((tm, GLA_DK), F32),
            pltpu.VMEM((tm, GLA_DK), F32),
            pltpu.VMEM((tm, GLA_DV), F32),
            pltpu.VMEM((GLA_HEADS, dk, dv), F32),
        ],
        compiler_params=pltpu.CompilerParams(
            dimension_semantics=("arbitrary",), vmem_limit_bytes=VMEM_LIMIT),
        name="gla_mixer_prompt",
    )(x, mod, mod, mod, w["norm_g"], w["w_in"], w["w_gate2"], w["b_gate"], w["out_g"],
      w["w_out"], w["tril"], w["ones_bd"])


def _gla_proj_kernel(x_ref, sh_ref, sc_ref, ng_ref, win_ref, wg2_ref, bg_ref, p_ref, la_ref):
    proj, log_a = _gla_project(x_ref[...], ng_ref[...], sc_ref[...], sh_ref[...],
                               win_ref[...], wg2_ref[...], bg_ref[...])
    p_ref[...] = proj
    la_ref[...] = log_a


def _gla_proj_call(x, mod, layer, tm, w):
    n, d = x.shape
    specs = _mod_specs(mod, layer, 0, tm, None)
    in_specs = [_row_spec(tm, d), specs[0], specs[1],
                _const_spec((1, d)),
                _const_spec((d, GLA_IN_PAD)),
                _const_spec((GLA_GATE_PAD, GLA_DK)),
                _const_spec((1, GLA_DK))]
    return pl.pallas_call(
        _gla_proj_kernel,
        grid=(n // tm,),
        in_specs=in_specs,
        out_specs=[_row_spec(tm, GLA_QKVG), _row_spec(tm, GLA_DK)],
        out_shape=[jax.ShapeDtypeStruct((n, GLA_QKVG), F32),
                   jax.ShapeDtypeStruct((n, GLA_DK), F32)],
        compiler_params=pltpu.CompilerParams(
            dimension_semantics=("arbitrary",), vmem_limit_bytes=VMEM_LIMIT),
        name="gla_proj_sample",
    )(x, mod, mod, w["norm_g"], w["w_in"], w["w_gate2"], w["b_gate"])


def _gla_step_kernel(p_ref, la_ref, s0_ref, o_ref, s1_ref, *, seqs, steps):
    dk, dv = GLA_DK_HEAD, GLA_DV_HEAD
    qscale = GLA_DK_HEAD ** -0.5
    per_tile = SUBLANES // steps
    row = lax.broadcasted_iota(jnp.int32, (SUBLANES, dk), 0)

    def seq_rows(tile, s):
        if s:
            tile = pltpu.roll(tile, SUBLANES - s * steps, 0)
        keep = lax.broadcasted_iota(jnp.int32, tile.shape, 0) < steps
        return jnp.where(keep, tile, 0.0)

    def tile_body(ti, carry):
        rows = pl.ds(pl.multiple_of(ti * SUBLANES, SUBLANES), SUBLANES)
        p_tile = p_ref[rows, :]
        la_tile = la_ref[rows, :]
        o_tile = jnp.zeros((SUBLANES, GLA_DV), F32)
        for s in range(per_tile):
            bi = ti * per_tile + s
            p = seq_rows(p_tile, s)
            la_all = seq_rows(la_tile, s)
            outs = []
            for h in range(GLA_HEADS):
                q = p[:, h * dk:(h + 1) * dk] * qscale
                k = p[:, GLA_DK + h * dk:GLA_DK + (h + 1) * dk]
                v = p[:, 2 * GLA_DK + h * dv:2 * GLA_DK + (h + 1) * dv]
                la = la_all[:, h * dk:(h + 1) * dk]
                b = jnp.zeros((SUBLANES, dk), F32)
                for t in range(steps):
                    b = b + jnp.where(row >= t, la[t:t + 1, :], 0.0)
                b_last = b[steps - 1:steps, :]
                s0 = s0_ref[bi, h]
                o = _dot(q * jnp.exp(b), s0)
                for j in range(steps):
                    e = jnp.exp(jnp.minimum(b - b[j:j + 1, :], 0.0))
                    z = jnp.where(row >= j, q * k[j:j + 1, :] * e, 0.0)
                    o = o + jnp.sum(z, axis=-1, keepdims=True) * v[j:j + 1, :]
                outs.append(o)
                m = jnp.where(row == steps, jnp.exp(b_last), k * jnp.exp(b_last - b))
                m_t = m.T
                s1_ref[bi, h] = m_t[:, steps:steps + 1] * s0 + _dot(m_t, v)
            o_seq = jnp.concatenate(outs, axis=1)
            o_tile = o_tile + (pltpu.roll(o_seq, s * steps, 0) if s else o_seq)
        o_ref[rows, :] = o_tile
        return carry

    lax.fori_loop(0, seqs // per_tile, tile_body, 0)


def _gla_step_call(proj, log_a, state, steps, seqs):
    n_seq = state.shape[0]
    dk, dv = GLA_DK_HEAD, GLA_DV_HEAD
    rows = lambda width: pl.BlockSpec((seqs * steps, width), lambda i: (i, 0))
    st_spec = pl.BlockSpec((seqs, GLA_HEADS, dk, dv), lambda i: (i, 0, 0, 0))
    return pl.pallas_call(
        functools.partial(_gla_step_kernel, seqs=seqs, steps=steps),
        grid=(n_seq // seqs,),
        in_specs=[rows(GLA_QKVG), rows(GLA_DK), st_spec],
        out_specs=[rows(GLA_DV), st_spec],
        out_shape=[jax.ShapeDtypeStruct((n_seq * steps, GLA_DV), F32),
                   jax.ShapeDtypeStruct(state.shape, F32)],
        compiler_params=pltpu.CompilerParams(
            dimension_semantics=("arbitrary",), vmem_limit_bytes=VMEM_LIMIT),
        name="gla_step_sample",
    )(proj, log_a, state)


def _gla_out_kernel(o_ref, p_ref, x_ref, gt_ref, og_ref, wout_ref, xo_ref):
    xo_ref[...] = _gla_output(o_ref[...], p_ref[...], x_ref[...], gt_ref[...],
                              og_ref[...], wout_ref[...])


def _gla_out_call(o, proj, x, mod, layer, tm, w):
    n, d = x.shape
    gate_spec = _mod_specs(mod, layer, 0, tm, None)[2]
    g_block = (2 * GLA_DK + GLA_DV) // GLA_DV
    in_specs = [_row_spec(tm, GLA_DV),
                pl.BlockSpec((tm, GLA_DV), lambda i: (i, g_block)),
                _row_spec(tm, d), gate_spec,
                _const_spec((1, GLA_DV_HEAD)),
                _const_spec((GLA_DV, d))]
    return pl.pallas_call(
        _gla_out_kernel,
        grid=(n // tm,),
        in_specs=in_specs,
        out_specs=_row_spec(tm, d),
        out_shape=jax.ShapeDtypeStruct((n, d), F32),
        compiler_params=pltpu.CompilerParams(
            dimension_semantics=("arbitrary",), vmem_limit_bytes=VMEM_LIMIT),
        name="gla_out_sample",
    )(o, proj, x, mod, w["out_g"], w["w_out"])


PROMPT_FFN_TILE = 1024
PROMPT_GMLP_TILE = 1024
PROMPT_GLA_TILE = 1024
SAMPLE_TILE = 256
SAMPLE_SEQS_PER_STEP = 16


def kernel(x_prompt, x_sample, c_prompt, c_sample, state_gla, ada_w, ada_b, norm_mix_g, norm_ffn_g,
           ffn_w1, ffn_b1, ffn_w2, ffn_b2, gmlp_w_in, gmlp_b_in, gmlp_ln_g, gmlp_ln_b, gmlp_w_s,
           gmlp_b_s, gmlp_w_out, gmlp_b_out, gla_w_in, gla_w_gate2, gla_b_gate, gla_norm_g,
           gla_w_out, final_norm_g):
    n_seq_p, seq_p, d = x_prompt.shape
    n_seq_s, seq_s, _ = x_sample.shape
    assert d == D_MODEL and GMLP_SUBTILE % CHUNK_A == 0 and GLA_SUBTILE % GLA_FAST_CHUNK == 0
    assert all(seq_p % t == 0 for t in (PROMPT_FFN_TILE, PROMPT_GMLP_TILE, PROMPT_GLA_TILE))
    assert seq_s < SUBLANES and SUBLANES % seq_s == 0 and (n_seq_s * seq_s) % SAMPLE_TILE == 0
    row = lambda a: a.reshape(1, -1)

    mod_p, mod_s = _ada_call(c_prompt, jnp.repeat(c_sample, seq_s, axis=0), ada_w, ada_b)
    mod_p = mod_p.reshape(mod_p.shape[0], n_seq_p, 1, N_MOD * d)

    causal = np.tril(np.ones((CHUNK_A, CHUNK_A), dtype=bool))
    ws = jnp.where(causal[None], gmlp_w_s, jnp.zeros_like(gmlp_w_s))
    reps = CHUNK_A // seq_s
    eye = np.eye(reps, dtype=np.float32)
    ws_s = jnp.einsum("ab,gij->gaibj", eye, ws[:, :seq_s, :seq_s]).reshape(
        GMLP_GROUPS, CHUNK_A, CHUNK_A)
    bias_p = jnp.repeat(gmlp_b_s.T, GMLP_GROUP_W, axis=1)
    bias_s = jnp.tile(bias_p[:seq_s], (reps, 1))

    gmlp_w = dict(norm_g=row(norm_mix_g[0]), w_in=gmlp_w_in.astype(BF16), b_in=row(gmlp_b_in),
                  ln_g=row(gmlp_ln_g), ln_b=row(gmlp_ln_b), w_out=gmlp_w_out.astype(BF16),
                  b_out=row(gmlp_b_out))
    gmlp_w = dict(gmlp_w, mix_p=ws.astype(BF16), bias_p=bias_p, mix_s=ws_s.astype(BF16),
                  bias_s=bias_s)

    depth = ffn_w1.shape[0]
    ffn_w = dict(norm_g=norm_ffn_g.reshape(depth, 1, d), w1=ffn_w1.astype(BF16),
                 b1=ffn_b1.reshape(depth, 1, D_FF), w2=ffn_w2.astype(BF16),
                 b2=ffn_b2.reshape(depth, 1, d), final_g=row(final_norm_g))

    w_in_pad = jnp.pad(gla_w_in, ((0, 0), (0, GLA_IN_PAD - gla_w_in.shape[1]))).astype(BF16)
    wg2_pad = jnp.pad(gla_w_gate2, ((0, GLA_GATE_PAD - GLA_GATE_RANK), (0, 0))).astype(BF16)
    blk_ones = np.kron(np.eye(2, dtype=np.float32),
                       np.ones((GLA_DK_HEAD, GLA_DK_HEAD), np.float32))
    gla_w = dict(norm_g=row(norm_mix_g[1]), w_in=w_in_pad, w_gate2=wg2_pad, b_gate=row(gla_b_gate),
                 out_g=row(gla_norm_g), w_out=gla_w_out.astype(BF16),
                 tril=jnp.asarray(np.tril(np.ones((GLA_TRIL, GLA_TRIL), np.float32)), BF16),
                 ones_bd=jnp.asarray(blk_ones, BF16))

    xp = x_prompt.reshape(n_seq_p * seq_p, d)
    xs = x_sample.reshape(n_seq_s * seq_s, d)
    grid_for = lambda tm_p: _TwoGroupGrid(xp.shape[0], xs.shape[0], tm_p, SAMPLE_TILE, seq_p)
    xp, xs, chunk_v = _gmlp_call(xp, xs, mod_p, mod_s, grid_for(PROMPT_GMLP_TILE), gmlp_w)
    xp, xs = _ffn_call(xp, xs, mod_p, mod_s, 0, grid_for(PROMPT_FFN_TILE), ffn_w, final_norm=False)
    xp, state_p = _gla_prompt_call(xp, mod_p, 1, n_seq_p, seq_p, PROMPT_GLA_TILE, gla_w)
    proj, log_a = _gla_proj_call(xs, mod_s, 1, SAMPLE_TILE, gla_w)
    o, state_s = _gla_step_call(proj, log_a, state_gla, seq_s, SAMPLE_SEQS_PER_STEP)
    xs = _gla_out_call(o, proj, xs, mod_s, 1, SAMPLE_TILE, gla_w)
    xp, xs = _ffn_call(xp, xs, mod_p, mod_s, 1, grid_for(PROMPT_FFN_TILE), ffn_w, final_norm=True)

    return (xp.reshape(x_prompt.shape), xs.reshape(x_sample.shape), state_p, state_s,
            chunk_v.reshape(n_seq_s, seq_s, GMLP_WIDTH))
```

```python
import functools

import jax
import jax.numpy as jnp
import numpy as np
from jax import lax
from jax.experimental import pallas as pl
from jax.experimental.pallas import tpu as pltpu

F32 = jnp.float32
BF16 = jnp.bfloat16

D_MODEL = 1024
N_MOD = 6
CHUNK_A = 128
GMLP_WIDTH = D_MODEL
GMLP_GROUPS = 4
GMLP_GROUP_W = GMLP_WIDTH // GMLP_GROUPS
GLA_HEADS = 4
GLA_DK = D_MODEL // 2
GLA_DV = D_MODEL
GLA_DK_HEAD = GLA_DK // GLA_HEADS
GLA_DV_HEAD = GLA_DV // GLA_HEADS
GLA_GATE_RANK = 16
GLA_TAU = 16.0
GLA_CHUNK = 64
GLA_FAST_CHUNK = 128
GLA_TRIL = max(GLA_CHUNK, GLA_FAST_CHUNK)
GLA_SAFE_EXPONENT = 80.0
GLA_QKVG = 2 * GLA_DK + 2 * GLA_DV
GLA_GATE_PAD = 128
GLA_IN_PAD = GLA_QKVG + GLA_GATE_PAD
D_FF = 4 * D_MODEL
EPS = 1e-6

SUBLANES = 8
BF16_SUBLANES = 16
ADA_ROWS = 256
FFN_CHUNK = 1024
FFN_SUBTILE = 512
GMLP_SUBTILE = 256
GLA_SUBTILE = 512
VMEM_LIMIT = 56 * 1024 * 1024
WEIGHT_STAGE_BYTES = 1024 * 1024


def _rms(x, g):
    return x * lax.rsqrt(jnp.mean(x * x, axis=-1, keepdims=True) + EPS) * g


def _modulated_norm(x, g, scale, shift):
    inv = lax.rsqrt(jnp.mean(x * x, axis=-1, keepdims=True) + EPS)
    return (x * inv) * (g * (1.0 + scale)) + shift


def _dot(a, b):
    return jnp.dot(a.astype(BF16), b.astype(BF16), preferred_element_type=F32)


def _dot_nt(a, b):
    return lax.dot_general(a.astype(BF16), b.astype(BF16), (((1,), (1,)), ((), ())),
                           preferred_element_type=F32)


def _dot_tn(a, b):
    return lax.dot_general(a.astype(BF16), b.astype(BF16), (((0,), (0,)), ((), ())),
                           preferred_element_type=F32)


def _gelu_tanh(x):
    c1 = -2.0 * 0.7978845608028654 * 1.4426950408889634
    c2 = c1 * 0.044715
    return x / (1.0 + jnp.exp2(x * (c1 + c2 * (x * x))))


def _silu(x):
    return x * jax.nn.sigmoid(x)


def _log_sigmoid(x):
    return -(jnp.maximum(-x, 0.0) + jnp.log1p(jnp.exp(-jnp.abs(x))))


def _bf16_terms(x, n):
    terms = []
    for _ in range(n - 1):
        t = x.astype(BF16)
        terms.append(t)
        x = x - t.astype(F32)
    terms.append(x.astype(BF16))
    return terms


def _ada_kernel(c_ref, w_ref, b_ref, e_ref, op_ref, os_ref, s_scr, acc_scr, *, n_prompt, tk):
    k = pl.program_id(1)
    n_seq = e_ref.shape[1] // 2

    @pl.when((pl.program_id(0) == 0) & (k == 0))
    def _():
        for kk in range(s_scr.shape[0]):
            s_scr[kk] = _silu(c_ref[:, kk * tk:(kk + 1) * tk]).astype(BF16)

    @pl.when(k == 0)
    def _():
        acc_scr[...] = jnp.broadcast_to(b_ref[...], acc_scr.shape)

    s = s_scr[k]
    for j in range(N_MOD):
        cols = slice(j * D_MODEL, (j + 1) * D_MODEL)
        acc_scr[:, cols] += jnp.dot(s, w_ref[:, cols].astype(BF16), preferred_element_type=F32)

    @pl.when(k == pl.num_programs(1) - 1)
    def _():
        op_ref[...] = acc_scr[0:n_prompt, :]
        for j in range(N_MOD):
            cols = slice(j * D_MODEL, (j + 1) * D_MODEL)
            terms = _bf16_terms(acc_scr[n_prompt:n_prompt + n_seq, cols], 2)
            os_ref[:, cols] = jnp.dot(e_ref[...], jnp.concatenate(terms, axis=0),
                                      preferred_element_type=F32)


def _ada_call(c_prompt, c_sample, tokens_per_seq, ada_w, ada_b):
    depth = ada_w.shape[0]
    n_prompt, n_seq = c_prompt.shape[0], c_sample.shape[0]
    n_sample = n_seq * tokens_per_seq
    d = D_MODEL
    pad = -(n_prompt + n_seq) % BF16_SUBLANES
    c_all = jnp.concatenate([c_prompt, c_sample, jnp.zeros((pad, d), F32)], axis=0)
    rows = c_all.shape[0]
    expand = np.tile(np.repeat(np.eye(n_seq, dtype=np.float32), tokens_per_seq, axis=0), (1, 2))
    tk = ADA_ROWS
    width = N_MOD * d
    return pl.pallas_call(
        functools.partial(_ada_kernel, n_prompt=n_prompt, tk=tk),
        grid=(depth, d // tk),
        in_specs=[
            pl.BlockSpec((rows, d), lambda l, k: (0, 0)),
            pl.BlockSpec((None, tk, width), lambda l, k: (l, k, 0)),
            pl.BlockSpec((None, 1, width), lambda l, k: (l, 0, 0)),
            pl.BlockSpec((n_sample, 2 * n_seq), lambda l, k: (0, 0)),
        ],
        out_specs=[
            pl.BlockSpec((None, n_prompt, width), lambda l, k: (l, 0, 0)),
            pl.BlockSpec((None, n_sample, width), lambda l, k: (l, 0, 0)),
        ],
        out_shape=[
            jax.ShapeDtypeStruct((depth, n_prompt, width), F32),
            jax.ShapeDtypeStruct((depth, n_sample, width), F32),
        ],
        scratch_shapes=[pltpu.VMEM((d // tk, rows, tk), BF16),
                        pltpu.VMEM((rows, width), F32)],
        compiler_params=pltpu.CompilerParams(
            dimension_semantics=("arbitrary", "arbitrary"), vmem_limit_bytes=VMEM_LIMIT),
        name="adaln_mod",
    )(c_all, ada_w, ada_b.reshape(depth, 1, N_MOD * d), jnp.asarray(expand, BF16))


def _const_spec(shape):
    zeros = (0,) * len(shape)
    return pl.BlockSpec(shape, lambda i: zeros, pipeline_mode=pl.Buffered(1))


def _layer_spec(shape, layer):
    zeros = (0,) * len(shape)
    return pl.BlockSpec((None,) + shape, lambda i: (layer,) + zeros,
                        pipeline_mode=pl.Buffered(1))


def _mod_specs(mod, layer, first_chunk, tm, rows_per_seq):
    d = D_MODEL
    specs = []
    for j in range(first_chunk, first_chunk + 3):
        if rows_per_seq is None:
            specs.append(pl.BlockSpec((None, tm, d), lambda i, j=j: (layer, i, j)))
        else:
            tiles = rows_per_seq // tm
            specs.append(pl.BlockSpec((None, None, 1, d),
                                      lambda i, j=j, tiles=tiles: (layer, i // tiles, 0, j)))
    return specs


def _mod_rows(ref, rows):
    return ref[...] if ref.shape[0] == 1 else ref[rows, :]


def _row_spec(tm, width):
    return pl.BlockSpec((tm, width), lambda i: (i, 0))


class _TwoGroupGrid:
    def __init__(self, n_prompt_rows, n_sample_rows, tm_p, tm_s, seq_len):
        self.tm_p, self.tm_s = tm_p, tm_s
        self.p_steps = n_prompt_rows // tm_p
        self.s_steps = n_sample_rows // tm_s
        self.tiles_per_seq = seq_len // tm_p
        self.steps = self.p_steps + self.s_steps

    def p_idx(self, i):
        return jnp.minimum(i, self.p_steps - 1)

    def s_idx(self, i):
        return jnp.maximum(i - self.p_steps, 0)

    def rows_p(self, width):
        return pl.BlockSpec((self.tm_p, width), lambda i: (self.p_idx(i), 0))

    def rows_s(self, width, col_block=0):
        return pl.BlockSpec((self.tm_s, width), lambda i: (self.s_idx(i), col_block))

    def mods_p(self, layer, first_chunk):
        return [pl.BlockSpec((None, None, 1, D_MODEL),
                             lambda i, j=j: (layer, self.p_idx(i) // self.tiles_per_seq, 0, j))
                for j in range(first_chunk, first_chunk + 3)]

    def mods_s(self, layer, first_chunk):
        return [pl.BlockSpec((None, self.tm_s, D_MODEL),
                             lambda i, j=j: (layer, self.s_idx(i), j))
                for j in range(first_chunk, first_chunk + 3)]


def _gmlp_rows(x_ref, sh_ref, sc_ref, gt_ref, mix_ref, bs_ref, xo_ref, v_ref, z_scr, w, tm, sub):
    ng_ref, win_ref, bin_ref, lng_ref, lnb_ref, wout_ref, bout_ref = w
    groups = [slice(r0, r0 + sub) for r0 in range(0, tm, sub)]

    def project(gi):
        rows = groups[gi]
        h = _modulated_norm(x_ref[rows, :], ng_ref[...], _mod_rows(sc_ref, rows),
                            _mod_rows(sh_ref, rows))
        z_scr[gi % 2] = _dot(h, win_ref[...])

    project(0)
    for gi, rows in enumerate(groups):
        if gi + 1 < len(groups):
            project(gi + 1)
        z = _gelu_tanh(z_scr[gi % 2] + bin_ref[...])
        u = z[:, :GMLP_WIDTH]
        v = z[:, GMLP_WIDTH:]
        mu = jnp.mean(v, axis=-1, keepdims=True)
        vc = v - mu
        var = jnp.mean(vc * vc, axis=-1, keepdims=True)
        v = vc * lax.rsqrt(var + EPS) * lng_ref[...] + lnb_ref[...]
        if v_ref is not None:
            v_ref[rows, :] = v
        vb = v.astype(BF16)
        mixed = []
        for c in range(sub // CHUNK_A):
            cols = []
            for g in range(GMLP_GROUPS):
                blk = vb[c * CHUNK_A:(c + 1) * CHUNK_A, g * GMLP_GROUP_W:(g + 1) * GMLP_GROUP_W]
                cols.append(jnp.dot(mix_ref[g], blk, preferred_element_type=F32))
            mixed.append(jnp.concatenate(cols, axis=1) + bs_ref[...])
        s = jnp.concatenate(mixed, axis=0)
        y = _dot(u * s, wout_ref[...]) + bout_ref[...]
        xo_ref[rows, :] = x_ref[rows, :] + _mod_rows(gt_ref, rows) * y


def _gmlp_kernel(xp_ref, shp_ref, scp_ref, gtp_ref, xs_ref, shs_ref, scs_ref, gts_ref,
                 ng_ref, win_hbm, bin_ref, lng_ref, lnb_ref, wout_hbm, bout_ref,
                 mixp_ref, bsp_ref, mixs_ref, bss_ref, op_ref, os_ref, vs_ref, z_scr,
                 win_ref, wout_ref, stage_in, stage_out, sems, *, grid):
    w = (ng_ref, win_ref, bin_ref, lng_ref, lnb_ref, wout_ref, bout_ref)
    step = pl.program_id(0)

    @pl.when(step == 0)
    def _():
        _load_as_bf16(win_hbm, win_ref, stage_in, sems.at[0])
        _load_as_bf16(wout_hbm, wout_ref, stage_out, sems.at[1])

    @pl.when(step < grid.p_steps)
    def _():
        _gmlp_rows(xp_ref, shp_ref, scp_ref, gtp_ref, mixp_ref, bsp_ref, op_ref, None, z_scr, w,
                   grid.tm_p, GMLP_SUBTILE)

    @pl.when(step >= grid.p_steps)
    def _():
        _gmlp_rows(xs_ref, shs_ref, scs_ref, gts_ref, mixs_ref, bss_ref, os_ref, vs_ref, z_scr, w,
                   grid.tm_s, GMLP_SUBTILE)


def _gmlp_call(xp, xs, mod_p, mod_s, grid, w):
    d = D_MODEL
    mix_spec = _const_spec((GMLP_GROUPS, CHUNK_A, CHUNK_A))
    bias_spec = _const_spec((CHUNK_A, GMLP_WIDTH))
    in_specs = ([grid.rows_p(d)] + grid.mods_p(0, 0) + [grid.rows_s(d)] + grid.mods_s(0, 0) + [
        _const_spec((1, d)),
        pl.BlockSpec(memory_space=pl.ANY),
        _const_spec((1, 2 * GMLP_WIDTH)),
        _const_spec((1, GMLP_WIDTH)),
        _const_spec((1, GMLP_WIDTH)),
        pl.BlockSpec(memory_space=pl.ANY),
        _const_spec((1, d)),
        mix_spec, bias_spec, mix_spec, bias_spec,
    ])
    stage_rows = lambda width: WEIGHT_STAGE_BYTES // (4 * width)
    return pl.pallas_call(
        functools.partial(_gmlp_kernel, grid=grid),
        grid=(grid.steps,),
        in_specs=in_specs,
        out_specs=[grid.rows_p(d), grid.rows_s(d), grid.rows_s(GMLP_WIDTH)],
        out_shape=[jax.ShapeDtypeStruct(xp.shape, F32), jax.ShapeDtypeStruct(xs.shape, F32),
                   jax.ShapeDtypeStruct((xs.shape[0], GMLP_WIDTH), F32)],
        scratch_shapes=[pltpu.VMEM((2, GMLP_SUBTILE, 2 * GMLP_WIDTH), F32),
                        pltpu.VMEM((d, 2 * GMLP_WIDTH), BF16),
                        pltpu.VMEM((GMLP_WIDTH, d), BF16),
                        pltpu.VMEM((2, stage_rows(2 * GMLP_WIDTH), 2 * GMLP_WIDTH), F32),
                        pltpu.VMEM((2, stage_rows(d), d), F32),
                        pltpu.SemaphoreType.DMA((2, 2))],
        compiler_params=pltpu.CompilerParams(
            dimension_semantics=("arbitrary",), vmem_limit_bytes=VMEM_LIMIT),
        name="gmlp_mixer",
    )(xp, mod_p, mod_p, mod_p, xs, mod_s, mod_s, mod_s, w["norm_g"], w["w_in"], w["b_in"],
      w["ln_g"], w["ln_b"], w["w_out"], w["b_out"], w["mix_p"], w["bias_p"], w["mix_s"],
      w["bias_s"])


def _ffn_rows(x_ref, sh_ref, sc_ref, gt_ref, o_ref, h_scr, w, final_norm, tm):
    ng_ref, w1_ref, b1_ref, w2_ref, b2_ref, fg_ref = w
    sub = min(FFN_SUBTILE, tm)
    groups = [slice(r0, r0 + sub) for r0 in range(0, tm, sub)]

    def normalise(gi):
        rows = groups[gi]
        h_scr[gi % 2, 0:sub, :] = _modulated_norm(
            x_ref[rows, :], ng_ref[...], _mod_rows(sc_ref, rows),
            _mod_rows(sh_ref, rows)).astype(BF16)

    normalise(0)
    for gi, rows in enumerate(groups):
        if gi + 1 < len(groups):
            normalise(gi + 1)
        h = h_scr[gi % 2, 0:sub, :]
        acc = jnp.zeros((sub, D_MODEL), F32)
        for j in range(D_FF // FFN_CHUNK):
            cols = slice(j * FFN_CHUNK, (j + 1) * FFN_CHUNK)
            a = jnp.dot(h, w1_ref[:, cols], preferred_element_type=F32) + b1_ref[:, cols]
            r = jnp.square(jnp.maximum(a, 0.0)).astype(BF16)
            acc = acc + jnp.dot(r, w2_ref[cols, :], preferred_element_type=F32)
        y = x_ref[rows, :] + _mod_rows(gt_ref, rows) * (acc + b2_ref[...])
        if final_norm:
            y = _rms(y, fg_ref[...])
        o_ref[rows, :] = y


def _load_as_bf16(src_hbm, dst_ref, stage_ref, sem_ref):
    rows_per_chunk = stage_ref.shape[1]
    n_chunks = dst_ref.shape[0] // rows_per_chunk

    def chunk_copy(c):
        rows = pl.ds(c * rows_per_chunk, rows_per_chunk)
        return pltpu.make_async_copy(src_hbm.at[rows, :], stage_ref.at[c % 2], sem_ref.at[c % 2])

    chunk_copy(0).start()
    for c in range(n_chunks):
        if c + 1 < n_chunks:
            chunk_copy(c + 1).start()
        chunk_copy(c).wait()
        dst_ref[c * rows_per_chunk:(c + 1) * rows_per_chunk, :] = stage_ref[c % 2].astype(BF16)


def _ffn_kernel(xp_ref, shp_ref, scp_ref, gtp_ref, xs_ref, shs_ref, scs_ref, gts_ref,
                ng_ref, w1_hbm, b1_ref, w2_hbm, b2_ref, fg_ref, op_ref, os_ref, h_scr,
                w1_ref, w2_ref, stage1, stage2, sems, *, final_norm, grid, layer):
    w = (ng_ref, w1_ref, b1_ref, w2_ref, b2_ref, fg_ref)
    step = pl.program_id(0)

    @pl.when(step == 0)
    def _():
        _load_as_bf16(w1_hbm.at[layer], w1_ref, stage1, sems.at[0])
        _load_as_bf16(w2_hbm.at[layer], w2_ref, stage2, sems.at[1])

    @pl.when(step < grid.p_steps)
    def _():
        _ffn_rows(xp_ref, shp_ref, scp_ref, gtp_ref, op_ref, h_scr, w, final_norm, grid.tm_p)

    @pl.when(step >= grid.p_steps)
    def _():
        _ffn_rows(xs_ref, shs_ref, scs_ref, gts_ref, os_ref, h_scr, w, final_norm, grid.tm_s)


def _ffn_call(xp, xs, mod_p, mod_s, layer, grid, w, final_norm):
    d = D_MODEL
    in_specs = ([grid.rows_p(d)] + grid.mods_p(layer, 3) + [grid.rows_s(d)]
                + grid.mods_s(layer, 3) + [
        _layer_spec((1, d), layer),
        pl.BlockSpec(memory_space=pl.ANY),
        _layer_spec((1, D_FF), layer),
        pl.BlockSpec(memory_space=pl.ANY),
        _layer_spec((1, d), layer),
        _const_spec((1, d)),
    ])
    stage_rows = lambda width: WEIGHT_STAGE_BYTES // (4 * width)
    return pl.pallas_call(
        functools.partial(_ffn_kernel, final_norm=final_norm, grid=grid, layer=layer),
        grid=(grid.steps,),
        in_specs=in_specs,
        out_specs=[grid.rows_p(d), grid.rows_s(d)],
        out_shape=[jax.ShapeDtypeStruct(xp.shape, F32), jax.ShapeDtypeStruct(xs.shape, F32)],
        scratch_shapes=[pltpu.VMEM((2, FFN_SUBTILE, d), BF16),
                        pltpu.VMEM((d, D_FF), BF16),
                        pltpu.VMEM((D_FF, d), BF16),
                        pltpu.VMEM((2, stage_rows(D_FF), D_FF), F32),
                        pltpu.VMEM((2, stage_rows(d), d), F32),
                        pltpu.SemaphoreType.DMA((2, 2))],
        compiler_params=pltpu.CompilerParams(
            dimension_semantics=("arbitrary",), vmem_limit_bytes=VMEM_LIMIT),
        name="sqrelu_mlp",
    )(xp, mod_p, mod_p, mod_p, xs, mod_s, mod_s, mod_s, w["norm_g"], w["w1"], w["b1"], w["w2"],
      w["b2"], w["final_g"])


def _gla_project(x, ng, scale, shift, win, wg2, bg):
    h = _modulated_norm(x, ng, scale, shift)
    proj = _dot(h, win)
    gate = _dot(proj[:, GLA_QKVG:], wg2) + bg
    log_a = _log_sigmoid(gate) * (1.0 / GLA_TAU)
    return proj[:, :GLA_QKVG], log_a


def _gla_output(o, gate_in, x, gt, og, wout):
    parts = []
    for h in range(GLA_HEADS):
        cols = slice(h * GLA_DV_HEAD, (h + 1) * GLA_DV_HEAD)
        parts.append(_rms(o[:, cols], og))
    on = jnp.concatenate(parts, axis=1)
    y = _dot(on * _silu(gate_in), wout)
    return x + gt * y


def _as_column(row):
    return jnp.broadcast_to(row, (SUBLANES, row.shape[1])).T[:, 0:1]


def _chunk_cumsum(tril, log_a):
    head = log_a.astype(BF16)
    rest = (log_a - head.astype(F32)).astype(BF16)
    return (jnp.dot(tril, head, preferred_element_type=F32)
            + jnp.dot(tril, rest, preferred_element_type=F32))


def _level_masks(c):
    rr = lax.broadcasted_iota(jnp.int32, (c, c), 0)
    cc = lax.broadcasted_iota(jnp.int32, (c, c), 1)
    masks = []
    s = c
    while s > SUBLANES:
        half = s // 2
        same = (rr ^ cc) < s
        masks.append(same & ((rr & half) != 0) & ((cc & half) == 0))
        s = half
    return masks, rr, cc


def _block_rows(x, s, r):
    c, w = x.shape
    return jnp.concatenate(
        [jnp.broadcast_to(x[p * s + r:p * s + r + 1, :], (s, w)) for p in range(c // s)], axis=0)


def _diag_terms(q, k, a):
    c, w = q.shape
    ii = lax.broadcasted_iota(jnp.int32, (c, w), 0) & (SUBLANES - 1)
    decay = jnp.zeros((c, w), F32)
    terms = [None] * SUBLANES
    for j in range(SUBLANES - 1, -1, -1):
        if j < SUBLANES - 1:
            decay = decay * _block_rows(a, SUBLANES, j + 1)
        decay = jnp.where(ii == j, 1.0, decay)
        terms[j] = q * _block_rows(k, SUBLANES, j) * decay
    return jnp.concatenate(terms, axis=0)


def _gla_chunk_pair(q2, k2, v2, la2, b2, st_refs, ones_bd, masks, rr, cc):
    c = q2[0].shape[0]
    zs = []
    for q, k, la in zip(q2, k2, la2):
        zs.append(_diag_terms(q, k, jnp.exp(la)))
    sums = jnp.dot(jnp.concatenate(zs, axis=1).astype(BF16), ones_bd,
                   preferred_element_type=F32)
    outs = []
    for hh, (q, k, v, b, st_ref) in enumerate(zip(q2, k2, v2, b2, st_refs)):
        dk = q.shape[1]
        attn = jnp.zeros((c, c), F32)
        base = rr & ~(SUBLANES - 1)
        for j in range(SUBLANES):
            rj = sums[j * c:(j + 1) * c, hh * dk:hh * dk + c]
            attn = jnp.where(cc == base + j, rj, attn)
        s = c
        for mask in masks:
            half = s // 2
            e = jnp.exp(-jnp.abs(b - _block_rows(b, s, half)))
            attn = jnp.where(mask, _dot_nt(q * e, k * e), attn)
            s = half
        b_last = b[c - 1:c, :]
        st = st_ref[...]
        o = _dot(q * jnp.exp(b), st) + _dot(attn, v)
        kd = k * jnp.exp(b_last - b)
        st_ref[...] = _as_column(jnp.exp(b_last)) * st + _dot_tn(kd, v)
        outs.append(o)
    return outs


def _gla_prompt_kernel(x_ref, sh_ref, sc_ref, gt_ref, ng_ref, win_ref, wg2_ref, bg_ref,
                       og_ref, wout_ref, tril_ref, ones_ref, xo_ref, s_ref,
                       proj_scr, ga_scr, la_scr, b_scr, o_scr, st_scr, *, tm, sub, tiles_per_seq):
    tile = pl.program_id(0) % tiles_per_seq

    @pl.when(tile == 0)
    def _():
        st_scr[...] = jnp.zeros(st_scr.shape, F32)

    dk, dv = GLA_DK_HEAD, GLA_DV_HEAD
    qscale = GLA_DK_HEAD ** -0.5
    cf = GLA_FAST_CHUNK
    c = GLA_CHUNK
    groups = [slice(r0, r0 + sub) for r0 in range(0, tm, sub)]

    def project(rows):
        h = _modulated_norm(x_ref[rows, :], ng_ref[...], _mod_rows(sc_ref, rows),
                            _mod_rows(sh_ref, rows))
        proj = _dot(h, win_ref[...])
        proj_scr[rows, :] = proj[:, :GLA_QKVG]
        ga_scr[rows, :] = proj[:, GLA_QKVG:]

    def decays(rows):
        gate = _dot(ga_scr[rows, :], wg2_ref[...]) + bg_ref[...]
        log_a = _log_sigmoid(gate) * (1.0 / GLA_TAU)
        la_scr[rows, :] = log_a
        for ci in range(sub // cf):
            b_scr[rows.start + ci * cf:rows.start + (ci + 1) * cf, :] = _chunk_cumsum(
                tril_ref[0:cf, 0:cf], log_a[ci * cf:(ci + 1) * cf, :])
        return jnp.min(b_scr[rows, :]) >= -GLA_SAFE_EXPONENT

    def head_slices(rows, h):
        q = proj_scr[rows, h * dk:(h + 1) * dk] * qscale
        k = proj_scr[rows, GLA_DK + h * dk:GLA_DK + (h + 1) * dk]
        v = proj_scr[rows, 2 * GLA_DK + h * dv:2 * GLA_DK + (h + 1) * dv]
        return q, k, v

    def single_ref_rows(group):
        rr = lax.broadcasted_iota(jnp.int32, (cf, cf), 0)
        cc = lax.broadcasted_iota(jnp.int32, (cf, cf), 1)
        causal = rr >= cc
        chunks = [slice(r0, r0 + cf) for r0 in range(group.start, group.stop, cf)]
        for h in range(GLA_HEADS):
            qes, vbs, scores, kvs, decays = [], [], [], [], []
            for rows in chunks:
                q, k, v = head_slices(rows, h)
                b = b_scr[rows, h * dk:(h + 1) * dk]
                decay_last = jnp.exp(b[cf - 1:cf, :])
                qe = (q * jnp.exp(b)).astype(BF16)
                kt = k * jnp.exp(-b)
                vb = v.astype(BF16)
                scores.append(_dot_nt(qe, kt))
                kvs.append(_dot_tn(kt * decay_last, vb))
                qes.append(qe)
                vbs.append(vb)
                decays.append(_as_column(decay_last))
            st = st_scr[h]
            states = []
            for kv, decay_col in zip(kvs, decays):
                states.append(st)
                st = decay_col * st + kv
            st_scr[h] = st
            for rows, qe, vb, sc, st_in in zip(chunks, qes, vbs, scores, states):
                attn = jnp.where(causal, sc, 0.0).astype(BF16)
                lhs = jnp.concatenate([qe, attn], axis=1)
                rhs = jnp.concatenate([st_in.astype(BF16), vb], axis=0)
                o_scr[rows, h * dv:(h + 1) * dv] = jnp.dot(lhs, rhs, preferred_element_type=F32)

    def robust_rows(group):
        def robust_body(ci, carry):
            rows = pl.ds(pl.multiple_of(group.start + ci * c, c), c)
            masks, rr, cc = _level_masks(c)
            b_c = _chunk_cumsum(tril_ref[0:c, 0:c], la_scr[rows, :])
            outs = []
            for h0 in range(0, GLA_HEADS, 2):
                heads = (h0, h0 + 1)
                qkv = [head_slices(rows, h) for h in heads]
                la2 = [la_scr[rows, h * dk:(h + 1) * dk] for h in heads]
                b2 = [b_c[:, h * dk:(h + 1) * dk] for h in heads]
                st_refs = [st_scr.at[h] for h in heads]
                outs += _gla_chunk_pair([t[0] for t in qkv], [t[1] for t in qkv],
                                        [t[2] for t in qkv], la2, b2, st_refs, ones_ref[...],
                                        masks, rr, cc)
            o_scr[rows, :] = jnp.concatenate(outs, axis=1)
            return carry

        lax.fori_loop(0, sub // c, robust_body, 0)

    def output(rows):
        gate_in = proj_scr[rows, 2 * GLA_DK + GLA_DV:]
        xo_ref[rows, :] = _gla_output(o_scr[rows, :], gate_in, x_ref[rows, :],
                                      _mod_rows(gt_ref, rows), og_ref[...], wout_ref[...])

    project(groups[0])
    for gi, rows in enumerate(groups):
        single_ref_safe = decays(rows)
        following = groups[gi + 1] if gi + 1 < len(groups) else None

        @pl.when(single_ref_safe)
        def _():
            if following is not None:
                project(following)
            single_ref_rows(rows)
            output(rows)

        @pl.when(jnp.logical_not(single_ref_safe))
        def _():
            if following is not None:
                project(following)
            robust_rows(rows)
            output(rows)

    @pl.when(tile == tiles_per_seq - 1)
    def _():
        s_ref[...] = st_scr[...]


def _gla_prompt_call(x, mod, layer, n_seq, seq_len, tm, w):
    n, d = x.shape
    tiles_per_seq = seq_len // tm
    dk, dv = GLA_DK_HEAD, GLA_DV_HEAD
    in_specs = [_row_spec(tm, d)] + _mod_specs(mod, layer, 0, tm, seq_len) + [
        _const_spec((1, d)),
        _const_spec((d, GLA_IN_PAD)),
        _const_spec((GLA_GATE_PAD, GLA_DK)),
        _const_spec((1, GLA_DK)),
        _const_spec((1, dv)),
        _const_spec((GLA_DV, d)),
        _const_spec((GLA_TRIL, GLA_TRIL)),
        _const_spec((2 * dk, 2 * dk)),
    ]
    out_specs = [
        _row_spec(tm, d),
        pl.BlockSpec((None, GLA_HEADS, dk, dv), lambda i: (i // tiles_per_seq, 0, 0, 0)),
    ]
    out_shape = [
        jax.ShapeDtypeStruct((n, d), F32),
        jax.ShapeDtypeStruct((n_seq, GLA_HEADS, dk, dv), F32),
    ]
    return pl.pallas_call(
        functools.partial(_gla_prompt_kernel, tm=tm, sub=GLA_SUBTILE,
                          tiles_per_seq=tiles_per_seq),
        grid=(n // tm,),
        in_specs=in_specs,
        out_specs=out_specs,
        out_shape=out_shape,
        scratch_shapes=[
            pltpu.VMEM((tm, GLA_QKVG), F32),
            pltpu.VMEM((tm, GLA_GATE_PAD), F32),
            pltpu.VMEM((tm, GLA_DK), F32),
            pltpu.VMEM((tm, GLA_DK), F32),
            pltpu.VMEM((tm, GLA_DV), F32),
            pltpu.VMEM((GLA_HEADS, dk, dv), F32),
        ],
        compiler_params=pltpu.CompilerParams(
            dimension_semantics=("arbitrary",), vmem_limit_bytes=VMEM_LIMIT),
        name="gla_mixer_prompt",
    )(x, mod, mod, mod, w["norm_g"], w["w_in"], w["w_gate2"], w["b_gate"], w["out_g"],
      w["w_out"], w["tril"], w["ones_bd"])


def _gla_proj_kernel(x_ref, sh_ref, sc_ref, ng_ref, win_ref, wg2_ref, bg_ref, p_ref, la_ref):
    proj, log_a = _gla_project(x_ref[...], ng_ref[...], sc_ref[...], sh_ref[...],
                               win_ref[...], wg2_ref[...], bg_ref[...])
    p_ref[...] = proj
    la_ref[...] = log_a


def _gla_proj_call(x, mod, layer, tm, w):
    n, d = x.shape
    specs = _mod_specs(mod, layer, 0, tm, None)
    in_specs = [_row_spec(tm, d), specs[0], specs[1],
                _const_spec((1, d)),
                _const_spec((d, GLA_IN_PAD)),
                _const_spec((GLA_GATE_PAD, GLA_DK)),
                _const_spec((1, GLA_DK))]
    return pl.pallas_call(
        _gla_proj_kernel,
        grid=(n // tm,),
        in_specs=in_specs,
        out_specs=[_row_spec(tm, GLA_QKVG), _row_spec(tm, GLA_DK)],
        out_shape=[jax.ShapeDtypeStruct((n, GLA_QKVG), F32),
                   jax.ShapeDtypeStruct((n, GLA_DK), F32)],
        compiler_params=pltpu.CompilerParams(
            dimension_semantics=("arbitrary",), vmem_limit_bytes=VMEM_LIMIT),
        name="gla_proj_sample",
    )(x, mod, mod, w["norm_g"], w["w_in"], w["w_gate2"], w["b_gate"])


def _gla_step_kernel(p_ref, la_ref, s0_ref, o_ref, s1_ref, *, seqs, steps):
    dk, dv = GLA_DK_HEAD, GLA_DV_HEAD
    qscale = GLA_DK_HEAD ** -0.5
    per_tile = SUBLANES // steps
    row = lax.broadcasted_iota(jnp.int32, (SUBLANES, dk), 0)

    def seq_rows(tile, s):
        if s:
            tile = pltpu.roll(tile, SUBLANES - s * steps, 0)
        keep = lax.broadcasted_iota(jnp.int32, tile.shape, 0) < steps
        return jnp.where(keep, tile, 0.0)

    def tile_body(ti, carry):
        rows = pl.ds(pl.multiple_of(ti * SUBLANES, SUBLANES), SUBLANES)
        p_tile = p_ref[rows, :]
        la_tile = la_ref[rows, :]
        o_tile = jnp.zeros((SUBLANES, GLA_DV), F32)
        for s in range(per_tile):
            bi = ti * per_tile + s
            p = seq_rows(p_tile, s)
            la_all = seq_rows(la_tile, s)
            outs = []
            for h in range(GLA_HEADS):
                q = p[:, h * dk:(h + 1) * dk] * qscale
                k = p[:, GLA_DK + h * dk:GLA_DK + (h + 1) * dk]
                v = p[:, 2 * GLA_DK + h * dv:2 * GLA_DK + (h + 1) * dv]
                la = la_all[:, h * dk:(h + 1) * dk]
                b = jnp.zeros((SUBLANES, dk), F32)
                for t in range(steps):
                    b = b + jnp.where(row >= t, la[t:t + 1, :], 0.0)
                b_last = b[steps - 1:steps, :]
                s0 = s0_ref[bi, h]
                o = _dot(q * jnp.exp(b), s0)
                for j in range(steps):
                    e = jnp.exp(jnp.minimum(b - b[j:j + 1, :], 0.0))
                    z = jnp.where(row >= j, q * k[j:j + 1, :] * e, 0.0)
                    o = o + jnp.sum(z, axis=-1, keepdims=True) * v[j:j + 1, :]
                outs.append(o)
                m = jnp.where(row == steps, jnp.exp(b_last), k * jnp.exp(b_last - b))
                m_t = m.T
                s1_ref[bi, h] = m_t[:, steps:steps + 1] * s0 + _dot(m_t, v)
            o_seq = jnp.concatenate(outs, axis=1)
            o_tile = o_tile + (pltpu.roll(o_seq, s * steps, 0) if s else o_seq)
        o_ref[rows, :] = o_tile
        return carry

    lax.fori_loop(0, seqs // per_tile, tile_body, 0)


def _gla_step_call(proj, log_a, state, steps, seqs):
    n_seq = state.shape[0]
    dk, dv = GLA_DK_HEAD, GLA_DV_HEAD
    rows = lambda width: pl.BlockSpec((seqs * steps, width), lambda i: (i, 0))
    st_spec = pl.BlockSpec((seqs, GLA_HEADS, dk, dv), lambda i: (i, 0, 0, 0))
    return pl.pallas_call(
        functools.partial(_gla_step_kernel, seqs=seqs, steps=steps),
        grid=(n_seq // seqs,),
        in_specs=[rows(GLA_QKVG), rows(GLA_DK), st_spec],
        out_specs=[rows(GLA_DV), st_spec],
        out_shape=[jax.ShapeDtypeStruct((n_seq * steps, GLA_DV), F32),
                   jax.ShapeDtypeStruct(state.shape, F32)],
        compiler_params=pltpu.CompilerParams(
            dimension_semantics=("arbitrary",), vmem_limit_bytes=VMEM_LIMIT),
        name="gla_step_sample",
    )(proj, log_a, state)


def _gla_out_kernel(o_ref, p_ref, x_ref, gt_ref, og_ref, wout_ref, xo_ref):
    xo_ref[...] = _gla_output(o_ref[...], p_ref[...], x_ref[...], gt_ref[...],
                              og_ref[...], wout_ref[...])


def _gla_out_call(o, proj, x, mod, layer, tm, w):
    n, d = x.shape
    gate_spec = _mod_specs(mod, layer, 0, tm, None)[2]
    g_block = (2 * GLA_DK + GLA_DV) // GLA_DV
    in_specs = [_row_spec(tm, GLA_DV),
                pl.BlockSpec((tm, GLA_DV), lambda i: (i, g_block)),
                _row_spec(tm, d), gate_spec,
                _const_spec((1, GLA_DV_HEAD)),
                _const_spec((GLA_DV, d))]
    return pl.pallas_call(
        _gla_out_kernel,
        grid=(n // tm,),
        in_specs=in_specs,
        out_specs=_row_spec(tm, d),
        out_shape=jax.ShapeDtypeStruct((n, d), F32),
        compiler_params=pltpu.CompilerParams(
            dimension_semantics=("arbitrary",), vmem_limit_bytes=VMEM_LIMIT),
        name="gla_out_sample",
    )(o, proj, x, mod, w["out_g"], w["w_out"])


PROMPT_FFN_TILE = 1024
PROMPT_GMLP_TILE = 1024
PROMPT_GLA_TILE = 1024
SAMPLE_TILE = 256
SAMPLE_SEQS_PER_STEP = 16


def kernel(x_prompt, x_sample, c_prompt, c_sample, state_gla, ada_w, ada_b, norm_mix_g, norm_ffn_g,
           ffn_w1, ffn_b1, ffn_w2, ffn_b2, gmlp_w_in, gmlp_b_in, gmlp_ln_g, gmlp_ln_b, gmlp_w_s,
           gmlp_b_s, gmlp_w_out, gmlp_b_out, gla_w_in, gla_w_gate2, gla_b_gate, gla_norm_g,
           gla_w_out, final_norm_g):
    n_seq_p, seq_p, d = x_prompt.shape
    n_seq_s, seq_s, _ = x_sample.shape
    assert d == D_MODEL and GMLP_SUBTILE % CHUNK_A == 0 and GLA_SUBTILE % GLA_FAST_CHUNK == 0
    assert all(seq_p % t == 0 for t in (PROMPT_FFN_TILE, PROMPT_GMLP_TILE, PROMPT_GLA_TILE))
    assert seq_s < SUBLANES and SUBLANES % seq_s == 0 and (n_seq_s * seq_s) % SAMPLE_TILE == 0
    row = lambda a: a.reshape(1, -1)

    mod_p, mod_s = _ada_call(c_prompt, c_sample, seq_s, ada_w, ada_b)
    mod_p = mod_p.reshape(mod_p.shape[0], n_seq_p, 1, N_MOD * d)

    causal = np.tril(np.ones((CHUNK_A, CHUNK_A), dtype=bool))
    ws = jnp.where(causal[None], gmlp_w_s, jnp.zeros_like(gmlp_w_s))
    reps = CHUNK_A // seq_s
    eye = np.eye(reps, dtype=np.float32)
    ws_s = jnp.einsum("ab,gij->gaibj", eye, ws[:, :seq_s, :seq_s]).reshape(
        GMLP_GROUPS, CHUNK_A, CHUNK_A)
    bias_p = jnp.repeat(gmlp_b_s.T, GMLP_GROUP_W, axis=1)
    bias_s = jnp.tile(bias_p[:seq_s], (reps, 1))

    gmlp_w = dict(norm_g=row(norm_mix_g[0]), w_in=gmlp_w_in, b_in=row(gmlp_b_in),
                  ln_g=row(gmlp_ln_g), ln_b=row(gmlp_ln_b), w_out=gmlp_w_out,
                  b_out=row(gmlp_b_out))
    gmlp_w = dict(gmlp_w, mix_p=ws.astype(BF16), bias_p=bias_p, mix_s=ws_s.astype(BF16),
                  bias_s=bias_s)

    depth = ffn_w1.shape[0]
    ffn_w = dict(norm_g=norm_ffn_g.reshape(depth, 1, d), w1=ffn_w1,
                 b1=ffn_b1.reshape(depth, 1, D_FF), w2=ffn_w2,
                 b2=ffn_b2.reshape(depth, 1, d), final_g=row(final_norm_g))

    w_in_pad = jnp.pad(gla_w_in, ((0, 0), (0, GLA_IN_PAD - gla_w_in.shape[1]))).astype(BF16)
    wg2_pad = jnp.pad(gla_w_gate2, ((0, GLA_GATE_PAD - GLA_GATE_RANK), (0, 0))).astype(BF16)
    blk_ones = np.kron(np.eye(2, dtype=np.float32),
                       np.ones((GLA_DK_HEAD, GLA_DK_HEAD), np.float32))
    gla_w = dict(norm_g=row(norm_mix_g[1]), w_in=w_in_pad, w_gate2=wg2_pad, b_gate=row(gla_b_gate),
                 out_g=row(gla_norm_g), w_out=gla_w_out.astype(BF16),
                 tril=jnp.asarray(np.tril(np.ones((GLA_TRIL, GLA_TRIL), np.float32)), BF16),
                 ones_bd=jnp.asarray(blk_ones, BF16))

    xp = x_prompt.reshape(n_seq_p * seq_p, d)
    xs = x_sample.reshape(n_seq_s * seq_s, d)
    grid_for = lambda tm_p: _TwoGroupGrid(xp.shape[0], xs.shape[0], tm_p, SAMPLE_TILE, seq_p)
    xp, xs, chunk_v = _gmlp_call(xp, xs, mod_p, mod_s, grid_for(PROMPT_GMLP_TILE), gmlp_w)
    xp, xs = _ffn_call(xp, xs, mod_p, mod_s, 0, grid_for(PROMPT_FFN_TILE), ffn_w, final_norm=False)
    xp, state_p = _gla_prompt_call(xp, mod_p, 1, n_seq_p, seq_p, PROMPT_GLA_TILE, gla_w)
    proj, log_a = _gla_proj_call(xs, mod_s, 1, SAMPLE_TILE, gla_w)
    o, state_s = _gla_step_call(proj, log_a, state_gla, seq_s, SAMPLE_SEQS_PER_STEP)
    xs = _gla_out_call(o, proj, xs, mod_s, 1, SAMPLE_TILE, gla_w)
    xp, xs = _ffn_call(xp, xs, mod_p, mod_s, 1, grid_for(PROMPT_FFN_TILE), ffn_w, final_norm=True)

    return (xp.reshape(x_prompt.shape), xs.reshape(x_sample.shape), state_p, state_s,
            chunk_v.reshape(n_seq_s, seq_s, GMLP_WIDTH))
```

```python
import functools

import jax
import jax.numpy as jnp
import numpy as np
from jax import lax
from jax.experimental import pallas as pl
from jax.experimental.pallas import tpu as pltpu

F32 = jnp.float32
BF16 = jnp.bfloat16

D_MODEL = 1024
N_MOD = 6
CHUNK_A = 128
GMLP_WIDTH = D_MODEL
GMLP_GROUPS = 4
GMLP_GROUP_W = GMLP_WIDTH // GMLP_GROUPS
GLA_HEADS = 4
GLA_DK = D_MODEL // 2
GLA_DV = D_MODEL
GLA_DK_HEAD = GLA_DK // GLA_HEADS
GLA_DV_HEAD = GLA_DV // GLA_HEADS
GLA_GATE_RANK = 16
GLA_TAU = 16.0
GLA_CHUNK = 64
GLA_FAST_CHUNK = 128
GLA_TRIL = max(GLA_CHUNK, GLA_FAST_CHUNK)
GLA_SAFE_EXPONENT = 80.0
GLA_QKVG = 2 * GLA_DK + 2 * GLA_DV
GLA_GATE_PAD = 128
GLA_IN_PAD = GLA_QKVG + GLA_GATE_PAD
D_FF = 4 * D_MODEL
EPS = 1e-6

SUBLANES = 8
BF16_SUBLANES = 16
ADA_ROWS = 256
FFN_CHUNK = 1024
FFN_SUBTILE = 512
GMLP_SUBTILE = 256
GLA_SUBTILE = 512
VMEM_LIMIT = 56 * 1024 * 1024
WEIGHT_LOAD_CHUNKS = 16


def _rms(x, g):
    return x * lax.rsqrt(jnp.mean(x * x, axis=-1, keepdims=True) + EPS) * g


def _modulated_norm(x, g, scale, shift):
    inv = lax.rsqrt(jnp.mean(x * x, axis=-1, keepdims=True) + EPS)
    return (x * inv) * (g * (1.0 + scale)) + shift


def _dot(a, b):
    return jnp.dot(a.astype(BF16), b.astype(BF16), preferred_element_type=F32)


def _dot_nt(a, b):
    return lax.dot_general(a.astype(BF16), b.astype(BF16), (((1,), (1,)), ((), ())),
                           preferred_element_type=F32)


def _dot_tn(a, b):
    return lax.dot_general(a.astype(BF16), b.astype(BF16), (((0,), (0,)), ((), ())),
                           preferred_element_type=F32)


def _gelu_tanh(x):
    c1 = -2.0 * 0.7978845608028654 * 1.4426950408889634
    c2 = c1 * 0.044715
    return x / (1.0 + jnp.exp2(x * (c1 + c2 * (x * x))))


def _silu(x):
    return x * jax.nn.sigmoid(x)


def _log_sigmoid(x):
    return -(jnp.maximum(-x, 0.0) + jnp.log1p(jnp.exp(-jnp.abs(x))))


def _bf16_terms(x, n):
    terms = []
    for _ in range(n - 1):
        t = x.astype(BF16)
        terms.append(t)
        x = x - t.astype(F32)
    terms.append(x.astype(BF16))
    return terms


def _ada_kernel(c_ref, wa_ref, wb_ref, b_ref, e_ref, op_ref, os_ref, s_scr, acc_scr,
                *, n_prompt, tk):
    k = pl.program_id(1)
    n_seq = e_ref.shape[1] // 2

    @pl.when((pl.program_id(0) == 0) & (k == 0))
    def _():
        for kk in range(s_scr.shape[0]):
            s_scr[kk] = _silu(c_ref[:, kk * tk:(kk + 1) * tk]).astype(BF16)

    @pl.when(k == 0)
    def _():
        acc_scr[...] = jnp.broadcast_to(b_ref[...], acc_scr.shape)

    s = s_scr[k]
    half = N_MOD // 2
    for j in range(N_MOD):
        cols = slice(j * D_MODEL, (j + 1) * D_MODEL)
        w_ref, jj = (wa_ref, j) if j < half else (wb_ref, j - half)
        w = w_ref[:, jj * D_MODEL:(jj + 1) * D_MODEL].astype(BF16)
        acc_scr[:, cols] += jnp.dot(s, w, preferred_element_type=F32)

    @pl.when(k == pl.num_programs(1) - 1)
    def _():
        op_ref[...] = acc_scr[0:n_prompt, :]
        for j in range(N_MOD):
            cols = slice(j * D_MODEL, (j + 1) * D_MODEL)
            terms = _bf16_terms(acc_scr[n_prompt:n_prompt + n_seq, cols], 2)
            os_ref[:, cols] = jnp.dot(e_ref[...], jnp.concatenate(terms, axis=0),
                                      preferred_element_type=F32)


def _ada_call(c_prompt, c_sample, tokens_per_seq, ada_w, ada_b):
    depth = ada_w.shape[0]
    n_prompt, n_seq = c_prompt.shape[0], c_sample.shape[0]
    n_sample = n_seq * tokens_per_seq
    d = D_MODEL
    pad = -(n_prompt + n_seq) % BF16_SUBLANES
    c_all = jnp.concatenate([c_prompt, c_sample, jnp.zeros((pad, d), F32)], axis=0)
    rows = c_all.shape[0]
    expand = np.tile(np.repeat(np.eye(n_seq, dtype=np.float32), tokens_per_seq, axis=0), (1, 2))
    tk = ADA_ROWS
    width = N_MOD * d
    return pl.pallas_call(
        functools.partial(_ada_kernel, n_prompt=n_prompt, tk=tk),
        grid=(depth, d // tk),
        in_specs=[
            pl.BlockSpec((rows, d), lambda l, k: (0, 0)),
            pl.BlockSpec((None, tk, width // 2), lambda l, k: (l, k, 0)),
            pl.BlockSpec((None, tk, width // 2), lambda l, k: (l, k, 1)),
            pl.BlockSpec((None, 1, width), lambda l, k: (l, 0, 0)),
            pl.BlockSpec((n_sample, 2 * n_seq), lambda l, k: (0, 0)),
        ],
        out_specs=[
            pl.BlockSpec((None, n_prompt, width), lambda l, k: (l, 0, 0)),
            pl.BlockSpec((None, n_sample, width), lambda l, k: (l, 0, 0)),
        ],
        out_shape=[
            jax.ShapeDtypeStruct((depth, n_prompt, width), F32),
            jax.ShapeDtypeStruct((depth, n_sample, width), F32),
        ],
        scratch_shapes=[pltpu.VMEM((d // tk, rows, tk), BF16),
                        pltpu.VMEM((rows, width), F32)],
        compiler_params=pltpu.CompilerParams(
            dimension_semantics=("arbitrary", "arbitrary"), vmem_limit_bytes=VMEM_LIMIT),
        name="adaln_mod",
    )(c_all, ada_w, ada_w, ada_b.reshape(depth, 1, N_MOD * d), jnp.asarray(expand, BF16))


def _const_spec(shape):
    zeros = (0,) * len(shape)
    return pl.BlockSpec(shape, lambda i: zeros, pipeline_mode=pl.Buffered(1))


def _layer_spec(shape, layer):
    zeros = (0,) * len(shape)
    return pl.BlockSpec((None,) + shape, lambda i: (layer,) + zeros,
                        pipeline_mode=pl.Buffered(1))


def _mod_specs(mod, layer, first_chunk, tm, rows_per_seq):
    d = D_MODEL
    specs = []
    for j in range(first_chunk, first_chunk + 3):
        if rows_per_seq is None:
            specs.append(pl.BlockSpec((None, tm, d), lambda i, j=j: (layer, i, j)))
        else:
            tiles = rows_per_seq // tm
            specs.append(pl.BlockSpec((None, None, 1, d),
                                      lambda i, j=j, tiles=tiles: (layer, i // tiles, 0, j)))
    return specs


def _mod_rows(ref, rows):
    return ref[...] if ref.shape[0] == 1 else ref[rows, :]


def _row_spec(tm, width):
    return pl.BlockSpec((tm, width), lambda i: (i, 0))


class _TwoGroupGrid:
    def __init__(self, n_prompt_rows, n_sample_rows, tm_p, tm_s, seq_len):
        self.tm_p, self.tm_s = tm_p, tm_s
        self.p_steps = n_prompt_rows // tm_p
        self.s_steps = n_sample_rows // tm_s
        self.tiles_per_seq = seq_len // tm_p
        self.steps = self.p_steps + self.s_steps

    def p_idx(self, i):
        return jnp.minimum(i, self.p_steps - 1)

    def s_idx(self, i):
        return jnp.maximum(i - self.p_steps, 0)

    def rows_p(self, width):
        return pl.BlockSpec((self.tm_p, width), lambda i: (self.p_idx(i), 0))

    def rows_s(self, width, col_block=0):
        return pl.BlockSpec((self.tm_s, width), lambda i: (self.s_idx(i), col_block))

    def mods_p(self, layer, first_chunk):
        return [pl.BlockSpec((None, None, 1, D_MODEL),
                             lambda i, j=j: (layer, self.p_idx(i) // self.tiles_per_seq, 0, j))
                for j in range(first_chunk, first_chunk + 3)]

    def mods_s(self, layer, first_chunk):
        return [pl.BlockSpec((None, self.tm_s, D_MODEL),
                             lambda i, j=j: (layer, self.s_idx(i), j))
                for j in range(first_chunk, first_chunk + 3)]


def _gmlp_rows(x_ref, sh_ref, sc_ref, gt_ref, mix_ref, bs_ref, xo_ref, v_ref, z_scr, w, tm, sub):
    ng_ref, win_ref, bin_ref, lng_ref, lnb_ref, wout_ref, bout_ref = w
    groups = [slice(r0, r0 + sub) for r0 in range(0, tm, sub)]

    def project(gi):
        rows = groups[gi]
        h = _modulated_norm(x_ref[rows, :], ng_ref[...], _mod_rows(sc_ref, rows),
                            _mod_rows(sh_ref, rows))
        z_scr[gi % 2] = _dot(h, win_ref[...])

    project(0)
    for gi, rows in enumerate(groups):
        if gi + 1 < len(groups):
            project(gi + 1)
        z = _gelu_tanh(z_scr[gi % 2] + bin_ref[...])
        u = z[:, :GMLP_WIDTH]
        v = z[:, GMLP_WIDTH:]
        mu = jnp.mean(v, axis=-1, keepdims=True)
        vc = v - mu
        var = jnp.mean(vc * vc, axis=-1, keepdims=True)
        v = vc * lax.rsqrt(var + EPS) * lng_ref[...] + lnb_ref[...]
        if v_ref is not None:
            v_ref[rows, :] = v
        vb = v.astype(BF16)
        mixed = []
        for c in range(sub // CHUNK_A):
            cols = []
            for g in range(GMLP_GROUPS):
                blk = vb[c * CHUNK_A:(c + 1) * CHUNK_A, g * GMLP_GROUP_W:(g + 1) * GMLP_GROUP_W]
                cols.append(jnp.dot(mix_ref[g], blk, preferred_element_type=F32))
            mixed.append(jnp.concatenate(cols, axis=1) + bs_ref[...])
        s = jnp.concatenate(mixed, axis=0)
        y = _dot(u * s, wout_ref[...]) + bout_ref[...]
        xo_ref[rows, :] = x_ref[rows, :] + _mod_rows(gt_ref, rows) * y


def _gmlp_kernel(xp_ref, shp_ref, scp_ref, gtp_ref, xs_ref, shs_ref, scs_ref, gts_ref,
                 ng_ref, win_hbm, bin_ref, lng_ref, lnb_ref, wout_hbm, bout_ref,
                 mixp_ref, bsp_ref, mixs_ref, bss_ref, op_ref, os_ref, vs_ref, z_scr,
                 win_ref, wout_ref, stage_in, stage_out, sems_in, sems_out, *, grid):
    w = (ng_ref, win_ref, bin_ref, lng_ref, lnb_ref, wout_ref, bout_ref)
    step = pl.program_id(0)

    @pl.when(step == 0)
    def _():
        _load_as_bf16([(win_hbm, win_ref, stage_in, sems_in),
                       (wout_hbm, wout_ref, stage_out, sems_out)])

    @pl.when(step < grid.p_steps)
    def _():
        _gmlp_rows(xp_ref, shp_ref, scp_ref, gtp_ref, mixp_ref, bsp_ref, op_ref, None, z_scr, w,
                   grid.tm_p, GMLP_SUBTILE)

    @pl.when(step >= grid.p_steps)
    def _():
        _gmlp_rows(xs_ref, shs_ref, scs_ref, gts_ref, mixs_ref, bss_ref, os_ref, vs_ref, z_scr, w,
                   grid.tm_s, GMLP_SUBTILE)


def _gmlp_call(xp, xs, mod_p, mod_s, grid, w):
    d = D_MODEL
    mix_spec = _const_spec((GMLP_GROUPS, CHUNK_A, CHUNK_A))
    bias_spec = _const_spec((CHUNK_A, GMLP_WIDTH))
    in_specs = ([grid.rows_p(d)] + grid.mods_p(0, 0) + [grid.rows_s(d)] + grid.mods_s(0, 0) + [
        _const_spec((1, d)),
        pl.BlockSpec(memory_space=pl.ANY),
        _const_spec((1, 2 * GMLP_WIDTH)),
        _const_spec((1, GMLP_WIDTH)),
        _const_spec((1, GMLP_WIDTH)),
        pl.BlockSpec(memory_space=pl.ANY),
        _const_spec((1, d)),
        mix_spec, bias_spec, mix_spec, bias_spec,
    ])
    return pl.pallas_call(
        functools.partial(_gmlp_kernel, grid=grid),
        grid=(grid.steps,),
        in_specs=in_specs,
        out_specs=[grid.rows_p(d), grid.rows_s(d), grid.rows_s(GMLP_WIDTH)],
        out_shape=[jax.ShapeDtypeStruct(xp.shape, F32), jax.ShapeDtypeStruct(xs.shape, F32),
                   jax.ShapeDtypeStruct((xs.shape[0], GMLP_WIDTH), F32)],
        scratch_shapes=[pltpu.VMEM((2, GMLP_SUBTILE, 2 * GMLP_WIDTH), F32),
                        pltpu.VMEM((d, 2 * GMLP_WIDTH), BF16),
                        pltpu.VMEM((GMLP_WIDTH, d), BF16),
                        pltpu.VMEM((2, d // WEIGHT_LOAD_CHUNKS, 2 * GMLP_WIDTH), F32),
                        pltpu.VMEM((2, GMLP_WIDTH // WEIGHT_LOAD_CHUNKS, d), F32),
                        pltpu.SemaphoreType.DMA((2,)),
                        pltpu.SemaphoreType.DMA((2,))],
        compiler_params=pltpu.CompilerParams(
            dimension_semantics=("arbitrary",), vmem_limit_bytes=VMEM_LIMIT),
        name="gmlp_mixer",
    )(xp, mod_p, mod_p, mod_p, xs, mod_s, mod_s, mod_s, w["norm_g"], w["w_in"], w["b_in"],
      w["ln_g"], w["ln_b"], w["w_out"], w["b_out"], w["mix_p"], w["bias_p"], w["mix_s"],
      w["bias_s"])


def _ffn_rows(x_ref, sh_ref, sc_ref, gt_ref, o_ref, h_scr, w, final_norm, tm):
    ng_ref, w1_ref, b1_ref, w2_ref, b2_ref, fg_ref = w
    sub = min(FFN_SUBTILE, tm)
    groups = [slice(r0, r0 + sub) for r0 in range(0, tm, sub)]

    def normalise(gi):
        rows = groups[gi]
        h_scr[gi % 2, 0:sub, :] = _modulated_norm(
            x_ref[rows, :], ng_ref[...], _mod_rows(sc_ref, rows),
            _mod_rows(sh_ref, rows)).astype(BF16)

    normalise(0)
    for gi, rows in enumerate(groups):
        if gi + 1 < len(groups):
            normalise(gi + 1)
        h = h_scr[gi % 2, 0:sub, :]
        acc = jnp.zeros((sub, D_MODEL), F32)
        for j in range(D_FF // FFN_CHUNK):
            cols = slice(j * FFN_CHUNK, (j + 1) * FFN_CHUNK)
            a = jnp.dot(h, w1_ref[:, cols], preferred_element_type=F32) + b1_ref[:, cols]
            r = jnp.square(jnp.maximum(a, 0.0)).astype(BF16)
            acc = acc + jnp.dot(r, w2_ref[cols, :], preferred_element_type=F32)
        y = x_ref[rows, :] + _mod_rows(gt_ref, rows) * (acc + b2_ref[...])
        if final_norm:
            y = _rms(y, fg_ref[...])
        o_ref[rows, :] = y


def _load_as_bf16(jobs):
    def chunk_copy(job, c):
        src_hbm, _, stage_ref, sem_ref = job
        rows = pl.ds(c * stage_ref.shape[1], stage_ref.shape[1])
        return pltpu.make_async_copy(src_hbm.at[rows, :], stage_ref.at[c % 2], sem_ref.at[c % 2])

    n_chunks = {job[1].shape[0] // job[2].shape[1] for job in jobs}
    assert len(n_chunks) == 1
    n_chunks = n_chunks.pop()
    for job in jobs:
        chunk_copy(job, 0).start()
    for c in range(n_chunks):
        for job in jobs:
            if c + 1 < n_chunks:
                chunk_copy(job, c + 1).start()
        for job in jobs:
            _, dst_ref, stage_ref, _ = job
            chunk_copy(job, c).wait()
            rows = stage_ref.shape[1]
            dst_ref[c * rows:(c + 1) * rows, :] = stage_ref[c % 2].astype(BF16)


def _ffn_kernel(xp_ref, shp_ref, scp_ref, gtp_ref, xs_ref, shs_ref, scs_ref, gts_ref,
                ng_ref, w1_hbm, b1_ref, w2_hbm, b2_ref, fg_ref, op_ref, os_ref, h_scr,
                w1_ref, w2_ref, stage1, stage2, sems1, sems2, *, final_norm, grid, layer):
    w = (ng_ref, w1_ref, b1_ref, w2_ref, b2_ref, fg_ref)
    step = pl.program_id(0)

    @pl.when(step == 0)
    def _():
        _load_as_bf16([(w1_hbm.at[layer], w1_ref, stage1, sems1),
                       (w2_hbm.at[layer], w2_ref, stage2, sems2)])

    @pl.when(step < grid.p_steps)
    def _():
        _ffn_rows(xp_ref, shp_ref, scp_ref, gtp_ref, op_ref, h_scr, w, final_norm, grid.tm_p)

    @pl.when(step >= grid.p_steps)
    def _():
        _ffn_rows(xs_ref, shs_ref, scs_ref, gts_ref, os_ref, h_scr, w, final_norm, grid.tm_s)


def _ffn_call(xp, xs, mod_p, mod_s, layer, grid, w, final_norm):
    d = D_MODEL
    in_specs = ([grid.rows_p(d)] + grid.mods_p(layer, 3) + [grid.rows_s(d)]
                + grid.mods_s(layer, 3) + [
        _layer_spec((1, d), layer),
        pl.BlockSpec(memory_space=pl.ANY),
        _layer_spec((1, D_FF), layer),
        pl.BlockSpec(memory_space=pl.ANY),
        _layer_spec((1, d), layer),
        _const_spec((1, d)),
    ])
    return pl.pallas_call(
        functools.partial(_ffn_kernel, final_norm=final_norm, grid=grid, layer=layer),
        grid=(grid.steps,),
        in_specs=in_specs,
        out_specs=[grid.rows_p(d), grid.rows_s(d)],
        out_shape=[jax.ShapeDtypeStruct(xp.shape, F32), jax.ShapeDtypeStruct(xs.shape, F32)],
        scratch_shapes=[pltpu.VMEM((2, FFN_SUBTILE, d), BF16),
                        pltpu.VMEM((d, D_FF), BF16),
                        pltpu.VMEM((D_FF, d), BF16),
                        pltpu.VMEM((2, d // WEIGHT_LOAD_CHUNKS, D_FF), F32),
                        pltpu.VMEM((2, D_FF // WEIGHT_LOAD_CHUNKS, d), F32),
                        pltpu.SemaphoreType.DMA((2,)),
                        pltpu.SemaphoreType.DMA((2,))],
        compiler_params=pltpu.CompilerParams(
            dimension_semantics=("arbitrary",), vmem_limit_bytes=VMEM_LIMIT),
        name="sqrelu_mlp",
    )(xp, mod_p, mod_p, mod_p, xs, mod_s, mod_s, mod_s, w["norm_g"], w["w1"], w["b1"], w["w2"],
      w["b2"], w["final_g"])


def _gla_project(x, ng, scale, shift, win, wg2, bg):
    h = _modulated_norm(x, ng, scale, shift)
    proj = _dot(h, win)
    gate = _dot(proj[:, GLA_QKVG:], wg2) + bg
    log_a = _log_sigmoid(gate) * (1.0 / GLA_TAU)
    return proj[:, :GLA_QKVG], log_a


def _gla_output(o, gate_in, x, gt, og, wout):
    parts = []
    for h in range(GLA_HEADS):
        cols = slice(h * GLA_DV_HEAD, (h + 1) * GLA_DV_HEAD)
        parts.append(_rms(o[:, cols], og))
    on = jnp.concatenate(parts, axis=1)
    y = _dot(on * _silu(gate_in), wout)
    return x + gt * y


def _as_column(row):
    return jnp.broadcast_to(row, (SUBLANES, row.shape[1])).T[:, 0:1]


def _chunk_cumsum(tril, log_a):
    head = log_a.astype(BF16)
    rest = (log_a - head.astype(F32)).astype(BF16)
    return (jnp.dot(tril, head, preferred_element_type=F32)
            + jnp.dot(tril, rest, preferred_element_type=F32))


def _level_masks(c):
    rr = lax.broadcasted_iota(jnp.int32, (c, c), 0)
    cc = lax.broadcasted_iota(jnp.int32, (c, c), 1)
    masks = []
    s = c
    while s > SUBLANES:
        half = s // 2
        same = (rr ^ cc) < s
        masks.append(same & ((rr & half) != 0) & ((cc & half) == 0))
        s = half
    return masks, rr, cc


def _block_rows(x, s, r):
    c, w = x.shape
    return jnp.concatenate(
        [jnp.broadcast_to(x[p * s + r:p * s + r + 1, :], (s, w)) for p in range(c // s)], axis=0)


def _diag_terms(q, k, a):
    c, w = q.shape
    ii = lax.broadcasted_iota(jnp.int32, (c, w), 0) & (SUBLANES - 1)
    decay = jnp.zeros((c, w), F32)
    terms = [None] * SUBLANES
    for j in range(SUBLANES - 1, -1, -1):
        if j < SUBLANES - 1:
            decay = decay * _block_rows(a, SUBLANES, j + 1)
        decay = jnp.where(ii == j, 1.0, decay)
        terms[j] = q * _block_rows(k, SUBLANES, j) * decay
    return jnp.concatenate(terms, axis=0)


def _gla_chunk_pair(q2, k2, v2, la2, b2, st_refs, ones_bd, masks, rr, cc):
    c = q2[0].shape[0]
    zs = []
    for q, k, la in zip(q2, k2, la2):
        zs.append(_diag_terms(q, k, jnp.exp(la)))
    sums = jnp.dot(jnp.concatenate(zs, axis=1).astype(BF16), ones_bd,
                   preferred_element_type=F32)
    outs = []
    for hh, (q, k, v, b, st_ref) in enumerate(zip(q2, k2, v2, b2, st_refs)):
        dk = q.shape[1]
        attn = jnp.zeros((c, c), F32)
        base = rr & ~(SUBLANES - 1)
        for j in range(SUBLANES):
            rj = sums[j * c:(j + 1) * c, hh * dk:hh * dk + c]
            attn = jnp.where(cc == base + j, rj, attn)
        s = c
        for mask in masks:
            half = s // 2
            e = jnp.exp(-jnp.abs(b - _block_rows(b, s, half)))
            attn = jnp.where(mask, _dot_nt(q * e, k * e), attn)
            s = half
        b_last = b[c - 1:c, :]
        st = st_ref[...]
        o = _dot(q * jnp.exp(b), st) + _dot(attn, v)
        kd = k * jnp.exp(b_last - b)
        st_ref[...] = _as_column(jnp.exp(b_last)) * st + _dot_tn(kd, v)
        outs.append(o)
    return outs


def _gla_prompt_kernel(x_ref, sh_ref, sc_ref, gt_ref, ng_ref, win_ref, wg2_ref, bg_ref,
                       og_ref, wout_ref, tril_ref, ones_ref, xo_ref, s_ref,
                       proj_scr, ga_scr, la_scr, b_scr, o_scr, st_scr, *, tm, sub, tiles_per_seq):
    tile = pl.program_id(0) % tiles_per_seq

    @pl.when(tile == 0)
    def _():
        st_scr[...] = jnp.zeros(st_scr.shape, F32)

    dk, dv = GLA_DK_HEAD, GLA_DV_HEAD
    qscale = GLA_DK_HEAD ** -0.5
    cf = GLA_FAST_CHUNK
    c = GLA_CHUNK
    groups = [slice(r0, r0 + sub) for r0 in range(0, tm, sub)]

    def project(rows):
        h = _modulated_norm(x_ref[rows, :], ng_ref[...], _mod_rows(sc_ref, rows),
                            _mod_rows(sh_ref, rows))
        proj = _dot(h, win_ref[...])
        proj_scr[rows, :] = proj[:, :GLA_QKVG]
        ga_scr[rows, :] = proj[:, GLA_QKVG:]

    def decays(rows):
        gate = _dot(ga_scr[rows, :], wg2_ref[...]) + bg_ref[...]
        log_a = _log_sigmoid(gate) * (1.0 / GLA_TAU)
        la_scr[rows, :] = log_a
        for ci in range(sub // cf):
            b_scr[rows.start + ci * cf:rows.start + (ci + 1) * cf, :] = _chunk_cumsum(
                tril_ref[0:cf, 0:cf], log_a[ci * cf:(ci + 1) * cf, :])
        return jnp.min(b_scr[rows, :]) >= -GLA_SAFE_EXPONENT

    def head_slices(rows, h):
        q = proj_scr[rows, h * dk:(h + 1) * dk] * qscale
        k = proj_scr[rows, GLA_DK + h * dk:GLA_DK + (h + 1) * dk]
        v = proj_scr[rows, 2 * GLA_DK + h * dv:2 * GLA_DK + (h + 1) * dv]
        return q, k, v

    def single_ref_rows(group):
        rr = lax.broadcasted_iota(jnp.int32, (cf, cf), 0)
        cc = lax.broadcasted_iota(jnp.int32, (cf, cf), 1)
        causal = rr >= cc
        chunks = [slice(r0, r0 + cf) for r0 in range(group.start, group.stop, cf)]
        for h in range(GLA_HEADS):
            qes, vbs, scores, kvs, decays = [], [], [], [], []
            for rows in chunks:
                q, k, v = head_slices(rows, h)
                b = b_scr[rows, h * dk:(h + 1) * dk]
                decay_last = jnp.exp(b[cf - 1:cf, :])
                qe = (q * jnp.exp(b)).astype(BF16)
                kt = k * jnp.exp(-b)
                vb = v.astype(BF16)
                scores.append(_dot_nt(qe, kt))
                kvs.append(_dot_tn(kt * decay_last, vb))
                qes.append(qe)
                vbs.append(vb)
                decays.append(_as_column(decay_last))
            st = st_scr[h]
            states = []
            for kv, decay_col in zip(kvs, decays):
                states.append(st)
                st = decay_col * st + kv
            st_scr[h] = st
            for rows, qe, vb, sc, st_in in zip(chunks, qes, vbs, scores, states):
                attn = jnp.where(causal, sc, 0.0).astype(BF16)
                lhs = jnp.concatenate([qe, attn], axis=1)
                rhs = jnp.concatenate([st_in.astype(BF16), vb], axis=0)
                o_scr[rows, h * dv:(h + 1) * dv] = jnp.dot(lhs, rhs, preferred_element_type=F32)

    def robust_rows(group):
        def robust_body(ci, carry):
            rows = pl.ds(pl.multiple_of(group.start + ci * c, c), c)
            masks, rr, cc = _level_masks(c)
            b_c = _chunk_cumsum(tril_ref[0:c, 0:c], la_scr[rows, :])
            outs = []
            for h0 in range(0, GLA_HEADS, 2):
                heads = (h0, h0 + 1)
                qkv = [head_slices(rows, h) for h in heads]
                la2 = [la_scr[rows, h * dk:(h + 1) * dk] for h in heads]
                b2 = [b_c[:, h * dk:(h + 1) * dk] for h in heads]
                st_refs = [st_scr.at[h] for h in heads]
                outs += _gla_chunk_pair([t[0] for t in qkv], [t[1] for t in qkv],
                                        [t[2] for t in qkv], la2, b2, st_refs, ones_ref[...],
                                        masks, rr, cc)
            o_scr[rows, :] = jnp.concatenate(outs, axis=1)
            return carry

        lax.fori_loop(0, sub // c, robust_body, 0)

    def output(rows):
        gate_in = proj_scr[rows, 2 * GLA_DK + GLA_DV:]
        xo_ref[rows, :] = _gla_output(o_scr[rows, :], gate_in, x_ref[rows, :],
                                      _mod_rows(gt_ref, rows), og_ref[...], wout_ref[...])

    project(groups[0])
    for gi, rows in enumerate(groups):
        single_ref_safe = decays(rows)
        following = groups[gi + 1] if gi + 1 < len(groups) else None

        @pl.when(single_ref_safe)
        def _():
            if following is not None:
                project(following)
            single_ref_rows(rows)
            output(rows)

        @pl.when(jnp.logical_not(single_ref_safe))
        def _():
            if following is not None:
                project(following)
            robust_rows(rows)
            output(rows)

    @pl.when(tile == tiles_per_seq - 1)
    def _():
        s_ref[...] = st_scr[...]


def _gla_prompt_call(x, mod, layer, n_seq, seq_len, tm, w):
    n, d = x.shape
    tiles_per_seq = seq_len // tm
    dk, dv = GLA_DK_HEAD, GLA_DV_HEAD
    in_specs = [_row_spec(tm, d)] + _mod_specs(mod, layer, 0, tm, seq_len) + [
        _const_spec((1, d)),
        _const_spec((d, GLA_IN_PAD)),
        _const_spec((GLA_GATE_PAD, GLA_DK)),
        _const_spec((1, GLA_DK)),
        _const_spec((1, dv)),
        _const_spec((GLA_DV, d)),
        _const_spec((GLA_TRIL, GLA_TRIL)),
        _const_spec((2 * dk, 2 * dk)),
    ]
    out_specs = [
        _row_spec(tm, d),
        pl.BlockSpec((None, GLA_HEADS, dk, dv), lambda i: (i // tiles_per_seq, 0, 0, 0)),
    ]
    out_shape = [
        jax.ShapeDtypeStruct((n, d), F32),
        jax.ShapeDtypeStruct((n_seq, GLA_HEADS, dk, dv), F32),
    ]
    return pl.pallas_call(
        functools.partial(_gla_prompt_kernel, tm=tm, sub=GLA_SUBTILE,
                          tiles_per_seq=tiles_per_seq),
        grid=(n // tm,),
        in_specs=in_specs,
        out_specs=out_specs,
        out_shape=out_shape,
        scratch_shapes=[
            pltpu.VMEM((tm, GLA_QKVG), F32),
            pltpu.VMEM((tm, GLA_GATE_PAD), F32),
            pltpu.VMEM((tm, GLA_DK), F32),
            pltpu.VMEM((tm, GLA_DK), F32),
            pltpu.VMEM((tm, GLA_DV), F32),
            pltpu.VMEM((GLA_HEADS, dk, dv), F32),
        ],
        compiler_params=pltpu.CompilerParams(
            dimension_semantics=("arbitrary",), vmem_limit_bytes=VMEM_LIMIT),
        name="gla_mixer_prompt",
    )(x, mod, mod, mod, w["norm_g"], w["w_in"], w["w_gate2"], w["b_gate"], w["out_g"],
      w["w_out"], w["tril"], w["ones_bd"])


def _gla_proj_kernel(x_ref, sh_ref, sc_ref, ng_ref, win_ref, wg2_ref, bg_ref, p_ref, la_ref):
    proj, log_a = _gla_project(x_ref[...], ng_ref[...], sc_ref[...], sh_ref[...],
                               win_ref[...], wg2_ref[...], bg_ref[...])
    p_ref[...] = proj
    la_ref[...] = log_a


def _gla_proj_call(x, mod, layer, tm, w):
    n, d = x.shape
    specs = _mod_specs(mod, layer, 0, tm, None)
    in_specs = [_row_spec(tm, d), specs[0], specs[1],
                _const_spec((1, d)),
                _const_spec((d, GLA_IN_PAD)),
                _const_spec((GLA_GATE_PAD, GLA_DK)),
                _const_spec((1, GLA_DK))]
    return pl.pallas_call(
        _gla_proj_kernel,
        grid=(n // tm,),
        in_specs=in_specs,
        out_specs=[_row_spec(tm, GLA_QKVG), _row_spec(tm, GLA_DK)],
        out_shape=[jax.ShapeDtypeStruct((n, GLA_QKVG), F32),
                   jax.ShapeDtypeStruct((n, GLA_DK), F32)],
        compiler_params=pltpu.CompilerParams(
            dimension_semantics=("arbitrary",), vmem_limit_bytes=VMEM_LIMIT),
        name="gla_proj_sample",
    )(x, mod, mod, w["norm_g"], w["w_in"], w["w_gate2"], w["b_gate"])


def _gla_step_kernel(p_ref, la_ref, s0_ref, o_ref, s1_ref, *, seqs, steps):
    dk, dv = GLA_DK_HEAD, GLA_DV_HEAD
    qscale = GLA_DK_HEAD ** -0.5
    per_tile = SUBLANES // steps
    row = lax.broadcasted_iota(jnp.int32, (SUBLANES, dk), 0)

    def seq_rows(tile, s):
        if s:
            tile = pltpu.roll(tile, SUBLANES - s * steps, 0)
        keep = lax.broadcasted_iota(jnp.int32, tile.shape, 0) < steps
        return jnp.where(keep, tile, 0.0)

    def tile_body(ti, carry):
        rows = pl.ds(pl.multiple_of(ti * SUBLANES, SUBLANES), SUBLANES)
        p_tile = p_ref[rows, :]
        la_tile = la_ref[rows, :]
        o_tile = jnp.zeros((SUBLANES, GLA_DV), F32)
        for s in range(per_tile):
            bi = ti * per_tile + s
            p = seq_rows(p_tile, s)
            la_all = seq_rows(la_tile, s)
            outs = []
            for h in range(GLA_HEADS):
                q = p[:, h * dk:(h + 1) * dk] * qscale
                k = p[:, GLA_DK + h * dk:GLA_DK + (h + 1) * dk]
                v = p[:, 2 * GLA_DK + h * dv:2 * GLA_DK + (h + 1) * dv]
                la = la_all[:, h * dk:(h + 1) * dk]
                b = jnp.zeros((SUBLANES, dk), F32)
                for t in range(steps):
                    b = b + jnp.where(row >= t, la[t:t + 1, :], 0.0)
                b_last = b[steps - 1:steps, :]
                s0 = s0_ref[bi, h]
                o = _dot(q * jnp.exp(b), s0)
                for j in range(steps):
                    e = jnp.exp(jnp.minimum(b - b[j:j + 1, :], 0.0))
                    z = jnp.where(row >= j, q * k[j:j + 1, :] * e, 0.0)
                    o = o + jnp.sum(z, axis=-1, keepdims=True) * v[j:j + 1, :]
                outs.append(o)
                m = jnp.where(row == steps, jnp.exp(b_last), k * jnp.exp(b_last - b))
                m_t = m.T
                s1_ref[bi, h] = m_t[:, steps:steps + 1] * s0 + _dot(m_t, v)
            o_seq = jnp.concatenate(outs, axis=1)
            o_tile = o_tile + (pltpu.roll(o_seq, s * steps, 0) if s else o_seq)
        o_ref[rows, :] = o_tile
        return carry

    lax.fori_loop(0, seqs // per_tile, tile_body, 0)


def _gla_step_call(proj, log_a, state, steps, seqs):
    n_seq = state.shape[0]
    dk, dv = GLA_DK_HEAD, GLA_DV_HEAD
    rows = lambda width: pl.BlockSpec((seqs * steps, width), lambda i: (i, 0))
    st_spec = pl.BlockSpec((seqs, GLA_HEADS, dk, dv), lambda i: (i, 0, 0, 0))
    return pl.pallas_call(
        functools.partial(_gla_step_kernel, seqs=seqs, steps=steps),
        grid=(n_seq // seqs,),
        in_specs=[rows(GLA_QKVG), rows(GLA_DK), st_spec],
        out_specs=[rows(GLA_DV), st_spec],
        out_shape=[jax.ShapeDtypeStruct((n_seq * steps, GLA_DV), F32),
                   jax.ShapeDtypeStruct(state.shape, F32)],
        compiler_params=pltpu.CompilerParams(
            dimension_semantics=("arbitrary",), vmem_limit_bytes=VMEM_LIMIT),
        name="gla_step_sample",
    )(proj, log_a, state)


def _gla_out_kernel(o_ref, p_ref, x_ref, gt_ref, og_ref, wout_ref, xo_ref):
    xo_ref[...] = _gla_output(o_ref[...], p_ref[...], x_ref[...], gt_ref[...],
                              og_ref[...], wout_ref[...])


def _gla_out_call(o, proj, x, mod, layer, tm, w):
    n, d = x.shape
    gate_spec = _mod_specs(mod, layer, 0, tm, None)[2]
    g_block = (2 * GLA_DK + GLA_DV) // GLA_DV
    in_specs = [_row_spec(tm, GLA_DV),
                pl.BlockSpec((tm, GLA_DV), lambda i: (i, g_block)),
                _row_spec(tm, d), gate_spec,
                _const_spec((1, GLA_DV_HEAD)),
                _const_spec((GLA_DV, d))]
    return pl.pallas_call(
        _gla_out_kernel,
        grid=(n // tm,),
        in_specs=in_specs,
        out_specs=_row_spec(tm, d),
        out_shape=jax.ShapeDtypeStruct((n, d), F32),
        compiler_params=pltpu.CompilerParams(
            dimension_semantics=("arbitrary",), vmem_limit_bytes=VMEM_LIMIT),
        name="gla_out_sample",
    )(o, proj, x, mod, w["out_g"], w["w_out"])


PROMPT_FFN_TILE = 1024
PROMPT_GMLP_TILE = 1024
PROMPT_GLA_TILE = 1024
SAMPLE_TILE = 256
SAMPLE_SEQS_PER_STEP = 16


def kernel(x_prompt, x_sample, c_prompt, c_sample, state_gla, ada_w, ada_b, norm_mix_g, norm_ffn_g,
           ffn_w1, ffn_b1, ffn_w2, ffn_b2, gmlp_w_in, gmlp_b_in, gmlp_ln_g, gmlp_ln_b, gmlp_w_s,
           gmlp_b_s, gmlp_w_out, gmlp_b_out, gla_w_in, gla_w_gate2, gla_b_gate, gla_norm_g,
           gla_w_out, final_norm_g):
    n_seq_p, seq_p, d = x_prompt.shape
    n_seq_s, seq_s, _ = x_sample.shape
    assert d == D_MODEL and GMLP_SUBTILE % CHUNK_A == 0 and GLA_SUBTILE % GLA_FAST_CHUNK == 0
    assert all(seq_p % t == 0 for t in (PROMPT_FFN_TILE, PROMPT_GMLP_TILE, PROMPT_GLA_TILE))
    assert seq_s < SUBLANES and SUBLANES % seq_s == 0 and (n_seq_s * seq_s) % SAMPLE_TILE == 0
    row = lambda a: a.reshape(1, -1)

    mod_p, mod_s = _ada_call(c_prompt, c_sample, seq_s, ada_w, ada_b)
    mod_p = mod_p.reshape(mod_p.shape[0], n_seq_p, 1, N_MOD * d)

    causal = np.tril(np.ones((CHUNK_A, CHUNK_A), dtype=bool))
    ws = jnp.where(causal[None], gmlp_w_s, jnp.zeros_like(gmlp_w_s))
    reps = CHUNK_A // seq_s
    eye = np.eye(reps, dtype=np.float32)
    ws_s = jnp.einsum("ab,gij->gaibj", eye, ws[:, :seq_s, :seq_s]).reshape(
        GMLP_GROUPS, CHUNK_A, CHUNK_A)
    bias_p = jnp.repeat(gmlp_b_s.T, GMLP_GROUP_W, axis=1)
    bias_s = jnp.tile(bias_p[:seq_s], (reps, 1))

    gmlp_w = dict(norm_g=row(norm_mix_g[0]), w_in=gmlp_w_in, b_in=row(gmlp_b_in),
                  ln_g=row(gmlp_ln_g), ln_b=row(gmlp_ln_b), w_out=gmlp_w_out,
                  b_out=row(gmlp_b_out))
    gmlp_w = dict(gmlp_w, mix_p=ws.astype(BF16), bias_p=bias_p, mix_s=ws_s.astype(BF16),
                  bias_s=bias_s)

    depth = ffn_w1.shape[0]
    ffn_w = dict(norm_g=norm_ffn_g.reshape(depth, 1, d), w1=ffn_w1,
                 b1=ffn_b1.reshape(depth, 1, D_FF), w2=ffn_w2,
                 b2=ffn_b2.reshape(depth, 1, d), final_g=row(final_norm_g))

    w_in_pad = jnp.pad(gla_w_in, ((0, 0), (0, GLA_IN_PAD - gla_w_in.shape[1]))).astype(BF16)
    wg2_pad = jnp.pad(gla_w_gate2, ((0, GLA_GATE_PAD - GLA_GATE_RANK), (0, 0))).astype(BF16)
    blk_ones = np.kron(np.eye(2, dtype=np.float32),
                       np.ones((GLA_DK_HEAD, GLA_DK_HEAD), np.float32))
    gla_w = dict(norm_g=row(norm_mix_g[1]), w_in=w_in_pad, w_gate2=wg2_pad, b_gate=row(gla_b_gate),
                 out_g=row(gla_norm_g), w_out=gla_w_out.astype(BF16),
                 tril=jnp.asarray(np.tril(np.ones((GLA_TRIL, GLA_TRIL), np.float32)), BF16),
                 ones_bd=jnp.asarray(blk_ones, BF16))

    xp = x_prompt.reshape(n_seq_p * seq_p, d)
    xs = x_sample.reshape(n_seq_s * seq_s, d)
    grid_for = lambda tm_p: _TwoGroupGrid(xp.shape[0], xs.shape[0], tm_p, SAMPLE_TILE, seq_p)
    xp, xs, chunk_v = _gmlp_call(xp, xs, mod_p, mod_s, grid_for(PROMPT_GMLP_TILE), gmlp_w)
    xp, xs = _ffn_call(xp, xs, mod_p, mod_s, 0, grid_for(PROMPT_FFN_TILE), ffn_w, final_norm=False)
    xp, state_p = _gla_prompt_call(xp, mod_p, 1, n_seq_p, seq_p, PROMPT_GLA_TILE, gla_w)
    proj, log_a = _gla_proj_call(xs, mod_s, 1, SAMPLE_TILE, gla_w)
    o, state_s = _gla_step_call(proj, log_a, state_gla, seq_s, SAMPLE_SEQS_PER_STEP)
    xs = _gla_out_call(o, proj, xs, mod_s, 1, SAMPLE_TILE, gla_w)
    xp, xs = _ffn_call(xp, xs, mod_p, mod_s, 1, grid_for(PROMPT_FFN_TILE), ffn_w, final_norm=True)

    return (xp.reshape(x_prompt.shape), xs.reshape(x_sample.shape), state_p, state_s,
            chunk_v.reshape(n_seq_s, seq_s, GMLP_WIDTH))
```

```python
import functools

import jax
import jax.numpy as jnp
import numpy as np
from jax import lax
from jax.experimental import pallas as pl
from jax.experimental.pallas import tpu as pltpu

F32 = jnp.float32
BF16 = jnp.bfloat16

D_MODEL = 1024
N_MOD = 6
CHUNK_A = 128
GMLP_WIDTH = D_MODEL
GMLP_GROUPS = 4
GMLP_GROUP_W = GMLP_WIDTH // GMLP_GROUPS
GLA_HEADS = 4
GLA_DK = D_MODEL // 2
GLA_DV = D_MODEL
GLA_DK_HEAD = GLA_DK // GLA_HEADS
GLA_DV_HEAD = GLA_DV // GLA_HEADS
GLA_GATE_RANK = 16
GLA_TAU = 16.0
GLA_CHUNK = 64
GLA_FAST_CHUNK = 128
GLA_TRIL = max(GLA_CHUNK, GLA_FAST_CHUNK)
GLA_SAFE_EXPONENT = 80.0
GLA_QKVG = 2 * GLA_DK + 2 * GLA_DV
GLA_GATE_PAD = 128
GLA_IN_PAD = GLA_QKVG + GLA_GATE_PAD
D_FF = 4 * D_MODEL
EPS = 1e-6

SUBLANES = 8
BF16_SUBLANES = 16
ADA_ROWS = 256
FFN_CHUNK = 1024
FFN_SUBTILE = 512
GMLP_SUBTILE = 256
GLA_SUBTILE = 512
VMEM_LIMIT = 56 * 1024 * 1024
WEIGHT_LOAD_CHUNKS = 16


def _rms(x, g):
    return x * lax.rsqrt(jnp.mean(x * x, axis=-1, keepdims=True) + EPS) * g


def _modulated_norm(x, g, scale, shift):
    inv = lax.rsqrt(jnp.mean(x * x, axis=-1, keepdims=True) + EPS)
    return (x * inv) * (g * (1.0 + scale)) + shift


def _dot(a, b):
    return jnp.dot(a.astype(BF16), b.astype(BF16), preferred_element_type=F32)


def _dot_nt(a, b):
    return lax.dot_general(a.astype(BF16), b.astype(BF16), (((1,), (1,)), ((), ())),
                           preferred_element_type=F32)


def _dot_tn(a, b):
    return lax.dot_general(a.astype(BF16), b.astype(BF16), (((0,), (0,)), ((), ())),
                           preferred_element_type=F32)


def _gelu_tanh(x):
    c1 = -2.0 * 0.7978845608028654 * 1.4426950408889634
    c2 = c1 * 0.044715
    return x / (1.0 + jnp.exp2(x * (c1 + c2 * (x * x))))


def _silu(x):
    return x * jax.nn.sigmoid(x)


def _log_sigmoid(x):
    return -(jnp.maximum(-x, 0.0) + jnp.log1p(jnp.exp(-jnp.abs(x))))


def _bf16_terms(x, n):
    terms = []
    for _ in range(n - 1):
        t = x.astype(BF16)
        terms.append(t)
        x = x - t.astype(F32)
    terms.append(x.astype(BF16))
    return terms


def _ada_kernel(c_ref, wa_ref, wb_ref, b_ref, op_ref, os_ref, s_scr, acc_scr, *, n_prompt, tk):
    k = pl.program_id(1)

    @pl.when((pl.program_id(0) == 0) & (k == 0))
    def _():
        for kk in range(s_scr.shape[0]):
            s_scr[kk] = _silu(c_ref[:, kk * tk:(kk + 1) * tk]).astype(BF16)

    @pl.when(k == 0)
    def _():
        acc_scr[...] = jnp.broadcast_to(b_ref[...], acc_scr.shape)

    s = s_scr[k]
    half = N_MOD // 2
    for j in range(N_MOD):
        cols = slice(j * D_MODEL, (j + 1) * D_MODEL)
        w_ref, jj = (wa_ref, j) if j < half else (wb_ref, j - half)
        w = w_ref[:, jj * D_MODEL:(jj + 1) * D_MODEL].astype(BF16)
        acc_scr[:, cols] += jnp.dot(s, w, preferred_element_type=F32)

    @pl.when(k == pl.num_programs(1) - 1)
    def _():
        op_ref[...] = acc_scr[0:n_prompt, :]
        os_ref[...] = acc_scr[n_prompt:n_prompt + os_ref.shape[0], :]


def _ada_call(c_prompt, c_sample, ada_w, ada_b):
    depth = ada_w.shape[0]
    n_prompt, n_seq = c_prompt.shape[0], c_sample.shape[0]
    d = D_MODEL
    pad = -(n_prompt + n_seq) % BF16_SUBLANES
    c_all = jnp.concatenate([c_prompt, c_sample, jnp.zeros((pad, d), F32)], axis=0)
    rows = c_all.shape[0]
    tk = ADA_ROWS
    width = N_MOD * d
    return pl.pallas_call(
        functools.partial(_ada_kernel, n_prompt=n_prompt, tk=tk),
        grid=(depth, d // tk),
        in_specs=[
            pl.BlockSpec((rows, d), lambda l, k: (0, 0)),
            pl.BlockSpec((None, tk, width // 2), lambda l, k: (l, k, 0)),
            pl.BlockSpec((None, tk, width // 2), lambda l, k: (l, k, 1)),
            pl.BlockSpec((None, 1, width), lambda l, k: (l, 0, 0)),
        ],
        out_specs=[
            pl.BlockSpec((None, n_prompt, width), lambda l, k: (l, 0, 0)),
            pl.BlockSpec((None, n_seq, width), lambda l, k: (l, 0, 0)),
        ],
        out_shape=[
            jax.ShapeDtypeStruct((depth, n_prompt, width), F32),
            jax.ShapeDtypeStruct((depth, n_seq, width), F32),
        ],
        scratch_shapes=[pltpu.VMEM((d // tk, rows, tk), BF16),
                        pltpu.VMEM((rows, width), F32)],
        compiler_params=pltpu.CompilerParams(
            dimension_semantics=("arbitrary", "arbitrary"), vmem_limit_bytes=VMEM_LIMIT),
        name="adaln_mod",
    )(c_all, ada_w, ada_w, ada_b.reshape(depth, 1, N_MOD * d))


def _const_spec(shape):
    zeros = (0,) * len(shape)
    return pl.BlockSpec(shape, lambda i: zeros, pipeline_mode=pl.Buffered(1))


def _layer_spec(shape, layer):
    zeros = (0,) * len(shape)
    return pl.BlockSpec((None,) + shape, lambda i: (layer,) + zeros,
                        pipeline_mode=pl.Buffered(1))


def _mod_specs(mod, layer, first_chunk, tm, rows_per_seq):
    d = D_MODEL
    specs = []
    for j in range(first_chunk, first_chunk + 3):
        if rows_per_seq < tm:
            specs.append(pl.BlockSpec((None, tm // rows_per_seq, d), lambda i, j=j: (layer, i, j)))
        else:
            tiles = rows_per_seq // tm
            specs.append(pl.BlockSpec((None, None, 1, d),
                                      lambda i, j=j, tiles=tiles: (layer, i // tiles, 0, j)))
    return specs


def _token_copy_matrix(tm, rows_per_seq):
    copy = np.repeat(np.eye(tm // rows_per_seq, dtype=np.float32), rows_per_seq, axis=0)
    return jnp.asarray(np.tile(copy, (1, 2)), BF16)


def _rows_to_tokens(e_ref, mod_ref):
    terms = jnp.concatenate(_bf16_terms(mod_ref[...], 2), axis=0)
    return jnp.dot(e_ref[...], terms, preferred_element_type=F32)


def _mod_rows(ref, rows):
    return ref[...] if ref.shape[0] == 1 else ref[rows, :]


def _row_spec(tm, width):
    return pl.BlockSpec((tm, width), lambda i: (i, 0))


class _TwoGroupGrid:
    def __init__(self, n_prompt_rows, n_sample_rows, tm_p, tm_s, seq_len, sample_seq_len):
        self.tm_p, self.tm_s = tm_p, tm_s
        self.sample_seq_len = sample_seq_len
        self.p_steps = n_prompt_rows // tm_p
        self.s_steps = n_sample_rows // tm_s
        self.tiles_per_seq = seq_len // tm_p
        self.steps = self.p_steps + self.s_steps

    def p_idx(self, i):
        return jnp.minimum(i, self.p_steps - 1)

    def s_idx(self, i):
        return jnp.maximum(i - self.p_steps, 0)

    def rows_p(self, width):
        return pl.BlockSpec((self.tm_p, width), lambda i: (self.p_idx(i), 0))

    def rows_s(self, width, col_block=0):
        return pl.BlockSpec((self.tm_s, width), lambda i: (self.s_idx(i), col_block))

    def mods_p(self, layer, first_chunk):
        return [pl.BlockSpec((None, None, 1, D_MODEL),
                             lambda i, j=j: (layer, self.p_idx(i) // self.tiles_per_seq, 0, j))
                for j in range(first_chunk, first_chunk + 3)]

    def mods_s(self, layer, first_chunk):
        return [pl.BlockSpec((None, self.tm_s // self.sample_seq_len, D_MODEL),
                             lambda i, j=j: (layer, self.s_idx(i), j))
                for j in range(first_chunk, first_chunk + 3)]

    def token_copy_spec(self):
        return _const_spec((self.tm_s, 2 * self.tm_s // self.sample_seq_len))

    def token_copy_matrix(self):
        return _token_copy_matrix(self.tm_s, self.sample_seq_len)


def _gmlp_rows(x_ref, sh_ref, sc_ref, gt_ref, mix_ref, bs_ref, xo_ref, v_ref, z_scr, w, tm, sub):
    ng_ref, win_ref, bin_ref, lng_ref, lnb_ref, wout_ref, bout_ref = w
    groups = [slice(r0, r0 + sub) for r0 in range(0, tm, sub)]

    def project(gi):
        rows = groups[gi]
        h = _modulated_norm(x_ref[rows, :], ng_ref[...], _mod_rows(sc_ref, rows),
                            _mod_rows(sh_ref, rows))
        z_scr[gi % 2] = _dot(h, win_ref[...])

    project(0)
    for gi, rows in enumerate(groups):
        if gi + 1 < len(groups):
            project(gi + 1)
        z = _gelu_tanh(z_scr[gi % 2] + bin_ref[...])
        u = z[:, :GMLP_WIDTH]
        v = z[:, GMLP_WIDTH:]
        mu = jnp.mean(v, axis=-1, keepdims=True)
        vc = v - mu
        var = jnp.mean(vc * vc, axis=-1, keepdims=True)
        v = vc * lax.rsqrt(var + EPS) * lng_ref[...] + lnb_ref[...]
        if v_ref is not None:
            v_ref[rows, :] = v
        vb = v.astype(BF16)
        mixed = []
        for c in range(sub // CHUNK_A):
            cols = []
            for g in range(GMLP_GROUPS):
                blk = vb[c * CHUNK_A:(c + 1) * CHUNK_A, g * GMLP_GROUP_W:(g + 1) * GMLP_GROUP_W]
                cols.append(jnp.dot(mix_ref[g], blk, preferred_element_type=F32))
            mixed.append(jnp.concatenate(cols, axis=1) + bs_ref[...])
        s = jnp.concatenate(mixed, axis=0)
        y = _dot(u * s, wout_ref[...]) + bout_ref[...]
        xo_ref[rows, :] = x_ref[rows, :] + _mod_rows(gt_ref, rows) * y


def _gmlp_kernel(xp_ref, shp_ref, scp_ref, gtp_ref, xs_ref, shs_ref, scs_ref, gts_ref, e_ref,
                 ng_ref, win_hbm, bin_ref, lng_ref, lnb_ref, wout_hbm, bout_ref,
                 mixp_ref, bsp_ref, mixs_ref, bss_ref, op_ref, os_ref, vs_ref, z_scr,
                 win_ref, wout_ref, stage_in, stage_out, sems_in, sems_out, *, grid):
    w = (ng_ref, win_ref, bin_ref, lng_ref, lnb_ref, wout_ref, bout_ref)
    step = pl.program_id(0)

    @pl.when(step == 0)
    def _():
        _load_as_bf16([(win_hbm, win_ref, stage_in, sems_in),
                       (wout_hbm, wout_ref, stage_out, sems_out)])

    @pl.when(step < grid.p_steps)
    def _():
        _gmlp_rows(xp_ref, shp_ref, scp_ref, gtp_ref, mixp_ref, bsp_ref, op_ref, None, z_scr, w,
                   grid.tm_p, GMLP_SUBTILE)

    @pl.when(step >= grid.p_steps)
    def _():
        sh, sc, gt = (_rows_to_tokens(e_ref, r) for r in (shs_ref, scs_ref, gts_ref))
        _gmlp_rows(xs_ref, sh, sc, gt, mixs_ref, bss_ref, os_ref, vs_ref, z_scr, w,
                   grid.tm_s, GMLP_SUBTILE)


def _gmlp_call(xp, xs, mod_p, mod_s, grid, w):
    d = D_MODEL
    mix_spec = _const_spec((GMLP_GROUPS, CHUNK_A, CHUNK_A))
    bias_spec = _const_spec((CHUNK_A, GMLP_WIDTH))
    in_specs = ([grid.rows_p(d)] + grid.mods_p(0, 0) + [grid.rows_s(d)] + grid.mods_s(0, 0) + [
        grid.token_copy_spec(),
        _const_spec((1, d)),
        pl.BlockSpec(memory_space=pl.ANY),
        _const_spec((1, 2 * GMLP_WIDTH)),
        _const_spec((1, GMLP_WIDTH)),
        _const_spec((1, GMLP_WIDTH)),
        pl.BlockSpec(memory_space=pl.ANY),
        _const_spec((1, d)),
        mix_spec, bias_spec, mix_spec, bias_spec,
    ])
    return pl.pallas_call(
        functools.partial(_gmlp_kernel, grid=grid),
        grid=(grid.steps,),
        in_specs=in_specs,
        out_specs=[grid.rows_p(d), grid.rows_s(d), grid.rows_s(GMLP_WIDTH)],
        out_shape=[jax.ShapeDtypeStruct(xp.shape, F32), jax.ShapeDtypeStruct(xs.shape, F32),
                   jax.ShapeDtypeStruct((xs.shape[0], GMLP_WIDTH), F32)],
        scratch_shapes=[pltpu.VMEM((2, GMLP_SUBTILE, 2 * GMLP_WIDTH), F32),
                        pltpu.VMEM((d, 2 * GMLP_WIDTH), BF16),
                        pltpu.VMEM((GMLP_WIDTH, d), BF16),
                        pltpu.VMEM((2, d // WEIGHT_LOAD_CHUNKS, 2 * GMLP_WIDTH), F32),
                        pltpu.VMEM((2, GMLP_WIDTH // WEIGHT_LOAD_CHUNKS, d), F32),
                        pltpu.SemaphoreType.DMA((2,)),
                        pltpu.SemaphoreType.DMA((2,))],
        compiler_params=pltpu.CompilerParams(
            dimension_semantics=("arbitrary",), vmem_limit_bytes=VMEM_LIMIT),
        name="gmlp_mixer",
    )(xp, mod_p, mod_p, mod_p, xs, mod_s, mod_s, mod_s, grid.token_copy_matrix(),
      w["norm_g"], w["w_in"], w["b_in"],
      w["ln_g"], w["ln_b"], w["w_out"], w["b_out"], w["mix_p"], w["bias_p"], w["mix_s"],
      w["bias_s"])


def _ffn_rows(x_ref, sh_ref, sc_ref, gt_ref, o_ref, h_scr, w, final_norm, tm):
    ng_ref, w1_ref, b1_ref, w2_ref, b2_ref, fg_ref = w
    sub = min(FFN_SUBTILE, tm)
    groups = [slice(r0, r0 + sub) for r0 in range(0, tm, sub)]

    def normalise(gi):
        rows = groups[gi]
        h_scr[gi % 2, 0:sub, :] = _modulated_norm(
            x_ref[rows, :], ng_ref[...], _mod_rows(sc_ref, rows),
            _mod_rows(sh_ref, rows)).astype(BF16)

    normalise(0)
    for gi, rows in enumerate(groups):
        if gi + 1 < len(groups):
            normalise(gi + 1)
        h = h_scr[gi % 2, 0:sub, :]
        acc = jnp.zeros((sub, D_MODEL), F32)
        for j in range(D_FF // FFN_CHUNK):
            cols = slice(j * FFN_CHUNK, (j + 1) * FFN_CHUNK)
            a = jnp.dot(h, w1_ref[:, cols], preferred_element_type=F32) + b1_ref[:, cols]
            r = jnp.square(jnp.maximum(a, 0.0)).astype(BF16)
            acc = acc + jnp.dot(r, w2_ref[cols, :], preferred_element_type=F32)
        y = x_ref[rows, :] + _mod_rows(gt_ref, rows) * (acc + b2_ref[...])
        if final_norm:
            y = _rms(y, fg_ref[...])
        o_ref[rows, :] = y


def _load_as_bf16(jobs):
    def chunk_copy(job, c):
        src_hbm, _, stage_ref, sem_ref = job
        rows = pl.ds(c * stage_ref.shape[1], stage_ref.shape[1])
        return pltpu.make_async_copy(src_hbm.at[rows, :], stage_ref.at[c % 2], sem_ref.at[c % 2])

    n_chunks = {job[1].shape[0] // job[2].shape[1] for job in jobs}
    assert len(n_chunks) == 1
    n_chunks = n_chunks.pop()
    for job in jobs:
        chunk_copy(job, 0).start()
    for c in range(n_chunks):
        for job in jobs:
            if c + 1 < n_chunks:
                chunk_copy(job, c + 1).start()
        for job in jobs:
            _, dst_ref, stage_ref, _ = job
            chunk_copy(job, c).wait()
            rows = stage_ref.shape[1]
            dst_ref[c * rows:(c + 1) * rows, :] = stage_ref[c % 2].astype(BF16)


def _ffn_kernel(xp_ref, shp_ref, scp_ref, gtp_ref, xs_ref, shs_ref, scs_ref, gts_ref, e_ref,
                ng_ref, w1_hbm, b1_ref, w2_hbm, b2_ref, fg_ref, op_ref, os_ref, h_scr,
                w1_ref, w2_ref, stage1, stage2, sems1, sems2, *, final_norm, grid, layer):
    w = (ng_ref, w1_ref, b1_ref, w2_ref, b2_ref, fg_ref)
    step = pl.program_id(0)

    @pl.when(step == 0)
    def _():
        _load_as_bf16([(w1_hbm.at[layer], w1_ref, stage1, sems1),
                       (w2_hbm.at[layer], w2_ref, stage2, sems2)])

    @pl.when(step < grid.p_steps)
    def _():
        _ffn_rows(xp_ref, shp_ref, scp_ref, gtp_ref, op_ref, h_scr, w, final_norm, grid.tm_p)

    @pl.when(step >= grid.p_steps)
    def _():
        sh, sc, gt = (_rows_to_tokens(e_ref, r) for r in (shs_ref, scs_ref, gts_ref))
        _ffn_rows(xs_ref, sh, sc, gt, os_ref, h_scr, w, final_norm, grid.tm_s)


def _ffn_call(xp, xs, mod_p, mod_s, layer, grid, w, final_norm):
    d = D_MODEL
    in_specs = ([grid.rows_p(d)] + grid.mods_p(layer, 3) + [grid.rows_s(d)]
                + grid.mods_s(layer, 3) + [
        grid.token_copy_spec(),
        _layer_spec((1, d), layer),
        pl.BlockSpec(memory_space=pl.ANY),
        _layer_spec((1, D_FF), layer),
        pl.BlockSpec(memory_space=pl.ANY),
        _layer_spec((1, d), layer),
        _const_spec((1, d)),
    ])
    return pl.pallas_call(
        functools.partial(_ffn_kernel, final_norm=final_norm, grid=grid, layer=layer),
        grid=(grid.steps,),
        in_specs=in_specs,
        out_specs=[grid.rows_p(d), grid.rows_s(d)],
        out_shape=[jax.ShapeDtypeStruct(xp.shape, F32), jax.ShapeDtypeStruct(xs.shape, F32)],
        scratch_shapes=[pltpu.VMEM((2, FFN_SUBTILE, d), BF16),
                        pltpu.VMEM((d, D_FF), BF16),
                        pltpu.VMEM((D_FF, d), BF16),
                        pltpu.VMEM((2, d // WEIGHT_LOAD_CHUNKS, D_FF), F32),
                        pltpu.VMEM((2, D_FF // WEIGHT_LOAD_CHUNKS, d), F32),
                        pltpu.SemaphoreType.DMA((2,)),
                        pltpu.SemaphoreType.DMA((2,))],
        compiler_params=pltpu.CompilerParams(
            dimension_semantics=("arbitrary",), vmem_limit_bytes=VMEM_LIMIT),
        name="sqrelu_mlp",
    )(xp, mod_p, mod_p, mod_p, xs, mod_s, mod_s, mod_s, grid.token_copy_matrix(),
      w["norm_g"], w["w1"], w["b1"], w["w2"],
      w["b2"], w["final_g"])


def _gla_project(x, ng, scale, shift, win, wg2, bg):
    h = _modulated_norm(x, ng, scale, shift)
    proj = _dot(h, win)
    gate = _dot(proj[:, GLA_QKVG:], wg2) + bg
    log_a = _log_sigmoid(gate) * (1.0 / GLA_TAU)
    return proj[:, :GLA_QKVG], log_a


def _gla_output(o, gate_in, x, gt, og, wout):
    parts = []
    for h in range(GLA_HEADS):
        cols = slice(h * GLA_DV_HEAD, (h + 1) * GLA_DV_HEAD)
        parts.append(_rms(o[:, cols], og))
    on = jnp.concatenate(parts, axis=1)
    y = _dot(on * _silu(gate_in), wout)
    return x + gt * y


def _as_column(row):
    return jnp.broadcast_to(row, (SUBLANES, row.shape[1])).T[:, 0:1]


def _chunk_cumsum(tril, log_a):
    head = log_a.astype(BF16)
    rest = (log_a - head.astype(F32)).astype(BF16)
    return (jnp.dot(tril, head, preferred_element_type=F32)
            + jnp.dot(tril, rest, preferred_element_type=F32))


def _level_masks(c):
    rr = lax.broadcasted_iota(jnp.int32, (c, c), 0)
    cc = lax.broadcasted_iota(jnp.int32, (c, c), 1)
    masks = []
    s = c
    while s > SUBLANES:
        half = s // 2
        same = (rr ^ cc) < s
        masks.append(same & ((rr & half) != 0) & ((cc & half) == 0))
        s = half
    return masks, rr, cc


def _block_rows(x, s, r):
    c, w = x.shape
    return jnp.concatenate(
        [jnp.broadcast_to(x[p * s + r:p * s + r + 1, :], (s, w)) for p in range(c // s)], axis=0)


def _diag_terms(q, k, a):
    c, w = q.shape
    ii = lax.broadcasted_iota(jnp.int32, (c, w), 0) & (SUBLANES - 1)
    decay = jnp.zeros((c, w), F32)
    terms = [None] * SUBLANES
    for j in range(SUBLANES - 1, -1, -1):
        if j < SUBLANES - 1:
            decay = decay * _block_rows(a, SUBLANES, j + 1)
        decay = jnp.where(ii == j, 1.0, decay)
        terms[j] = q * _block_rows(k, SUBLANES, j) * decay
    return jnp.concatenate(terms, axis=0)


def _gla_chunk_pair(q2, k2, v2, la2, b2, st_refs, ones_bd, masks, rr, cc):
    c = q2[0].shape[0]
    zs = []
    for q, k, la in zip(q2, k2, la2):
        zs.append(_diag_terms(q, k, jnp.exp(la)))
    sums = jnp.dot(jnp.concatenate(zs, axis=1).astype(BF16), ones_bd,
                   preferred_element_type=F32)
    outs = []
    for hh, (q, k, v, b, st_ref) in enumerate(zip(q2, k2, v2, b2, st_refs)):
        dk = q.shape[1]
        attn = jnp.zeros((c, c), F32)
        base = rr & ~(SUBLANES - 1)
        for j in range(SUBLANES):
            rj = sums[j * c:(j + 1) * c, hh * dk:hh * dk + c]
            attn = jnp.where(cc == base + j, rj, attn)
        s = c
        for mask in masks:
            half = s // 2
            e = jnp.exp(-jnp.abs(b - _block_rows(b, s, half)))
            attn = jnp.where(mask, _dot_nt(q * e, k * e), attn)
            s = half
        b_last = b[c - 1:c, :]
        st = st_ref[...]
        o = _dot(q * jnp.exp(b), st) + _dot(attn, v)
        kd = k * jnp.exp(b_last - b)
        st_ref[...] = _as_column(jnp.exp(b_last)) * st + _dot_tn(kd, v)
        outs.append(o)
    return outs


def _gla_prompt_kernel(x_ref, sh_ref, sc_ref, gt_ref, ng_ref, win_ref, wg2_ref, bg_ref,
                       og_ref, wout_ref, tril_ref, ones_ref, xo_ref, s_ref,
                       proj_scr, ga_scr, la_scr, b_scr, o_scr, st_scr, *, tm, sub, tiles_per_seq):
    tile = pl.program_id(0) % tiles_per_seq

    @pl.when(tile == 0)
    def _():
        st_scr[...] = jnp.zeros(st_scr.shape, F32)

    dk, dv = GLA_DK_HEAD, GLA_DV_HEAD
    qscale = GLA_DK_HEAD ** -0.5
    cf = GLA_FAST_CHUNK
    c = GLA_CHUNK
    groups = [slice(r0, r0 + sub) for r0 in range(0, tm, sub)]

    def project(rows):
        h = _modulated_norm(x_ref[rows, :], ng_ref[...], _mod_rows(sc_ref, rows),
                            _mod_rows(sh_ref, rows))
        proj = _dot(h, win_ref[...])
        proj_scr[rows, :] = proj[:, :GLA_QKVG]
        ga_scr[rows, :] = proj[:, GLA_QKVG:]

    def decays(rows):
        gate = _dot(ga_scr[rows, :], wg2_ref[...]) + bg_ref[...]
        log_a = _log_sigmoid(gate) * (1.0 / GLA_TAU)
        la_scr[rows, :] = log_a
        for ci in range(sub // cf):
            b_scr[rows.start + ci * cf:rows.start + (ci + 1) * cf, :] = _chunk_cumsum(
                tril_ref[0:cf, 0:cf], log_a[ci * cf:(ci + 1) * cf, :])
        return jnp.min(b_scr[rows, :]) >= -GLA_SAFE_EXPONENT

    def head_slices(rows, h):
        q = proj_scr[rows, h * dk:(h + 1) * dk] * qscale
        k = proj_scr[rows, GLA_DK + h * dk:GLA_DK + (h + 1) * dk]
        v = proj_scr[rows, 2 * GLA_DK + h * dv:2 * GLA_DK + (h + 1) * dv]
        return q, k, v

    def single_ref_rows(group):
        rr = lax.broadcasted_iota(jnp.int32, (cf, cf), 0)
        cc = lax.broadcasted_iota(jnp.int32, (cf, cf), 1)
        causal = rr >= cc
        chunks = [slice(r0, r0 + cf) for r0 in range(group.start, group.stop, cf)]
        for h in range(GLA_HEADS):
            qes, vbs, scores, kvs, decays = [], [], [], [], []
            for rows in chunks:
                q, k, v = head_slices(rows, h)
                b = b_scr[rows, h * dk:(h + 1) * dk]
                decay_last = jnp.exp(b[cf - 1:cf, :])
                qe = (q * jnp.exp(b)).astype(BF16)
                kt = k * jnp.exp(-b)
                vb = v.astype(BF16)
                scores.append(_dot_nt(qe, kt))
                kvs.append(_dot_tn(kt * decay_last, vb))
                qes.append(qe)
                vbs.append(vb)
                decays.append(_as_column(decay_last))
            st = st_scr[h]
            states = []
            for kv, decay_col in zip(kvs, decays):
                states.append(st)
                st = decay_col * st + kv
            st_scr[h] = st
            for rows, qe, vb, sc, st_in in zip(chunks, qes, vbs, scores, states):
                attn = jnp.where(causal, sc, 0.0).astype(BF16)
                lhs = jnp.concatenate([qe, attn], axis=1)
                rhs = jnp.concatenate([st_in.astype(BF16), vb], axis=0)
                o_scr[rows, h * dv:(h + 1) * dv] = jnp.dot(lhs, rhs, preferred_element_type=F32)

    def robust_rows(group):
        def robust_body(ci, carry):
            rows = pl.ds(pl.multiple_of(group.start + ci * c, c), c)
            masks, rr, cc = _level_masks(c)
            b_c = _chunk_cumsum(tril_ref[0:c, 0:c], la_scr[rows, :])
            outs = []
            for h0 in range(0, GLA_HEADS, 2):
                heads = (h0, h0 + 1)
                qkv = [head_slices(rows, h) for h in heads]
                la2 = [la_scr[rows, h * dk:(h + 1) * dk] for h in heads]
                b2 = [b_c[:, h * dk:(h + 1) * dk] for h in heads]
                st_refs = [st_scr.at[h] for h in heads]
                outs += _gla_chunk_pair([t[0] for t in qkv], [t[1] for t in qkv],
                                        [t[2] for t in qkv], la2, b2, st_refs, ones_ref[...],
                                        masks, rr, cc)
            o_scr[rows, :] = jnp.concatenate(outs, axis=1)
            return carry

        lax.fori_loop(0, sub // c, robust_body, 0)

    def output(rows):
        gate_in = proj_scr[rows, 2 * GLA_DK + GLA_DV:]
        xo_ref[rows, :] = _gla_output(o_scr[rows, :], gate_in, x_ref[rows, :],
                                      _mod_rows(gt_ref, rows), og_ref[...], wout_ref[...])

    project(groups[0])
    for gi, rows in enumerate(groups):
        single_ref_safe = decays(rows)
        following = groups[gi + 1] if gi + 1 < len(groups) else None

        @pl.when(single_ref_safe)
        def _():
            if following is not None:
                project(following)
            single_ref_rows(rows)
            output(rows)

        @pl.when(jnp.logical_not(single_ref_safe))
        def _():
            if following is not None:
                project(following)
            robust_rows(rows)
            output(rows)

    @pl.when(tile == tiles_per_seq - 1)
    def _():
        s_ref[...] = st_scr[...]


def _gla_prompt_call(x, mod, layer, n_seq, seq_len, tm, w):
    n, d = x.shape
    tiles_per_seq = seq_len // tm
    dk, dv = GLA_DK_HEAD, GLA_DV_HEAD
    in_specs = [_row_spec(tm, d)] + _mod_specs(mod, layer, 0, tm, seq_len) + [
        _const_spec((1, d)),
        _const_spec((d, GLA_IN_PAD)),
        _const_spec((GLA_GATE_PAD, GLA_DK)),
        _const_spec((1, GLA_DK)),
        _const_spec((1, dv)),
        _const_spec((GLA_DV, d)),
        _const_spec((GLA_TRIL, GLA_TRIL)),
        _const_spec((2 * dk, 2 * dk)),
    ]
    out_specs = [
        _row_spec(tm, d),
        pl.BlockSpec((None, GLA_HEADS, dk, dv), lambda i: (i // tiles_per_seq, 0, 0, 0)),
    ]
    out_shape = [
        jax.ShapeDtypeStruct((n, d), F32),
        jax.ShapeDtypeStruct((n_seq, GLA_HEADS, dk, dv), F32),
    ]
    return pl.pallas_call(
        functools.partial(_gla_prompt_kernel, tm=tm, sub=GLA_SUBTILE,
                          tiles_per_seq=tiles_per_seq),
        grid=(n // tm,),
        in_specs=in_specs,
        out_specs=out_specs,
        out_shape=out_shape,
        scratch_shapes=[
            pltpu.VMEM((tm, GLA_QKVG), F32),
            pltpu.VMEM((tm, GLA_GATE_PAD), F32),
            pltpu.VMEM((tm, GLA_DK), F32),
            pltpu.VMEM((tm, GLA_DK), F32),
            pltpu.VMEM((tm, GLA_DV), F32),
            pltpu.VMEM((GLA_HEADS, dk, dv), F32),
        ],
        compiler_params=pltpu.CompilerParams(
            dimension_semantics=("arbitrary",), vmem_limit_bytes=VMEM_LIMIT),
        name="gla_mixer_prompt",
    )(x, mod, mod, mod, w["norm_g"], w["w_in"], w["w_gate2"], w["b_gate"], w["out_g"],
      w["w_out"], w["tril"], w["ones_bd"])


def _gla_proj_kernel(x_ref, sh_ref, sc_ref, e_ref, ng_ref, win_ref, wg2_ref, bg_ref, p_ref,
                     la_ref):
    proj, log_a = _gla_project(x_ref[...], ng_ref[...], _rows_to_tokens(e_ref, sc_ref),
                               _rows_to_tokens(e_ref, sh_ref), win_ref[...], wg2_ref[...],
                               bg_ref[...])
    p_ref[...] = proj
    la_ref[...] = log_a


def _gla_proj_call(x, mod, layer, tm, seq_len, w):
    n, d = x.shape
    specs = _mod_specs(mod, layer, 0, tm, seq_len)
    in_specs = [_row_spec(tm, d), specs[0], specs[1],
                _const_spec((tm, 2 * tm // seq_len)),
                _const_spec((1, d)),
                _const_spec((d, GLA_IN_PAD)),
                _const_spec((GLA_GATE_PAD, GLA_DK)),
                _const_spec((1, GLA_DK))]
    return pl.pallas_call(
        _gla_proj_kernel,
        grid=(n // tm,),
        in_specs=in_specs,
        out_specs=[_row_spec(tm, GLA_QKVG), _row_spec(tm, GLA_DK)],
        out_shape=[jax.ShapeDtypeStruct((n, GLA_QKVG), F32),
                   jax.ShapeDtypeStruct((n, GLA_DK), F32)],
        compiler_params=pltpu.CompilerParams(
            dimension_semantics=("arbitrary",), vmem_limit_bytes=VMEM_LIMIT),
        name="gla_proj_sample",
    )(x, mod, mod, _token_copy_matrix(tm, seq_len), w["norm_g"], w["w_in"], w["w_gate2"],
      w["b_gate"])


def _gla_step_kernel(p_ref, la_ref, s0_ref, o_ref, s1_ref, *, seqs, steps):
    dk, dv = GLA_DK_HEAD, GLA_DV_HEAD
    qscale = GLA_DK_HEAD ** -0.5
    per_tile = SUBLANES // steps
    row = lax.broadcasted_iota(jnp.int32, (SUBLANES, dk), 0)

    def seq_rows(tile, s):
        if s:
            tile = pltpu.roll(tile, SUBLANES - s * steps, 0)
        keep = lax.broadcasted_iota(jnp.int32, tile.shape, 0) < steps
        return jnp.where(keep, tile, 0.0)

    def tile_body(ti, carry):
        rows = pl.ds(pl.multiple_of(ti * SUBLANES, SUBLANES), SUBLANES)
        p_tile = p_ref[rows, :]
        la_tile = la_ref[rows, :]
        o_tile = jnp.zeros((SUBLANES, GLA_DV), F32)
        for s in range(per_tile):
            bi = ti * per_tile + s
            p = seq_rows(p_tile, s)
            la_all = seq_rows(la_tile, s)
            outs = []
            for h in range(GLA_HEADS):
                q = p[:, h * dk:(h + 1) * dk] * qscale
                k = p[:, GLA_DK + h * dk:GLA_DK + (h + 1) * dk]
                v = p[:, 2 * GLA_DK + h * dv:2 * GLA_DK + (h + 1) * dv]
                la = la_all[:, h * dk:(h + 1) * dk]
                b = jnp.zeros((SUBLANES, dk), F32)
                for t in range(steps):
                    b = b + jnp.where(row >= t, la[t:t + 1, :], 0.0)
                b_last = b[steps - 1:steps, :]
                s0 = s0_ref[bi, h]
                o = _dot(q * jnp.exp(b), s0)
                for j in range(steps):
                    e = jnp.exp(jnp.minimum(b - b[j:j + 1, :], 0.0))
                    z = jnp.where(row >= j, q * k[j:j + 1, :] * e, 0.0)
                    o = o + jnp.sum(z, axis=-1, keepdims=True) * v[j:j + 1, :]
                outs.append(o)
                m = jnp.where(row == steps, jnp.exp(b_last), k * jnp.exp(b_last - b))
                m_t = m.T
                s1_ref[bi, h] = m_t[:, steps:steps + 1] * s0 + _dot(m_t, v)
            o_seq = jnp.concatenate(outs, axis=1)
            o_tile = o_tile + (pltpu.roll(o_seq, s * steps, 0) if s else o_seq)
        o_ref[rows, :] = o_tile
        return carry

    lax.fori_loop(0, seqs // per_tile, tile_body, 0)


def _gla_step_call(proj, log_a, state, steps, seqs):
    n_seq = state.shape[0]
    dk, dv = GLA_DK_HEAD, GLA_DV_HEAD
    rows = lambda width: pl.BlockSpec((seqs * steps, width), lambda i: (i, 0))
    st_spec = pl.BlockSpec((seqs, GLA_HEADS, dk, dv), lambda i: (i, 0, 0, 0))
    return pl.pallas_call(
        functools.partial(_gla_step_kernel, seqs=seqs, steps=steps),
        grid=(n_seq // seqs,),
        in_specs=[rows(GLA_QKVG), rows(GLA_DK), st_spec],
        out_specs=[rows(GLA_DV), st_spec],
        out_shape=[jax.ShapeDtypeStruct((n_seq * steps, GLA_DV), F32),
                   jax.ShapeDtypeStruct(state.shape, F32)],
        compiler_params=pltpu.CompilerParams(
            dimension_semantics=("arbitrary",), vmem_limit_bytes=VMEM_LIMIT),
        name="gla_step_sample",
    )(proj, log_a, state)


def _gla_out_kernel(o_ref, p_ref, x_ref, gt_ref, e_ref, og_ref, wout_ref, xo_ref):
    xo_ref[...] = _gla_output(o_ref[...], p_ref[...], x_ref[...], _rows_to_tokens(e_ref, gt_ref),
                              og_ref[...], wout_ref[...])


def _gla_out_call(o, proj, x, mod, layer, tm, seq_len, w):
    n, d = x.shape
    gate_spec = _mod_specs(mod, layer, 0, tm, seq_len)[2]
    g_block = (2 * GLA_DK + GLA_DV) // GLA_DV
    in_specs = [_row_spec(tm, GLA_DV),
                pl.BlockSpec((tm, GLA_DV), lambda i: (i, g_block)),
                _row_spec(tm, d), gate_spec,
                _const_spec((tm, 2 * tm // seq_len)),
                _const_spec((1, GLA_DV_HEAD)),
                _const_spec((GLA_DV, d))]
    return pl.pallas_call(
        _gla_out_kernel,
        grid=(n // tm,),
        in_specs=in_specs,
        out_specs=_row_spec(tm, d),
        out_shape=jax.ShapeDtypeStruct((n, d), F32),
        compiler_params=pltpu.CompilerParams(
            dimension_semantics=("arbitrary",), vmem_limit_bytes=VMEM_LIMIT),
        name="gla_out_sample",
    )(o, proj, x, mod, _token_copy_matrix(tm, seq_len), w["out_g"], w["w_out"])


PROMPT_FFN_TILE = 1024
PROMPT_GMLP_TILE = 1024
PROMPT_GLA_TILE = 1024
SAMPLE_TILE = 256
SAMPLE_SEQS_PER_STEP = 16


def kernel(x_prompt, x_sample, c_prompt, c_sample, state_gla, ada_w, ada_b, norm_mix_g, norm_ffn_g,
           ffn_w1, ffn_b1, ffn_w2, ffn_b2, gmlp_w_in, gmlp_b_in, gmlp_ln_g, gmlp_ln_b, gmlp_w_s,
           gmlp_b_s, gmlp_w_out, gmlp_b_out, gla_w_in, gla_w_gate2, gla_b_gate, gla_norm_g,
           gla_w_out, final_norm_g):
    n_seq_p, seq_p, d = x_prompt.shape
    n_seq_s, seq_s, _ = x_sample.shape
    assert d == D_MODEL and GMLP_SUBTILE % CHUNK_A == 0 and GLA_SUBTILE % GLA_FAST_CHUNK == 0
    assert all(seq_p % t == 0 for t in (PROMPT_FFN_TILE, PROMPT_GMLP_TILE, PROMPT_GLA_TILE))
    assert seq_s < SUBLANES and SUBLANES % seq_s == 0 and (n_seq_s * seq_s) % SAMPLE_TILE == 0
    row = lambda a: a.reshape(1, -1)

    mod_p, mod_s = _ada_call(c_prompt, c_sample, ada_w, ada_b)
    mod_p = mod_p.reshape(mod_p.shape[0], n_seq_p, 1, N_MOD * d)

    causal = np.tril(np.ones((CHUNK_A, CHUNK_A), dtype=bool))
    ws = jnp.where(causal[None], gmlp_w_s, jnp.zeros_like(gmlp_w_s))
    reps = CHUNK_A // seq_s
    eye = np.eye(reps, dtype=np.float32)
    ws_s = jnp.einsum("ab,gij->gaibj", eye, ws[:, :seq_s, :seq_s]).reshape(
        GMLP_GROUPS, CHUNK_A, CHUNK_A)
    bias_p = jnp.repeat(gmlp_b_s.T, GMLP_GROUP_W, axis=1)
    bias_s = jnp.tile(bias_p[:seq_s], (reps, 1))

    gmlp_w = dict(norm_g=row(norm_mix_g[0]), w_in=gmlp_w_in, b_in=row(gmlp_b_in),
                  ln_g=row(gmlp_ln_g), ln_b=row(gmlp_ln_b), w_out=gmlp_w_out,
                  b_out=row(gmlp_b_out))
    gmlp_w = dict(gmlp_w, mix_p=ws.astype(BF16), bias_p=bias_p, mix_s=ws_s.astype(BF16),
                  bias_s=bias_s)

    depth = ffn_w1.shape[0]
    ffn_w = dict(norm_g=norm_ffn_g.reshape(depth, 1, d), w1=ffn_w1,
                 b1=ffn_b1.reshape(depth, 1, D_FF), w2=ffn_w2,
                 b2=ffn_b2.reshape(depth, 1, d), final_g=row(final_norm_g))

    w_in_pad = jnp.pad(gla_w_in, ((0, 0), (0, GLA_IN_PAD - gla_w_in.shape[1]))).astype(BF16)
    wg2_pad = jnp.pad(gla_w_gate2, ((0, GLA_GATE_PAD - GLA_GATE_RANK), (0, 0))).astype(BF16)
    blk_ones = np.kron(np.eye(2, dtype=np.float32),
                       np.ones((GLA_DK_HEAD, GLA_DK_HEAD), np.float32))
    gla_w = dict(norm_g=row(norm_mix_g[1]), w_in=w_in_pad, w_gate2=wg2_pad, b_gate=row(gla_b_gate),
                 out_g=row(gla_norm_g), w_out=gla_w_out.astype(BF16),
                 tril=jnp.asarray(np.tril(np.ones((GLA_TRIL, GLA_TRIL), np.float32)), BF16),
                 ones_bd=jnp.asarray(blk_ones, BF16))

    xp = x_prompt.reshape(n_seq_p * seq_p, d)
    xs = x_sample.reshape(n_seq_s * seq_s, d)
    grid_for = lambda tm_p: _TwoGroupGrid(xp.shape[0], xs.shape[0], tm_p, SAMPLE_TILE, seq_p,
                                          seq_s)
    xp, xs, chunk_v = _gmlp_call(xp, xs, mod_p, mod_s, grid_for(PROMPT_GMLP_TILE), gmlp_w)
    xp, xs = _ffn_call(xp, xs, mod_p, mod_s, 0, grid_for(PROMPT_FFN_TILE), ffn_w, final_norm=False)
    xp, state_p = _gla_prompt_call(xp, mod_p, 1, n_seq_p, seq_p, PROMPT_GLA_TILE, gla_w)
    proj, log_a = _gla_proj_call(xs, mod_s, 1, SAMPLE_TILE, seq_s, gla_w)
    o, state_s = _gla_step_call(proj, log_a, state_gla, seq_s, SAMPLE_SEQS_PER_STEP)
    xs = _gla_out_call(o, proj, xs, mod_s, 1, SAMPLE_TILE, seq_s, gla_w)
    xp, xs = _ffn_call(xp, xs, mod_p, mod_s, 1, grid_for(PROMPT_FFN_TILE), ffn_w, final_norm=True)

    return (xp.reshape(x_prompt.shape), xs.reshape(x_sample.shape), state_p, state_s,
            chunk_v.reshape(n_seq_s, seq_s, GMLP_WIDTH))
```

```python
import functools

import jax
import jax.numpy as jnp
import numpy as np
from jax import lax
from jax.experimental import pallas as pl
from jax.experimental.pallas import tpu as pltpu

F32 = jnp.float32
BF16 = jnp.bfloat16

D_MODEL = 1024
N_MOD = 6
CHUNK_A = 128
GMLP_WIDTH = D_MODEL
GMLP_GROUPS = 4
GMLP_GROUP_W = GMLP_WIDTH // GMLP_GROUPS
GLA_HEADS = 4
GLA_DK = D_MODEL // 2
GLA_DV = D_MODEL
GLA_DK_HEAD = GLA_DK // GLA_HEADS
GLA_DV_HEAD = GLA_DV // GLA_HEADS
GLA_GATE_RANK = 16
GLA_TAU = 16.0
GLA_CHUNK = 64
GLA_FAST_CHUNK = 128
GLA_TRIL = max(GLA_CHUNK, GLA_FAST_CHUNK)
GLA_SAFE_EXPONENT = 80.0
GLA_QKVG = 2 * GLA_DK + 2 * GLA_DV
GLA_GATE_PAD = 128
D_FF = 4 * D_MODEL
EPS = 1e-6

SUBLANES = 8
BF16_SUBLANES = 16
ADA_ROWS = 256
FFN_CHUNK = 1024
FFN_SUBTILE = 512
GMLP_SUBTILE = 256
GLA_SUBTILE = 512
VMEM_LIMIT = 56 * 1024 * 1024
WEIGHT_LOAD_CHUNKS = 32
WEIGHT_STAGE_SLOTS = 4


def _rms(x, g):
    return x * lax.rsqrt(jnp.mean(x * x, axis=-1, keepdims=True) + EPS) * g


def _modulated_norm(x, g, scale, shift):
    inv = lax.rsqrt(jnp.mean(x * x, axis=-1, keepdims=True) + EPS)
    return (x * inv) * (g * (1.0 + scale)) + shift


def _dot(a, b):
    return jnp.dot(a.astype(BF16), b.astype(BF16), preferred_element_type=F32)


def _dot_nt(a, b):
    return lax.dot_general(a.astype(BF16), b.astype(BF16), (((1,), (1,)), ((), ())),
                           preferred_element_type=F32)


def _dot_tn(a, b):
    return lax.dot_general(a.astype(BF16), b.astype(BF16), (((0,), (0,)), ((), ())),
                           preferred_element_type=F32)


def _gelu_tanh(x):
    c1 = -2.0 * 0.7978845608028654 * 1.4426950408889634
    c2 = c1 * 0.044715
    return x / (1.0 + jnp.exp2(x * (c1 + c2 * (x * x))))


def _silu(x):
    return x * jax.nn.sigmoid(x)


def _log_sigmoid(x):
    return -(jnp.maximum(-x, 0.0) + jnp.log1p(jnp.exp(-jnp.abs(x))))


def _bf16_terms(x, n):
    terms = []
    for _ in range(n - 1):
        t = x.astype(BF16)
        terms.append(t)
        x = x - t.astype(F32)
    terms.append(x.astype(BF16))
    return terms


def _ada_kernel(c_ref, wa_ref, wb_ref, b_ref, op_ref, os_ref, s_scr, acc_scr, *, n_prompt, tk):
    k = pl.program_id(1)

    @pl.when((pl.program_id(0) == 0) & (k == 0))
    def _():
        for kk in range(s_scr.shape[0]):
            s_scr[kk] = _silu(c_ref[:, kk * tk:(kk + 1) * tk]).astype(BF16)

    @pl.when(k == 0)
    def _():
        acc_scr[...] = jnp.broadcast_to(b_ref[...], acc_scr.shape)

    s = s_scr[k]
    half = N_MOD // 2
    for j in range(N_MOD):
        cols = slice(j * D_MODEL, (j + 1) * D_MODEL)
        w_ref, jj = (wa_ref, j) if j < half else (wb_ref, j - half)
        w = w_ref[:, jj * D_MODEL:(jj + 1) * D_MODEL].astype(BF16)
        acc_scr[:, cols] += jnp.dot(s, w, preferred_element_type=F32)

    @pl.when(k == pl.num_programs(1) - 1)
    def _():
        op_ref[...] = acc_scr[0:n_prompt, :]
        os_ref[...] = acc_scr[n_prompt:n_prompt + os_ref.shape[0], :]


def _ada_call(c_prompt, c_sample, ada_w, ada_b):
    depth = ada_w.shape[0]
    n_prompt, n_seq = c_prompt.shape[0], c_sample.shape[0]
    d = D_MODEL
    pad = -(n_prompt + n_seq) % BF16_SUBLANES
    c_all = jnp.concatenate([c_prompt, c_sample, jnp.zeros((pad, d), F32)], axis=0)
    rows = c_all.shape[0]
    tk = ADA_ROWS
    width = N_MOD * d
    return pl.pallas_call(
        functools.partial(_ada_kernel, n_prompt=n_prompt, tk=tk),
        grid=(depth, d // tk),
        in_specs=[
            pl.BlockSpec((rows, d), lambda l, k: (0, 0)),
            pl.BlockSpec((None, tk, width // 2), lambda l, k: (l, k, 0)),
            pl.BlockSpec((None, tk, width // 2), lambda l, k: (l, k, 1)),
            pl.BlockSpec((None, 1, width), lambda l, k: (l, 0, 0)),
        ],
        out_specs=[
            pl.BlockSpec((None, n_prompt, width), lambda l, k: (l, 0, 0)),
            pl.BlockSpec((None, n_seq, width), lambda l, k: (l, 0, 0)),
        ],
        out_shape=[
            jax.ShapeDtypeStruct((depth, n_prompt, width), F32),
            jax.ShapeDtypeStruct((depth, n_seq, width), F32),
        ],
        scratch_shapes=[pltpu.VMEM((d // tk, rows, tk), BF16),
                        pltpu.VMEM((rows, width), F32)],
        compiler_params=pltpu.CompilerParams(
            dimension_semantics=("arbitrary", "arbitrary"), vmem_limit_bytes=VMEM_LIMIT),
        name="adaln_mod",
    )(c_all, ada_w, ada_w, ada_b.reshape(depth, 1, N_MOD * d))


def _const_spec(shape):
    zeros = (0,) * len(shape)
    return pl.BlockSpec(shape, lambda i: zeros, pipeline_mode=pl.Buffered(1))


def _layer_spec(shape, layer):
    zeros = (0,) * len(shape)
    return pl.BlockSpec((None,) + shape, lambda i: (layer,) + zeros,
                        pipeline_mode=pl.Buffered(1))


def _mod_specs(mod, layer, first_chunk, tm, rows_per_seq):
    d = D_MODEL
    specs = []
    for j in range(first_chunk, first_chunk + 3):
        if rows_per_seq < tm:
            specs.append(pl.BlockSpec((None, tm // rows_per_seq, d), lambda i, j=j: (layer, i, j)))
        else:
            tiles = rows_per_seq // tm
            specs.append(pl.BlockSpec((None, None, 1, d),
                                      lambda i, j=j, tiles=tiles: (layer, i // tiles, 0, j)))
    return specs


def _token_copy_matrix(tm, rows_per_seq):
    copy = np.repeat(np.eye(tm // rows_per_seq, dtype=np.float32), rows_per_seq, axis=0)
    return jnp.asarray(np.tile(copy, (1, 2)), BF16)


def _rows_to_tokens(e_ref, mod_ref):
    terms = jnp.concatenate(_bf16_terms(mod_ref[...], 2), axis=0)
    return jnp.dot(e_ref[...], terms, preferred_element_type=F32)


def _mod_rows(ref, rows):
    return ref[...] if ref.shape[0] == 1 else ref[rows, :]


def _row_spec(tm, width):
    return pl.BlockSpec((tm, width), lambda i: (i, 0))


class _TwoGroupGrid:
    def __init__(self, n_prompt_rows, n_sample_rows, tm_p, tm_s, seq_len, sample_seq_len):
        self.tm_p, self.tm_s = tm_p, tm_s
        self.sample_seq_len = sample_seq_len
        self.p_steps = n_prompt_rows // tm_p
        self.s_steps = n_sample_rows // tm_s
        self.tiles_per_seq = seq_len // tm_p
        self.steps = self.p_steps + self.s_steps

    def p_idx(self, i):
        return jnp.minimum(i, self.p_steps - 1)

    def s_idx(self, i):
        return jnp.maximum(i - self.p_steps, 0)

    def rows_p(self, width):
        return pl.BlockSpec((self.tm_p, width), lambda i: (self.p_idx(i), 0))

    def rows_s(self, width, col_block=0):
        return pl.BlockSpec((self.tm_s, width), lambda i: (self.s_idx(i), col_block))

    def mods_p(self, layer, first_chunk):
        return [pl.BlockSpec((None, None, 1, D_MODEL),
                             lambda i, j=j: (layer, self.p_idx(i) // self.tiles_per_seq, 0, j))
                for j in range(first_chunk, first_chunk + 3)]

    def mods_s(self, layer, first_chunk):
        return [pl.BlockSpec((None, self.tm_s // self.sample_seq_len, D_MODEL),
                             lambda i, j=j: (layer, self.s_idx(i), j))
                for j in range(first_chunk, first_chunk + 3)]

    def token_copy_spec(self):
        return _const_spec((self.tm_s, 2 * self.tm_s // self.sample_seq_len))

    def token_copy_matrix(self):
        return _token_copy_matrix(self.tm_s, self.sample_seq_len)


def _gmlp_rows(x_ref, sh_ref, sc_ref, gt_ref, mix_ref, bs_ref, xo_ref, v_ref, z_scr, w, tm, sub):
    ng_ref, win_ref, bin_ref, lng_ref, lnb_ref, wout_ref, bout_ref = w
    groups = [slice(r0, r0 + sub) for r0 in range(0, tm, sub)]

    def project(gi):
        rows = groups[gi]
        h = _modulated_norm(x_ref[rows, :], ng_ref[...], _mod_rows(sc_ref, rows),
                            _mod_rows(sh_ref, rows))
        z_scr[gi % 2] = _dot(h, win_ref[...])

    project(0)
    for gi, rows in enumerate(groups):
        if gi + 1 < len(groups):
            project(gi + 1)
        z = _gelu_tanh(z_scr[gi % 2] + bin_ref[...])
        u = z[:, :GMLP_WIDTH]
        v = z[:, GMLP_WIDTH:]
        mu = jnp.mean(v, axis=-1, keepdims=True)
        vc = v - mu
        var = jnp.mean(vc * vc, axis=-1, keepdims=True)
        v = vc * lax.rsqrt(var + EPS) * lng_ref[...] + lnb_ref[...]
        if v_ref is not None:
            v_ref[rows, :] = v
        vb = v.astype(BF16)
        mixed = []
        for c in range(sub // CHUNK_A):
            cols = []
            for g in range(GMLP_GROUPS):
                blk = vb[c * CHUNK_A:(c + 1) * CHUNK_A, g * GMLP_GROUP_W:(g + 1) * GMLP_GROUP_W]
                cols.append(jnp.dot(mix_ref[g], blk, preferred_element_type=F32))
            mixed.append(jnp.concatenate(cols, axis=1) + bs_ref[...])
        s = jnp.concatenate(mixed, axis=0)
        y = _dot(u * s, wout_ref[...]) + bout_ref[...]
        xo_ref[rows, :] = x_ref[rows, :] + _mod_rows(gt_ref, rows) * y


def _gmlp_kernel(xp_ref, shp_ref, scp_ref, gtp_ref, xs_ref, shs_ref, scs_ref, gts_ref, e_ref,
                 ng_ref, win_hbm, bin_ref, lng_ref, lnb_ref, wout_hbm, bout_ref,
                 mixp_ref, bsp_ref, mixs_ref, bss_ref, op_ref, os_ref, vs_ref, z_scr,
                 win_ref, wout_ref, stage_in, stage_out, sems_in, sems_out, *, grid):
    w = (ng_ref, win_ref, bin_ref, lng_ref, lnb_ref, wout_ref, bout_ref)
    step = pl.program_id(0)

    @pl.when(step == 0)
    def _():
        _load_as_bf16([(win_hbm, win_ref, stage_in, sems_in),
                       (wout_hbm, wout_ref, stage_out, sems_out)])

    @pl.when(step < grid.p_steps)
    def _():
        _gmlp_rows(xp_ref, shp_ref, scp_ref, gtp_ref, mixp_ref, bsp_ref, op_ref, None, z_scr, w,
                   grid.tm_p, GMLP_SUBTILE)

    @pl.when(step >= grid.p_steps)
    def _():
        sh, sc, gt = (_rows_to_tokens(e_ref, r) for r in (shs_ref, scs_ref, gts_ref))
        _gmlp_rows(xs_ref, sh, sc, gt, mixs_ref, bss_ref, os_ref, vs_ref, z_scr, w,
                   grid.tm_s, GMLP_SUBTILE)


def _gmlp_call(xp, xs, mod_p, mod_s, grid, w):
    d = D_MODEL
    mix_spec = _const_spec((GMLP_GROUPS, CHUNK_A, CHUNK_A))
    bias_spec = _const_spec((CHUNK_A, GMLP_WIDTH))
    in_specs = ([grid.rows_p(d)] + grid.mods_p(0, 0) + [grid.rows_s(d)] + grid.mods_s(0, 0) + [
        grid.token_copy_spec(),
        _const_spec((1, d)),
        pl.BlockSpec(memory_space=pl.ANY),
        _const_spec((1, 2 * GMLP_WIDTH)),
        _const_spec((1, GMLP_WIDTH)),
        _const_spec((1, GMLP_WIDTH)),
        pl.BlockSpec(memory_space=pl.ANY),
        _const_spec((1, d)),
        mix_spec, bias_spec, mix_spec, bias_spec,
    ])
    return pl.pallas_call(
        functools.partial(_gmlp_kernel, grid=grid),
        grid=(grid.steps,),
        in_specs=in_specs,
        out_specs=[grid.rows_p(d), grid.rows_s(d), grid.rows_s(GMLP_WIDTH)],
        out_shape=[jax.ShapeDtypeStruct(xp.shape, F32), jax.ShapeDtypeStruct(xs.shape, F32),
                   jax.ShapeDtypeStruct((xs.shape[0], GMLP_WIDTH), F32)],
        scratch_shapes=[pltpu.VMEM((2, GMLP_SUBTILE, 2 * GMLP_WIDTH), F32),
                        pltpu.VMEM((d, 2 * GMLP_WIDTH), BF16),
                        pltpu.VMEM((GMLP_WIDTH, d), BF16),
                        pltpu.VMEM((WEIGHT_STAGE_SLOTS, d // WEIGHT_LOAD_CHUNKS, 2 * GMLP_WIDTH),
                                   F32),
                        pltpu.VMEM((WEIGHT_STAGE_SLOTS, GMLP_WIDTH // WEIGHT_LOAD_CHUNKS, d), F32),
                        pltpu.SemaphoreType.DMA((WEIGHT_STAGE_SLOTS,)),
                        pltpu.SemaphoreType.DMA((WEIGHT_STAGE_SLOTS,))],
        compiler_params=pltpu.CompilerParams(
            dimension_semantics=("arbitrary",), vmem_limit_bytes=VMEM_LIMIT),
        name="gmlp_mixer",
    )(xp, mod_p, mod_p, mod_p, xs, mod_s, mod_s, mod_s, grid.token_copy_matrix(),
      w["norm_g"], w["w_in"], w["b_in"],
      w["ln_g"], w["ln_b"], w["w_out"], w["b_out"], w["mix_p"], w["bias_p"], w["mix_s"],
      w["bias_s"])


def _ffn_rows(x_ref, sh_ref, sc_ref, gt_ref, o_ref, h_scr, w, final_norm, tm):
    ng_ref, w1_ref, b1_ref, w2_ref, b2_ref, fg_ref = w
    sub = min(FFN_SUBTILE, tm)
    groups = [slice(r0, r0 + sub) for r0 in range(0, tm, sub)]

    def normalise(gi):
        rows = groups[gi]
        h_scr[gi % 2, 0:sub, :] = _modulated_norm(
            x_ref[rows, :], ng_ref[...], _mod_rows(sc_ref, rows),
            _mod_rows(sh_ref, rows)).astype(BF16)

    normalise(0)
    for gi, rows in enumerate(groups):
        if gi + 1 < len(groups):
            normalise(gi + 1)
        h = h_scr[gi % 2, 0:sub, :]
        acc = jnp.zeros((sub, D_MODEL), F32)
        for j in range(D_FF // FFN_CHUNK):
            cols = slice(j * FFN_CHUNK, (j + 1) * FFN_CHUNK)
            a = jnp.dot(h, w1_ref[:, cols], preferred_element_type=F32) + b1_ref[:, cols]
            r = jnp.square(jnp.maximum(a, 0.0)).astype(BF16)
            acc = acc + jnp.dot(r, w2_ref[cols, :], preferred_element_type=F32)
        y = x_ref[rows, :] + _mod_rows(gt_ref, rows) * (acc + b2_ref[...])
        if final_norm:
            y = _rms(y, fg_ref[...])
        o_ref[rows, :] = y


def _load_as_bf16(jobs):
    def chunk_copy(job, c):
        src_hbm, _, stage_ref, sem_ref = job
        slot = c % stage_ref.shape[0]
        rows = pl.ds(c * stage_ref.shape[1], stage_ref.shape[1])
        return pltpu.make_async_copy(src_hbm.at[rows, :], stage_ref.at[slot], sem_ref.at[slot])

    n_chunks = {job[1].shape[0] // job[2].shape[1] for job in jobs}
    n_slots = {job[2].shape[0] for job in jobs}
    assert len(n_chunks) == 1 and len(n_slots) == 1
    n_chunks, n_slots = n_chunks.pop(), n_slots.pop()
    for c in range(min(n_slots - 1, n_chunks)):
        for job in jobs:
            chunk_copy(job, c).start()
    for c in range(n_chunks):
        for job in jobs:
            if c + n_slots - 1 < n_chunks:
                chunk_copy(job, c + n_slots - 1).start()
        for job in jobs:
            _, dst_ref, stage_ref, _ = job
            chunk_copy(job, c).wait()
            rows = stage_ref.shape[1]
            dst_ref[c * rows:(c + 1) * rows, :] = stage_ref[c % n_slots].astype(BF16)


def _ffn_kernel(xp_ref, shp_ref, scp_ref, gtp_ref, xs_ref, shs_ref, scs_ref, gts_ref, e_ref,
                ng_ref, w1_hbm, b1_ref, w2_hbm, b2_ref, fg_ref, op_ref, os_ref, h_scr,
                w1_ref, w2_ref, stage1, stage2, sems1, sems2, *, final_norm, grid, layer):
    w = (ng_ref, w1_ref, b1_ref, w2_ref, b2_ref, fg_ref)
    step = pl.program_id(0)

    @pl.when(step == 0)
    def _():
        _load_as_bf16([(w1_hbm.at[layer], w1_ref, stage1, sems1),
                       (w2_hbm.at[layer], w2_ref, stage2, sems2)])

    @pl.when(step < grid.p_steps)
    def _():
        _ffn_rows(xp_ref, shp_ref, scp_ref, gtp_ref, op_ref, h_scr, w, final_norm, grid.tm_p)

    @pl.when(step >= grid.p_steps)
    def _():
        sh, sc, gt = (_rows_to_tokens(e_ref, r) for r in (shs_ref, scs_ref, gts_ref))
        _ffn_rows(xs_ref, sh, sc, gt, os_ref, h_scr, w, final_norm, grid.tm_s)


def _ffn_call(xp, xs, mod_p, mod_s, layer, grid, w, final_norm):
    d = D_MODEL
    in_specs = ([grid.rows_p(d)] + grid.mods_p(layer, 3) + [grid.rows_s(d)]
                + grid.mods_s(layer, 3) + [
        grid.token_copy_spec(),
        _layer_spec((1, d), layer),
        pl.BlockSpec(memory_space=pl.ANY),
        _layer_spec((1, D_FF), layer),
        pl.BlockSpec(memory_space=pl.ANY),
        _layer_spec((1, d), layer),
        _const_spec((1, d)),
    ])
    return pl.pallas_call(
        functools.partial(_ffn_kernel, final_norm=final_norm, grid=grid, layer=layer),
        grid=(grid.steps,),
        in_specs=in_specs,
        out_specs=[grid.rows_p(d), grid.rows_s(d)],
        out_shape=[jax.ShapeDtypeStruct(xp.shape, F32), jax.ShapeDtypeStruct(xs.shape, F32)],
        scratch_shapes=[pltpu.VMEM((2, FFN_SUBTILE, d), BF16),
                        pltpu.VMEM((d, D_FF), BF16),
                        pltpu.VMEM((D_FF, d), BF16),
                        pltpu.VMEM((WEIGHT_STAGE_SLOTS, d // WEIGHT_LOAD_CHUNKS, D_FF), F32),
                        pltpu.VMEM((WEIGHT_STAGE_SLOTS, D_FF // WEIGHT_LOAD_CHUNKS, d), F32),
                        pltpu.SemaphoreType.DMA((WEIGHT_STAGE_SLOTS,)),
                        pltpu.SemaphoreType.DMA((WEIGHT_STAGE_SLOTS,))],
        compiler_params=pltpu.CompilerParams(
            dimension_semantics=("arbitrary",), vmem_limit_bytes=VMEM_LIMIT),
        name="sqrelu_mlp",
    )(xp, mod_p, mod_p, mod_p, xs, mod_s, mod_s, mod_s, grid.token_copy_matrix(),
      w["norm_g"], w["w1"], w["b1"], w["w2"],
      w["b2"], w["final_g"])


def _gla_project(x, ng, scale, shift, win, wa, wg2, bg):
    h = _modulated_norm(x, ng, scale, shift).astype(BF16)
    proj = _dot(h, win)
    gate = _dot(_dot(h, wa), wg2) + bg
    log_a = _log_sigmoid(gate) * (1.0 / GLA_TAU)
    return proj, log_a


def _gla_output(o, gate_in, x, gt, og, wout):
    parts = []
    for h in range(GLA_HEADS):
        cols = slice(h * GLA_DV_HEAD, (h + 1) * GLA_DV_HEAD)
        parts.append(_rms(o[:, cols], og))
    on = jnp.concatenate(parts, axis=1)
    y = _dot(on * _silu(gate_in), wout)
    return x + gt * y


def _as_column(row):
    return jnp.broadcast_to(row, (SUBLANES, row.shape[1])).T[:, 0:1]


def _chunk_cumsum(tril, log_a):
    head = log_a.astype(BF16)
    rest = (log_a - head.astype(F32)).astype(BF16)
    return (jnp.dot(tril, head, preferred_element_type=F32)
            + jnp.dot(tril, rest, preferred_element_type=F32))


def _level_masks(c):
    rr = lax.broadcasted_iota(jnp.int32, (c, c), 0)
    cc = lax.broadcasted_iota(jnp.int32, (c, c), 1)
    masks = []
    s = c
    while s > SUBLANES:
        half = s // 2
        same = (rr ^ cc) < s
        masks.append(same & ((rr & half) != 0) & ((cc & half) == 0))
        s = half
    return masks, rr, cc


def _block_rows(x, s, r):
    c, w = x.shape
    return jnp.concatenate(
        [jnp.broadcast_to(x[p * s + r:p * s + r + 1, :], (s, w)) for p in range(c // s)], axis=0)


def _diag_terms(q, k, a):
    c, w = q.shape
    ii = lax.broadcasted_iota(jnp.int32, (c, w), 0) & (SUBLANES - 1)
    decay = jnp.zeros((c, w), F32)
    terms = [None] * SUBLANES
    for j in range(SUBLANES - 1, -1, -1):
        if j < SUBLANES - 1:
            decay = decay * _block_rows(a, SUBLANES, j + 1)
        decay = jnp.where(ii == j, 1.0, decay)
        terms[j] = q * _block_rows(k, SUBLANES, j) * decay
    return jnp.concatenate(terms, axis=0)


def _gla_chunk_pair(q2, k2, v2, la2, b2, st_refs, ones_bd, masks, rr, cc):
    c = q2[0].shape[0]
    zs = []
    for q, k, la in zip(q2, k2, la2):
        zs.append(_diag_terms(q, k, jnp.exp(la)))
    sums = jnp.dot(jnp.concatenate(zs, axis=1).astype(BF16), ones_bd,
                   preferred_element_type=F32)
    outs = []
    for hh, (q, k, v, b, st_ref) in enumerate(zip(q2, k2, v2, b2, st_refs)):
        dk = q.shape[1]
        attn = jnp.zeros((c, c), F32)
        base = rr & ~(SUBLANES - 1)
        for j in range(SUBLANES):
            rj = sums[j * c:(j + 1) * c, hh * dk:hh * dk + c]
            attn = jnp.where(cc == base + j, rj, attn)
        s = c
        for mask in masks:
            half = s // 2
            e = jnp.exp(-jnp.abs(b - _block_rows(b, s, half)))
            attn = jnp.where(mask, _dot_nt(q * e, k * e), attn)
            s = half
        b_last = b[c - 1:c, :]
        st = st_ref[...]
        o = _dot(q * jnp.exp(b), st) + _dot(attn, v)
        kd = k * jnp.exp(b_last - b)
        st_ref[...] = _as_column(jnp.exp(b_last)) * st + _dot_tn(kd, v)
        outs.append(o)
    return outs


def _gla_prompt_kernel(x_ref, sh_ref, sc_ref, gt_ref, ng_ref, win_hbm, wa_ref, wg2_ref, bg_ref,
                       og_ref, wout_hbm, tril_ref, ones_ref, xo_ref, s_ref, win_out, wout_out,
                       proj_scr, ga_scr, la_scr, b_scr, o_scr, st_scr, win_ref, wout_ref,
                       stage_in, stage_out, sems_in, sems_out, sems_export,
                       *, tm, sub, tiles_per_seq):
    tile = pl.program_id(0) % tiles_per_seq

    def export_copies():
        return (pltpu.make_async_copy(win_ref, win_out, sems_export.at[0]),
                pltpu.make_async_copy(wout_ref, wout_out, sems_export.at[1]))

    @pl.when(pl.program_id(0) == 0)
    def _():
        _load_as_bf16([(win_hbm.at[:, pl.ds(0, GLA_QKVG)], win_ref, stage_in, sems_in),
                       (wout_hbm, wout_ref, stage_out, sems_out)])
        for copy in export_copies():
            copy.start()

    @pl.when(pl.program_id(0) == pl.num_programs(0) - 1)
    def _():
        for copy in export_copies():
            copy.wait()

    @pl.when(tile == 0)
    def _():
        st_scr[...] = jnp.zeros(st_scr.shape, F32)

    dk, dv = GLA_DK_HEAD, GLA_DV_HEAD
    qscale = GLA_DK_HEAD ** -0.5
    cf = GLA_FAST_CHUNK
    c = GLA_CHUNK
    groups = [slice(r0, r0 + sub) for r0 in range(0, tm, sub)]

    def project(rows):
        h = _modulated_norm(x_ref[rows, :], ng_ref[...], _mod_rows(sc_ref, rows),
                            _mod_rows(sh_ref, rows)).astype(BF16)
        proj_scr[rows, :] = _dot(h, win_ref[...])
        ga_scr[rows, :] = _dot(h, wa_ref[...])

    def decays(rows):
        gate = _dot(ga_scr[rows, :], wg2_ref[...]) + bg_ref[...]
        log_a = _log_sigmoid(gate) * (1.0 / GLA_TAU)
        la_scr[rows, :] = log_a
        for ci in range(sub // cf):
            b_scr[rows.start + ci * cf:rows.start + (ci + 1) * cf, :] = _chunk_cumsum(
                tril_ref[0:cf, 0:cf], log_a[ci * cf:(ci + 1) * cf, :])
        return jnp.min(b_scr[rows, :]) >= -GLA_SAFE_EXPONENT

    def head_slices(rows, h):
        q = proj_scr[rows, h * dk:(h + 1) * dk] * qscale
        k = proj_scr[rows, GLA_DK + h * dk:GLA_DK + (h + 1) * dk]
        v = proj_scr[rows, 2 * GLA_DK + h * dv:2 * GLA_DK + (h + 1) * dv]
        return q, k, v

    def single_ref_rows(group):
        rr = lax.broadcasted_iota(jnp.int32, (cf, cf), 0)
        cc = lax.broadcasted_iota(jnp.int32, (cf, cf), 1)
        causal = rr >= cc
        chunks = [slice(r0, r0 + cf) for r0 in range(group.start, group.stop, cf)]
        for h in range(GLA_HEADS):
            qes, vbs, scores, kvs, decays = [], [], [], [], []
            for rows in chunks:
                q, k, v = head_slices(rows, h)
                b = b_scr[rows, h * dk:(h + 1) * dk]
                decay_last = jnp.exp(b[cf - 1:cf, :])
                qe = (q * jnp.exp(b)).astype(BF16)
                kt = k * jnp.exp(-b)
                vb = v.astype(BF16)
                scores.append(_dot_nt(qe, kt))
                kvs.append(_dot_tn(kt * decay_last, vb))
                qes.append(qe)
                vbs.append(vb)
                decays.append(_as_column(decay_last))
            st = st_scr[h]
            states = []
            for kv, decay_col in zip(kvs, decays):
                states.append(st)
                st = decay_col * st + kv
            st_scr[h] = st
            for rows, qe, vb, sc, st_in in zip(chunks, qes, vbs, scores, states):
                attn = jnp.where(causal, sc, 0.0).astype(BF16)
                lhs = jnp.concatenate([qe, attn], axis=1)
                rhs = jnp.concatenate([st_in.astype(BF16), vb], axis=0)
                o_scr[rows, h * dv:(h + 1) * dv] = jnp.dot(lhs, rhs, preferred_element_type=F32)

    def robust_rows(group):
        def robust_body(ci, carry):
            rows = pl.ds(pl.multiple_of(group.start + ci * c, c), c)
            masks, rr, cc = _level_masks(c)
            b_c = _chunk_cumsum(tril_ref[0:c, 0:c], la_scr[rows, :])
            outs = []
            for h0 in range(0, GLA_HEADS, 2):
                heads = (h0, h0 + 1)
                qkv = [head_slices(rows, h) for h in heads]
                la2 = [la_scr[rows, h * dk:(h + 1) * dk] for h in heads]
                b2 = [b_c[:, h * dk:(h + 1) * dk] for h in heads]
                st_refs = [st_scr.at[h] for h in heads]
                outs += _gla_chunk_pair([t[0] for t in qkv], [t[1] for t in qkv],
                                        [t[2] for t in qkv], la2, b2, st_refs, ones_ref[...],
                                        masks, rr, cc)
            o_scr[rows, :] = jnp.concatenate(outs, axis=1)
            return carry

        lax.fori_loop(0, sub // c, robust_body, 0)

    def output(rows):
        gate_in = proj_scr[rows, 2 * GLA_DK + GLA_DV:]
        xo_ref[rows, :] = _gla_output(o_scr[rows, :], gate_in, x_ref[rows, :],
                                      _mod_rows(gt_ref, rows), og_ref[...], wout_ref[...])

    project(groups[0])
    for gi, rows in enumerate(groups):
        single_ref_safe = decays(rows)
        following = groups[gi + 1] if gi + 1 < len(groups) else None

        @pl.when(single_ref_safe)
        def _():
            if following is not None:
                project(following)
            single_ref_rows(rows)
            output(rows)

        @pl.when(jnp.logical_not(single_ref_safe))
        def _():
            if following is not None:
                project(following)
            robust_rows(rows)
            output(rows)

    @pl.when(tile == tiles_per_seq - 1)
    def _():
        s_ref[...] = st_scr[...]


def _gla_prompt_call(x, mod, layer, n_seq, seq_len, tm, w):
    n, d = x.shape
    tiles_per_seq = seq_len // tm
    dk, dv = GLA_DK_HEAD, GLA_DV_HEAD
    in_specs = [_row_spec(tm, d)] + _mod_specs(mod, layer, 0, tm, seq_len) + [
        _const_spec((1, d)),
        pl.BlockSpec(memory_space=pl.ANY),
        _const_spec((d, GLA_GATE_PAD)),
        _const_spec((GLA_GATE_PAD, GLA_DK)),
        _const_spec((1, GLA_DK)),
        _const_spec((1, dv)),
        pl.BlockSpec(memory_space=pl.ANY),
        _const_spec((GLA_TRIL, GLA_TRIL)),
        _const_spec((2 * dk, 2 * dk)),
    ]
    out_specs = [
        _row_spec(tm, d),
        pl.BlockSpec((None, GLA_HEADS, dk, dv), lambda i: (i // tiles_per_seq, 0, 0, 0)),
        pl.BlockSpec(memory_space=pl.ANY),
        pl.BlockSpec(memory_space=pl.ANY),
    ]
    out_shape = [
        jax.ShapeDtypeStruct((n, d), F32),
        jax.ShapeDtypeStruct((n_seq, GLA_HEADS, dk, dv), F32),
        jax.ShapeDtypeStruct((d, GLA_QKVG), BF16),
        jax.ShapeDtypeStruct((GLA_DV, d), BF16),
    ]
    return pl.pallas_call(
        functools.partial(_gla_prompt_kernel, tm=tm, sub=GLA_SUBTILE,
                          tiles_per_seq=tiles_per_seq),
        grid=(n // tm,),
        in_specs=in_specs,
        out_specs=out_specs,
        out_shape=out_shape,
        scratch_shapes=[
            pltpu.VMEM((tm, GLA_QKVG), F32),
            pltpu.VMEM((tm, GLA_GATE_PAD), F32),
            pltpu.VMEM((tm, GLA_DK), F32),
            pltpu.VMEM((tm, GLA_DK), F32),
            pltpu.VMEM((tm, GLA_DV), F32),
            pltpu.VMEM((GLA_HEADS, dk, dv), F32),
            pltpu.VMEM((d, GLA_QKVG), BF16),
            pltpu.VMEM((GLA_DV, d), BF16),
            pltpu.VMEM((WEIGHT_STAGE_SLOTS, d // WEIGHT_LOAD_CHUNKS, GLA_QKVG), F32),
            pltpu.VMEM((WEIGHT_STAGE_SLOTS, GLA_DV // WEIGHT_LOAD_CHUNKS, d), F32),
            pltpu.SemaphoreType.DMA((WEIGHT_STAGE_SLOTS,)),
            pltpu.SemaphoreType.DMA((WEIGHT_STAGE_SLOTS,)),
            pltpu.SemaphoreType.DMA((2,)),
        ],
        compiler_params=pltpu.CompilerParams(
            dimension_semantics=("arbitrary",), vmem_limit_bytes=VMEM_LIMIT),
        name="gla_mixer_prompt",
    )(x, mod, mod, mod, w["norm_g"], w["w_in"], w["w_a"], w["w_gate2"], w["b_gate"], w["out_g"],
      w["w_out"], w["tril"], w["ones_bd"])


def _gla_proj_kernel(x_ref, sh_ref, sc_ref, e_ref, ng_ref, win_ref, wa_ref, wg2_ref, bg_ref,
                     p_ref, la_ref):
    proj, log_a = _gla_project(x_ref[...], ng_ref[...], _rows_to_tokens(e_ref, sc_ref),
                               _rows_to_tokens(e_ref, sh_ref), win_ref[...], wa_ref[...],
                               wg2_ref[...], bg_ref[...])
    p_ref[...] = proj
    la_ref[...] = log_a


def _gla_proj_call(x, mod, layer, tm, seq_len, w):
    n, d = x.shape
    specs = _mod_specs(mod, layer, 0, tm, seq_len)
    in_specs = [_row_spec(tm, d), specs[0], specs[1],
                _const_spec((tm, 2 * tm // seq_len)),
                _const_spec((1, d)),
                _const_spec((d, GLA_QKVG)),
                _const_spec((d, GLA_GATE_PAD)),
                _const_spec((GLA_GATE_PAD, GLA_DK)),
                _const_spec((1, GLA_DK))]
    return pl.pallas_call(
        _gla_proj_kernel,
        grid=(n // tm,),
        in_specs=in_specs,
        out_specs=[_row_spec(tm, GLA_QKVG), _row_spec(tm, GLA_DK)],
        out_shape=[jax.ShapeDtypeStruct((n, GLA_QKVG), F32),
                   jax.ShapeDtypeStruct((n, GLA_DK), F32)],
        compiler_params=pltpu.CompilerParams(
            dimension_semantics=("arbitrary",), vmem_limit_bytes=VMEM_LIMIT),
        name="gla_proj_sample",
    )(x, mod, mod, _token_copy_matrix(tm, seq_len), w["norm_g"], w["w_in_bf16"], w["w_a"],
      w["w_gate2"], w["b_gate"])


def _gla_step_kernel(p_ref, la_ref, s0_ref, o_ref, s1_ref, *, seqs, steps):
    dk, dv = GLA_DK_HEAD, GLA_DV_HEAD
    qscale = GLA_DK_HEAD ** -0.5
    per_tile = SUBLANES // steps
    row = lax.broadcasted_iota(jnp.int32, (SUBLANES, dk), 0)

    def seq_rows(tile, s):
        if s:
            tile = pltpu.roll(tile, SUBLANES - s * steps, 0)
        keep = lax.broadcasted_iota(jnp.int32, tile.shape, 0) < steps
        return jnp.where(keep, tile, 0.0)

    def tile_body(ti, carry):
        rows = pl.ds(pl.multiple_of(ti * SUBLANES, SUBLANES), SUBLANES)
        p_tile = p_ref[rows, :]
        la_tile = la_ref[rows, :]
        o_tile = jnp.zeros((SUBLANES, GLA_DV), F32)
        for s in range(per_tile):
            bi = ti * per_tile + s
            p = seq_rows(p_tile, s)
            la_all = seq_rows(la_tile, s)
            outs = []
            for h in range(GLA_HEADS):
                q = p[:, h * dk:(h + 1) * dk] * qscale
                k = p[:, GLA_DK + h * dk:GLA_DK + (h + 1) * dk]
                v = p[:, 2 * GLA_DK + h * dv:2 * GLA_DK + (h + 1) * dv]
                la = la_all[:, h * dk:(h + 1) * dk]
                b = jnp.zeros((SUBLANES, dk), F32)
                for t in range(steps):
                    b = b + jnp.where(row >= t, la[t:t + 1, :], 0.0)
                b_last = b[steps - 1:steps, :]
                s0 = s0_ref[bi, h]
                o = _dot(q * jnp.exp(b), s0)
                for j in range(steps):
                    e = jnp.exp(jnp.minimum(b - b[j:j + 1, :], 0.0))
                    z = jnp.where(row >= j, q * k[j:j + 1, :] * e, 0.0)
                    o = o + jnp.sum(z, axis=-1, keepdims=True) * v[j:j + 1, :]
                outs.append(o)
                m = jnp.where(row == steps, jnp.exp(b_last), k * jnp.exp(b_last - b))
                m_t = m.T
                s1_ref[bi, h] = m_t[:, steps:steps + 1] * s0 + _dot(m_t, v)
            o_seq = jnp.concatenate(outs, axis=1)
            o_tile = o_tile + (pltpu.roll(o_seq, s * steps, 0) if s else o_seq)
        o_ref[rows, :] = o_tile
        return carry

    lax.fori_loop(0, seqs // per_tile, tile_body, 0)


def _gla_step_call(proj, log_a, state, steps, seqs):
    n_seq = state.shape[0]
    dk, dv = GLA_DK_HEAD, GLA_DV_HEAD
    rows = lambda width: pl.BlockSpec((seqs * steps, width), lambda i: (i, 0))
    st_spec = pl.BlockSpec((seqs, GLA_HEADS, dk, dv), lambda i: (i, 0, 0, 0))
    return pl.pallas_call(
        functools.partial(_gla_step_kernel, seqs=seqs, steps=steps),
        grid=(n_seq // seqs,),
        in_specs=[rows(GLA_QKVG), rows(GLA_DK), st_spec],
        out_specs=[rows(GLA_DV), st_spec],
        out_shape=[jax.ShapeDtypeStruct((n_seq * steps, GLA_DV), F32),
                   jax.ShapeDtypeStruct(state.shape, F32)],
        compiler_params=pltpu.CompilerParams(
            dimension_semantics=("arbitrary",), vmem_limit_bytes=VMEM_LIMIT),
        name="gla_step_sample",
    )(proj, log_a, state)


def _gla_out_kernel(o_ref, p_ref, x_ref, gt_ref, e_ref, og_ref, wout_ref, xo_ref):
    xo_ref[...] = _gla_output(o_ref[...], p_ref[...], x_ref[...], _rows_to_tokens(e_ref, gt_ref),
                              og_ref[...], wout_ref[...])


def _gla_out_call(o, proj, x, mod, layer, tm, seq_len, w):
    n, d = x.shape
    gate_spec = _mod_specs(mod, layer, 0, tm, seq_len)[2]
    g_block = (2 * GLA_DK + GLA_DV) // GLA_DV
    in_specs = [_row_spec(tm, GLA_DV),
                pl.BlockSpec((tm, GLA_DV), lambda i: (i, g_block)),
                _row_spec(tm, d), gate_spec,
                _const_spec((tm, 2 * tm // seq_len)),
                _const_spec((1, GLA_DV_HEAD)),
                _const_spec((GLA_DV, d))]
    return pl.pallas_call(
        _gla_out_kernel,
        grid=(n // tm,),
        in_specs=in_specs,
        out_specs=_row_spec(tm, d),
        out_shape=jax.ShapeDtypeStruct((n, d), F32),
        compiler_params=pltpu.CompilerParams(
            dimension_semantics=("arbitrary",), vmem_limit_bytes=VMEM_LIMIT),
        name="gla_out_sample",
    )(o, proj, x, mod, _token_copy_matrix(tm, seq_len), w["out_g"], w["w_out_bf16"])


PROMPT_FFN_TILE = 1024
PROMPT_GMLP_TILE = 1024
PROMPT_GLA_TILE = 1024
SAMPLE_TILE = 256
SAMPLE_SEQS_PER_STEP = 16


def kernel(x_prompt, x_sample, c_prompt, c_sample, state_gla, ada_w, ada_b, norm_mix_g, norm_ffn_g,
           ffn_w1, ffn_b1, ffn_w2, ffn_b2, gmlp_w_in, gmlp_b_in, gmlp_ln_g, gmlp_ln_b, gmlp_w_s,
           gmlp_b_s, gmlp_w_out, gmlp_b_out, gla_w_in, gla_w_gate2, gla_b_gate, gla_norm_g,
           gla_w_out, final_norm_g):
    n_seq_p, seq_p, d = x_prompt.shape
    n_seq_s, seq_s, _ = x_sample.shape
    assert d == D_MODEL and GMLP_SUBTILE % CHUNK_A == 0 and GLA_SUBTILE % GLA_FAST_CHUNK == 0
    assert all(seq_p % t == 0 for t in (PROMPT_FFN_TILE, PROMPT_GMLP_TILE, PROMPT_GLA_TILE))
    assert seq_s < SUBLANES and SUBLANES % seq_s == 0 and (n_seq_s * seq_s) % SAMPLE_TILE == 0
    row = lambda a: a.reshape(1, -1)

    mod_p, mod_s = _ada_call(c_prompt, c_sample, ada_w, ada_b)
    mod_p = mod_p.reshape(mod_p.shape[0], n_seq_p, 1, N_MOD * d)

    causal = np.tril(np.ones((CHUNK_A, CHUNK_A), dtype=bool))
    ws = jnp.where(causal[None], gmlp_w_s, jnp.zeros_like(gmlp_w_s))
    reps = CHUNK_A // seq_s
    eye = np.eye(reps, dtype=np.float32)
    ws_s = jnp.einsum("ab,gij->gaibj", eye, ws[:, :seq_s, :seq_s]).reshape(
        GMLP_GROUPS, CHUNK_A, CHUNK_A)
    bias_p = jnp.repeat(gmlp_b_s.T, GMLP_GROUP_W, axis=1)
    bias_s = jnp.tile(bias_p[:seq_s], (reps, 1))

    gmlp_w = dict(norm_g=row(norm_mix_g[0]), w_in=gmlp_w_in, b_in=row(gmlp_b_in),
                  ln_g=row(gmlp_ln_g), ln_b=row(gmlp_ln_b), w_out=gmlp_w_out,
                  b_out=row(gmlp_b_out))
    gmlp_w = dict(gmlp_w, mix_p=ws.astype(BF16), bias_p=bias_p, mix_s=ws_s.astype(BF16),
                  bias_s=bias_s)

    depth = ffn_w1.shape[0]
    ffn_w = dict(norm_g=norm_ffn_g.reshape(depth, 1, d), w1=ffn_w1,
                 b1=ffn_b1.reshape(depth, 1, D_FF), w2=ffn_w2,
                 b2=ffn_b2.reshape(depth, 1, d), final_g=row(final_norm_g))

    w_a_pad = jnp.pad(gla_w_in[:, GLA_QKVG:],
                      ((0, 0), (0, GLA_GATE_PAD - GLA_GATE_RANK))).astype(BF16)
    wg2_pad = jnp.pad(gla_w_gate2, ((0, GLA_GATE_PAD - GLA_GATE_RANK), (0, 0))).astype(BF16)
    blk_ones = np.kron(np.eye(2, dtype=np.float32),
                       np.ones((GLA_DK_HEAD, GLA_DK_HEAD), np.float32))
    gla_w = dict(norm_g=row(norm_mix_g[1]), w_in=gla_w_in, w_a=w_a_pad, w_gate2=wg2_pad,
                 b_gate=row(gla_b_gate), out_g=row(gla_norm_g), w_out=gla_w_out,
                 tril=jnp.asarray(np.tril(np.ones((GLA_TRIL, GLA_TRIL), np.float32)), BF16),
                 ones_bd=jnp.asarray(blk_ones, BF16))

    xp = x_prompt.reshape(n_seq_p * seq_p, d)
    xs = x_sample.reshape(n_seq_s * seq_s, d)
    grid_for = lambda tm_p: _TwoGroupGrid(xp.shape[0], xs.shape[0], tm_p, SAMPLE_TILE, seq_p,
                                          seq_s)
    xp, xs, chunk_v = _gmlp_call(xp, xs, mod_p, mod_s, grid_for(PROMPT_GMLP_TILE), gmlp_w)
    xp, xs = _ffn_call(xp, xs, mod_p, mod_s, 0, grid_for(PROMPT_FFN_TILE), ffn_w, final_norm=False)
    xp, state_p, w_in_bf16, w_out_bf16 = _gla_prompt_call(xp, mod_p, 1, n_seq_p, seq_p,
                                                          PROMPT_GLA_TILE, gla_w)
    gla_w = dict(gla_w, w_in_bf16=w_in_bf16, w_out_bf16=w_out_bf16)
    proj, log_a = _gla_proj_call(xs, mod_s, 1, SAMPLE_TILE, seq_s, gla_w)
    o, state_s = _gla_step_call(proj, log_a, state_gla, seq_s, SAMPLE_SEQS_PER_STEP)
    xs = _gla_out_call(o, proj, xs, mod_s, 1, SAMPLE_TILE, seq_s, gla_w)
    xp, xs = _ffn_call(xp, xs, mod_p, mod_s, 1, grid_for(PROMPT_FFN_TILE), ffn_w, final_norm=True)

    return (xp.reshape(x_prompt.shape), xs.reshape(x_sample.shape), state_p, state_s,
            chunk_v.reshape(n_seq_s, seq_s, GMLP_WIDTH))
```

```python
import functools

import jax
import jax.numpy as jnp
import numpy as np
from jax import lax
from jax.experimental import pallas as pl
from jax.experimental.pallas import tpu as pltpu

F32 = jnp.float32
BF16 = jnp.bfloat16

D_MODEL = 1024
N_MOD = 6
CHUNK_A = 128
GMLP_WIDTH = D_MODEL
GMLP_GROUPS = 4
GMLP_GROUP_W = GMLP_WIDTH // GMLP_GROUPS
GLA_HEADS = 4
GLA_DK = D_MODEL // 2
GLA_DV = D_MODEL
GLA_DK_HEAD = GLA_DK // GLA_HEADS
GLA_DV_HEAD = GLA_DV // GLA_HEADS
GLA_GATE_RANK = 16
GLA_TAU = 16.0
GLA_CHUNK = 64
GLA_FAST_CHUNK = 128
GLA_TRIL = max(GLA_CHUNK, GLA_FAST_CHUNK)
GLA_SAFE_EXPONENT = 80.0
GLA_QKVG = 2 * GLA_DK + 2 * GLA_DV
GLA_GATE_PAD = 128
GLA_IN_PAD = GLA_QKVG + GLA_GATE_PAD
D_FF = 4 * D_MODEL
EPS = 1e-6

LANES = 128
SUBLANES = 8
BF16_SUBLANES = 16
ADA_ROWS = 256
FFN_CHUNK = 1024
FFN_SUBTILE = 512
GMLP_SUBTILE = 256
GLA_SUBTILE = 512
VMEM_LIMIT = 56 * 1024 * 1024
WEIGHT_LOAD_CHUNKS = 32
WEIGHT_STAGE_SLOTS = 4


def _rms(x, g):
    return x * lax.rsqrt(jnp.mean(x * x, axis=-1, keepdims=True) + EPS) * g


def _modulated_norm(x, g, scale, shift):
    inv = lax.rsqrt(jnp.mean(x * x, axis=-1, keepdims=True) + EPS)
    return (x * inv) * (g * (1.0 + scale)) + shift


def _dot(a, b):
    return jnp.dot(a.astype(BF16), b.astype(BF16), preferred_element_type=F32)


def _dot_nt(a, b):
    return lax.dot_general(a.astype(BF16), b.astype(BF16), (((1,), (1,)), ((), ())),
                           preferred_element_type=F32)


def _dot_tn(a, b):
    return lax.dot_general(a.astype(BF16), b.astype(BF16), (((0,), (0,)), ((), ())),
                           preferred_element_type=F32)


def _gelu_tanh(x):
    c1 = -2.0 * 0.7978845608028654 * 1.4426950408889634
    c2 = c1 * 0.044715
    return x / (1.0 + jnp.exp2(x * (c1 + c2 * (x * x))))


def _silu(x):
    return x * jax.nn.sigmoid(x)


def _log_sigmoid(x):
    return -(jnp.maximum(-x, 0.0) + jnp.log1p(jnp.exp(-jnp.abs(x))))


def _bf16_terms(x, n):
    terms = []
    for _ in range(n - 1):
        t = x.astype(BF16)
        terms.append(t)
        x = x - t.astype(F32)
    terms.append(x.astype(BF16))
    return terms


def _ada_kernel(c_ref, wa_ref, wb_ref, b_ref, op_ref, os_ref, s_scr, acc_scr, *, n_prompt, tk):
    k = pl.program_id(1)

    @pl.when((pl.program_id(0) == 0) & (k == 0))
    def _():
        for kk in range(s_scr.shape[0]):
            s_scr[kk] = _silu(c_ref[:, kk * tk:(kk + 1) * tk]).astype(BF16)

    @pl.when(k == 0)
    def _():
        acc_scr[...] = jnp.broadcast_to(b_ref[...], acc_scr.shape)

    s = s_scr[k]
    half = N_MOD // 2
    for j in range(N_MOD):
        cols = slice(j * D_MODEL, (j + 1) * D_MODEL)
        w_ref, jj = (wa_ref, j) if j < half else (wb_ref, j - half)
        w = w_ref[:, jj * D_MODEL:(jj + 1) * D_MODEL].astype(BF16)
        acc_scr[:, cols] += jnp.dot(s, w, preferred_element_type=F32)

    @pl.when(k == pl.num_programs(1) - 1)
    def _():
        op_ref[...] = acc_scr[0:n_prompt, :]
        os_ref[...] = acc_scr[n_prompt:n_prompt + os_ref.shape[0], :]


def _ada_call(c_prompt, c_sample, ada_w, ada_b):
    depth = ada_w.shape[0]
    n_prompt, n_seq = c_prompt.shape[0], c_sample.shape[0]
    d = D_MODEL
    pad = -(n_prompt + n_seq) % BF16_SUBLANES
    c_all = jnp.concatenate([c_prompt, c_sample, jnp.zeros((pad, d), F32)], axis=0)
    rows = c_all.shape[0]
    tk = ADA_ROWS
    width = N_MOD * d
    return pl.pallas_call(
        functools.partial(_ada_kernel, n_prompt=n_prompt, tk=tk),
        grid=(depth, d // tk),
        in_specs=[
            pl.BlockSpec((rows, d), lambda l, k: (0, 0)),
            pl.BlockSpec((None, tk, width // 2), lambda l, k: (l, k, 0)),
            pl.BlockSpec((None, tk, width // 2), lambda l, k: (l, k, 1)),
            pl.BlockSpec((None, 1, width), lambda l, k: (l, 0, 0)),
        ],
        out_specs=[
            pl.BlockSpec((None, n_prompt, width), lambda l, k: (l, 0, 0)),
            pl.BlockSpec((None, n_seq, width), lambda l, k: (l, 0, 0)),
        ],
        out_shape=[
            jax.ShapeDtypeStruct((depth, n_prompt, width), F32),
            jax.ShapeDtypeStruct((depth, n_seq, width), F32),
        ],
        scratch_shapes=[pltpu.VMEM((d // tk, rows, tk), BF16),
                        pltpu.VMEM((rows, width), F32)],
        compiler_params=pltpu.CompilerParams(
            dimension_semantics=("arbitrary", "arbitrary"), vmem_limit_bytes=VMEM_LIMIT),
        name="adaln_mod",
    )(c_all, ada_w, ada_w, ada_b.reshape(depth, 1, N_MOD * d))


def _const_spec(shape):
    zeros = (0,) * len(shape)
    return pl.BlockSpec(shape, lambda i: zeros, pipeline_mode=pl.Buffered(1))


def _layer_spec(shape, layer):
    zeros = (0,) * len(shape)
    return pl.BlockSpec((None,) + shape, lambda i: (layer,) + zeros,
                        pipeline_mode=pl.Buffered(1))


def _mod_specs(mod, layer, first_chunk, tm, rows_per_seq):
    d = D_MODEL
    specs = []
    for j in range(first_chunk, first_chunk + 3):
        if rows_per_seq < tm:
            specs.append(pl.BlockSpec((None, tm // rows_per_seq, d), lambda i, j=j: (layer, i, j)))
        else:
            specs.append(pl.BlockSpec((None, mod.shape[1], d), lambda i, j=j: (layer, 0, j)))
    return specs


def _token_copy_matrix(tm, rows_per_seq):
    copy = np.repeat(np.eye(tm // rows_per_seq, dtype=np.float32), rows_per_seq, axis=0)
    return jnp.asarray(np.tile(copy, (1, 2)), BF16)


def _rows_to_tokens(e_ref, mod_ref):
    terms = jnp.concatenate(_bf16_terms(mod_ref[...], 2), axis=0)
    return jnp.dot(e_ref[...], terms, preferred_element_type=F32)


def _mod_rows(ref, rows):
    return ref[...] if ref.shape[0] == 1 else ref[rows, :]


def _row_spec(tm, width):
    return pl.BlockSpec((tm, width), lambda i: (i, 0))


class _TwoGroupGrid:
    def __init__(self, n_prompt_rows, n_sample_rows, tm_p, tm_s, seq_len, sample_seq_len):
        self.tm_p, self.tm_s = tm_p, tm_s
        self.sample_seq_len = sample_seq_len
        self.p_steps = n_prompt_rows // tm_p
        self.s_steps = n_sample_rows // tm_s
        self.tiles_per_seq = seq_len // tm_p
        self.n_prompt_seqs = n_prompt_rows // seq_len
        self.steps = self.p_steps + self.s_steps

    def p_idx(self, i):
        return jnp.minimum(i, self.p_steps - 1)

    def s_idx(self, i):
        return jnp.maximum(i - self.p_steps, 0)

    def rows_p(self, width):
        return pl.BlockSpec((self.tm_p, width), lambda i: (self.p_idx(i), 0))

    def rows_s(self, width, col_block=0):
        return pl.BlockSpec((self.tm_s, width), lambda i: (self.s_idx(i), col_block))

    def mods_p(self, layer, first_chunk):
        return [pl.BlockSpec((None, self.n_prompt_seqs, D_MODEL), lambda i, j=j: (layer, 0, j))
                for j in range(first_chunk, first_chunk + 3)]

    def prompt_row(self, ref, step):
        return ref[pl.ds(step // self.tiles_per_seq, 1), :]

    def mods_s(self, layer, first_chunk):
        return [pl.BlockSpec((None, self.tm_s // self.sample_seq_len, D_MODEL),
                             lambda i, j=j: (layer, self.s_idx(i), j))
                for j in range(first_chunk, first_chunk + 3)]

    def seqs_s(self, width):
        seq = self.sample_seq_len
        return pl.BlockSpec((self.tm_s // seq, seq, width), lambda i: (self.s_idx(i), 0, 0))

    def token_copy_spec(self):
        return _const_spec((self.tm_s, 2 * self.tm_s // self.sample_seq_len))

    def token_copy_matrix(self):
        return _token_copy_matrix(self.tm_s, self.sample_seq_len)


def _gmlp_rows(x_ref, sh_ref, sc_ref, gt_ref, mix_ref, bs_ref, xo_ref, v_ref, z_scr, w, tm, sub):
    ng_ref, win_ref, bin_ref, lng_ref, lnb_ref, wout_ref, bout_ref = w
    groups = [slice(r0, r0 + sub) for r0 in range(0, tm, sub)]

    def project(gi):
        rows = groups[gi]
        h = _modulated_norm(x_ref[rows, :], ng_ref[...], _mod_rows(sc_ref, rows),
                            _mod_rows(sh_ref, rows))
        z_scr[gi % 2] = _dot(h, win_ref[...])

    project(0)
    for gi, rows in enumerate(groups):
        if gi + 1 < len(groups):
            project(gi + 1)
        z = _gelu_tanh(z_scr[gi % 2] + bin_ref[...])
        u = z[:, :GMLP_WIDTH]
        v = z[:, GMLP_WIDTH:]
        mu = jnp.mean(v, axis=-1, keepdims=True)
        vc = v - mu
        var = jnp.mean(vc * vc, axis=-1, keepdims=True)
        v = vc * lax.rsqrt(var + EPS) * lng_ref[...] + lnb_ref[...]
        if v_ref is not None:
            v_ref[rows, :] = v
        vb = v.astype(BF16)
        mixed = []
        for c in range(sub // CHUNK_A):
            cols = []
            for g in range(GMLP_GROUPS):
                blk = vb[c * CHUNK_A:(c + 1) * CHUNK_A, g * GMLP_GROUP_W:(g + 1) * GMLP_GROUP_W]
                cols.append(jnp.dot(mix_ref[g], blk, preferred_element_type=F32))
            mixed.append(jnp.concatenate(cols, axis=1) + bs_ref[...])
        s = jnp.concatenate(mixed, axis=0)
        y = _dot(u * s, wout_ref[...]) + bout_ref[...]
        xo_ref[rows, :] = x_ref[rows, :] + _mod_rows(gt_ref, rows) * y


def _rows_to_seq_layout(rows_ref, seq_ref, slab_scr):
    n_seq, seq, width = seq_ref.shape
    for j in range(width // LANES):
        lanes = slice(j * LANES, (j + 1) * LANES)
        slab_scr[...] = rows_ref[:, lanes]
        for t in range(seq):
            seq_ref[:, t, lanes] = slab_scr[pl.ds(t, n_seq, stride=seq), :]


def _seq_layout_to_rows(seq_ref, rows_ref, slab_scr):
    n_seq, seq, width = seq_ref.shape
    for j in range(width // LANES):
        lanes = slice(j * LANES, (j + 1) * LANES)
        for t in range(seq):
            slab_scr[pl.ds(t, n_seq, stride=seq), :] = seq_ref[:, t, lanes]
        rows_ref[:, lanes] = slab_scr[...]


def _gmlp_kernel(xp_ref, shp_ref, scp_ref, gtp_ref, xs_ref, shs_ref, scs_ref, gts_ref, e_ref,
                 ng_ref, win_hbm, bin_ref, lng_ref, lnb_ref, wout_hbm, bout_ref,
                 mixp_ref, bsp_ref, mixs_ref, bss_ref, op_ref, os_ref, vs_ref, z_scr, v_scr,
                 x_scr, slab_scr,
                 win_ref, wout_ref, stage_in, stage_out, sems_in, sems_out, *, grid):
    w = (ng_ref, win_ref, bin_ref, lng_ref, lnb_ref, wout_ref, bout_ref)
    step = pl.program_id(0)

    @pl.when(step == 0)
    def _():
        _load_as_bf16([(win_hbm, win_ref, stage_in, sems_in),
                       (wout_hbm, wout_ref, stage_out, sems_out)])

    @pl.when(step < grid.p_steps)
    def _():
        sh, sc, gt = (grid.prompt_row(r, step) for r in (shp_ref, scp_ref, gtp_ref))
        _gmlp_rows(xp_ref, sh, sc, gt, mixp_ref, bsp_ref, op_ref, None, z_scr, w,
                   grid.tm_p, GMLP_SUBTILE)

    @pl.when(step >= grid.p_steps)
    def _():
        sh, sc, gt = (_rows_to_tokens(e_ref, r) for r in (shs_ref, scs_ref, gts_ref))
        _seq_layout_to_rows(xs_ref, x_scr, slab_scr)
        _gmlp_rows(x_scr, sh, sc, gt, mixs_ref, bss_ref, os_ref, v_scr, z_scr, w,
                   grid.tm_s, GMLP_SUBTILE)
        _rows_to_seq_layout(v_scr, vs_ref, slab_scr)


def _gmlp_call(xp, xs_seq, mod_p, mod_s, grid, w):
    d = D_MODEL
    n_sample = xs_seq.shape[0] * xs_seq.shape[1]
    mix_spec = _const_spec((GMLP_GROUPS, CHUNK_A, CHUNK_A))
    bias_spec = _const_spec((CHUNK_A, GMLP_WIDTH))
    in_specs = ([grid.rows_p(d)] + grid.mods_p(0, 0) + [grid.seqs_s(d)] + grid.mods_s(0, 0) + [
        grid.token_copy_spec(),
        _const_spec((1, d)),
        pl.BlockSpec(memory_space=pl.ANY),
        _const_spec((1, 2 * GMLP_WIDTH)),
        _const_spec((1, GMLP_WIDTH)),
        _const_spec((1, GMLP_WIDTH)),
        pl.BlockSpec(memory_space=pl.ANY),
        _const_spec((1, d)),
        mix_spec, bias_spec, mix_spec, bias_spec,
    ])
    return pl.pallas_call(
        functools.partial(_gmlp_kernel, grid=grid),
        grid=(grid.steps,),
        in_specs=in_specs,
        out_specs=[grid.rows_p(d), grid.rows_s(d), grid.seqs_s(GMLP_WIDTH)],
        out_shape=[jax.ShapeDtypeStruct(xp.shape, F32),
                   jax.ShapeDtypeStruct((n_sample, d), F32),
                   jax.ShapeDtypeStruct(xs_seq.shape[:2] + (GMLP_WIDTH,), F32)],
        scratch_shapes=[pltpu.VMEM((2, GMLP_SUBTILE, 2 * GMLP_WIDTH), F32),
                        pltpu.VMEM((grid.tm_s, GMLP_WIDTH), F32),
                        pltpu.VMEM((grid.tm_s, d), F32),
                        pltpu.VMEM((grid.tm_s, LANES), F32),
                        pltpu.VMEM((d, 2 * GMLP_WIDTH), BF16),
                        pltpu.VMEM((GMLP_WIDTH, d), BF16),
                        pltpu.VMEM((WEIGHT_STAGE_SLOTS, d // WEIGHT_LOAD_CHUNKS, 2 * GMLP_WIDTH),
                                   F32),
                        pltpu.VMEM((WEIGHT_STAGE_SLOTS, GMLP_WIDTH // WEIGHT_LOAD_CHUNKS, d), F32),
                        pltpu.SemaphoreType.DMA((WEIGHT_STAGE_SLOTS,)),
                        pltpu.SemaphoreType.DMA((WEIGHT_STAGE_SLOTS,))],
        compiler_params=pltpu.CompilerParams(
            dimension_semantics=("arbitrary",), vmem_limit_bytes=VMEM_LIMIT),
        name="gmlp_mixer",
    )(xp, mod_p, mod_p, mod_p, xs_seq, mod_s, mod_s, mod_s, grid.token_copy_matrix(),
      w["norm_g"], w["w_in"], w["b_in"],
      w["ln_g"], w["ln_b"], w["w_out"], w["b_out"], w["mix_p"], w["bias_p"], w["mix_s"],
      w["bias_s"])


def _ffn_rows(x_ref, sh_ref, sc_ref, gt_ref, o_ref, h_scr, w, final_norm, tm):
    ng_ref, w1_ref, b1_ref, w2_ref, b2_ref, fg_ref = w
    sub = min(FFN_SUBTILE, tm)
    groups = [slice(r0, r0 + sub) for r0 in range(0, tm, sub)]

    def normalise(gi):
        rows = groups[gi]
        h_scr[gi % 2, 0:sub, :] = _modulated_norm(
            x_ref[rows, :], ng_ref[...], _mod_rows(sc_ref, rows),
            _mod_rows(sh_ref, rows)).astype(BF16)

    normalise(0)
    for gi, rows in enumerate(groups):
        if gi + 1 < len(groups):
            normalise(gi + 1)
        h = h_scr[gi % 2, 0:sub, :]
        acc = jnp.zeros((sub, D_MODEL), F32)
        for j in range(D_FF // FFN_CHUNK):
            cols = slice(j * FFN_CHUNK, (j + 1) * FFN_CHUNK)
            a = jnp.dot(h, w1_ref[:, cols], preferred_element_type=F32) + b1_ref[:, cols]
            r = jnp.square(jnp.maximum(a, 0.0)).astype(BF16)
            acc = acc + jnp.dot(r, w2_ref[cols, :], preferred_element_type=F32)
        y = x_ref[rows, :] + _mod_rows(gt_ref, rows) * (acc + b2_ref[...])
        if final_norm:
            y = _rms(y, fg_ref[...])
        o_ref[rows, :] = y


def _load_as_bf16(jobs):
    def chunk_copy(job, c):
        src_hbm, _, stage_ref, sem_ref = job
        slot = c % stage_ref.shape[0]
        rows = pl.ds(c * stage_ref.shape[1], stage_ref.shape[1])
        return pltpu.make_async_copy(src_hbm.at[rows, :], stage_ref.at[slot], sem_ref.at[slot])

    n_chunks = {job[1].shape[0] // job[2].shape[1] for job in jobs}
    n_slots = {job[2].shape[0] for job in jobs}
    assert len(n_chunks) == 1 and len(n_slots) == 1
    n_chunks, n_slots = n_chunks.pop(), n_slots.pop()
    for c in range(min(n_slots - 1, n_chunks)):
        for job in jobs:
            chunk_copy(job, c).start()
    for c in range(n_chunks):
        for job in jobs:
            if c + n_slots - 1 < n_chunks:
                chunk_copy(job, c + n_slots - 1).start()
        for job in jobs:
            _, dst_ref, stage_ref, _ = job
            chunk_copy(job, c).wait()
            rows = stage_ref.shape[1]
            dst_ref[c * rows:(c + 1) * rows, :] = stage_ref[c % n_slots].astype(BF16)


def _ffn_kernel(xp_ref, shp_ref, scp_ref, gtp_ref, xs_ref, shs_ref, scs_ref, gts_ref, e_ref,
                ng_ref, w1_hbm, b1_ref, w2_hbm, b2_ref, fg_ref, op_ref, os_ref, h_scr,
                w1_ref, w2_ref, stage1, stage2, sems1, sems2, ys_scr, slab_scr,
                *, final_norm, grid, layer):
    w = (ng_ref, w1_ref, b1_ref, w2_ref, b2_ref, fg_ref)
    step = pl.program_id(0)

    @pl.when(step == 0)
    def _():
        _load_as_bf16([(w1_hbm.at[layer], w1_ref, stage1, sems1),
                       (w2_hbm.at[layer], w2_ref, stage2, sems2)])

    @pl.when(step < grid.p_steps)
    def _():
        sh, sc, gt = (grid.prompt_row(r, step) for r in (shp_ref, scp_ref, gtp_ref))
        _ffn_rows(xp_ref, sh, sc, gt, op_ref, h_scr, w, final_norm, grid.tm_p)

    @pl.when(step >= grid.p_steps)
    def _():
        sh, sc, gt = (_rows_to_tokens(e_ref, r) for r in (shs_ref, scs_ref, gts_ref))
        if final_norm:
            _ffn_rows(xs_ref, sh, sc, gt, ys_scr, h_scr, w, final_norm, grid.tm_s)
            _rows_to_seq_layout(ys_scr, os_ref, slab_scr)
        else:
            _ffn_rows(xs_ref, sh, sc, gt, os_ref, h_scr, w, final_norm, grid.tm_s)


def _ffn_call(xp, xs, mod_p, mod_s, layer, grid, w, final_norm):
    d = D_MODEL
    seq_shape = (xs.shape[0] // grid.sample_seq_len, grid.sample_seq_len, d)
    in_specs = ([grid.rows_p(d)] + grid.mods_p(layer, 3) + [grid.rows_s(d)]
                + grid.mods_s(layer, 3) + [
        grid.token_copy_spec(),
        _layer_spec((1, d), layer),
        pl.BlockSpec(memory_space=pl.ANY),
        _layer_spec((1, D_FF), layer),
        pl.BlockSpec(memory_space=pl.ANY),
        _layer_spec((1, d), layer),
        _const_spec((1, d)),
    ])
    return pl.pallas_call(
        functools.partial(_ffn_kernel, final_norm=final_norm, grid=grid, layer=layer),
        grid=(grid.steps,),
        in_specs=in_specs,
        out_specs=[grid.rows_p(d), grid.seqs_s(d) if final_norm else grid.rows_s(d)],
        out_shape=[jax.ShapeDtypeStruct(xp.shape, F32),
                   jax.ShapeDtypeStruct(seq_shape if final_norm else xs.shape, F32)],
        scratch_shapes=[pltpu.VMEM((2, FFN_SUBTILE, d), BF16),
                        pltpu.VMEM((d, D_FF), BF16),
                        pltpu.VMEM((D_FF, d), BF16),
                        pltpu.VMEM((WEIGHT_STAGE_SLOTS, d // WEIGHT_LOAD_CHUNKS, D_FF), F32),
                        pltpu.VMEM((WEIGHT_STAGE_SLOTS, D_FF // WEIGHT_LOAD_CHUNKS, d), F32),
                        pltpu.SemaphoreType.DMA((WEIGHT_STAGE_SLOTS,)),
                        pltpu.SemaphoreType.DMA((WEIGHT_STAGE_SLOTS,)),
                        pltpu.VMEM((grid.tm_s, d), F32),
                        pltpu.VMEM((grid.tm_s, LANES), F32)],
        compiler_params=pltpu.CompilerParams(
            dimension_semantics=("arbitrary",), vmem_limit_bytes=VMEM_LIMIT),
        name="sqrelu_mlp",
    )(xp, mod_p, mod_p, mod_p, xs, mod_s, mod_s, mod_s, grid.token_copy_matrix(),
      w["norm_g"], w["w1"], w["b1"], w["w2"],
      w["b2"], w["final_g"])


def _gla_project(x, ng, scale, shift, win, wg2, bg):
    h = _modulated_norm(x, ng, scale, shift)
    proj = _dot(h, win)
    gate = _dot(proj[:, GLA_QKVG:], wg2) + bg
    log_a = _log_sigmoid(gate) * (1.0 / GLA_TAU)
    return proj[:, :GLA_QKVG], log_a


def _gla_output(o, gate_in, x, gt, og, wout):
    parts = []
    for h in range(GLA_HEADS):
        cols = slice(h * GLA_DV_HEAD, (h + 1) * GLA_DV_HEAD)
        parts.append(_rms(o[:, cols], og))
    on = jnp.concatenate(parts, axis=1)
    y = _dot(on * _silu(gate_in), wout)
    return x + gt * y


def _as_column(row):
    return jnp.broadcast_to(row, (SUBLANES, row.shape[1])).T[:, 0:1]


def _chunk_cumsum(tril, log_a):
    head = log_a.astype(BF16)
    rest = (log_a - head.astype(F32)).astype(BF16)
    return (jnp.dot(tril, head, preferred_element_type=F32)
            + jnp.dot(tril, rest, preferred_element_type=F32))


def _level_masks(c):
    rr = lax.broadcasted_iota(jnp.int32, (c, c), 0)
    cc = lax.broadcasted_iota(jnp.int32, (c, c), 1)
    masks = []
    s = c
    while s > SUBLANES:
        half = s // 2
        same = (rr ^ cc) < s
        masks.append(same & ((rr & half) != 0) & ((cc & half) == 0))
        s = half
    return masks, rr, cc


def _block_rows(x, s, r):
    c, w = x.shape
    return jnp.concatenate(
        [jnp.broadcast_to(x[p * s + r:p * s + r + 1, :], (s, w)) for p in range(c // s)], axis=0)


def _diag_terms(q, k, a):
    c, w = q.shape
    ii = lax.broadcasted_iota(jnp.int32, (c, w), 0) & (SUBLANES - 1)
    decay = jnp.zeros((c, w), F32)
    terms = [None] * SUBLANES
    for j in range(SUBLANES - 1, -1, -1):
        if j < SUBLANES - 1:
            decay = decay * _block_rows(a, SUBLANES, j + 1)
        decay = jnp.where(ii == j, 1.0, decay)
        terms[j] = q * _block_rows(k, SUBLANES, j) * decay
    return jnp.concatenate(terms, axis=0)


def _gla_chunk_pair(q2, k2, v2, la2, b2, st_refs, ones_bd, masks, rr, cc):
    c = q2[0].shape[0]
    zs = []
    for q, k, la in zip(q2, k2, la2):
        zs.append(_diag_terms(q, k, jnp.exp(la)))
    sums = jnp.dot(jnp.concatenate(zs, axis=1).astype(BF16), ones_bd,
                   preferred_element_type=F32)
    outs = []
    for hh, (q, k, v, b, st_ref) in enumerate(zip(q2, k2, v2, b2, st_refs)):
        dk = q.shape[1]
        attn = jnp.zeros((c, c), F32)
        base = rr & ~(SUBLANES - 1)
        for j in range(SUBLANES):
            rj = sums[j * c:(j + 1) * c, hh * dk:hh * dk + c]
            attn = jnp.where(cc == base + j, rj, attn)
        s = c
        for mask in masks:
            half = s // 2
            e = jnp.exp(-jnp.abs(b - _block_rows(b, s, half)))
            attn = jnp.where(mask, _dot_nt(q * e, k * e), attn)
            s = half
        b_last = b[c - 1:c, :]
        st = st_ref[...]
        o = _dot(q * jnp.exp(b), st) + _dot(attn, v)
        kd = k * jnp.exp(b_last - b)
        st_ref[...] = _as_column(jnp.exp(b_last)) * st + _dot_tn(kd, v)
        outs.append(o)
    return outs


def _gla_prompt_kernel(x_ref, sh_ref, sc_ref, gt_ref, ng_ref, win_ref, wg2_ref, bg_ref,
                       og_ref, wout_ref, tril_ref, ones_ref, xo_ref, s_ref,
                       proj_scr, ga_scr, la_scr, b_scr, o_scr, st_scr, *, tm, sub, tiles_per_seq):
    tile = pl.program_id(0) % tiles_per_seq
    seq_row = pl.ds(pl.program_id(0) // tiles_per_seq, 1)
    shift, scale, gate_mod = (r[seq_row, :] for r in (sh_ref, sc_ref, gt_ref))

    @pl.when(tile == 0)
    def _():
        st_scr[...] = jnp.zeros(st_scr.shape, F32)

    dk, dv = GLA_DK_HEAD, GLA_DV_HEAD
    qscale = GLA_DK_HEAD ** -0.5
    cf = GLA_FAST_CHUNK
    c = GLA_CHUNK
    groups = [slice(r0, r0 + sub) for r0 in range(0, tm, sub)]

    def project(rows):
        h = _modulated_norm(x_ref[rows, :], ng_ref[...], scale, shift)
        proj = _dot(h, win_ref[...])
        proj_scr[rows, :] = proj[:, :GLA_QKVG]
        ga_scr[rows, :] = proj[:, GLA_QKVG:]

    def decays(rows):
        gate = _dot(ga_scr[rows, :], wg2_ref[...]) + bg_ref[...]
        log_a = _log_sigmoid(gate) * (1.0 / GLA_TAU)
        la_scr[rows, :] = log_a
        for ci in range(sub // cf):
            b_scr[rows.start + ci * cf:rows.start + (ci + 1) * cf, :] = _chunk_cumsum(
                tril_ref[0:cf, 0:cf], log_a[ci * cf:(ci + 1) * cf, :])
        return jnp.min(b_scr[rows, :]) >= -GLA_SAFE_EXPONENT

    def head_slices(rows, h):
        q = proj_scr[rows, h * dk:(h + 1) * dk] * qscale
        k = proj_scr[rows, GLA_DK + h * dk:GLA_DK + (h + 1) * dk]
        v = proj_scr[rows, 2 * GLA_DK + h * dv:2 * GLA_DK + (h + 1) * dv]
        return q, k, v

    def single_ref_rows(group):
        rr = lax.broadcasted_iota(jnp.int32, (cf, cf), 0)
        cc = lax.broadcasted_iota(jnp.int32, (cf, cf), 1)
        causal = rr >= cc
        chunks = [slice(r0, r0 + cf) for r0 in range(group.start, group.stop, cf)]
        for h in range(GLA_HEADS):
            qes, vbs, scores, kvs, decays = [], [], [], [], []
            for rows in chunks:
                q, k, v = head_slices(rows, h)
                b = b_scr[rows, h * dk:(h + 1) * dk]
                decay_last = jnp.exp(b[cf - 1:cf, :])
                qe = (q * jnp.exp(b)).astype(BF16)
                kt = k * jnp.exp(-b)
                vb = v.astype(BF16)
                scores.append(_dot_nt(qe, kt))
                kvs.append(_dot_tn(kt * decay_last, vb))
                qes.append(qe)
                vbs.append(vb)
                decays.append(_as_column(decay_last))
            st = st_scr[h]
            states = []
            for kv, decay_col in zip(kvs, decays):
                states.append(st)
                st = decay_col * st + kv
            st_scr[h] = st
            for rows, qe, vb, sc, st_in in zip(chunks, qes, vbs, scores, states):
                attn = jnp.where(causal, sc, 0.0).astype(BF16)
                lhs = jnp.concatenate([qe, attn], axis=1)
                rhs = jnp.concatenate([st_in.astype(BF16), vb], axis=0)
                o_scr[rows, h * dv:(h + 1) * dv] = jnp.dot(lhs, rhs, preferred_element_type=F32)

    def robust_rows(group):
        def robust_body(ci, carry):
            rows = pl.ds(pl.multiple_of(group.start + ci * c, c), c)
            masks, rr, cc = _level_masks(c)
            b_c = _chunk_cumsum(tril_ref[0:c, 0:c], la_scr[rows, :])
            outs = []
            for h0 in range(0, GLA_HEADS, 2):
                heads = (h0, h0 + 1)
                qkv = [head_slices(rows, h) for h in heads]
                la2 = [la_scr[rows, h * dk:(h + 1) * dk] for h in heads]
                b2 = [b_c[:, h * dk:(h + 1) * dk] for h in heads]
                st_refs = [st_scr.at[h] for h in heads]
                outs += _gla_chunk_pair([t[0] for t in qkv], [t[1] for t in qkv],
                                        [t[2] for t in qkv], la2, b2, st_refs, ones_ref[...],
                                        masks, rr, cc)
            o_scr[rows, :] = jnp.concatenate(outs, axis=1)
            return carry

        lax.fori_loop(0, sub // c, robust_body, 0)

    def output(rows):
        gate_in = proj_scr[rows, 2 * GLA_DK + GLA_DV:]
        xo_ref[rows, :] = _gla_output(o_scr[rows, :], gate_in, x_ref[rows, :], gate_mod,
                                      og_ref[...], wout_ref[...])

    project(groups[0])
    for gi, rows in enumerate(groups):
        single_ref_safe = decays(rows)
        following = groups[gi + 1] if gi + 1 < len(groups) else None

        @pl.when(single_ref_safe)
        def _():
            if following is not None:
                project(following)
            single_ref_rows(rows)
            output(rows)

        @pl.when(jnp.logical_not(single_ref_safe))
        def _():
            if following is not None:
                project(following)
            robust_rows(rows)
            output(rows)

    @pl.when(tile == tiles_per_seq - 1)
    def _():
        s_ref[...] = st_scr[...]


def _gla_prompt_call(x, mod, layer, n_seq, seq_len, tm, w):
    n, d = x.shape
    tiles_per_seq = seq_len // tm
    dk, dv = GLA_DK_HEAD, GLA_DV_HEAD
    in_specs = [_row_spec(tm, d)] + _mod_specs(mod, layer, 0, tm, seq_len) + [
        _const_spec((1, d)),
        _const_spec((d, GLA_IN_PAD)),
        _const_spec((GLA_GATE_PAD, GLA_DK)),
        _const_spec((1, GLA_DK)),
        _const_spec((1, dv)),
        _const_spec((GLA_DV, d)),
        _const_spec((GLA_TRIL, GLA_TRIL)),
        _const_spec((2 * dk, 2 * dk)),
    ]
    out_specs = [
        _row_spec(tm, d),
        pl.BlockSpec((None, GLA_HEADS, dk, dv), lambda i: (i // tiles_per_seq, 0, 0, 0)),
    ]
    out_shape = [
        jax.ShapeDtypeStruct((n, d), F32),
        jax.ShapeDtypeStruct((n_seq, GLA_HEADS, dk, dv), F32),
    ]
    return pl.pallas_call(
        functools.partial(_gla_prompt_kernel, tm=tm, sub=GLA_SUBTILE,
                          tiles_per_seq=tiles_per_seq),
        grid=(n // tm,),
        in_specs=in_specs,
        out_specs=out_specs,
        out_shape=out_shape,
        scratch_shapes=[
            pltpu.VMEM((tm, GLA_QKVG), F32),
            pltpu.VMEM((tm, GLA_GATE_PAD), F32),
            pltpu.VMEM((tm, GLA_DK), F32),
            pltpu.VMEM((tm, GLA_DK), F32),
            pltpu.VMEM((tm, GLA_DV), F32),
            pltpu.VMEM((GLA_HEADS, dk, dv), F32),
        ],
        compiler_params=pltpu.CompilerParams(
            dimension_semantics=("arbitrary",), vmem_limit_bytes=VMEM_LIMIT),
        name="gla_mixer_prompt",
    )(x, mod, mod, mod, w["norm_g"], w["w_in"], w["w_gate2"], w["b_gate"], w["out_g"],
      w["w_out"], w["tril"], w["ones_bd"])


def _gla_proj_kernel(x_ref, sh_ref, sc_ref, e_ref, ng_ref, win_ref, wg2_ref, bg_ref, p_ref,
                     la_ref):
    proj, log_a = _gla_project(x_ref[...], ng_ref[...], _rows_to_tokens(e_ref, sc_ref),
                               _rows_to_tokens(e_ref, sh_ref), win_ref[...], wg2_ref[...],
                               bg_ref[...])
    p_ref[...] = proj
    la_ref[...] = log_a


def _gla_proj_call(x, mod, layer, tm, seq_len, w):
    n, d = x.shape
    specs = _mod_specs(mod, layer, 0, tm, seq_len)
    in_specs = [_row_spec(tm, d), specs[0], specs[1],
                _const_spec((tm, 2 * tm // seq_len)),
                _const_spec((1, d)),
                _const_spec((d, GLA_IN_PAD)),
                _const_spec((GLA_GATE_PAD, GLA_DK)),
                _const_spec((1, GLA_DK))]
    return pl.pallas_call(
        _gla_proj_kernel,
        grid=(n // tm,),
        in_specs=in_specs,
        out_specs=[_row_spec(tm, GLA_QKVG), _row_spec(tm, GLA_DK)],
        out_shape=[jax.ShapeDtypeStruct((n, GLA_QKVG), F32),
                   jax.ShapeDtypeStruct((n, GLA_DK), F32)],
        compiler_params=pltpu.CompilerParams(
            dimension_semantics=("arbitrary",), vmem_limit_bytes=VMEM_LIMIT),
        name="gla_proj_sample",
    )(x, mod, mod, _token_copy_matrix(tm, seq_len), w["norm_g"], w["w_in"], w["w_gate2"],
      w["b_gate"])


def _gla_step_kernel(p_ref, la_ref, s0_ref, o_ref, s1_ref, *, seqs, steps):
    dk, dv = GLA_DK_HEAD, GLA_DV_HEAD
    qscale = GLA_DK_HEAD ** -0.5
    per_tile = SUBLANES // steps
    row = lax.broadcasted_iota(jnp.int32, (SUBLANES, dk), 0)

    def seq_rows(tile, s):
        if s:
            tile = pltpu.roll(tile, SUBLANES - s * steps, 0)
        keep = lax.broadcasted_iota(jnp.int32, tile.shape, 0) < steps
        return jnp.where(keep, tile, 0.0)

    def tile_body(ti, carry):
        rows = pl.ds(pl.multiple_of(ti * SUBLANES, SUBLANES), SUBLANES)
        p_tile = p_ref[rows, :]
        la_tile = la_ref[rows, :]
        o_tile = jnp.zeros((SUBLANES, GLA_DV), F32)
        for s in range(per_tile):
            bi = ti * per_tile + s
            p = seq_rows(p_tile, s)
            la_all = seq_rows(la_tile, s)
            outs = []
            for h in range(GLA_HEADS):
                q = p[:, h * dk:(h + 1) * dk] * qscale
                k = p[:, GLA_DK + h * dk:GLA_DK + (h + 1) * dk]
                v = p[:, 2 * GLA_DK + h * dv:2 * GLA_DK + (h + 1) * dv]
                la = la_all[:, h * dk:(h + 1) * dk]
                b = jnp.zeros((SUBLANES, dk), F32)
                for t in range(steps):
                    b = b + jnp.where(row >= t, la[t:t + 1, :], 0.0)
                b_last = b[steps - 1:steps, :]
                s0 = s0_ref[bi, h]
                o = _dot(q * jnp.exp(b), s0)
                for j in range(steps):
                    e = jnp.exp(jnp.minimum(b - b[j:j + 1, :], 0.0))
                    z = jnp.where(row >= j, q * k[j:j + 1, :] * e, 0.0)
                    o = o + jnp.sum(z, axis=-1, keepdims=True) * v[j:j + 1, :]
                outs.append(o)
                m = jnp.where(row == steps, jnp.exp(b_last), k * jnp.exp(b_last - b))
                m_t = m.T
                s1_ref[bi, h] = m_t[:, steps:steps + 1] * s0 + _dot(m_t, v)
            o_seq = jnp.concatenate(outs, axis=1)
            o_tile = o_tile + (pltpu.roll(o_seq, s * steps, 0) if s else o_seq)
        o_ref[rows, :] = o_tile
        return carry

    lax.fori_loop(0, seqs // per_tile, tile_body, 0)


def _gla_step_call(proj, log_a, state, steps, seqs):
    n_seq = state.shape[0]
    dk, dv = GLA_DK_HEAD, GLA_DV_HEAD
    rows = lambda width: pl.BlockSpec((seqs * steps, width), lambda i: (i, 0))
    st_spec = pl.BlockSpec((seqs, GLA_HEADS, dk, dv), lambda i: (i, 0, 0, 0))
    return pl.pallas_call(
        functools.partial(_gla_step_kernel, seqs=seqs, steps=steps),
        grid=(n_seq // seqs,),
        in_specs=[rows(GLA_QKVG), rows(GLA_DK), st_spec],
        out_specs=[rows(GLA_DV), st_spec],
        out_shape=[jax.ShapeDtypeStruct((n_seq * steps, GLA_DV), F32),
                   jax.ShapeDtypeStruct(state.shape, F32)],
        compiler_params=pltpu.CompilerParams(
            dimension_semantics=("arbitrary",), vmem_limit_bytes=VMEM_LIMIT),
        name="gla_step_sample",
    )(proj, log_a, state)


def _gla_out_kernel(o_ref, p_ref, x_ref, gt_ref, e_ref, og_ref, wout_ref, xo_ref):
    xo_ref[...] = _gla_output(o_ref[...], p_ref[...], x_ref[...], _rows_to_tokens(e_ref, gt_ref),
                              og_ref[...], wout_ref[...])


def _gla_out_call(o, proj, x, mod, layer, tm, seq_len, w):
    n, d = x.shape
    gate_spec = _mod_specs(mod, layer, 0, tm, seq_len)[2]
    g_block = (2 * GLA_DK + GLA_DV) // GLA_DV
    in_specs = [_row_spec(tm, GLA_DV),
                pl.BlockSpec((tm, GLA_DV), lambda i: (i, g_block)),
                _row_spec(tm, d), gate_spec,
                _const_spec((tm, 2 * tm // seq_len)),
                _const_spec((1, GLA_DV_HEAD)),
                _const_spec((GLA_DV, d))]
    return pl.pallas_call(
        _gla_out_kernel,
        grid=(n // tm,),
        in_specs=in_specs,
        out_specs=_row_spec(tm, d),
        out_shape=jax.ShapeDtypeStruct((n, d), F32),
        compiler_params=pltpu.CompilerParams(
            dimension_semantics=("arbitrary",), vmem_limit_bytes=VMEM_LIMIT),
        name="gla_out_sample",
    )(o, proj, x, mod, _token_copy_matrix(tm, seq_len), w["out_g"], w["w_out"])


PROMPT_FFN_TILE = 1024
PROMPT_GMLP_TILE = 1024
PROMPT_GLA_TILE = 1024
SAMPLE_TILE = 256
SAMPLE_SEQS_PER_STEP = 16


def kernel(x_prompt, x_sample, c_prompt, c_sample, state_gla, ada_w, ada_b, norm_mix_g, norm_ffn_g,
           ffn_w1, ffn_b1, ffn_w2, ffn_b2, gmlp_w_in, gmlp_b_in, gmlp_ln_g, gmlp_ln_b, gmlp_w_s,
           gmlp_b_s, gmlp_w_out, gmlp_b_out, gla_w_in, gla_w_gate2, gla_b_gate, gla_norm_g,
           gla_w_out, final_norm_g):
    n_seq_p, seq_p, d = x_prompt.shape
    n_seq_s, seq_s, _ = x_sample.shape
    assert d == D_MODEL and GMLP_SUBTILE % CHUNK_A == 0 and GLA_SUBTILE % GLA_FAST_CHUNK == 0
    assert all(seq_p % t == 0 for t in (PROMPT_FFN_TILE, PROMPT_GMLP_TILE, PROMPT_GLA_TILE))
    assert seq_s < SUBLANES and SUBLANES % seq_s == 0 and (n_seq_s * seq_s) % SAMPLE_TILE == 0
    row = lambda a: a.reshape(1, -1)

    mod_p, mod_s = _ada_call(c_prompt, c_sample, ada_w, ada_b)

    causal = np.tril(np.ones((CHUNK_A, CHUNK_A), dtype=bool))
    ws = jnp.where(causal[None], gmlp_w_s, jnp.zeros_like(gmlp_w_s))
    token = np.arange(CHUNK_A) % seq_s
    same_seq = (np.arange(CHUNK_A)[:, None] // seq_s) == (np.arange(CHUNK_A)[None, :] // seq_s)
    ws_s = jnp.where(same_seq[None], ws[:, token[:, None], token[None, :]], 0.0)
    bias_p = jnp.repeat(gmlp_b_s.T, GMLP_GROUP_W, axis=1)
    bias_s = bias_p[token]

    gmlp_w = dict(norm_g=row(norm_mix_g[0]), w_in=gmlp_w_in, b_in=row(gmlp_b_in),
                  ln_g=row(gmlp_ln_g), ln_b=row(gmlp_ln_b), w_out=gmlp_w_out,
                  b_out=row(gmlp_b_out))
    gmlp_w = dict(gmlp_w, mix_p=ws.astype(BF16), bias_p=bias_p, mix_s=ws_s.astype(BF16),
                  bias_s=bias_s)

    depth = ffn_w1.shape[0]
    ffn_w = dict(norm_g=norm_ffn_g.reshape(depth, 1, d), w1=ffn_w1,
                 b1=ffn_b1.reshape(depth, 1, D_FF), w2=ffn_w2,
                 b2=ffn_b2.reshape(depth, 1, d), final_g=row(final_norm_g))

    w_in_pad = jnp.pad(gla_w_in, ((0, 0), (0, GLA_IN_PAD - gla_w_in.shape[1]))).astype(BF16)
    wg2_pad = jnp.pad(gla_w_gate2, ((0, GLA_GATE_PAD - GLA_GATE_RANK), (0, 0))).astype(BF16)
    blk_ones = np.kron(np.eye(2, dtype=np.float32),
                       np.ones((GLA_DK_HEAD, GLA_DK_HEAD), np.float32))
    gla_w = dict(norm_g=row(norm_mix_g[1]), w_in=w_in_pad, w_gate2=wg2_pad, b_gate=row(gla_b_gate),
                 out_g=row(gla_norm_g), w_out=gla_w_out.astype(BF16),
                 tril=jnp.asarray(np.tril(np.ones((GLA_TRIL, GLA_TRIL), np.float32)), BF16),
                 ones_bd=jnp.asarray(blk_ones, BF16))

    xp = x_prompt.reshape(n_seq_p * seq_p, d)
    grid_for = lambda tm_p: _TwoGroupGrid(xp.shape[0], n_seq_s * seq_s, tm_p, SAMPLE_TILE, seq_p,
                                          seq_s)
    xp, xs, chunk_v = _gmlp_call(xp, x_sample, mod_p, mod_s, grid_for(PROMPT_GMLP_TILE), gmlp_w)
    xp, xs = _ffn_call(xp, xs, mod_p, mod_s, 0, grid_for(PROMPT_FFN_TILE), ffn_w, final_norm=False)
    xp, state_p = _gla_prompt_call(xp, mod_p, 1, n_seq_p, seq_p, PROMPT_GLA_TILE, gla_w)
    proj, log_a = _gla_proj_call(xs, mod_s, 1, SAMPLE_TILE, seq_s, gla_w)
    o, state_s = _gla_step_call(proj, log_a, state_gla, seq_s, SAMPLE_SEQS_PER_STEP)
    xs = _gla_out_call(o, proj, xs, mod_s, 1, SAMPLE_TILE, seq_s, gla_w)
    xp, xs = _ffn_call(xp, xs, mod_p, mod_s, 1, grid_for(PROMPT_FFN_TILE), ffn_w, final_norm=True)

    return (xp.reshape(x_prompt.shape), xs, state_p, state_s, chunk_v)
```

```python
import functools

import jax
import jax.numpy as jnp
import numpy as np
from jax import lax
from jax.experimental import pallas as pl
from jax.experimental.pallas import tpu as pltpu

F32 = jnp.float32
BF16 = jnp.bfloat16

D_MODEL = 1024
N_MOD = 6
CHUNK_A = 128
GMLP_WIDTH = D_MODEL
GMLP_GROUPS = 4
GMLP_GROUP_W = GMLP_WIDTH // GMLP_GROUPS
GLA_HEADS = 4
GLA_DK = D_MODEL // 2
GLA_DV = D_MODEL
GLA_DK_HEAD = GLA_DK // GLA_HEADS
GLA_DV_HEAD = GLA_DV // GLA_HEADS
GLA_GATE_RANK = 16
GLA_TAU = 16.0
GLA_CHUNK = 64
GLA_FAST_CHUNK = 128
GLA_TRIL = max(GLA_CHUNK, GLA_FAST_CHUNK)
GLA_SAFE_EXPONENT = 80.0
GLA_QKVG = 2 * GLA_DK + 2 * GLA_DV
GLA_GATE_PAD = 128
GLA_IN_PAD = GLA_QKVG + GLA_GATE_PAD
D_FF = 4 * D_MODEL
EPS = 1e-6

LANES = 128
SUBLANES = 8
BF16_SUBLANES = 16
ADA_ROWS = 256
FFN_CHUNK = 1024
FFN_SUBTILE = 512
GMLP_SUBTILE = 256
GLA_SUBTILE = 512
VMEM_LIMIT = 56 * 1024 * 1024
WEIGHT_LOAD_CHUNKS = 32
WEIGHT_STAGE_SLOTS = 4


def _rms(x, g):
    return x * lax.rsqrt(jnp.mean(x * x, axis=-1, keepdims=True) + EPS) * g


def _modulated_norm(x, g, scale, shift):
    inv = lax.rsqrt(jnp.mean(x * x, axis=-1, keepdims=True) + EPS)
    return (x * inv) * (g * (1.0 + scale)) + shift


def _dot(a, b):
    return jnp.dot(a.astype(BF16), b.astype(BF16), preferred_element_type=F32)


def _dot_nt(a, b):
    return lax.dot_general(a.astype(BF16), b.astype(BF16), (((1,), (1,)), ((), ())),
                           preferred_element_type=F32)


def _dot_tn(a, b):
    return lax.dot_general(a.astype(BF16), b.astype(BF16), (((0,), (0,)), ((), ())),
                           preferred_element_type=F32)


def _gelu_tanh(x):
    c1 = -2.0 * 0.7978845608028654 * 1.4426950408889634
    c2 = c1 * 0.044715
    return x / (1.0 + jnp.exp2(x * (c1 + c2 * (x * x))))


def _silu(x):
    return x * jax.nn.sigmoid(x)


def _log_sigmoid(x):
    return -(jnp.maximum(-x, 0.0) + jnp.log1p(jnp.exp(-jnp.abs(x))))


def _bf16_terms(x, n):
    terms = []
    for _ in range(n - 1):
        t = x.astype(BF16)
        terms.append(t)
        x = x - t.astype(F32)
    terms.append(x.astype(BF16))
    return terms


def _ada_kernel(c_ref, wa_ref, wb_ref, b_ref, op_ref, os_ref, s_scr, acc_scr, *, n_prompt, tk):
    k = pl.program_id(1)

    @pl.when((pl.program_id(0) == 0) & (k == 0))
    def _():
        for kk in range(s_scr.shape[0]):
            s_scr[kk] = _silu(c_ref[:, kk * tk:(kk + 1) * tk]).astype(BF16)

    @pl.when(k == 0)
    def _():
        acc_scr[...] = jnp.broadcast_to(b_ref[...], acc_scr.shape)

    s = s_scr[k]
    half = N_MOD // 2
    for j in range(N_MOD):
        cols = slice(j * D_MODEL, (j + 1) * D_MODEL)
        w_ref, jj = (wa_ref, j) if j < half else (wb_ref, j - half)
        w = w_ref[:, jj * D_MODEL:(jj + 1) * D_MODEL].astype(BF16)
        acc_scr[:, cols] += jnp.dot(s, w, preferred_element_type=F32)

    @pl.when(k == pl.num_programs(1) - 1)
    def _():
        op_ref[...] = acc_scr[0:n_prompt, :]
        os_ref[...] = acc_scr[n_prompt:n_prompt + os_ref.shape[0], :]


def _ada_call(c_prompt, c_sample, ada_w, ada_b):
    depth = ada_w.shape[0]
    n_prompt, n_seq = c_prompt.shape[0], c_sample.shape[0]
    d = D_MODEL
    pad = -(n_prompt + n_seq) % BF16_SUBLANES
    c_all = jnp.concatenate([c_prompt, c_sample, jnp.zeros((pad, d), F32)], axis=0)
    rows = c_all.shape[0]
    tk = ADA_ROWS
    width = N_MOD * d
    return pl.pallas_call(
        functools.partial(_ada_kernel, n_prompt=n_prompt, tk=tk),
        grid=(depth, d // tk),
        in_specs=[
            pl.BlockSpec((rows, d), lambda l, k: (0, 0)),
            pl.BlockSpec((None, tk, width // 2), lambda l, k: (l, k, 0)),
            pl.BlockSpec((None, tk, width // 2), lambda l, k: (l, k, 1)),
            pl.BlockSpec((None, 1, width), lambda l, k: (l, 0, 0)),
        ],
        out_specs=[
            pl.BlockSpec((None, n_prompt, width), lambda l, k: (l, 0, 0)),
            pl.BlockSpec((None, n_seq, width), lambda l, k: (l, 0, 0)),
        ],
        out_shape=[
            jax.ShapeDtypeStruct((depth, n_prompt, width), F32),
            jax.ShapeDtypeStruct((depth, n_seq, width), F32),
        ],
        scratch_shapes=[pltpu.VMEM((d // tk, rows, tk), BF16),
                        pltpu.VMEM((rows, width), F32)],
        compiler_params=pltpu.CompilerParams(
            dimension_semantics=("arbitrary", "arbitrary"), vmem_limit_bytes=VMEM_LIMIT),
        name="adaln_mod",
    )(c_all, ada_w, ada_w, ada_b.reshape(depth, 1, N_MOD * d))


def _const_spec(shape):
    zeros = (0,) * len(shape)
    return pl.BlockSpec(shape, lambda i: zeros, pipeline_mode=pl.Buffered(1))


def _layer_spec(shape, layer):
    zeros = (0,) * len(shape)
    return pl.BlockSpec((None,) + shape, lambda i: (layer,) + zeros,
                        pipeline_mode=pl.Buffered(1))


def _mod_specs(mod, layer, first_chunk, tm, rows_per_seq):
    d = D_MODEL
    specs = []
    for j in range(first_chunk, first_chunk + 3):
        if rows_per_seq < tm:
            specs.append(pl.BlockSpec((None, tm // rows_per_seq, d), lambda i, j=j: (layer, i, j)))
        else:
            specs.append(pl.BlockSpec((None, mod.shape[1], d), lambda i, j=j: (layer, 0, j)))
    return specs


def _token_copy_matrix(tm, rows_per_seq):
    copy = np.repeat(np.eye(tm // rows_per_seq, dtype=np.float32), rows_per_seq, axis=0)
    return jnp.asarray(np.tile(copy, (1, 2)), BF16)


def _rows_to_tokens(e_ref, mod_ref):
    terms = jnp.concatenate(_bf16_terms(mod_ref[...], 2), axis=0)
    return jnp.dot(e_ref[...], terms, preferred_element_type=F32)


def _mod_rows(ref, rows):
    return ref[...] if ref.shape[0] == 1 else ref[rows, :]


def _row_spec(tm, width):
    return pl.BlockSpec((tm, width), lambda i: (i, 0))


class _TwoGroupGrid:
    def __init__(self, n_prompt_rows, n_sample_rows, tm_p, tm_s, seq_len, sample_seq_len):
        self.tm_p, self.tm_s = tm_p, tm_s
        self.sample_seq_len = sample_seq_len
        self.p_steps = n_prompt_rows // tm_p
        self.s_steps = n_sample_rows // tm_s
        self.tiles_per_seq = seq_len // tm_p
        self.n_prompt_seqs = n_prompt_rows // seq_len
        self.steps = self.p_steps + self.s_steps

    def p_idx(self, i):
        return jnp.minimum(i, self.p_steps - 1)

    def s_idx(self, i):
        return jnp.maximum(i - self.p_steps, 0)

    def rows_p(self, width):
        return pl.BlockSpec((self.tm_p, width), lambda i: (self.p_idx(i), 0))

    def rows_s(self, width, col_block=0):
        return pl.BlockSpec((self.tm_s, width), lambda i: (self.s_idx(i), col_block))

    def mods_p(self, layer, first_chunk):
        return [pl.BlockSpec((None, self.n_prompt_seqs, D_MODEL), lambda i, j=j: (layer, 0, j))
                for j in range(first_chunk, first_chunk + 3)]

    def prompt_row(self, ref, step):
        return ref[pl.ds(step // self.tiles_per_seq, 1), :]

    def mods_s(self, layer, first_chunk):
        return [pl.BlockSpec((None, self.tm_s // self.sample_seq_len, D_MODEL),
                             lambda i, j=j: (layer, self.s_idx(i), j))
                for j in range(first_chunk, first_chunk + 3)]

    def seqs_s(self, width):
        seq = self.sample_seq_len
        return pl.BlockSpec((self.tm_s // seq, seq, width), lambda i: (self.s_idx(i), 0, 0))

    def token_copy_spec(self):
        return _const_spec((self.tm_s, 2 * self.tm_s // self.sample_seq_len))

    def token_copy_matrix(self):
        return _token_copy_matrix(self.tm_s, self.sample_seq_len)


def _gmlp_rows(x_ref, sh_ref, sc_ref, gt_ref, mix_ref, bs_ref, xo_ref, v_ref, z_scr, w, tm, sub):
    ng_ref, win_ref, bin_ref, lng_ref, lnb_ref, wout_ref, bout_ref = w
    groups = [slice(r0, r0 + sub) for r0 in range(0, tm, sub)]

    def project(gi):
        rows = groups[gi]
        h = _modulated_norm(x_ref[rows, :], ng_ref[...], _mod_rows(sc_ref, rows),
                            _mod_rows(sh_ref, rows))
        z_scr[gi % 2] = _dot(h, win_ref[...])

    project(0)
    for gi, rows in enumerate(groups):
        if gi + 1 < len(groups):
            project(gi + 1)
        z = _gelu_tanh(z_scr[gi % 2] + bin_ref[...])
        u = z[:, :GMLP_WIDTH]
        v = z[:, GMLP_WIDTH:]
        mu = jnp.mean(v, axis=-1, keepdims=True)
        vc = v - mu
        var = jnp.mean(vc * vc, axis=-1, keepdims=True)
        v = vc * lax.rsqrt(var + EPS) * lng_ref[...] + lnb_ref[...]
        if v_ref is not None:
            v_ref[rows, :] = v
        vb = v.astype(BF16)
        mixed = []
        for c in range(sub // CHUNK_A):
            cols = []
            for g in range(GMLP_GROUPS):
                blk = vb[c * CHUNK_A:(c + 1) * CHUNK_A, g * GMLP_GROUP_W:(g + 1) * GMLP_GROUP_W]
                cols.append(jnp.dot(mix_ref[g], blk, preferred_element_type=F32))
            mixed.append(jnp.concatenate(cols, axis=1) + bs_ref[...])
        s = jnp.concatenate(mixed, axis=0)
        y = _dot(u * s, wout_ref[...]) + bout_ref[...]
        xo_ref[rows, :] = x_ref[rows, :] + _mod_rows(gt_ref, rows) * y


def _rows_to_seq_layout(rows_ref, seq_ref, slab_scr):
    n_seq, seq, width = seq_ref.shape
    for j in range(width // LANES):
        lanes = slice(j * LANES, (j + 1) * LANES)
        slab_scr[...] = rows_ref[:, lanes]
        for t in range(seq):
            seq_ref[:, t, lanes] = slab_scr[pl.ds(t, n_seq, stride=seq), :]


def _seq_layout_to_rows(seq_ref, rows_ref, slab_scr):
    n_seq, seq, width = seq_ref.shape
    for j in range(width // LANES):
        lanes = slice(j * LANES, (j + 1) * LANES)
        for t in range(seq):
            slab_scr[pl.ds(t, n_seq, stride=seq), :] = seq_ref[:, t, lanes]
        rows_ref[:, lanes] = slab_scr[...]


def _gmlp_kernel(xp_ref, shp_ref, scp_ref, gtp_ref, xs_ref, shs_ref, scs_ref, gts_ref, e_ref,
                 ng_ref, win_hbm, bin_ref, lng_ref, lnb_ref, wout_hbm, bout_ref,
                 mixp_ref, bsp_ref, mixs_ref, bss_ref, op_ref, os_ref, vs_ref, z_scr, v_scr,
                 x_scr, slab_scr,
                 win_ref, wout_ref, stage_in, stage_out, sems_in, sems_out, *, grid):
    w = (ng_ref, win_ref, bin_ref, lng_ref, lnb_ref, wout_ref, bout_ref)
    step = pl.program_id(0)

    @pl.when(step == 0)
    def _():
        _load_as_bf16([(win_hbm, win_ref, stage_in, sems_in),
                       (wout_hbm, wout_ref, stage_out, sems_out)])

    @pl.when(step < grid.p_steps)
    def _():
        sh, sc, gt = (grid.prompt_row(r, step) for r in (shp_ref, scp_ref, gtp_ref))
        _gmlp_rows(xp_ref, sh, sc, gt, mixp_ref, bsp_ref, op_ref, None, z_scr, w,
                   grid.tm_p, GMLP_SUBTILE)

    @pl.when(step >= grid.p_steps)
    def _():
        sh, sc, gt = (_rows_to_tokens(e_ref, r) for r in (shs_ref, scs_ref, gts_ref))
        _seq_layout_to_rows(xs_ref, x_scr, slab_scr)
        _gmlp_rows(x_scr, sh, sc, gt, mixs_ref, bss_ref, os_ref, v_scr, z_scr, w,
                   grid.tm_s, GMLP_SUBTILE)
        _rows_to_seq_layout(v_scr, vs_ref, slab_scr)


def _gmlp_call(xp, xs_seq, mod_p, mod_s, grid, w):
    d = D_MODEL
    n_sample = xs_seq.shape[0] * xs_seq.shape[1]
    mix_spec = _const_spec((GMLP_GROUPS, CHUNK_A, CHUNK_A))
    bias_spec = _const_spec((CHUNK_A, GMLP_WIDTH))
    in_specs = ([grid.rows_p(d)] + grid.mods_p(0, 0) + [grid.seqs_s(d)] + grid.mods_s(0, 0) + [
        grid.token_copy_spec(),
        _const_spec((1, d)),
        pl.BlockSpec(memory_space=pl.ANY),
        _const_spec((1, 2 * GMLP_WIDTH)),
        _const_spec((1, GMLP_WIDTH)),
        _const_spec((1, GMLP_WIDTH)),
        pl.BlockSpec(memory_space=pl.ANY),
        _const_spec((1, d)),
        mix_spec, bias_spec, mix_spec, bias_spec,
    ])
    return pl.pallas_call(
        functools.partial(_gmlp_kernel, grid=grid),
        grid=(grid.steps,),
        in_specs=in_specs,
        out_specs=[grid.rows_p(d), grid.rows_s(d), grid.seqs_s(GMLP_WIDTH)],
        out_shape=[jax.ShapeDtypeStruct(xp.shape, F32),
                   jax.ShapeDtypeStruct((n_sample, d), F32),
                   jax.ShapeDtypeStruct(xs_seq.shape[:2] + (GMLP_WIDTH,), F32)],
        scratch_shapes=[pltpu.VMEM((2, GMLP_SUBTILE, 2 * GMLP_WIDTH), F32),
                        pltpu.VMEM((grid.tm_s, GMLP_WIDTH), F32),
                        pltpu.VMEM((grid.tm_s, d), F32),
                        pltpu.VMEM((grid.tm_s, LANES), F32),
                        pltpu.VMEM((d, 2 * GMLP_WIDTH), BF16),
                        pltpu.VMEM((GMLP_WIDTH, d), BF16),
                        pltpu.VMEM((WEIGHT_STAGE_SLOTS, d // WEIGHT_LOAD_CHUNKS, 2 * GMLP_WIDTH),
                                   F32),
                        pltpu.VMEM((WEIGHT_STAGE_SLOTS, GMLP_WIDTH // WEIGHT_LOAD_CHUNKS, d), F32),
                        pltpu.SemaphoreType.DMA((WEIGHT_STAGE_SLOTS,)),
                        pltpu.SemaphoreType.DMA((WEIGHT_STAGE_SLOTS,))],
        compiler_params=pltpu.CompilerParams(
            dimension_semantics=("arbitrary",), vmem_limit_bytes=VMEM_LIMIT),
        name="gmlp_mixer",
    )(xp, mod_p, mod_p, mod_p, xs_seq, mod_s, mod_s, mod_s, grid.token_copy_matrix(),
      w["norm_g"], w["w_in"], w["b_in"],
      w["ln_g"], w["ln_b"], w["w_out"], w["b_out"], w["mix_p"], w["bias_p"], w["mix_s"],
      w["bias_s"])


def _ffn_rows(x_ref, sh_ref, sc_ref, gt_ref, o_ref, h_scr, w, final_norm, tm):
    ng_ref, w1_ref, b1_ref, w2_ref, b2_ref, fg_ref = w
    sub = min(FFN_SUBTILE, tm)
    groups = [slice(r0, r0 + sub) for r0 in range(0, tm, sub)]

    def normalise(gi):
        rows = groups[gi]
        h_scr[gi % 2, 0:sub, :] = _modulated_norm(
            x_ref[rows, :], ng_ref[...], _mod_rows(sc_ref, rows),
            _mod_rows(sh_ref, rows)).astype(BF16)

    normalise(0)
    for gi, rows in enumerate(groups):
        if gi + 1 < len(groups):
            normalise(gi + 1)
        h = h_scr[gi % 2, 0:sub, :]
        acc = jnp.zeros((sub, D_MODEL), F32)
        for j in range(D_FF // FFN_CHUNK):
            cols = slice(j * FFN_CHUNK, (j + 1) * FFN_CHUNK)
            a = jnp.dot(h, w1_ref[:, cols], preferred_element_type=F32) + b1_ref[:, cols]
            r = jnp.square(jnp.maximum(a, 0.0)).astype(BF16)
            acc = acc + jnp.dot(r, w2_ref[cols, :], preferred_element_type=F32)
        y = x_ref[rows, :] + _mod_rows(gt_ref, rows) * (acc + b2_ref[...])
        if final_norm:
            y = _rms(y, fg_ref[...])
        o_ref[rows, :] = y


def _load_as_bf16(jobs):
    def chunk_copy(job, c):
        src_hbm, _, stage_ref, sem_ref = job
        slot = c % stage_ref.shape[0]
        rows = pl.ds(c * stage_ref.shape[1], stage_ref.shape[1])
        return pltpu.make_async_copy(src_hbm.at[rows, :], stage_ref.at[slot], sem_ref.at[slot])

    n_chunks = {job[1].shape[0] // job[2].shape[1] for job in jobs}
    n_slots = {job[2].shape[0] for job in jobs}
    assert len(n_chunks) == 1 and len(n_slots) == 1
    n_chunks, n_slots = n_chunks.pop(), n_slots.pop()
    for c in range(min(n_slots - 1, n_chunks)):
        for job in jobs:
            chunk_copy(job, c).start()
    for c in range(n_chunks):
        for job in jobs:
            if c + n_slots - 1 < n_chunks:
                chunk_copy(job, c + n_slots - 1).start()
        for job in jobs:
            _, dst_ref, stage_ref, _ = job
            chunk_copy(job, c).wait()
            rows = stage_ref.shape[1]
            dst_ref[c * rows:(c + 1) * rows, :] = stage_ref[c % n_slots].astype(BF16)


def _ffn_kernel(xp_ref, shp_ref, scp_ref, gtp_ref, xs_ref, shs_ref, scs_ref, gts_ref, e_ref,
                ng_ref, w1_hbm, b1_ref, w2_hbm, b2_ref, fg_ref, op_ref, os_ref, h_scr,
                w1_ref, w2_ref, stage1, stage2, sems1, sems2, ys_scr, slab_scr,
                *, final_norm, grid, layer):
    w = (ng_ref, w1_ref, b1_ref, w2_ref, b2_ref, fg_ref)
    step = pl.program_id(0)

    @pl.when(step == 0)
    def _():
        _load_as_bf16([(w1_hbm.at[layer], w1_ref, stage1, sems1),
                       (w2_hbm.at[layer], w2_ref, stage2, sems2)])

    @pl.when(step < grid.p_steps)
    def _():
        sh, sc, gt = (grid.prompt_row(r, step) for r in (shp_ref, scp_ref, gtp_ref))
        _ffn_rows(xp_ref, sh, sc, gt, op_ref, h_scr, w, final_norm, grid.tm_p)

    @pl.when(step >= grid.p_steps)
    def _():
        sh, sc, gt = (_rows_to_tokens(e_ref, r) for r in (shs_ref, scs_ref, gts_ref))
        if final_norm:
            _ffn_rows(xs_ref, sh, sc, gt, ys_scr, h_scr, w, final_norm, grid.tm_s)
            _rows_to_seq_layout(ys_scr, os_ref, slab_scr)
        else:
            _ffn_rows(xs_ref, sh, sc, gt, os_ref, h_scr, w, final_norm, grid.tm_s)


def _ffn_call(xp, xs, mod_p, mod_s, layer, grid, w, final_norm):
    d = D_MODEL
    seq_shape = (xs.shape[0] // grid.sample_seq_len, grid.sample_seq_len, d)
    in_specs = ([grid.rows_p(d)] + grid.mods_p(layer, 3) + [grid.rows_s(d)]
                + grid.mods_s(layer, 3) + [
        grid.token_copy_spec(),
        _layer_spec((1, d), layer),
        pl.BlockSpec(memory_space=pl.ANY),
        _layer_spec((1, D_FF), layer),
        pl.BlockSpec(memory_space=pl.ANY),
        _layer_spec((1, d), layer),
        _const_spec((1, d)),
    ])
    return pl.pallas_call(
        functools.partial(_ffn_kernel, final_norm=final_norm, grid=grid, layer=layer),
        grid=(grid.steps,),
        in_specs=in_specs,
        out_specs=[grid.rows_p(d), grid.seqs_s(d) if final_norm else grid.rows_s(d)],
        out_shape=[jax.ShapeDtypeStruct(xp.shape, F32),
                   jax.ShapeDtypeStruct(seq_shape if final_norm else xs.shape, F32)],
        scratch_shapes=[pltpu.VMEM((2, FFN_SUBTILE, d), BF16),
                        pltpu.VMEM((d, D_FF), BF16),
                        pltpu.VMEM((D_FF, d), BF16),
                        pltpu.VMEM((WEIGHT_STAGE_SLOTS, d // WEIGHT_LOAD_CHUNKS, D_FF), F32),
                        pltpu.VMEM((WEIGHT_STAGE_SLOTS, D_FF // WEIGHT_LOAD_CHUNKS, d), F32),
                        pltpu.SemaphoreType.DMA((WEIGHT_STAGE_SLOTS,)),
                        pltpu.SemaphoreType.DMA((WEIGHT_STAGE_SLOTS,)),
                        pltpu.VMEM((grid.tm_s, d), F32),
                        pltpu.VMEM((grid.tm_s, LANES), F32)],
        compiler_params=pltpu.CompilerParams(
            dimension_semantics=("arbitrary",), vmem_limit_bytes=VMEM_LIMIT),
        name="sqrelu_mlp",
    )(xp, mod_p, mod_p, mod_p, xs, mod_s, mod_s, mod_s, grid.token_copy_matrix(),
      w["norm_g"], w["w1"], w["b1"], w["w2"],
      w["b2"], w["final_g"])


def _gla_project(x, ng, scale, shift, win, wg2, bg):
    h = _modulated_norm(x, ng, scale, shift)
    proj = _dot(h, win)
    gate = _dot(proj[:, GLA_QKVG:], wg2) + bg
    log_a = _log_sigmoid(gate) * (1.0 / GLA_TAU)
    return proj[:, :GLA_QKVG], log_a


def _gla_output(o, gate_in, x, gt, og, wout):
    parts = []
    for h in range(GLA_HEADS):
        cols = slice(h * GLA_DV_HEAD, (h + 1) * GLA_DV_HEAD)
        parts.append(_rms(o[:, cols], og))
    on = jnp.concatenate(parts, axis=1)
    y = _dot(on * _silu(gate_in), wout)
    return x + gt * y


def _as_column(row):
    return jnp.broadcast_to(row, (SUBLANES, row.shape[1])).T[:, 0:1]


def _chunk_cumsum(tril, log_a):
    head = log_a.astype(BF16)
    rest = (log_a - head.astype(F32)).astype(BF16)
    return (jnp.dot(tril, head, preferred_element_type=F32)
            + jnp.dot(tril, rest, preferred_element_type=F32))


def _level_masks(c):
    rr = lax.broadcasted_iota(jnp.int32, (c, c), 0)
    cc = lax.broadcasted_iota(jnp.int32, (c, c), 1)
    masks = []
    s = c
    while s > SUBLANES:
        half = s // 2
        same = (rr ^ cc) < s
        masks.append(same & ((rr & half) != 0) & ((cc & half) == 0))
        s = half
    return masks, rr, cc


def _block_rows(x, s, r):
    c, w = x.shape
    return jnp.concatenate(
        [jnp.broadcast_to(x[p * s + r:p * s + r + 1, :], (s, w)) for p in range(c // s)], axis=0)


def _diag_terms(q, k, a):
    c, w = q.shape
    ii = lax.broadcasted_iota(jnp.int32, (c, w), 0) & (SUBLANES - 1)
    decay = jnp.zeros((c, w), F32)
    terms = [None] * SUBLANES
    for j in range(SUBLANES - 1, -1, -1):
        if j < SUBLANES - 1:
            decay = decay * _block_rows(a, SUBLANES, j + 1)
        decay = jnp.where(ii == j, 1.0, decay)
        terms[j] = q * _block_rows(k, SUBLANES, j) * decay
    return jnp.concatenate(terms, axis=0)


def _gla_chunk_pair(q2, k2, v2, la2, b2, st_refs, ones_bd, masks, rr, cc):
    c = q2[0].shape[0]
    zs = []
    for q, k, la in zip(q2, k2, la2):
        zs.append(_diag_terms(q, k, jnp.exp(la)))
    sums = jnp.dot(jnp.concatenate(zs, axis=1).astype(BF16), ones_bd,
                   preferred_element_type=F32)
    outs = []
    for hh, (q, k, v, b, st_ref) in enumerate(zip(q2, k2, v2, b2, st_refs)):
        dk = q.shape[1]
        attn = jnp.zeros((c, c), F32)
        base = rr & ~(SUBLANES - 1)
        for j in range(SUBLANES):
            rj = sums[j * c:(j + 1) * c, hh * dk:hh * dk + c]
            attn = jnp.where(cc == base + j, rj, attn)
        s = c
        for mask in masks:
            half = s // 2
            e = jnp.exp(-jnp.abs(b - _block_rows(b, s, half)))
            attn = jnp.where(mask, _dot_nt(q * e, k * e), attn)
            s = half
        b_last = b[c - 1:c, :]
        st = st_ref[...]
        o = _dot(q * jnp.exp(b), st) + _dot(attn, v)
        kd = k * jnp.exp(b_last - b)
        st_ref[...] = _as_column(jnp.exp(b_last)) * st + _dot_tn(kd, v)
        outs.append(o)
    return outs


def _gla_prompt_kernel(x_ref, sh_ref, sc_ref, gt_ref, ng_ref, win_ref, wg2_ref, bg_ref,
                       og_ref, wout_ref, tril_ref, ones_ref, xo_ref, s_ref,
                       proj_scr, ga_scr, la_scr, b_scr, o_scr, st_scr, *, tm, sub, tiles_per_seq):
    tile = pl.program_id(0) % tiles_per_seq
    seq_row = pl.ds(pl.program_id(0) // tiles_per_seq, 1)
    shift, scale, gate_mod = (r[seq_row, :] for r in (sh_ref, sc_ref, gt_ref))

    @pl.when(tile == 0)
    def _():
        st_scr[...] = jnp.zeros(st_scr.shape, F32)

    dk, dv = GLA_DK_HEAD, GLA_DV_HEAD
    qscale = GLA_DK_HEAD ** -0.5
    cf = GLA_FAST_CHUNK
    c = GLA_CHUNK
    groups = [slice(r0, r0 + sub) for r0 in range(0, tm, sub)]

    def project(rows):
        h = _modulated_norm(x_ref[rows, :], ng_ref[...], scale, shift)
        proj = _dot(h, win_ref[...])
        proj_scr[rows, :] = proj[:, :GLA_QKVG]
        ga_scr[rows, :] = proj[:, GLA_QKVG:]

    def decays(rows):
        gate = _dot(ga_scr[rows, :], wg2_ref[...]) + bg_ref[...]
        log_a = _log_sigmoid(gate) * (1.0 / GLA_TAU)
        la_scr[rows, :] = log_a
        for ci in range(sub // cf):
            b_scr[rows.start + ci * cf:rows.start + (ci + 1) * cf, :] = _chunk_cumsum(
                tril_ref[0:cf, 0:cf], log_a[ci * cf:(ci + 1) * cf, :])
        return jnp.min(b_scr[rows, :]) >= -GLA_SAFE_EXPONENT

    def head_slices(rows, h):
        q = proj_scr[rows, h * dk:(h + 1) * dk] * qscale
        k = proj_scr[rows, GLA_DK + h * dk:GLA_DK + (h + 1) * dk]
        v = proj_scr[rows, 2 * GLA_DK + h * dv:2 * GLA_DK + (h + 1) * dv]
        return q, k, v

    def single_ref_rows(group):
        rr = lax.broadcasted_iota(jnp.int32, (cf, cf), 0)
        cc = lax.broadcasted_iota(jnp.int32, (cf, cf), 1)
        causal = rr >= cc
        chunks = [slice(r0, r0 + cf) for r0 in range(group.start, group.stop, cf)]
        for h in range(GLA_HEADS):
            qes, vbs, scores, kvs, decays = [], [], [], [], []
            for rows in chunks:
                q, k, v = head_slices(rows, h)
                b = b_scr[rows, h * dk:(h + 1) * dk]
                decay_last = jnp.exp(b[cf - 1:cf, :])
                qe = (q * jnp.exp(b)).astype(BF16)
                kt = k * jnp.exp(-b)
                vb = v.astype(BF16)
                scores.append(_dot_nt(qe, kt))
                kvs.append(_dot_tn(kt * decay_last, vb))
                qes.append(qe)
                vbs.append(vb)
                decays.append(_as_column(decay_last))
            st = st_scr[h]
            states = []
            for kv, decay_col in zip(kvs, decays):
                states.append(st)
                st = decay_col * st + kv
            st_scr[h] = st
            for rows, qe, vb, sc, st_in in zip(chunks, qes, vbs, scores, states):
                attn = jnp.where(causal, sc, 0.0).astype(BF16)
                lhs = jnp.concatenate([qe, attn], axis=1)
                rhs = jnp.concatenate([st_in.astype(BF16), vb], axis=0)
                o_scr[rows, h * dv:(h + 1) * dv] = jnp.dot(lhs, rhs, preferred_element_type=F32)

    def robust_rows(group):
        def robust_body(ci, carry):
            rows = pl.ds(pl.multiple_of(group.start + ci * c, c), c)
            masks, rr, cc = _level_masks(c)
            b_c = _chunk_cumsum(tril_ref[0:c, 0:c], la_scr[rows, :])
            outs = []
            for h0 in range(0, GLA_HEADS, 2):
                heads = (h0, h0 + 1)
                qkv = [head_slices(rows, h) for h in heads]
                la2 = [la_scr[rows, h * dk:(h + 1) * dk] for h in heads]
                b2 = [b_c[:, h * dk:(h + 1) * dk] for h in heads]
                st_refs = [st_scr.at[h] for h in heads]
                outs += _gla_chunk_pair([t[0] for t in qkv], [t[1] for t in qkv],
                                        [t[2] for t in qkv], la2, b2, st_refs, ones_ref[...],
                                        masks, rr, cc)
            o_scr[rows, :] = jnp.concatenate(outs, axis=1)
            return carry

        lax.fori_loop(0, sub // c, robust_body, 0)

    def output(rows):
        gate_in = proj_scr[rows, 2 * GLA_DK + GLA_DV:]
        xo_ref[rows, :] = _gla_output(o_scr[rows, :], gate_in, x_ref[rows, :], gate_mod,
                                      og_ref[...], wout_ref[...])

    project(groups[0])
    for gi, rows in enumerate(groups):
        single_ref_safe = decays(rows)
        following = groups[gi + 1] if gi + 1 < len(groups) else None

        @pl.when(single_ref_safe)
        def _():
            if following is not None:
                project(following)
            single_ref_rows(rows)
            output(rows)

        @pl.when(jnp.logical_not(single_ref_safe))
        def _():
            if following is not None:
                project(following)
            robust_rows(rows)
            output(rows)

    @pl.when(tile == tiles_per_seq - 1)
    def _():
        s_ref[...] = st_scr[...]


def _gla_prompt_call(x, mod, layer, n_seq, seq_len, tm, w):
    n, d = x.shape
    tiles_per_seq = seq_len // tm
    dk, dv = GLA_DK_HEAD, GLA_DV_HEAD
    in_specs = [_row_spec(tm, d)] + _mod_specs(mod, layer, 0, tm, seq_len) + [
        _const_spec((1, d)),
        _const_spec((d, GLA_IN_PAD)),
        _const_spec((GLA_GATE_PAD, GLA_DK)),
        _const_spec((1, GLA_DK)),
        _const_spec((1, dv)),
        _const_spec((GLA_DV, d)),
        _const_spec((GLA_TRIL, GLA_TRIL)),
        _const_spec((2 * dk, 2 * dk)),
    ]
    out_specs = [
        _row_spec(tm, d),
        pl.BlockSpec((None, GLA_HEADS, dk, dv), lambda i: (i // tiles_per_seq, 0, 0, 0)),
    ]
    out_shape = [
        jax.ShapeDtypeStruct((n, d), F32),
        jax.ShapeDtypeStruct((n_seq, GLA_HEADS, dk, dv), F32),
    ]
    return pl.pallas_call(
        functools.partial(_gla_prompt_kernel, tm=tm, sub=GLA_SUBTILE,
                          tiles_per_seq=tiles_per_seq),
        grid=(n // tm,),
        in_specs=in_specs,
        out_specs=out_specs,
        out_shape=out_shape,
        scratch_shapes=[
            pltpu.VMEM((tm, GLA_QKVG), F32),
            pltpu.VMEM((tm, GLA_GATE_PAD), F32),
            pltpu.VMEM((tm, GLA_DK), F32),
            pltpu.VMEM((tm, GLA_DK), F32),
            pltpu.VMEM((tm, GLA_DV), F32),
            pltpu.VMEM((GLA_HEADS, dk, dv), F32),
        ],
        compiler_params=pltpu.CompilerParams(
            dimension_semantics=("arbitrary",), vmem_limit_bytes=VMEM_LIMIT),
        name="gla_mixer_prompt",
    )(x, mod, mod, mod, w["norm_g"], w["w_in"], w["w_gate2"], w["b_gate"], w["out_g"],
      w["w_out"], w["tril"], w["ones_bd"])


def _gla_proj_kernel(x_ref, sh_ref, sc_ref, e_ref, ng_ref, win_ref, wg2_ref, bg_ref, p_ref,
                     la_ref):
    proj, log_a = _gla_project(x_ref[...], ng_ref[...], _rows_to_tokens(e_ref, sc_ref),
                               _rows_to_tokens(e_ref, sh_ref), win_ref[...], wg2_ref[...],
                               bg_ref[...])
    p_ref[...] = proj
    la_ref[...] = log_a


def _gla_proj_call(x, mod, layer, tm, seq_len, w):
    n, d = x.shape
    specs = _mod_specs(mod, layer, 0, tm, seq_len)
    in_specs = [_row_spec(tm, d), specs[0], specs[1],
                _const_spec((tm, 2 * tm // seq_len)),
                _const_spec((1, d)),
                _const_spec((d, GLA_IN_PAD)),
                _const_spec((GLA_GATE_PAD, GLA_DK)),
                _const_spec((1, GLA_DK))]
    return pl.pallas_call(
        _gla_proj_kernel,
        grid=(n // tm,),
        in_specs=in_specs,
        out_specs=[_row_spec(tm, GLA_QKVG), _row_spec(tm, GLA_DK)],
        out_shape=[jax.ShapeDtypeStruct((n, GLA_QKVG), F32),
                   jax.ShapeDtypeStruct((n, GLA_DK), F32)],
        compiler_params=pltpu.CompilerParams(
            dimension_semantics=("arbitrary",), vmem_limit_bytes=VMEM_LIMIT),
        name="gla_proj_sample",
    )(x, mod, mod, _token_copy_matrix(tm, seq_len), w["norm_g"], w["w_in"], w["w_gate2"],
      w["b_gate"])


def _gla_step_kernel(p_ref, la_ref, s0_ref, o_ref, s1_ref, *, seqs, steps):
    dk, dv = GLA_DK_HEAD, GLA_DV_HEAD
    qscale = GLA_DK_HEAD ** -0.5
    per_tile = SUBLANES // steps
    row = lax.broadcasted_iota(jnp.int32, (SUBLANES, dk), 0)

    def seq_rows(tile, s):
        if s:
            tile = pltpu.roll(tile, SUBLANES - s * steps, 0)
        keep = lax.broadcasted_iota(jnp.int32, tile.shape, 0) < steps
        return jnp.where(keep, tile, 0.0)

    def tile_body(ti, carry):
        rows = pl.ds(pl.multiple_of(ti * SUBLANES, SUBLANES), SUBLANES)
        p_tile = p_ref[rows, :]
        la_tile = la_ref[rows, :]
        o_tile = jnp.zeros((SUBLANES, GLA_DV), F32)
        for s in range(per_tile):
            bi = ti * per_tile + s
            p = seq_rows(p_tile, s)
            la_all = seq_rows(la_tile, s)
            outs = []
            for h in range(GLA_HEADS):
                q = p[:, h * dk:(h + 1) * dk] * qscale
                k = p[:, GLA_DK + h * dk:GLA_DK + (h + 1) * dk]
                v = p[:, 2 * GLA_DK + h * dv:2 * GLA_DK + (h + 1) * dv]
                la = la_all[:, h * dk:(h + 1) * dk]
                b = jnp.zeros((SUBLANES, dk), F32)
                for t in range(steps):
                    b = b + jnp.where(row >= t, la[t:t + 1, :], 0.0)
                b_last = b[steps - 1:steps, :]
                s0 = s0_ref[bi, h]
                o = _dot(q * jnp.exp(b), s0)
                for j in range(steps):
                    e = jnp.exp(jnp.minimum(b - b[j:j + 1, :], 0.0))
                    z = jnp.where(row >= j, q * k[j:j + 1, :] * e, 0.0)
                    o = o + jnp.sum(z, axis=-1, keepdims=True) * v[j:j + 1, :]
                outs.append(o)
                m = jnp.where(row == steps, jnp.exp(b_last), k * jnp.exp(b_last - b))
                m_t = m.T
                s1_ref[bi, h] = m_t[:, steps:steps + 1] * s0 + _dot(m_t, v)
            o_seq = jnp.concatenate(outs, axis=1)
            o_tile = o_tile + (pltpu.roll(o_seq, s * steps, 0) if s else o_seq)
        o_ref[rows, :] = o_tile
        return carry

    lax.fori_loop(0, seqs // per_tile, tile_body, 0)


def _gla_step_call(proj, log_a, state, steps, seqs):
    n_seq = state.shape[0]
    dk, dv = GLA_DK_HEAD, GLA_DV_HEAD
    rows = lambda width: pl.BlockSpec((seqs * steps, width), lambda i: (i, 0))
    st_spec = pl.BlockSpec((seqs, GLA_HEADS, dk, dv), lambda i: (i, 0, 0, 0))
    return pl.pallas_call(
        functools.partial(_gla_step_kernel, seqs=seqs, steps=steps),
        grid=(n_seq // seqs,),
        in_specs=[rows(GLA_QKVG), rows(GLA_DK), st_spec],
        out_specs=[rows(GLA_DV), st_spec],
        out_shape=[jax.ShapeDtypeStruct((n_seq * steps, GLA_DV), F32),
                   jax.ShapeDtypeStruct(state.shape, F32)],
        compiler_params=pltpu.CompilerParams(
            dimension_semantics=("arbitrary",), vmem_limit_bytes=VMEM_LIMIT),
        name="gla_step_sample",
    )(proj, log_a, state)


def _gla_out_kernel(o_ref, p_ref, x_ref, gt_ref, e_ref, og_ref, wout_ref, xo_ref):
    xo_ref[...] = _gla_output(o_ref[...], p_ref[...], x_ref[...], _rows_to_tokens(e_ref, gt_ref),
                              og_ref[...], wout_ref[...])


def _gla_out_call(o, proj, x, mod, layer, tm, seq_len, w):
    n, d = x.shape
    gate_spec = _mod_specs(mod, layer, 0, tm, seq_len)[2]
    g_block = (2 * GLA_DK + GLA_DV) // GLA_DV
    in_specs = [_row_spec(tm, GLA_DV),
                pl.BlockSpec((tm, GLA_DV), lambda i: (i, g_block)),
                _row_spec(tm, d), gate_spec,
                _const_spec((tm, 2 * tm // seq_len)),
                _const_spec((1, GLA_DV_HEAD)),
                _const_spec((GLA_DV, d))]
    return pl.pallas_call(
        _gla_out_kernel,
        grid=(n // tm,),
        in_specs=in_specs,
        out_specs=_row_spec(tm, d),
        out_shape=jax.ShapeDtypeStruct((n, d), F32),
        compiler_params=pltpu.CompilerParams(
            dimension_semantics=("arbitrary",), vmem_limit_bytes=VMEM_LIMIT),
        name="gla_out_sample",
    )(o, proj, x, mod, _token_copy_matrix(tm, seq_len), w["out_g"], w["w_out"])


PROMPT_FFN_TILE = 1024
PROMPT_GMLP_TILE = 1024
PROMPT_GLA_TILE = 1024
SAMPLE_TILE = 256
SAMPLE_SEQS_PER_STEP = 16


def kernel(x_prompt, x_sample, c_prompt, c_sample, state_gla, ada_w, ada_b, norm_mix_g, norm_ffn_g,
           ffn_w1, ffn_b1, ffn_w2, ffn_b2, gmlp_w_in, gmlp_b_in, gmlp_ln_g, gmlp_ln_b, gmlp_w_s,
           gmlp_b_s, gmlp_w_out, gmlp_b_out, gla_w_in, gla_w_gate2, gla_b_gate, gla_norm_g,
           gla_w_out, final_norm_g):
    n_seq_p, seq_p, d = x_prompt.shape
    n_seq_s, seq_s, _ = x_sample.shape
    assert d == D_MODEL and GMLP_SUBTILE % CHUNK_A == 0 and GLA_SUBTILE % GLA_FAST_CHUNK == 0
    assert all(seq_p % t == 0 for t in (PROMPT_FFN_TILE, PROMPT_GMLP_TILE, PROMPT_GLA_TILE))
    assert seq_s < SUBLANES and SUBLANES % seq_s == 0 and (n_seq_s * seq_s) % SAMPLE_TILE == 0
    row = lambda a: a.reshape(1, -1)

    mod_p, mod_s = _ada_call(c_prompt, c_sample, ada_w, ada_b)

    causal = np.tril(np.ones((CHUNK_A, CHUNK_A), dtype=bool))
    ws = jnp.where(causal[None], gmlp_w_s, jnp.zeros_like(gmlp_w_s))
    rows = np.arange(CHUNK_A)
    pick = (rows[:, None] % seq_s == np.arange(seq_s)[None, :]).astype(np.float32)
    same_seq = (rows[:, None] // seq_s) == (rows[None, :] // seq_s)
    exact = lax.Precision.HIGHEST
    ws_s = jnp.where(same_seq[None], jnp.einsum("ri,gij,cj->grc", pick, ws[:, :seq_s, :seq_s], pick,
                                                precision=exact), 0.0)
    bias_p = jnp.repeat(gmlp_b_s.T, GMLP_GROUP_W, axis=1)
    bias_s = jnp.dot(pick, bias_p[:seq_s], precision=exact)

    gmlp_w = dict(norm_g=row(norm_mix_g[0]), w_in=gmlp_w_in, b_in=row(gmlp_b_in),
                  ln_g=row(gmlp_ln_g), ln_b=row(gmlp_ln_b), w_out=gmlp_w_out,
                  b_out=row(gmlp_b_out))
    gmlp_w = dict(gmlp_w, mix_p=ws.astype(BF16), bias_p=bias_p, mix_s=ws_s.astype(BF16),
                  bias_s=bias_s)

    depth = ffn_w1.shape[0]
    ffn_w = dict(norm_g=norm_ffn_g.reshape(depth, 1, d), w1=ffn_w1,
                 b1=ffn_b1.reshape(depth, 1, D_FF), w2=ffn_w2,
                 b2=ffn_b2.reshape(depth, 1, d), final_g=row(final_norm_g))

    w_in_pad = jnp.pad(gla_w_in, ((0, 0), (0, GLA_IN_PAD - gla_w_in.shape[1]))).astype(BF16)
    wg2_pad = jnp.pad(gla_w_gate2, ((0, GLA_GATE_PAD - GLA_GATE_RANK), (0, 0))).astype(BF16)
    blk_ones = np.kron(np.eye(2, dtype=np.float32),
                       np.ones((GLA_DK_HEAD, GLA_DK_HEAD), np.float32))
    gla_w = dict(norm_g=row(norm_mix_g[1]), w_in=w_in_pad, w_gate2=wg2_pad, b_gate=row(gla_b_gate),
                 out_g=row(gla_norm_g), w_out=gla_w_out.astype(BF16),
                 tril=jnp.asarray(np.tril(np.ones((GLA_TRIL, GLA_TRIL), np.float32)), BF16),
                 ones_bd=jnp.asarray(blk_ones, BF16))

    xp = x_prompt.reshape(n_seq_p * seq_p, d)
    grid_for = lambda tm_p: _TwoGroupGrid(xp.shape[0], n_seq_s * seq_s, tm_p, SAMPLE_TILE, seq_p,
                                          seq_s)
    xp, xs, chunk_v = _gmlp_call(xp, x_sample, mod_p, mod_s, grid_for(PROMPT_GMLP_TILE), gmlp_w)
    xp, xs = _ffn_call(xp, xs, mod_p, mod_s, 0, grid_for(PROMPT_FFN_TILE), ffn_w, final_norm=False)
    xp, state_p = _gla_prompt_call(xp, mod_p, 1, n_seq_p, seq_p, PROMPT_GLA_TILE, gla_w)
    proj, log_a = _gla_proj_call(xs, mod_s, 1, SAMPLE_TILE, seq_s, gla_w)
    o, state_s = _gla_step_call(proj, log_a, state_gla, seq_s, SAMPLE_SEQS_PER_STEP)
    xs = _gla_out_call(o, proj, xs, mod_s, 1, SAMPLE_TILE, seq_s, gla_w)
    xp, xs = _ffn_call(xp, xs, mod_p, mod_s, 1, grid_for(PROMPT_FFN_TILE), ffn_w, final_norm=True)

    return (xp.reshape(x_prompt.shape), xs, state_p, state_s, chunk_v)
```

```python
import functools

import jax
import jax.numpy as jnp
import numpy as np
from jax import lax
from jax.experimental import pallas as pl
from jax.experimental.pallas import tpu as pltpu

F32 = jnp.float32
BF16 = jnp.bfloat16

D_MODEL = 1024
N_MOD = 6
CHUNK_A = 128
GMLP_WIDTH = D_MODEL
GMLP_GROUPS = 4
GMLP_GROUP_W = GMLP_WIDTH // GMLP_GROUPS
GLA_HEADS = 4
GLA_DK = D_MODEL // 2
GLA_DV = D_MODEL
GLA_DK_HEAD = GLA_DK // GLA_HEADS
GLA_DV_HEAD = GLA_DV // GLA_HEADS
GLA_GATE_RANK = 16
GLA_TAU = 16.0
GLA_CHUNK = 64
GLA_FAST_CHUNK = 128
GLA_TRIL = max(GLA_CHUNK, GLA_FAST_CHUNK)
GLA_SAFE_EXPONENT = 80.0
GLA_QKVG = 2 * GLA_DK + 2 * GLA_DV
GLA_GATE_PAD = 128
GLA_IN_PAD = GLA_QKVG + GLA_GATE_PAD
D_FF = 4 * D_MODEL
EPS = 1e-6

LANES = 128
SUBLANES = 8
BF16_SUBLANES = 16
ADA_ROWS = 256
FFN_CHUNK = 1024
FFN_SUBTILE = 512
GMLP_SUBTILE = 512
GLA_SUBTILE = 512
VMEM_LIMIT = 56 * 1024 * 1024
WEIGHT_LOAD_CHUNKS = 32
WEIGHT_STAGE_SLOTS = 4


def _rms(x, g):
    return x * lax.rsqrt(jnp.mean(x * x, axis=-1, keepdims=True) + EPS) * g


def _modulated_norm(x, g, scale, shift):
    inv = lax.rsqrt(jnp.mean(x * x, axis=-1, keepdims=True) + EPS)
    return (x * inv) * (g * (1.0 + scale)) + shift


def _dot(a, b):
    return jnp.dot(a.astype(BF16), b.astype(BF16), preferred_element_type=F32)


def _dot_nt(a, b):
    return lax.dot_general(a.astype(BF16), b.astype(BF16), (((1,), (1,)), ((), ())),
                           preferred_element_type=F32)


def _dot_tn(a, b):
    return lax.dot_general(a.astype(BF16), b.astype(BF16), (((0,), (0,)), ((), ())),
                           preferred_element_type=F32)


def _gelu_tanh(x):
    c1 = -2.0 * 0.7978845608028654 * 1.4426950408889634
    c2 = c1 * 0.044715
    return x / (1.0 + jnp.exp2(x * (c1 + c2 * (x * x))))


def _silu(x):
    return x * jax.nn.sigmoid(x)


def _log_sigmoid(x):
    return -(jnp.maximum(-x, 0.0) + jnp.log1p(jnp.exp(-jnp.abs(x))))


def _bf16_terms(x, n):
    terms = []
    for _ in range(n - 1):
        t = x.astype(BF16)
        terms.append(t)
        x = x - t.astype(F32)
    terms.append(x.astype(BF16))
    return terms


def _ada_kernel(c_ref, wa_ref, wb_ref, b_ref, op_ref, os_ref, s_scr, acc_scr, *, n_prompt, tk):
    k = pl.program_id(1)

    @pl.when((pl.program_id(0) == 0) & (k == 0))
    def _():
        for kk in range(s_scr.shape[0]):
            s_scr[kk] = _silu(c_ref[:, kk * tk:(kk + 1) * tk]).astype(BF16)

    @pl.when(k == 0)
    def _():
        acc_scr[...] = jnp.broadcast_to(b_ref[...], acc_scr.shape)

    s = s_scr[k]
    half = N_MOD // 2
    for j in range(N_MOD):
        cols = slice(j * D_MODEL, (j + 1) * D_MODEL)
        w_ref, jj = (wa_ref, j) if j < half else (wb_ref, j - half)
        w = w_ref[:, jj * D_MODEL:(jj + 1) * D_MODEL].astype(BF16)
        acc_scr[:, cols] += jnp.dot(s, w, preferred_element_type=F32)

    @pl.when(k == pl.num_programs(1) - 1)
    def _():
        op_ref[...] = acc_scr[0:n_prompt, :]
        os_ref[...] = acc_scr[n_prompt:n_prompt + os_ref.shape[0], :]


def _ada_call(c_prompt, c_sample, ada_w, ada_b):
    depth = ada_w.shape[0]
    n_prompt, n_seq = c_prompt.shape[0], c_sample.shape[0]
    d = D_MODEL
    pad = -(n_prompt + n_seq) % BF16_SUBLANES
    c_all = jnp.concatenate([c_prompt, c_sample, jnp.zeros((pad, d), F32)], axis=0)
    rows = c_all.shape[0]
    tk = ADA_ROWS
    width = N_MOD * d
    return pl.pallas_call(
        functools.partial(_ada_kernel, n_prompt=n_prompt, tk=tk),
        grid=(depth, d // tk),
        in_specs=[
            pl.BlockSpec((rows, d), lambda l, k: (0, 0)),
            pl.BlockSpec((None, tk, width // 2), lambda l, k: (l, k, 0)),
            pl.BlockSpec((None, tk, width // 2), lambda l, k: (l, k, 1)),
            pl.BlockSpec((None, 1, width), lambda l, k: (l, 0, 0)),
        ],
        out_specs=[
            pl.BlockSpec((None, n_prompt, width), lambda l, k: (l, 0, 0)),
            pl.BlockSpec((None, n_seq, width), lambda l, k: (l, 0, 0)),
        ],
        out_shape=[
            jax.ShapeDtypeStruct((depth, n_prompt, width), F32),
            jax.ShapeDtypeStruct((depth, n_seq, width), F32),
        ],
        scratch_shapes=[pltpu.VMEM((d // tk, rows, tk), BF16),
                        pltpu.VMEM((rows, width), F32)],
        compiler_params=pltpu.CompilerParams(
            dimension_semantics=("arbitrary", "arbitrary"), vmem_limit_bytes=VMEM_LIMIT),
        name="adaln_mod",
    )(c_all, ada_w, ada_w, ada_b.reshape(depth, 1, N_MOD * d))


def _const_spec(shape):
    zeros = (0,) * len(shape)
    return pl.BlockSpec(shape, lambda i: zeros, pipeline_mode=pl.Buffered(1))


def _layer_spec(shape, layer):
    zeros = (0,) * len(shape)
    return pl.BlockSpec((None,) + shape, lambda i: (layer,) + zeros,
                        pipeline_mode=pl.Buffered(1))


def _mod_specs(mod, layer, first_chunk, tm, rows_per_seq):
    d = D_MODEL
    specs = []
    for j in range(first_chunk, first_chunk + 3):
        if rows_per_seq < tm:
            specs.append(pl.BlockSpec((None, tm // rows_per_seq, d), lambda i, j=j: (layer, i, j)))
        else:
            specs.append(pl.BlockSpec((None, mod.shape[1], d), lambda i, j=j: (layer, 0, j)))
    return specs


def _token_copy_matrix(tm, rows_per_seq):
    copy = np.repeat(np.eye(tm // rows_per_seq, dtype=np.float32), rows_per_seq, axis=0)
    return jnp.asarray(np.tile(copy, (1, 2)), BF16)


def _rows_to_tokens(e_ref, mod_ref):
    terms = jnp.concatenate(_bf16_terms(mod_ref[...], 2), axis=0)
    return jnp.dot(e_ref[...], terms, preferred_element_type=F32)


def _mod_rows(ref, rows):
    return ref[...] if ref.shape[0] == 1 else ref[rows, :]


def _row_spec(tm, width):
    return pl.BlockSpec((tm, width), lambda i: (i, 0))


class _TwoGroupGrid:
    def __init__(self, n_prompt_rows, n_sample_rows, tm_p, tm_s, seq_len, sample_seq_len):
        self.tm_p, self.tm_s = tm_p, tm_s
        self.sample_seq_len = sample_seq_len
        self.p_steps = n_prompt_rows // tm_p
        self.s_steps = n_sample_rows // tm_s
        self.tiles_per_seq = seq_len // tm_p
        self.n_prompt_seqs = n_prompt_rows // seq_len
        self.steps = self.p_steps + self.s_steps

    def p_idx(self, i):
        return jnp.minimum(i, self.p_steps - 1)

    def s_idx(self, i):
        return jnp.maximum(i - self.p_steps, 0)

    def rows_p(self, width):
        return pl.BlockSpec((self.tm_p, width), lambda i: (self.p_idx(i), 0))

    def rows_s(self, width, col_block=0):
        return pl.BlockSpec((self.tm_s, width), lambda i: (self.s_idx(i), col_block))

    def mods_p(self, layer, first_chunk):
        return [pl.BlockSpec((None, self.n_prompt_seqs, D_MODEL), lambda i, j=j: (layer, 0, j))
                for j in range(first_chunk, first_chunk + 3)]

    def prompt_row(self, ref, step):
        return ref[pl.ds(step // self.tiles_per_seq, 1), :]

    def mods_s(self, layer, first_chunk):
        return [pl.BlockSpec((None, self.tm_s // self.sample_seq_len, D_MODEL),
                             lambda i, j=j: (layer, self.s_idx(i), j))
                for j in range(first_chunk, first_chunk + 3)]

    def seqs_s(self, width):
        seq = self.sample_seq_len
        return pl.BlockSpec((self.tm_s // seq, seq, width), lambda i: (self.s_idx(i), 0, 0))

    def token_copy_spec(self):
        return _const_spec((self.tm_s, 2 * self.tm_s // self.sample_seq_len))

    def token_copy_matrix(self):
        return _token_copy_matrix(self.tm_s, self.sample_seq_len)


def _gmlp_rows(x_ref, sh_ref, sc_ref, gt_ref, mix_ref, bs_ref, xo_ref, v_ref, z_scr, w, tm, sub):
    ng_ref, win_ref, bin_ref, lng_ref, lnb_ref, wout_ref, bout_ref = w
    sub = min(sub, tm)
    groups = [slice(r0, r0 + sub) for r0 in range(0, tm, sub)]

    def project(gi):
        rows = groups[gi]
        h = _modulated_norm(x_ref[rows, :], ng_ref[...], _mod_rows(sc_ref, rows),
                            _mod_rows(sh_ref, rows))
        z_scr[gi % 2, 0:sub, :] = _dot(h, win_ref[...])

    project(0)
    for gi, rows in enumerate(groups):
        if gi + 1 < len(groups):
            project(gi + 1)
        z = _gelu_tanh(z_scr[gi % 2, 0:sub, :] + bin_ref[...])
        u = z[:, :GMLP_WIDTH]
        v = z[:, GMLP_WIDTH:]
        mu = jnp.mean(v, axis=-1, keepdims=True)
        vc = v - mu
        var = jnp.mean(vc * vc, axis=-1, keepdims=True)
        v = vc * lax.rsqrt(var + EPS) * lng_ref[...] + lnb_ref[...]
        if v_ref is not None:
            v_ref[rows, :] = v
        vb = v.astype(BF16)
        mixed = []
        for c in range(sub // CHUNK_A):
            cols = []
            for g in range(GMLP_GROUPS):
                blk = vb[c * CHUNK_A:(c + 1) * CHUNK_A, g * GMLP_GROUP_W:(g + 1) * GMLP_GROUP_W]
                cols.append(jnp.dot(mix_ref[g], blk, preferred_element_type=F32))
            mixed.append(jnp.concatenate(cols, axis=1) + bs_ref[...])
        s = jnp.concatenate(mixed, axis=0)
        y = _dot(u * s, wout_ref[...]) + bout_ref[...]
        xo_ref[rows, :] = x_ref[rows, :] + _mod_rows(gt_ref, rows) * y


def _rows_to_seq_layout(rows_ref, seq_ref, slab_scr):
    n_seq, seq, width = seq_ref.shape
    for j in range(width // LANES):
        lanes = slice(j * LANES, (j + 1) * LANES)
        slab_scr[...] = rows_ref[:, lanes]
        for t in range(seq):
            seq_ref[:, t, lanes] = slab_scr[pl.ds(t, n_seq, stride=seq), :]


def _seq_layout_to_rows(seq_ref, rows_ref, slab_scr):
    n_seq, seq, width = seq_ref.shape
    for j in range(width // LANES):
        lanes = slice(j * LANES, (j + 1) * LANES)
        for t in range(seq):
            slab_scr[pl.ds(t, n_seq, stride=seq), :] = seq_ref[:, t, lanes]
        rows_ref[:, lanes] = slab_scr[...]


def _gmlp_kernel(xp_ref, shp_ref, scp_ref, gtp_ref, xs_ref, shs_ref, scs_ref, gts_ref, e_ref,
                 ng_ref, win_hbm, bin_ref, lng_ref, lnb_ref, wout_hbm, bout_ref,
                 mixp_ref, bsp_ref, mixs_ref, bss_ref, op_ref, os_ref, vs_ref, z_scr, v_scr,
                 x_scr, slab_scr,
                 win_ref, wout_ref, stage_in, stage_out, sems_in, sems_out, *, grid):
    w = (ng_ref, win_ref, bin_ref, lng_ref, lnb_ref, wout_ref, bout_ref)
    step = pl.program_id(0)

    @pl.when(step == 0)
    def _():
        _load_as_bf16([(win_hbm, win_ref, stage_in, sems_in),
                       (wout_hbm, wout_ref, stage_out, sems_out)])

    @pl.when(step < grid.p_steps)
    def _():
        sh, sc, gt = (grid.prompt_row(r, step) for r in (shp_ref, scp_ref, gtp_ref))
        _gmlp_rows(xp_ref, sh, sc, gt, mixp_ref, bsp_ref, op_ref, None, z_scr, w,
                   grid.tm_p, GMLP_SUBTILE)

    @pl.when(step >= grid.p_steps)
    def _():
        sh, sc, gt = (_rows_to_tokens(e_ref, r) for r in (shs_ref, scs_ref, gts_ref))
        _seq_layout_to_rows(xs_ref, x_scr, slab_scr)
        _gmlp_rows(x_scr, sh, sc, gt, mixs_ref, bss_ref, os_ref, v_scr, z_scr, w,
                   grid.tm_s, GMLP_SUBTILE)
        _rows_to_seq_layout(v_scr, vs_ref, slab_scr)


def _gmlp_call(xp, xs_seq, mod_p, mod_s, grid, w):
    d = D_MODEL
    n_sample = xs_seq.shape[0] * xs_seq.shape[1]
    mix_spec = _const_spec((GMLP_GROUPS, CHUNK_A, CHUNK_A))
    bias_spec = _const_spec((CHUNK_A, GMLP_WIDTH))
    in_specs = ([grid.rows_p(d)] + grid.mods_p(0, 0) + [grid.seqs_s(d)] + grid.mods_s(0, 0) + [
        grid.token_copy_spec(),
        _const_spec((1, d)),
        pl.BlockSpec(memory_space=pl.ANY),
        _const_spec((1, 2 * GMLP_WIDTH)),
        _const_spec((1, GMLP_WIDTH)),
        _const_spec((1, GMLP_WIDTH)),
        pl.BlockSpec(memory_space=pl.ANY),
        _const_spec((1, d)),
        mix_spec, bias_spec, mix_spec, bias_spec,
    ])
    return pl.pallas_call(
        functools.partial(_gmlp_kernel, grid=grid),
        grid=(grid.steps,),
        in_specs=in_specs,
        out_specs=[grid.rows_p(d), grid.rows_s(d), grid.seqs_s(GMLP_WIDTH)],
        out_shape=[jax.ShapeDtypeStruct(xp.shape, F32),
                   jax.ShapeDtypeStruct((n_sample, d), F32),
                   jax.ShapeDtypeStruct(xs_seq.shape[:2] + (GMLP_WIDTH,), F32)],
        scratch_shapes=[pltpu.VMEM((2, GMLP_SUBTILE, 2 * GMLP_WIDTH), F32),
                        pltpu.VMEM((grid.tm_s, GMLP_WIDTH), F32),
                        pltpu.VMEM((grid.tm_s, d), F32),
                        pltpu.VMEM((grid.tm_s, LANES), F32),
                        pltpu.VMEM((d, 2 * GMLP_WIDTH), BF16),
                        pltpu.VMEM((GMLP_WIDTH, d), BF16),
                        pltpu.VMEM((WEIGHT_STAGE_SLOTS, d // WEIGHT_LOAD_CHUNKS, 2 * GMLP_WIDTH),
                                   F32),
                        pltpu.VMEM((WEIGHT_STAGE_SLOTS, GMLP_WIDTH // WEIGHT_LOAD_CHUNKS, d), F32),
                        pltpu.SemaphoreType.DMA((WEIGHT_STAGE_SLOTS,)),
                        pltpu.SemaphoreType.DMA((WEIGHT_STAGE_SLOTS,))],
        compiler_params=pltpu.CompilerParams(
            dimension_semantics=("arbitrary",), vmem_limit_bytes=VMEM_LIMIT),
        name="gmlp_mixer",
    )(xp, mod_p, mod_p, mod_p, xs_seq, mod_s, mod_s, mod_s, grid.token_copy_matrix(),
      w["norm_g"], w["w_in"], w["b_in"],
      w["ln_g"], w["ln_b"], w["w_out"], w["b_out"], w["mix_p"], w["bias_p"], w["mix_s"],
      w["bias_s"])


def _ffn_rows(x_ref, sh_ref, sc_ref, gt_ref, o_ref, h_scr, w, final_norm, tm):
    ng_ref, w1_ref, b1_ref, w2_ref, b2_ref, fg_ref = w
    sub = min(FFN_SUBTILE, tm)
    groups = [slice(r0, r0 + sub) for r0 in range(0, tm, sub)]

    def normalise(gi):
        rows = groups[gi]
        h_scr[gi % 2, 0:sub, :] = _modulated_norm(
            x_ref[rows, :], ng_ref[...], _mod_rows(sc_ref, rows),
            _mod_rows(sh_ref, rows)).astype(BF16)

    normalise(0)
    for gi, rows in enumerate(groups):
        if gi + 1 < len(groups):
            normalise(gi + 1)
        h = h_scr[gi % 2, 0:sub, :]
        acc = jnp.zeros((sub, D_MODEL), F32)
        for j in range(D_FF // FFN_CHUNK):
            cols = slice(j * FFN_CHUNK, (j + 1) * FFN_CHUNK)
            a = jnp.dot(h, w1_ref[:, cols], preferred_element_type=F32) + b1_ref[:, cols]
            r = jnp.square(jnp.maximum(a, 0.0)).astype(BF16)
            acc = acc + jnp.dot(r, w2_ref[cols, :], preferred_element_type=F32)
        y = x_ref[rows, :] + _mod_rows(gt_ref, rows) * (acc + b2_ref[...])
        if final_norm:
            y = _rms(y, fg_ref[...])
        o_ref[rows, :] = y


def _load_as_bf16(jobs):
    def chunk_copy(job, c):
        src_hbm, _, stage_ref, sem_ref = job
        slot = c % stage_ref.shape[0]
        rows = pl.ds(c * stage_ref.shape[1], stage_ref.shape[1])
        return pltpu.make_async_copy(src_hbm.at[rows, :], stage_ref.at[slot], sem_ref.at[slot])

    n_chunks = {job[1].shape[0] // job[2].shape[1] for job in jobs}
    n_slots = {job[2].shape[0] for job in jobs}
    assert len(n_chunks) == 1 and len(n_slots) == 1
    n_chunks, n_slots = n_chunks.pop(), n_slots.pop()
    for c in range(min(n_slots - 1, n_chunks)):
        for job in jobs:
            chunk_copy(job, c).start()
    for c in range(n_chunks):
        for job in jobs:
            if c + n_slots - 1 < n_chunks:
                chunk_copy(job, c + n_slots - 1).start()
        for job in jobs:
            _, dst_ref, stage_ref, _ = job
            chunk_copy(job, c).wait()
            rows = stage_ref.shape[1]
            dst_ref[c * rows:(c + 1) * rows, :] = stage_ref[c % n_slots].astype(BF16)


def _ffn_kernel(xp_ref, shp_ref, scp_ref, gtp_ref, xs_ref, shs_ref, scs_ref, gts_ref, e_ref,
                ng_ref, w1_hbm, b1_ref, w2_hbm, b2_ref, fg_ref, op_ref, os_ref, h_scr,
                w1_ref, w2_ref, stage1, stage2, sems1, sems2, ys_scr, slab_scr,
                *, final_norm, grid, layer):
    w = (ng_ref, w1_ref, b1_ref, w2_ref, b2_ref, fg_ref)
    step = pl.program_id(0)

    @pl.when(step == 0)
    def _():
        _load_as_bf16([(w1_hbm.at[layer], w1_ref, stage1, sems1),
                       (w2_hbm.at[layer], w2_ref, stage2, sems2)])

    @pl.when(step < grid.p_steps)
    def _():
        sh, sc, gt = (grid.prompt_row(r, step) for r in (shp_ref, scp_ref, gtp_ref))
        _ffn_rows(xp_ref, sh, sc, gt, op_ref, h_scr, w, final_norm, grid.tm_p)

    @pl.when(step >= grid.p_steps)
    def _():
        sh, sc, gt = (_rows_to_tokens(e_ref, r) for r in (shs_ref, scs_ref, gts_ref))
        if final_norm:
            _ffn_rows(xs_ref, sh, sc, gt, ys_scr, h_scr, w, final_norm, grid.tm_s)
            _rows_to_seq_layout(ys_scr, os_ref, slab_scr)
        else:
            _ffn_rows(xs_ref, sh, sc, gt, os_ref, h_scr, w, final_norm, grid.tm_s)


def _ffn_call(xp, xs, mod_p, mod_s, layer, grid, w, final_norm):
    d = D_MODEL
    seq_shape = (xs.shape[0] // grid.sample_seq_len, grid.sample_seq_len, d)
    in_specs = ([grid.rows_p(d)] + grid.mods_p(layer, 3) + [grid.rows_s(d)]
                + grid.mods_s(layer, 3) + [
        grid.token_copy_spec(),
        _layer_spec((1, d), layer),
        pl.BlockSpec(memory_space=pl.ANY),
        _layer_spec((1, D_FF), layer),
        pl.BlockSpec(memory_space=pl.ANY),
        _layer_spec((1, d), layer),
        _const_spec((1, d)),
    ])
    return pl.pallas_call(
        functools.partial(_ffn_kernel, final_norm=final_norm, grid=grid, layer=layer),
        grid=(grid.steps,),
        in_specs=in_specs,
        out_specs=[grid.rows_p(d), grid.seqs_s(d) if final_norm else grid.rows_s(d)],
        out_shape=[jax.ShapeDtypeStruct(xp.shape, F32),
                   jax.ShapeDtypeStruct(seq_shape if final_norm else xs.shape, F32)],
        scratch_shapes=[pltpu.VMEM((2, FFN_SUBTILE, d), BF16),
                        pltpu.VMEM((d, D_FF), BF16),
                        pltpu.VMEM((D_FF, d), BF16),
                        pltpu.VMEM((WEIGHT_STAGE_SLOTS, d // WEIGHT_LOAD_CHUNKS, D_FF), F32),
                        pltpu.VMEM((WEIGHT_STAGE_SLOTS, D_FF // WEIGHT_LOAD_CHUNKS, d), F32),
                        pltpu.SemaphoreType.DMA((WEIGHT_STAGE_SLOTS,)),
                        pltpu.SemaphoreType.DMA((WEIGHT_STAGE_SLOTS,)),
                        pltpu.VMEM((grid.tm_s, d), F32),
                        pltpu.VMEM((grid.tm_s, LANES), F32)],
        compiler_params=pltpu.CompilerParams(
            dimension_semantics=("arbitrary",), vmem_limit_bytes=VMEM_LIMIT),
        name="sqrelu_mlp",
    )(xp, mod_p, mod_p, mod_p, xs, mod_s, mod_s, mod_s, grid.token_copy_matrix(),
      w["norm_g"], w["w1"], w["b1"], w["w2"],
      w["b2"], w["final_g"])


def _gla_project(x, ng, scale, shift, win, wg2, bg):
    h = _modulated_norm(x, ng, scale, shift)
    proj = _dot(h, win)
    gate = _dot(proj[:, GLA_QKVG:], wg2) + bg
    log_a = _log_sigmoid(gate) * (1.0 / GLA_TAU)
    return proj[:, :GLA_QKVG], log_a


def _gla_output(o, gate_in, x, gt, og, wout):
    parts = []
    for h in range(GLA_HEADS):
        cols = slice(h * GLA_DV_HEAD, (h + 1) * GLA_DV_HEAD)
        parts.append(_rms(o[:, cols], og))
    on = jnp.concatenate(parts, axis=1)
    y = _dot(on * _silu(gate_in), wout)
    return x + gt * y


def _as_column(row):
    return jnp.broadcast_to(row, (SUBLANES, row.shape[1])).T[:, 0:1]


def _chunk_cumsum(tril, log_a):
    head = log_a.astype(BF16)
    rest = (log_a - head.astype(F32)).astype(BF16)
    return (jnp.dot(tril, head, preferred_element_type=F32)
            + jnp.dot(tril, rest, preferred_element_type=F32))


def _level_masks(c):
    rr = lax.broadcasted_iota(jnp.int32, (c, c), 0)
    cc = lax.broadcasted_iota(jnp.int32, (c, c), 1)
    masks = []
    s = c
    while s > SUBLANES:
        half = s // 2
        same = (rr ^ cc) < s
        masks.append(same & ((rr & half) != 0) & ((cc & half) == 0))
        s = half
    return masks, rr, cc


def _block_rows(x, s, r):
    c, w = x.shape
    return jnp.concatenate(
        [jnp.broadcast_to(x[p * s + r:p * s + r + 1, :], (s, w)) for p in range(c // s)], axis=0)


def _diag_terms(q, k, a):
    c, w = q.shape
    ii = lax.broadcasted_iota(jnp.int32, (c, w), 0) & (SUBLANES - 1)
    decay = jnp.zeros((c, w), F32)
    terms = [None] * SUBLANES
    for j in range(SUBLANES - 1, -1, -1):
        if j < SUBLANES - 1:
            decay = decay * _block_rows(a, SUBLANES, j + 1)
        decay = jnp.where(ii == j, 1.0, decay)
        terms[j] = q * _block_rows(k, SUBLANES, j) * decay
    return jnp.concatenate(terms, axis=0)


def _gla_chunk_pair(q2, k2, v2, la2, b2, st_refs, ones_bd, masks, rr, cc):
    c = q2[0].shape[0]
    zs = []
    for q, k, la in zip(q2, k2, la2):
        zs.append(_diag_terms(q, k, jnp.exp(la)))
    sums = jnp.dot(jnp.concatenate(zs, axis=1).astype(BF16), ones_bd,
                   preferred_element_type=F32)
    outs = []
    for hh, (q, k, v, b, st_ref) in enumerate(zip(q2, k2, v2, b2, st_refs)):
        dk = q.shape[1]
        attn = jnp.zeros((c, c), F32)
        base = rr & ~(SUBLANES - 1)
        for j in range(SUBLANES):
            rj = sums[j * c:(j + 1) * c, hh * dk:hh * dk + c]
            attn = jnp.where(cc == base + j, rj, attn)
        s = c
        for mask in masks:
            half = s // 2
            e = jnp.exp(-jnp.abs(b - _block_rows(b, s, half)))
            attn = jnp.where(mask, _dot_nt(q * e, k * e), attn)
            s = half
        b_last = b[c - 1:c, :]
        st = st_ref[...]
        o = _dot(q * jnp.exp(b), st) + _dot(attn, v)
        kd = k * jnp.exp(b_last - b)
        st_ref[...] = _as_column(jnp.exp(b_last)) * st + _dot_tn(kd, v)
        outs.append(o)
    return outs


def _gla_prompt_kernel(x_ref, sh_ref, sc_ref, gt_ref, ng_ref, win_ref, wg2_ref, bg_ref,
                       og_ref, wout_ref, tril_ref, ones_ref, xo_ref, s_ref,
                       proj_scr, ga_scr, la_scr, b_scr, o_scr, st_scr, *, tm, sub, tiles_per_seq):
    tile = pl.program_id(0) % tiles_per_seq
    seq_row = pl.ds(pl.program_id(0) // tiles_per_seq, 1)
    shift, scale, gate_mod = (r[seq_row, :] for r in (sh_ref, sc_ref, gt_ref))

    @pl.when(tile == 0)
    def _():
        st_scr[...] = jnp.zeros(st_scr.shape, F32)

    dk, dv = GLA_DK_HEAD, GLA_DV_HEAD
    qscale = GLA_DK_HEAD ** -0.5
    cf = GLA_FAST_CHUNK
    c = GLA_CHUNK
    groups = [slice(r0, r0 + sub) for r0 in range(0, tm, sub)]

    def project(rows):
        h = _modulated_norm(x_ref[rows, :], ng_ref[...], scale, shift)
        proj = _dot(h, win_ref[...])
        proj_scr[rows, :] = proj[:, :GLA_QKVG]
        ga_scr[rows, :] = proj[:, GLA_QKVG:]

    def decays(rows):
        gate = _dot(ga_scr[rows, :], wg2_ref[...]) + bg_ref[...]
        log_a = _log_sigmoid(gate) * (1.0 / GLA_TAU)
        la_scr[rows, :] = log_a
        for ci in range(sub // cf):
            b_scr[rows.start + ci * cf:rows.start + (ci + 1) * cf, :] = _chunk_cumsum(
                tril_ref[0:cf, 0:cf], log_a[ci * cf:(ci + 1) * cf, :])
        return jnp.min(b_scr[rows, :]) >= -GLA_SAFE_EXPONENT

    def head_slices(rows, h):
        q = proj_scr[rows, h * dk:(h + 1) * dk] * qscale
        k = proj_scr[rows, GLA_DK + h * dk:GLA_DK + (h + 1) * dk]
        v = proj_scr[rows, 2 * GLA_DK + h * dv:2 * GLA_DK + (h + 1) * dv]
        return q, k, v

    def single_ref_rows(group):
        rr = lax.broadcasted_iota(jnp.int32, (cf, cf), 0)
        cc = lax.broadcasted_iota(jnp.int32, (cf, cf), 1)
        causal = rr >= cc
        chunks = [slice(r0, r0 + cf) for r0 in range(group.start, group.stop, cf)]
        for h in range(GLA_HEADS):
            qes, vbs, scores, kvs, decays = [], [], [], [], []
            for rows in chunks:
                q, k, v = head_slices(rows, h)
                b = b_scr[rows, h * dk:(h + 1) * dk]
                decay_last = jnp.exp(b[cf - 1:cf, :])
                qe = (q * jnp.exp(b)).astype(BF16)
                kt = k * jnp.exp(-b)
                vb = v.astype(BF16)
                scores.append(_dot_nt(qe, kt))
                kvs.append(_dot_tn(kt * decay_last, vb))
                qes.append(qe)
                vbs.append(vb)
                decays.append(_as_column(decay_last))
            st = st_scr[h]
            states = []
            for kv, decay_col in zip(kvs, decays):
                states.append(st)
                st = decay_col * st + kv
            st_scr[h] = st
            for rows, qe, vb, sc, st_in in zip(chunks, qes, vbs, scores, states):
                attn = jnp.where(causal, sc, 0.0).astype(BF16)
                lhs = jnp.concatenate([qe, attn], axis=1)
                rhs = jnp.concatenate([st_in.astype(BF16), vb], axis=0)
                o_scr[rows, h * dv:(h + 1) * dv] = jnp.dot(lhs, rhs, preferred_element_type=F32)

    def robust_rows(group):
        def robust_body(ci, carry):
            rows = pl.ds(pl.multiple_of(group.start + ci * c, c), c)
            masks, rr, cc = _level_masks(c)
            b_c = _chunk_cumsum(tril_ref[0:c, 0:c], la_scr[rows, :])
            outs = []
            for h0 in range(0, GLA_HEADS, 2):
                heads = (h0, h0 + 1)
                qkv = [head_slices(rows, h) for h in heads]
                la2 = [la_scr[rows, h * dk:(h + 1) * dk] for h in heads]
                b2 = [b_c[:, h * dk:(h + 1) * dk] for h in heads]
                st_refs = [st_scr.at[h] for h in heads]
                outs += _gla_chunk_pair([t[0] for t in qkv], [t[1] for t in qkv],
                                        [t[2] for t in qkv], la2, b2, st_refs, ones_ref[...],
                                        masks, rr, cc)
            o_scr[rows, :] = jnp.concatenate(outs, axis=1)
            return carry

        lax.fori_loop(0, sub // c, robust_body, 0)

    def output(rows):
        gate_in = proj_scr[rows, 2 * GLA_DK + GLA_DV:]
        xo_ref[rows, :] = _gla_output(o_scr[rows, :], gate_in, x_ref[rows, :], gate_mod,
                                      og_ref[...], wout_ref[...])

    project(groups[0])
    for gi, rows in enumerate(groups):
        single_ref_safe = decays(rows)
        following = groups[gi + 1] if gi + 1 < len(groups) else None

        @pl.when(single_ref_safe)
        def _():
            if following is not None:
                project(following)
            single_ref_rows(rows)
            output(rows)

        @pl.when(jnp.logical_not(single_ref_safe))
        def _():
            if following is not None:
                project(following)
            robust_rows(rows)
            output(rows)

    @pl.when(tile == tiles_per_seq - 1)
    def _():
        s_ref[...] = st_scr[...]


def _gla_prompt_call(x, mod, layer, n_seq, seq_len, tm, w):
    n, d = x.shape
    tiles_per_seq = seq_len // tm
    dk, dv = GLA_DK_HEAD, GLA_DV_HEAD
    in_specs = [_row_spec(tm, d)] + _mod_specs(mod, layer, 0, tm, seq_len) + [
        _const_spec((1, d)),
        _const_spec((d, GLA_IN_PAD)),
        _const_spec((GLA_GATE_PAD, GLA_DK)),
        _const_spec((1, GLA_DK)),
        _const_spec((1, dv)),
        _const_spec((GLA_DV, d)),
        _const_spec((GLA_TRIL, GLA_TRIL)),
        _const_spec((2 * dk, 2 * dk)),
    ]
    out_specs = [
        _row_spec(tm, d),
        pl.BlockSpec((None, GLA_HEADS, dk, dv), lambda i: (i // tiles_per_seq, 0, 0, 0)),
    ]
    out_shape = [
        jax.ShapeDtypeStruct((n, d), F32),
        jax.ShapeDtypeStruct((n_seq, GLA_HEADS, dk, dv), F32),
    ]
    return pl.pallas_call(
        functools.partial(_gla_prompt_kernel, tm=tm, sub=GLA_SUBTILE,
                          tiles_per_seq=tiles_per_seq),
        grid=(n // tm,),
        in_specs=in_specs,
        out_specs=out_specs,
        out_shape=out_shape,
        scratch_shapes=[
            pltpu.VMEM((tm, GLA_QKVG), F32),
            pltpu.VMEM((tm, GLA_GATE_PAD), F32),
            pltpu.VMEM((tm, GLA_DK), F32),
            pltpu.VMEM((tm, GLA_DK), F32),
            pltpu.VMEM((tm, GLA_DV), F32),
            pltpu.VMEM((GLA_HEADS, dk, dv), F32),
        ],
        compiler_params=pltpu.CompilerParams(
            dimension_semantics=("arbitrary",), vmem_limit_bytes=VMEM_LIMIT),
        name="gla_mixer_prompt",
    )(x, mod, mod, mod, w["norm_g"], w["w_in"], w["w_gate2"], w["b_gate"], w["out_g"],
      w["w_out"], w["tril"], w["ones_bd"])


def _gla_proj_kernel(x_ref, sh_ref, sc_ref, e_ref, ng_ref, win_ref, wg2_ref, bg_ref, p_ref,
                     la_ref):
    proj, log_a = _gla_project(x_ref[...], ng_ref[...], _rows_to_tokens(e_ref, sc_ref),
                               _rows_to_tokens(e_ref, sh_ref), win_ref[...], wg2_ref[...],
                               bg_ref[...])
    p_ref[...] = proj
    la_ref[...] = log_a


def _gla_proj_call(x, mod, layer, tm, seq_len, w):
    n, d = x.shape
    specs = _mod_specs(mod, layer, 0, tm, seq_len)
    in_specs = [_row_spec(tm, d), specs[0], specs[1],
                _const_spec((tm, 2 * tm // seq_len)),
                _const_spec((1, d)),
                _const_spec((d, GLA_IN_PAD)),
                _const_spec((GLA_GATE_PAD, GLA_DK)),
                _const_spec((1, GLA_DK))]
    return pl.pallas_call(
        _gla_proj_kernel,
        grid=(n // tm,),
        in_specs=in_specs,
        out_specs=[_row_spec(tm, GLA_QKVG), _row_spec(tm, GLA_DK)],
        out_shape=[jax.ShapeDtypeStruct((n, GLA_QKVG), F32),
                   jax.ShapeDtypeStruct((n, GLA_DK), F32)],
        compiler_params=pltpu.CompilerParams(
            dimension_semantics=("arbitrary",), vmem_limit_bytes=VMEM_LIMIT),
        name="gla_proj_sample",
    )(x, mod, mod, _token_copy_matrix(tm, seq_len), w["norm_g"], w["w_in"], w["w_gate2"],
      w["b_gate"])


def _gla_step_kernel(p_ref, la_ref, s0_ref, o_ref, s1_ref, *, seqs, steps):
    dk, dv = GLA_DK_HEAD, GLA_DV_HEAD
    qscale = GLA_DK_HEAD ** -0.5
    per_tile = SUBLANES // steps
    row = lax.broadcasted_iota(jnp.int32, (SUBLANES, dk), 0)

    def seq_rows(tile, s):
        if s:
            tile = pltpu.roll(tile, SUBLANES - s * steps, 0)
        keep = lax.broadcasted_iota(jnp.int32, tile.shape, 0) < steps
        return jnp.where(keep, tile, 0.0)

    def tile_body(ti, carry):
        rows = pl.ds(pl.multiple_of(ti * SUBLANES, SUBLANES), SUBLANES)
        p_tile = p_ref[rows, :]
        la_tile = la_ref[rows, :]
        o_tile = jnp.zeros((SUBLANES, GLA_DV), F32)
        for s in range(per_tile):
            bi = ti * per_tile + s
            p = seq_rows(p_tile, s)
            la_all = seq_rows(la_tile, s)
            outs = []
            for h in range(GLA_HEADS):
                q = p[:, h * dk:(h + 1) * dk] * qscale
                k = p[:, GLA_DK + h * dk:GLA_DK + (h + 1) * dk]
                v = p[:, 2 * GLA_DK + h * dv:2 * GLA_DK + (h + 1) * dv]
                la = la_all[:, h * dk:(h + 1) * dk]
                b = jnp.zeros((SUBLANES, dk), F32)
                for t in range(steps):
                    b = b + jnp.where(row >= t, la[t:t + 1, :], 0.0)
                b_last = b[steps - 1:steps, :]
                s0 = s0_ref[bi, h]
                o = _dot(q * jnp.exp(b), s0)
                for j in range(steps):
                    e = jnp.exp(jnp.minimum(b - b[j:j + 1, :], 0.0))
                    z = jnp.where(row >= j, q * k[j:j + 1, :] * e, 0.0)
                    o = o + jnp.sum(z, axis=-1, keepdims=True) * v[j:j + 1, :]
                outs.append(o)
                m = jnp.where(row == steps, jnp.exp(b_last), k * jnp.exp(b_last - b))
                m_t = m.T
                s1_ref[bi, h] = m_t[:, steps:steps + 1] * s0 + _dot(m_t, v)
            o_seq = jnp.concatenate(outs, axis=1)
            o_tile = o_tile + (pltpu.roll(o_seq, s * steps, 0) if s else o_seq)
        o_ref[rows, :] = o_tile
        return carry

    lax.fori_loop(0, seqs // per_tile, tile_body, 0)


def _gla_step_call(proj, log_a, state, steps, seqs):
    n_seq = state.shape[0]
    dk, dv = GLA_DK_HEAD, GLA_DV_HEAD
    rows = lambda width: pl.BlockSpec((seqs * steps, width), lambda i: (i, 0))
    st_spec = pl.BlockSpec((seqs, GLA_HEADS, dk, dv), lambda i: (i, 0, 0, 0))
    return pl.pallas_call(
        functools.partial(_gla_step_kernel, seqs=seqs, steps=steps),
        grid=(n_seq // seqs,),
        in_specs=[rows(GLA_QKVG), rows(GLA_DK), st_spec],
        out_specs=[rows(GLA_DV), st_spec],
        out_shape=[jax.ShapeDtypeStruct((n_seq * steps, GLA_DV), F32),
                   jax.ShapeDtypeStruct(state.shape, F32)],
        compiler_params=pltpu.CompilerParams(
            dimension_semantics=("arbitrary",), vmem_limit_bytes=VMEM_LIMIT),
        name="gla_step_sample",
    )(proj, log_a, state)


def _gla_out_kernel(o_ref, p_ref, x_ref, gt_ref, e_ref, og_ref, wout_ref, xo_ref):
    xo_ref[...] = _gla_output(o_ref[...], p_ref[...], x_ref[...], _rows_to_tokens(e_ref, gt_ref),
                              og_ref[...], wout_ref[...])


def _gla_out_call(o, proj, x, mod, layer, tm, seq_len, w):
    n, d = x.shape
    gate_spec = _mod_specs(mod, layer, 0, tm, seq_len)[2]
    g_block = (2 * GLA_DK + GLA_DV) // GLA_DV
    in_specs = [_row_spec(tm, GLA_DV),
                pl.BlockSpec((tm, GLA_DV), lambda i: (i, g_block)),
                _row_spec(tm, d), gate_spec,
                _const_spec((tm, 2 * tm // seq_len)),
                _const_spec((1, GLA_DV_HEAD)),
                _const_spec((GLA_DV, d))]
    return pl.pallas_call(
        _gla_out_kernel,
        grid=(n // tm,),
        in_specs=in_specs,
        out_specs=_row_spec(tm, d),
        out_shape=jax.ShapeDtypeStruct((n, d), F32),
        compiler_params=pltpu.CompilerParams(
            dimension_semantics=("arbitrary",), vmem_limit_bytes=VMEM_LIMIT),
        name="gla_out_sample",
    )(o, proj, x, mod, _token_copy_matrix(tm, seq_len), w["out_g"], w["w_out"])


PROMPT_FFN_TILE = 1024
PROMPT_GMLP_TILE = 1024
PROMPT_GLA_TILE = 1024
SAMPLE_TILE = 256
SAMPLE_SEQS_PER_STEP = 16


def kernel(x_prompt, x_sample, c_prompt, c_sample, state_gla, ada_w, ada_b, norm_mix_g, norm_ffn_g,
           ffn_w1, ffn_b1, ffn_w2, ffn_b2, gmlp_w_in, gmlp_b_in, gmlp_ln_g, gmlp_ln_b, gmlp_w_s,
           gmlp_b_s, gmlp_w_out, gmlp_b_out, gla_w_in, gla_w_gate2, gla_b_gate, gla_norm_g,
           gla_w_out, final_norm_g):
    n_seq_p, seq_p, d = x_prompt.shape
    n_seq_s, seq_s, _ = x_sample.shape
    assert d == D_MODEL and GMLP_SUBTILE % CHUNK_A == 0 and GLA_SUBTILE % GLA_FAST_CHUNK == 0
    assert all(seq_p % t == 0 for t in (PROMPT_FFN_TILE, PROMPT_GMLP_TILE, PROMPT_GLA_TILE))
    assert seq_s < SUBLANES and SUBLANES % seq_s == 0 and (n_seq_s * seq_s) % SAMPLE_TILE == 0
    row = lambda a: a.reshape(1, -1)

    mod_p, mod_s = _ada_call(c_prompt, c_sample, ada_w, ada_b)

    causal = np.tril(np.ones((CHUNK_A, CHUNK_A), dtype=bool))
    ws = jnp.where(causal[None], gmlp_w_s, jnp.zeros_like(gmlp_w_s))
    rows = np.arange(CHUNK_A)
    pick = (rows[:, None] % seq_s == np.arange(seq_s)[None, :]).astype(np.float32)
    same_seq = (rows[:, None] // seq_s) == (rows[None, :] // seq_s)
    exact = lax.Precision.HIGHEST
    ws_s = jnp.where(same_seq[None], jnp.einsum("ri,gij,cj->grc", pick, ws[:, :seq_s, :seq_s], pick,
                                                precision=exact), 0.0)
    bias_p = jnp.repeat(gmlp_b_s.T, GMLP_GROUP_W, axis=1)
    bias_s = jnp.dot(pick, bias_p[:seq_s], precision=exact)

    gmlp_w = dict(norm_g=row(norm_mix_g[0]), w_in=gmlp_w_in, b_in=row(gmlp_b_in),
                  ln_g=row(gmlp_ln_g), ln_b=row(gmlp_ln_b), w_out=gmlp_w_out,
                  b_out=row(gmlp_b_out))
    gmlp_w = dict(gmlp_w, mix_p=ws.astype(BF16), bias_p=bias_p, mix_s=ws_s.astype(BF16),
                  bias_s=bias_s)

    depth = ffn_w1.shape[0]
    ffn_w = dict(norm_g=norm_ffn_g.reshape(depth, 1, d), w1=ffn_w1,
                 b1=ffn_b1.reshape(depth, 1, D_FF), w2=ffn_w2,
                 b2=ffn_b2.reshape(depth, 1, d), final_g=row(final_norm_g))

    w_in_pad = jnp.pad(gla_w_in, ((0, 0), (0, GLA_IN_PAD - gla_w_in.shape[1]))).astype(BF16)
    wg2_pad = jnp.pad(gla_w_gate2, ((0, GLA_GATE_PAD - GLA_GATE_RANK), (0, 0))).astype(BF16)
    blk_ones = np.kron(np.eye(2, dtype=np.float32),
                       np.ones((GLA_DK_HEAD, GLA_DK_HEAD), np.float32))
    gla_w = dict(norm_g=row(norm_mix_g[1]), w_in=w_in_pad, w_gate2=wg2_pad, b_gate=row(gla_b_gate),
                 out_g=row(gla_norm_g), w_out=gla_w_out.astype(BF16),
                 tril=jnp.asarray(np.tril(np.ones((GLA_TRIL, GLA_TRIL), np.float32)), BF16),
                 ones_bd=jnp.asarray(blk_ones, BF16))

    xp = x_prompt.reshape(n_seq_p * seq_p, d)
    grid_for = lambda tm_p: _TwoGroupGrid(xp.shape[0], n_seq_s * seq_s, tm_p, SAMPLE_TILE, seq_p,
                                          seq_s)
    xp, xs, chunk_v = _gmlp_call(xp, x_sample, mod_p, mod_s, grid_for(PROMPT_GMLP_TILE), gmlp_w)
    xp, xs = _ffn_call(xp, xs, mod_p, mod_s, 0, grid_for(PROMPT_FFN_TILE), ffn_w, final_norm=False)
    xp, state_p = _gla_prompt_call(xp, mod_p, 1, n_seq_p, seq_p, PROMPT_GLA_TILE, gla_w)
    proj, log_a = _gla_proj_call(xs, mod_s, 1, SAMPLE_TILE, seq_s, gla_w)
    o, state_s = _gla_step_call(proj, log_a, state_gla, seq_s, SAMPLE_SEQS_PER_STEP)
    xs = _gla_out_call(o, proj, xs, mod_s, 1, SAMPLE_TILE, seq_s, gla_w)
    xp, xs = _ffn_call(xp, xs, mod_p, mod_s, 1, grid_for(PROMPT_FFN_TILE), ffn_w, final_norm=True)

    return (xp.reshape(x_prompt.shape), xs, state_p, state_s, chunk_v)
```

```python
import functools

import jax
import jax.numpy as jnp
import numpy as np
from jax import lax
from jax.experimental import pallas as pl
from jax.experimental.pallas import tpu as pltpu

F32 = jnp.float32
BF16 = jnp.bfloat16

D_MODEL = 1024
N_MOD = 6
CHUNK_A = 128
GMLP_WIDTH = D_MODEL
GMLP_GROUPS = 4
GMLP_GROUP_W = GMLP_WIDTH // GMLP_GROUPS
GLA_HEADS = 4
GLA_DK = D_MODEL // 2
GLA_DV = D_MODEL
GLA_DK_HEAD = GLA_DK // GLA_HEADS
GLA_DV_HEAD = GLA_DV // GLA_HEADS
GLA_GATE_RANK = 16
GLA_TAU = 16.0
GLA_CHUNK = 64
GLA_FAST_CHUNK = 128
GLA_TRIL = max(GLA_CHUNK, GLA_FAST_CHUNK)
GLA_SAFE_EXPONENT = 80.0
GLA_QKVG = 2 * GLA_DK + 2 * GLA_DV
GLA_GATE_PAD = 128
GLA_IN_PAD = GLA_QKVG + GLA_GATE_PAD
D_FF = 4 * D_MODEL
EPS = 1e-6

LANES = 128
SUBLANES = 8
BF16_SUBLANES = 16
ADA_ROWS = 256
FFN_CHUNK = 1024
FFN_SUBTILE = 1024
GMLP_SUBTILE = 512
GLA_SUBTILE = 512
VMEM_LIMIT = 56 * 1024 * 1024
WEIGHT_LOAD_CHUNKS = 32
WEIGHT_STAGE_SLOTS = 4


def _rms(x, g):
    return x * lax.rsqrt(jnp.mean(x * x, axis=-1, keepdims=True) + EPS) * g


def _modulated_norm(x, g, scale, shift):
    inv = lax.rsqrt(jnp.mean(x * x, axis=-1, keepdims=True) + EPS)
    return (x * inv) * (g * (1.0 + scale)) + shift


def _dot(a, b):
    return jnp.dot(a.astype(BF16), b.astype(BF16), preferred_element_type=F32)


def _dot_nt(a, b):
    return lax.dot_general(a.astype(BF16), b.astype(BF16), (((1,), (1,)), ((), ())),
                           preferred_element_type=F32)


def _dot_tn(a, b):
    return lax.dot_general(a.astype(BF16), b.astype(BF16), (((0,), (0,)), ((), ())),
                           preferred_element_type=F32)


def _gelu_tanh(x):
    c1 = -2.0 * 0.7978845608028654 * 1.4426950408889634
    c2 = c1 * 0.044715
    return x / (1.0 + jnp.exp2(x * (c1 + c2 * (x * x))))


def _silu(x):
    return x * jax.nn.sigmoid(x)


def _log_sigmoid(x):
    return -(jnp.maximum(-x, 0.0) + jnp.log1p(jnp.exp(-jnp.abs(x))))


def _bf16_terms(x, n):
    terms = []
    for _ in range(n - 1):
        t = x.astype(BF16)
        terms.append(t)
        x = x - t.astype(F32)
    terms.append(x.astype(BF16))
    return terms


def _ada_kernel(c_ref, wa_ref, wb_ref, b_ref, op_ref, os_ref, s_scr, acc_scr, *, n_prompt, tk):
    k = pl.program_id(1)

    @pl.when((pl.program_id(0) == 0) & (k == 0))
    def _():
        for kk in range(s_scr.shape[0]):
            s_scr[kk] = _silu(c_ref[:, kk * tk:(kk + 1) * tk]).astype(BF16)

    @pl.when(k == 0)
    def _():
        acc_scr[...] = jnp.broadcast_to(b_ref[...], acc_scr.shape)

    s = s_scr[k]
    half = N_MOD // 2
    for j in range(N_MOD):
        cols = slice(j * D_MODEL, (j + 1) * D_MODEL)
        w_ref, jj = (wa_ref, j) if j < half else (wb_ref, j - half)
        w = w_ref[:, jj * D_MODEL:(jj + 1) * D_MODEL].astype(BF16)
        acc_scr[:, cols] += jnp.dot(s, w, preferred_element_type=F32)

    @pl.when(k == pl.num_programs(1) - 1)
    def _():
        op_ref[...] = acc_scr[0:n_prompt, :]
        os_ref[...] = acc_scr[n_prompt:n_prompt + os_ref.shape[0], :]


def _ada_call(c_prompt, c_sample, ada_w, ada_b):
    depth = ada_w.shape[0]
    n_prompt, n_seq = c_prompt.shape[0], c_sample.shape[0]
    d = D_MODEL
    pad = -(n_prompt + n_seq) % BF16_SUBLANES
    c_all = jnp.concatenate([c_prompt, c_sample, jnp.zeros((pad, d), F32)], axis=0)
    rows = c_all.shape[0]
    tk = ADA_ROWS
    width = N_MOD * d
    return pl.pallas_call(
        functools.partial(_ada_kernel, n_prompt=n_prompt, tk=tk),
        grid=(depth, d // tk),
        in_specs=[
            pl.BlockSpec((rows, d), lambda l, k: (0, 0)),
            pl.BlockSpec((None, tk, width // 2), lambda l, k: (l, k, 0)),
            pl.BlockSpec((None, tk, width // 2), lambda l, k: (l, k, 1)),
            pl.BlockSpec((None, 1, width), lambda l, k: (l, 0, 0)),
        ],
        out_specs=[
            pl.BlockSpec((None, n_prompt, width), lambda l, k: (l, 0, 0)),
            pl.BlockSpec((None, n_seq, width), lambda l, k: (l, 0, 0)),
        ],
        out_shape=[
            jax.ShapeDtypeStruct((depth, n_prompt, width), F32),
            jax.ShapeDtypeStruct((depth, n_seq, width), F32),
        ],
        scratch_shapes=[pltpu.VMEM((d // tk, rows, tk), BF16),
                        pltpu.VMEM((rows, width), F32)],
        compiler_params=pltpu.CompilerParams(
            dimension_semantics=("arbitrary", "arbitrary"), vmem_limit_bytes=VMEM_LIMIT),
        name="adaln_mod",
    )(c_all, ada_w, ada_w, ada_b.reshape(depth, 1, N_MOD * d))


def _const_spec(shape):
    zeros = (0,) * len(shape)
    return pl.BlockSpec(shape, lambda i: zeros, pipeline_mode=pl.Buffered(1))


def _layer_spec(shape, layer):
    zeros = (0,) * len(shape)
    return pl.BlockSpec((None,) + shape, lambda i: (layer,) + zeros,
                        pipeline_mode=pl.Buffered(1))


def _mod_specs(mod, layer, first_chunk, tm, rows_per_seq):
    d = D_MODEL
    specs = []
    for j in range(first_chunk, first_chunk + 3):
        if rows_per_seq < tm:
            specs.append(pl.BlockSpec((None, tm // rows_per_seq, d), lambda i, j=j: (layer, i, j)))
        else:
            specs.append(pl.BlockSpec((None, mod.shape[1], d), lambda i, j=j: (layer, 0, j)))
    return specs


def _token_copy_matrix(tm, rows_per_seq):
    copy = np.repeat(np.eye(tm // rows_per_seq, dtype=np.float32), rows_per_seq, axis=0)
    return jnp.asarray(np.tile(copy, (1, 2)), BF16)


def _rows_to_tokens(e_ref, mod_ref):
    terms = jnp.concatenate(_bf16_terms(mod_ref[...], 2), axis=0)
    return jnp.dot(e_ref[...], terms, preferred_element_type=F32)


def _mod_rows(ref, rows):
    return ref[...] if ref.shape[0] == 1 else ref[rows, :]


def _row_spec(tm, width):
    return pl.BlockSpec((tm, width), lambda i: (i, 0))


class _TwoGroupGrid:
    def __init__(self, n_prompt_rows, n_sample_rows, tm_p, tm_s, seq_len, sample_seq_len):
        self.tm_p, self.tm_s = tm_p, tm_s
        self.sample_seq_len = sample_seq_len
        self.p_steps = n_prompt_rows // tm_p
        self.s_steps = n_sample_rows // tm_s
        self.tiles_per_seq = seq_len // tm_p
        self.n_prompt_seqs = n_prompt_rows // seq_len
        self.steps = self.p_steps + self.s_steps

    def p_idx(self, i):
        return jnp.minimum(i, self.p_steps - 1)

    def s_idx(self, i):
        return jnp.maximum(i - self.p_steps, 0)

    def rows_p(self, width):
        return pl.BlockSpec((self.tm_p, width), lambda i: (self.p_idx(i), 0))

    def rows_s(self, width, col_block=0):
        return pl.BlockSpec((self.tm_s, width), lambda i: (self.s_idx(i), col_block))

    def mods_p(self, layer, first_chunk):
        return [pl.BlockSpec((None, self.n_prompt_seqs, D_MODEL), lambda i, j=j: (layer, 0, j))
                for j in range(first_chunk, first_chunk + 3)]

    def prompt_row(self, ref, step):
        return ref[pl.ds(step // self.tiles_per_seq, 1), :]

    def mods_s(self, layer, first_chunk):
        return [pl.BlockSpec((None, self.tm_s // self.sample_seq_len, D_MODEL),
                             lambda i, j=j: (layer, self.s_idx(i), j))
                for j in range(first_chunk, first_chunk + 3)]

    def seqs_s(self, width):
        seq = self.sample_seq_len
        return pl.BlockSpec((self.tm_s // seq, seq, width), lambda i: (self.s_idx(i), 0, 0))

    def token_copy_spec(self):
        return _const_spec((self.tm_s, 2 * self.tm_s // self.sample_seq_len))

    def token_copy_matrix(self):
        return _token_copy_matrix(self.tm_s, self.sample_seq_len)


def _gmlp_rows(x_ref, sh_ref, sc_ref, gt_ref, mix_ref, bs_ref, xo_ref, v_ref, z_scr, w, tm, sub):
    ng_ref, win_ref, bin_ref, lng_ref, lnb_ref, wout_ref, bout_ref = w
    sub = min(sub, tm)
    groups = [slice(r0, r0 + sub) for r0 in range(0, tm, sub)]

    def project(gi):
        rows = groups[gi]
        h = _modulated_norm(x_ref[rows, :], ng_ref[...], _mod_rows(sc_ref, rows),
                            _mod_rows(sh_ref, rows))
        z_scr[gi % 2, 0:sub, :] = _dot(h, win_ref[...])

    project(0)
    for gi, rows in enumerate(groups):
        if gi + 1 < len(groups):
            project(gi + 1)
        z = _gelu_tanh(z_scr[gi % 2, 0:sub, :] + bin_ref[...])
        u = z[:, :GMLP_WIDTH]
        v = z[:, GMLP_WIDTH:]
        mu = jnp.mean(v, axis=-1, keepdims=True)
        vc = v - mu
        var = jnp.mean(vc * vc, axis=-1, keepdims=True)
        v = vc * lax.rsqrt(var + EPS) * lng_ref[...] + lnb_ref[...]
        if v_ref is not None:
            v_ref[rows, :] = v
        vb = v.astype(BF16)
        mixed = []
        for c in range(sub // CHUNK_A):
            cols = []
            for g in range(GMLP_GROUPS):
                blk = vb[c * CHUNK_A:(c + 1) * CHUNK_A, g * GMLP_GROUP_W:(g + 1) * GMLP_GROUP_W]
                cols.append(jnp.dot(mix_ref[g], blk, preferred_element_type=F32))
            mixed.append(jnp.concatenate(cols, axis=1) + bs_ref[...])
        s = jnp.concatenate(mixed, axis=0)
        y = _dot(u * s, wout_ref[...]) + bout_ref[...]
        xo_ref[rows, :] = x_ref[rows, :] + _mod_rows(gt_ref, rows) * y


def _rows_to_seq_layout(rows_ref, seq_ref, slab_scr):
    n_seq, seq, width = seq_ref.shape
    for j in range(width // LANES):
        lanes = slice(j * LANES, (j + 1) * LANES)
        slab_scr[...] = rows_ref[:, lanes]
        for t in range(seq):
            seq_ref[:, t, lanes] = slab_scr[pl.ds(t, n_seq, stride=seq), :]


def _seq_layout_to_rows(seq_ref, rows_ref, slab_scr):
    n_seq, seq, width = seq_ref.shape
    for j in range(width // LANES):
        lanes = slice(j * LANES, (j + 1) * LANES)
        for t in range(seq):
            slab_scr[pl.ds(t, n_seq, stride=seq), :] = seq_ref[:, t, lanes]
        rows_ref[:, lanes] = slab_scr[...]


def _gmlp_kernel(xp_ref, shp_ref, scp_ref, gtp_ref, xs_ref, shs_ref, scs_ref, gts_ref, e_ref,
                 ng_ref, win_hbm, bin_ref, lng_ref, lnb_ref, wout_hbm, bout_ref,
                 mixp_ref, bsp_ref, mixs_ref, bss_ref, op_ref, os_ref, vs_ref, z_scr, v_scr,
                 x_scr, slab_scr,
                 win_ref, wout_ref, stage_in, stage_out, sems_in, sems_out, *, grid):
    w = (ng_ref, win_ref, bin_ref, lng_ref, lnb_ref, wout_ref, bout_ref)
    step = pl.program_id(0)

    @pl.when(step == 0)
    def _():
        _load_as_bf16([(win_hbm, win_ref, stage_in, sems_in),
                       (wout_hbm, wout_ref, stage_out, sems_out)])

    @pl.when(step < grid.p_steps)
    def _():
        sh, sc, gt = (grid.prompt_row(r, step) for r in (shp_ref, scp_ref, gtp_ref))
        _gmlp_rows(xp_ref, sh, sc, gt, mixp_ref, bsp_ref, op_ref, None, z_scr, w,
                   grid.tm_p, GMLP_SUBTILE)

    @pl.when(step >= grid.p_steps)
    def _():
        sh, sc, gt = (_rows_to_tokens(e_ref, r) for r in (shs_ref, scs_ref, gts_ref))
        _seq_layout_to_rows(xs_ref, x_scr, slab_scr)
        _gmlp_rows(x_scr, sh, sc, gt, mixs_ref, bss_ref, os_ref, v_scr, z_scr, w,
                   grid.tm_s, GMLP_SUBTILE)
        _rows_to_seq_layout(v_scr, vs_ref, slab_scr)


def _gmlp_call(xp, xs_seq, mod_p, mod_s, grid, w):
    d = D_MODEL
    n_sample = xs_seq.shape[0] * xs_seq.shape[1]
    mix_spec = _const_spec((GMLP_GROUPS, CHUNK_A, CHUNK_A))
    bias_spec = _const_spec((CHUNK_A, GMLP_WIDTH))
    in_specs = ([grid.rows_p(d)] + grid.mods_p(0, 0) + [grid.seqs_s(d)] + grid.mods_s(0, 0) + [
        grid.token_copy_spec(),
        _const_spec((1, d)),
        pl.BlockSpec(memory_space=pl.ANY),
        _const_spec((1, 2 * GMLP_WIDTH)),
        _const_spec((1, GMLP_WIDTH)),
        _const_spec((1, GMLP_WIDTH)),
        pl.BlockSpec(memory_space=pl.ANY),
        _const_spec((1, d)),
        mix_spec, bias_spec, mix_spec, bias_spec,
    ])
    return pl.pallas_call(
        functools.partial(_gmlp_kernel, grid=grid),
        grid=(grid.steps,),
        in_specs=in_specs,
        out_specs=[grid.rows_p(d), grid.rows_s(d), grid.seqs_s(GMLP_WIDTH)],
        out_shape=[jax.ShapeDtypeStruct(xp.shape, F32),
                   jax.ShapeDtypeStruct((n_sample, d), F32),
                   jax.ShapeDtypeStruct(xs_seq.shape[:2] + (GMLP_WIDTH,), F32)],
        scratch_shapes=[pltpu.VMEM((2, GMLP_SUBTILE, 2 * GMLP_WIDTH), F32),
                        pltpu.VMEM((grid.tm_s, GMLP_WIDTH), F32),
                        pltpu.VMEM((grid.tm_s, d), F32),
                        pltpu.VMEM((grid.tm_s, LANES), F32),
                        pltpu.VMEM((d, 2 * GMLP_WIDTH), BF16),
                        pltpu.VMEM((GMLP_WIDTH, d), BF16),
                        pltpu.VMEM((WEIGHT_STAGE_SLOTS, d // WEIGHT_LOAD_CHUNKS, 2 * GMLP_WIDTH),
                                   F32),
                        pltpu.VMEM((WEIGHT_STAGE_SLOTS, GMLP_WIDTH // WEIGHT_LOAD_CHUNKS, d), F32),
                        pltpu.SemaphoreType.DMA((WEIGHT_STAGE_SLOTS,)),
                        pltpu.SemaphoreType.DMA((WEIGHT_STAGE_SLOTS,))],
        compiler_params=pltpu.CompilerParams(
            dimension_semantics=("arbitrary",), vmem_limit_bytes=VMEM_LIMIT),
        name="gmlp_mixer",
    )(xp, mod_p, mod_p, mod_p, xs_seq, mod_s, mod_s, mod_s, grid.token_copy_matrix(),
      w["norm_g"], w["w_in"], w["b_in"],
      w["ln_g"], w["ln_b"], w["w_out"], w["b_out"], w["mix_p"], w["bias_p"], w["mix_s"],
      w["bias_s"])


def _ffn_rows(x_ref, sh_ref, sc_ref, gt_ref, o_ref, h_scr, w, final_norm, tm):
    ng_ref, w1_ref, b1_ref, w2_ref, b2_ref, fg_ref = w
    sub = min(FFN_SUBTILE, tm)
    groups = [slice(r0, r0 + sub) for r0 in range(0, tm, sub)]

    def normalise(gi):
        rows = groups[gi]
        h_scr[gi % 2, 0:sub, :] = _modulated_norm(
            x_ref[rows, :], ng_ref[...], _mod_rows(sc_ref, rows),
            _mod_rows(sh_ref, rows)).astype(BF16)

    normalise(0)
    for gi, rows in enumerate(groups):
        if gi + 1 < len(groups):
            normalise(gi + 1)
        h = h_scr[gi % 2, 0:sub, :]
        acc = jnp.zeros((sub, D_MODEL), F32)
        for j in range(D_FF // FFN_CHUNK):
            cols = slice(j * FFN_CHUNK, (j + 1) * FFN_CHUNK)
            a = jnp.dot(h, w1_ref[:, cols], preferred_element_type=F32) + b1_ref[:, cols]
            r = jnp.square(jnp.maximum(a, 0.0)).astype(BF16)
            acc = acc + jnp.dot(r, w2_ref[cols, :], preferred_element_type=F32)
        y = x_ref[rows, :] + _mod_rows(gt_ref, rows) * (acc + b2_ref[...])
        if final_norm:
            y = _rms(y, fg_ref[...])
        o_ref[rows, :] = y


def _load_as_bf16(jobs):
    def chunk_copy(job, c):
        src_hbm, _, stage_ref, sem_ref = job
        slot = c % stage_ref.shape[0]
        rows = pl.ds(c * stage_ref.shape[1], stage_ref.shape[1])
        return pltpu.make_async_copy(src_hbm.at[rows, :], stage_ref.at[slot], sem_ref.at[slot])

    n_chunks = {job[1].shape[0] // job[2].shape[1] for job in jobs}
    n_slots = {job[2].shape[0] for job in jobs}
    assert len(n_chunks) == 1 and len(n_slots) == 1
    n_chunks, n_slots = n_chunks.pop(), n_slots.pop()
    for c in range(min(n_slots - 1, n_chunks)):
        for job in jobs:
            chunk_copy(job, c).start()
    for c in range(n_chunks):
        for job in jobs:
            if c + n_slots - 1 < n_chunks:
                chunk_copy(job, c + n_slots - 1).start()
        for job in jobs:
            _, dst_ref, stage_ref, _ = job
            chunk_copy(job, c).wait()
            rows = stage_ref.shape[1]
            dst_ref[c * rows:(c + 1) * rows, :] = stage_ref[c % n_slots].astype(BF16)


def _ffn_kernel(xp_ref, shp_ref, scp_ref, gtp_ref, xs_ref, shs_ref, scs_ref, gts_ref, e_ref,
                ng_ref, w1_hbm, b1_ref, w2_hbm, b2_ref, fg_ref, op_ref, os_ref, h_scr,
                w1_ref, w2_ref, stage1, stage2, sems1, sems2, ys_scr, slab_scr,
                *, final_norm, grid, layer):
    w = (ng_ref, w1_ref, b1_ref, w2_ref, b2_ref, fg_ref)
    step = pl.program_id(0)

    @pl.when(step == 0)
    def _():
        _load_as_bf16([(w1_hbm.at[layer], w1_ref, stage1, sems1),
                       (w2_hbm.at[layer], w2_ref, stage2, sems2)])

    @pl.when(step < grid.p_steps)
    def _():
        sh, sc, gt = (grid.prompt_row(r, step) for r in (shp_ref, scp_ref, gtp_ref))
        _ffn_rows(xp_ref, sh, sc, gt, op_ref, h_scr, w, final_norm, grid.tm_p)

    @pl.when(step >= grid.p_steps)
    def _():
        sh, sc, gt = (_rows_to_tokens(e_ref, r) for r in (shs_ref, scs_ref, gts_ref))
        if final_norm:
            _ffn_rows(xs_ref, sh, sc, gt, ys_scr, h_scr, w, final_norm, grid.tm_s)
            _rows_to_seq_layout(ys_scr, os_ref, slab_scr)
        else:
            _ffn_rows(xs_ref, sh, sc, gt, os_ref, h_scr, w, final_norm, grid.tm_s)


def _ffn_call(xp, xs, mod_p, mod_s, layer, grid, w, final_norm):
    d = D_MODEL
    seq_shape = (xs.shape[0] // grid.sample_seq_len, grid.sample_seq_len, d)
    in_specs = ([grid.rows_p(d)] + grid.mods_p(layer, 3) + [grid.rows_s(d)]
                + grid.mods_s(layer, 3) + [
        grid.token_copy_spec(),
        _layer_spec((1, d), layer),
        pl.BlockSpec(memory_space=pl.ANY),
        _layer_spec((1, D_FF), layer),
        pl.BlockSpec(memory_space=pl.ANY),
        _layer_spec((1, d), layer),
        _const_spec((1, d)),
    ])
    return pl.pallas_call(
        functools.partial(_ffn_kernel, final_norm=final_norm, grid=grid, layer=layer),
        grid=(grid.steps,),
        in_specs=in_specs,
        out_specs=[grid.rows_p(d), grid.seqs_s(d) if final_norm else grid.rows_s(d)],
        out_shape=[jax.ShapeDtypeStruct(xp.shape, F32),
                   jax.ShapeDtypeStruct(seq_shape if final_norm else xs.shape, F32)],
        scratch_shapes=[pltpu.VMEM((2, FFN_SUBTILE, d), BF16),
                        pltpu.VMEM((d, D_FF), BF16),
                        pltpu.VMEM((D_FF, d), BF16),
                        pltpu.VMEM((WEIGHT_STAGE_SLOTS, d // WEIGHT_LOAD_CHUNKS, D_FF), F32),
                        pltpu.VMEM((WEIGHT_STAGE_SLOTS, D_FF // WEIGHT_LOAD_CHUNKS, d), F32),
                        pltpu.SemaphoreType.DMA((WEIGHT_STAGE_SLOTS,)),
                        pltpu.SemaphoreType.DMA((WEIGHT_STAGE_SLOTS,)),
                        pltpu.VMEM((grid.tm_s, d), F32),
                        pltpu.VMEM((grid.tm_s, LANES), F32)],
        compiler_params=pltpu.CompilerParams(
            dimension_semantics=("arbitrary",), vmem_limit_bytes=VMEM_LIMIT),
        name="sqrelu_mlp",
    )(xp, mod_p, mod_p, mod_p, xs, mod_s, mod_s, mod_s, grid.token_copy_matrix(),
      w["norm_g"], w["w1"], w["b1"], w["w2"],
      w["b2"], w["final_g"])


def _gla_project(x, ng, scale, shift, win, wg2, bg):
    h = _modulated_norm(x, ng, scale, shift)
    proj = _dot(h, win)
    gate = _dot(proj[:, GLA_QKVG:], wg2) + bg
    log_a = _log_sigmoid(gate) * (1.0 / GLA_TAU)
    return proj[:, :GLA_QKVG], log_a


def _gla_output(o, gate_in, x, gt, og, wout):
    parts = []
    for h in range(GLA_HEADS):
        cols = slice(h * GLA_DV_HEAD, (h + 1) * GLA_DV_HEAD)
        parts.append(_rms(o[:, cols], og))
    on = jnp.concatenate(parts, axis=1)
    y = _dot(on * _silu(gate_in), wout)
    return x + gt * y


def _as_column(row):
    return jnp.broadcast_to(row, (SUBLANES, row.shape[1])).T[:, 0:1]


def _chunk_cumsum(tril, log_a):
    head = log_a.astype(BF16)
    rest = (log_a - head.astype(F32)).astype(BF16)
    return (jnp.dot(tril, head, preferred_element_type=F32)
            + jnp.dot(tril, rest, preferred_element_type=F32))


def _level_masks(c):
    rr = lax.broadcasted_iota(jnp.int32, (c, c), 0)
    cc = lax.broadcasted_iota(jnp.int32, (c, c), 1)
    masks = []
    s = c
    while s > SUBLANES:
        half = s // 2
        same = (rr ^ cc) < s
        masks.append(same & ((rr & half) != 0) & ((cc & half) == 0))
        s = half
    return masks, rr, cc


def _block_rows(x, s, r):
    c, w = x.shape
    return jnp.concatenate(
        [jnp.broadcast_to(x[p * s + r:p * s + r + 1, :], (s, w)) for p in range(c // s)], axis=0)


def _diag_terms(q, k, a):
    c, w = q.shape
    ii = lax.broadcasted_iota(jnp.int32, (c, w), 0) & (SUBLANES - 1)
    decay = jnp.zeros((c, w), F32)
    terms = [None] * SUBLANES
    for j in range(SUBLANES - 1, -1, -1):
        if j < SUBLANES - 1:
            decay = decay * _block_rows(a, SUBLANES, j + 1)
        decay = jnp.where(ii == j, 1.0, decay)
        terms[j] = q * _block_rows(k, SUBLANES, j) * decay
    return jnp.concatenate(terms, axis=0)


def _gla_chunk_pair(q2, k2, v2, la2, b2, st_refs, ones_bd, masks, rr, cc):
    c = q2[0].shape[0]
    zs = []
    for q, k, la in zip(q2, k2, la2):
        zs.append(_diag_terms(q, k, jnp.exp(la)))
    sums = jnp.dot(jnp.concatenate(zs, axis=1).astype(BF16), ones_bd,
                   preferred_element_type=F32)
    outs = []
    for hh, (q, k, v, b, st_ref) in enumerate(zip(q2, k2, v2, b2, st_refs)):
        dk = q.shape[1]
        attn = jnp.zeros((c, c), F32)
        base = rr & ~(SUBLANES - 1)
        for j in range(SUBLANES):
            rj = sums[j * c:(j + 1) * c, hh * dk:hh * dk + c]
            attn = jnp.where(cc == base + j, rj, attn)
        s = c
        for mask in masks:
            half = s // 2
            e = jnp.exp(-jnp.abs(b - _block_rows(b, s, half)))
            attn = jnp.where(mask, _dot_nt(q * e, k * e), attn)
            s = half
        b_last = b[c - 1:c, :]
        st = st_ref[...]
        o = _dot(q * jnp.exp(b), st) + _dot(attn, v)
        kd = k * jnp.exp(b_last - b)
        st_ref[...] = _as_column(jnp.exp(b_last)) * st + _dot_tn(kd, v)
        outs.append(o)
    return outs


def _gla_prompt_kernel(x_ref, sh_ref, sc_ref, gt_ref, ng_ref, win_ref, wg2_ref, bg_ref,
                       og_ref, wout_ref, tril_ref, ones_ref, xo_ref, s_ref,
                       proj_scr, ga_scr, la_scr, b_scr, o_scr, st_scr, *, tm, sub, tiles_per_seq):
    tile = pl.program_id(0) % tiles_per_seq
    seq_row = pl.ds(pl.program_id(0) // tiles_per_seq, 1)
    shift, scale, gate_mod = (r[seq_row, :] for r in (sh_ref, sc_ref, gt_ref))

    @pl.when(tile == 0)
    def _():
        st_scr[...] = jnp.zeros(st_scr.shape, F32)

    dk, dv = GLA_DK_HEAD, GLA_DV_HEAD
    qscale = GLA_DK_HEAD ** -0.5
    cf = GLA_FAST_CHUNK
    c = GLA_CHUNK
    groups = [slice(r0, r0 + sub) for r0 in range(0, tm, sub)]

    def project(rows):
        h = _modulated_norm(x_ref[rows, :], ng_ref[...], scale, shift)
        proj = _dot(h, win_ref[...])
        proj_scr[rows, :] = proj[:, :GLA_QKVG]
        ga_scr[rows, :] = proj[:, GLA_QKVG:]

    def decays(rows):
        gate = _dot(ga_scr[rows, :], wg2_ref[...]) + bg_ref[...]
        log_a = _log_sigmoid(gate) * (1.0 / GLA_TAU)
        la_scr[rows, :] = log_a
        for ci in range(sub // cf):
            b_scr[rows.start + ci * cf:rows.start + (ci + 1) * cf, :] = _chunk_cumsum(
                tril_ref[0:cf, 0:cf], log_a[ci * cf:(ci + 1) * cf, :])
        return jnp.min(b_scr[rows, :]) >= -GLA_SAFE_EXPONENT

    def head_slices(rows, h):
        q = proj_scr[rows, h * dk:(h + 1) * dk] * qscale
        k = proj_scr[rows, GLA_DK + h * dk:GLA_DK + (h + 1) * dk]
        v = proj_scr[rows, 2 * GLA_DK + h * dv:2 * GLA_DK + (h + 1) * dv]
        return q, k, v

    def single_ref_rows(group):
        rr = lax.broadcasted_iota(jnp.int32, (cf, cf), 0)
        cc = lax.broadcasted_iota(jnp.int32, (cf, cf), 1)
        causal = rr >= cc
        chunks = [slice(r0, r0 + cf) for r0 in range(group.start, group.stop, cf)]
        for h in range(GLA_HEADS):
            qes, vbs, scores, kvs, decays = [], [], [], [], []
            for rows in chunks:
                q, k, v = head_slices(rows, h)
                b = b_scr[rows, h * dk:(h + 1) * dk]
                decay_last = jnp.exp(b[cf - 1:cf, :])
                qe = (q * jnp.exp(b)).astype(BF16)
                kt = k * jnp.exp(-b)
                vb = v.astype(BF16)
                scores.append(_dot_nt(qe, kt))
                kvs.append(_dot_tn(kt * decay_last, vb))
                qes.append(qe)
                vbs.append(vb)
                decays.append(_as_column(decay_last))
            st = st_scr[h]
            states = []
            for kv, decay_col in zip(kvs, decays):
                states.append(st)
                st = decay_col * st + kv
            st_scr[h] = st
            for rows, qe, vb, sc, st_in in zip(chunks, qes, vbs, scores, states):
                attn = jnp.where(causal, sc, 0.0).astype(BF16)
                lhs = jnp.concatenate([qe, attn], axis=1)
                rhs = jnp.concatenate([st_in.astype(BF16), vb], axis=0)
                o_scr[rows, h * dv:(h + 1) * dv] = jnp.dot(lhs, rhs, preferred_element_type=F32)

    def robust_rows(group):
        def robust_body(ci, carry):
            rows = pl.ds(pl.multiple_of(group.start + ci * c, c), c)
            masks, rr, cc = _level_masks(c)
            b_c = _chunk_cumsum(tril_ref[0:c, 0:c], la_scr[rows, :])
            outs = []
            for h0 in range(0, GLA_HEADS, 2):
                heads = (h0, h0 + 1)
                qkv = [head_slices(rows, h) for h in heads]
                la2 = [la_scr[rows, h * dk:(h + 1) * dk] for h in heads]
                b2 = [b_c[:, h * dk:(h + 1) * dk] for h in heads]
                st_refs = [st_scr.at[h] for h in heads]
                outs += _gla_chunk_pair([t[0] for t in qkv], [t[1] for t in qkv],
                                        [t[2] for t in qkv], la2, b2, st_refs, ones_ref[...],
                                        masks, rr, cc)
            o_scr[rows, :] = jnp.concatenate(outs, axis=1)
            return carry

        lax.fori_loop(0, sub // c, robust_body, 0)

    def output(rows):
        gate_in = proj_scr[rows, 2 * GLA_DK + GLA_DV:]
        xo_ref[rows, :] = _gla_output(o_scr[rows, :], gate_in, x_ref[rows, :], gate_mod,
                                      og_ref[...], wout_ref[...])

    project(groups[0])
    for gi, rows in enumerate(groups):
        single_ref_safe = decays(rows)
        following = groups[gi + 1] if gi + 1 < len(groups) else None

        @pl.when(single_ref_safe)
        def _():
            if following is not None:
                project(following)
            single_ref_rows(rows)
            output(rows)

        @pl.when(jnp.logical_not(single_ref_safe))
        def _():
            if following is not None:
                project(following)
            robust_rows(rows)
            output(rows)

    @pl.when(tile == tiles_per_seq - 1)
    def _():
        s_ref[...] = st_scr[...]


def _gla_prompt_call(x, mod, layer, n_seq, seq_len, tm, w):
    n, d = x.shape
    tiles_per_seq = seq_len // tm
    dk, dv = GLA_DK_HEAD, GLA_DV_HEAD
    in_specs = [_row_spec(tm, d)] + _mod_specs(mod, layer, 0, tm, seq_len) + [
        _const_spec((1, d)),
        _const_spec((d, GLA_IN_PAD)),
        _const_spec((GLA_GATE_PAD, GLA_DK)),
        _const_spec((1, GLA_DK)),
        _const_spec((1, dv)),
        _const_spec((GLA_DV, d)),
        _const_spec((GLA_TRIL, GLA_TRIL)),
        _const_spec((2 * dk, 2 * dk)),
    ]
    out_specs = [
        _row_spec(tm, d),
        pl.BlockSpec((None, GLA_HEADS, dk, dv), lambda i: (i // tiles_per_seq, 0, 0, 0)),
    ]
    out_shape = [
        jax.ShapeDtypeStruct((n, d), F32),
        jax.ShapeDtypeStruct((n_seq, GLA_HEADS, dk, dv), F32),
    ]
    return pl.pallas_call(
        functools.partial(_gla_prompt_kernel, tm=tm, sub=GLA_SUBTILE,
                          tiles_per_seq=tiles_per_seq),
        grid=(n // tm,),
        in_specs=in_specs,
        out_specs=out_specs,
        out_shape=out_shape,
        scratch_shapes=[
            pltpu.VMEM((tm, GLA_QKVG), F32),
            pltpu.VMEM((tm, GLA_GATE_PAD), F32),
            pltpu.VMEM((tm, GLA_DK), F32),
            pltpu.VMEM((tm, GLA_DK), F32),
            pltpu.VMEM((tm, GLA_DV), F32),
            pltpu.VMEM((GLA_HEADS, dk, dv), F32),
        ],
        compiler_params=pltpu.CompilerParams(
            dimension_semantics=("arbitrary",), vmem_limit_bytes=VMEM_LIMIT),
        name="gla_mixer_prompt",
    )(x, mod, mod, mod, w["norm_g"], w["w_in"], w["w_gate2"], w["b_gate"], w["out_g"],
      w["w_out"], w["tril"], w["ones_bd"])


def _gla_proj_kernel(x_ref, sh_ref, sc_ref, e_ref, ng_ref, win_ref, wg2_ref, bg_ref, p_ref,
                     la_ref):
    proj, log_a = _gla_project(x_ref[...], ng_ref[...], _rows_to_tokens(e_ref, sc_ref),
                               _rows_to_tokens(e_ref, sh_ref), win_ref[...], wg2_ref[...],
                               bg_ref[...])
    p_ref[...] = proj
    la_ref[...] = log_a


def _gla_proj_call(x, mod, layer, tm, seq_len, w):
    n, d = x.shape
    specs = _mod_specs(mod, layer, 0, tm, seq_len)
    in_specs = [_row_spec(tm, d), specs[0], specs[1],
                _const_spec((tm, 2 * tm // seq_len)),
                _const_spec((1, d)),
                _const_spec((d, GLA_IN_PAD)),
                _const_spec((GLA_GATE_PAD, GLA_DK)),
                _const_spec((1, GLA_DK))]
    return pl.pallas_call(
        _gla_proj_kernel,
        grid=(n // tm,),
        in_specs=in_specs,
        out_specs=[_row_spec(tm, GLA_QKVG), _row_spec(tm, GLA_DK)],
        out_shape=[jax.ShapeDtypeStruct((n, GLA_QKVG), F32),
                   jax.ShapeDtypeStruct((n, GLA_DK), F32)],
        compiler_params=pltpu.CompilerParams(
            dimension_semantics=("arbitrary",), vmem_limit_bytes=VMEM_LIMIT),
        name="gla_proj_sample",
    )(x, mod, mod, _token_copy_matrix(tm, seq_len), w["norm_g"], w["w_in"], w["w_gate2"],
      w["b_gate"])


def _gla_step_kernel(p_ref, la_ref, s0_ref, o_ref, s1_ref, *, seqs, steps):
    dk, dv = GLA_DK_HEAD, GLA_DV_HEAD
    qscale = GLA_DK_HEAD ** -0.5
    per_tile = SUBLANES // steps
    row = lax.broadcasted_iota(jnp.int32, (SUBLANES, dk), 0)

    def seq_rows(tile, s):
        if s:
            tile = pltpu.roll(tile, SUBLANES - s * steps, 0)
        keep = lax.broadcasted_iota(jnp.int32, tile.shape, 0) < steps
        return jnp.where(keep, tile, 0.0)

    def tile_body(ti, carry):
        rows = pl.ds(pl.multiple_of(ti * SUBLANES, SUBLANES), SUBLANES)
        p_tile = p_ref[rows, :]
        la_tile = la_ref[rows, :]
        o_tile = jnp.zeros((SUBLANES, GLA_DV), F32)
        for s in range(per_tile):
            bi = ti * per_tile + s
            p = seq_rows(p_tile, s)
            la_all = seq_rows(la_tile, s)
            outs = []
            for h in range(GLA_HEADS):
                q = p[:, h * dk:(h + 1) * dk] * qscale
                k = p[:, GLA_DK + h * dk:GLA_DK + (h + 1) * dk]
                v = p[:, 2 * GLA_DK + h * dv:2 * GLA_DK + (h + 1) * dv]
                la = la_all[:, h * dk:(h + 1) * dk]
                b = jnp.zeros((SUBLANES, dk), F32)
                for t in range(steps):
                    b = b + jnp.where(row >= t, la[t:t + 1, :], 0.0)
                b_last = b[steps - 1:steps, :]
                s0 = s0_ref[bi, h]
                o = _dot(q * jnp.exp(b), s0)
                for j in range(steps):
                    e = jnp.exp(jnp.minimum(b - b[j:j + 1, :], 0.0))
                    z = jnp.where(row >= j, q * k[j:j + 1, :] * e, 0.0)
                    o = o + jnp.sum(z, axis=-1, keepdims=True) * v[j:j + 1, :]
                outs.append(o)
                m = jnp.where(row == steps, jnp.exp(b_last), k * jnp.exp(b_last - b))
                m_t = m.T
                s1_ref[bi, h] = m_t[:, steps:steps + 1] * s0 + _dot(m_t, v)
            o_seq = jnp.concatenate(outs, axis=1)
            o_tile = o_tile + (pltpu.roll(o_seq, s * steps, 0) if s else o_seq)
        o_ref[rows, :] = o_tile
        return carry

    lax.fori_loop(0, seqs // per_tile, tile_body, 0)


def _gla_step_call(proj, log_a, state, steps, seqs):
    n_seq = state.shape[0]
    dk, dv = GLA_DK_HEAD, GLA_DV_HEAD
    rows = lambda width: pl.BlockSpec((seqs * steps, width), lambda i: (i, 0))
    st_spec = pl.BlockSpec((seqs, GLA_HEADS, dk, dv), lambda i: (i, 0, 0, 0))
    return pl.pallas_call(
        functools.partial(_gla_step_kernel, seqs=seqs, steps=steps),
        grid=(n_seq // seqs,),
        in_specs=[rows(GLA_QKVG), rows(GLA_DK), st_spec],
        out_specs=[rows(GLA_DV), st_spec],
        out_shape=[jax.ShapeDtypeStruct((n_seq * steps, GLA_DV), F32),
                   jax.ShapeDtypeStruct(state.shape, F32)],
        compiler_params=pltpu.CompilerParams(
            dimension_semantics=("arbitrary",), vmem_limit_bytes=VMEM_LIMIT),
        name="gla_step_sample",
    )(proj, log_a, state)


def _gla_out_kernel(o_ref, p_ref, x_ref, gt_ref, e_ref, og_ref, wout_ref, xo_ref):
    xo_ref[...] = _gla_output(o_ref[...], p_ref[...], x_ref[...], _rows_to_tokens(e_ref, gt_ref),
                              og_ref[...], wout_ref[...])


def _gla_out_call(o, proj, x, mod, layer, tm, seq_len, w):
    n, d = x.shape
    gate_spec = _mod_specs(mod, layer, 0, tm, seq_len)[2]
    g_block = (2 * GLA_DK + GLA_DV) // GLA_DV
    in_specs = [_row_spec(tm, GLA_DV),
                pl.BlockSpec((tm, GLA_DV), lambda i: (i, g_block)),
                _row_spec(tm, d), gate_spec,
                _const_spec((tm, 2 * tm // seq_len)),
                _const_spec((1, GLA_DV_HEAD)),
                _const_spec((GLA_DV, d))]
    return pl.pallas_call(
        _gla_out_kernel,
        grid=(n // tm,),
        in_specs=in_specs,
        out_specs=_row_spec(tm, d),
        out_shape=jax.ShapeDtypeStruct((n, d), F32),
        compiler_params=pltpu.CompilerParams(
            dimension_semantics=("arbitrary",), vmem_limit_bytes=VMEM_LIMIT),
        name="gla_out_sample",
    )(o, proj, x, mod, _token_copy_matrix(tm, seq_len), w["out_g"], w["w_out"])


PROMPT_FFN_TILE = 1024
PROMPT_GMLP_TILE = 1024
PROMPT_GLA_TILE = 1024
SAMPLE_TILE = 256
SAMPLE_SEQS_PER_STEP = 16


def kernel(x_prompt, x_sample, c_prompt, c_sample, state_gla, ada_w, ada_b, norm_mix_g, norm_ffn_g,
           ffn_w1, ffn_b1, ffn_w2, ffn_b2, gmlp_w_in, gmlp_b_in, gmlp_ln_g, gmlp_ln_b, gmlp_w_s,
           gmlp_b_s, gmlp_w_out, gmlp_b_out, gla_w_in, gla_w_gate2, gla_b_gate, gla_norm_g,
           gla_w_out, final_norm_g):
    n_seq_p, seq_p, d = x_prompt.shape
    n_seq_s, seq_s, _ = x_sample.shape
    assert d == D_MODEL and GMLP_SUBTILE % CHUNK_A == 0 and GLA_SUBTILE % GLA_FAST_CHUNK == 0
    assert all(seq_p % t == 0 for t in (PROMPT_FFN_TILE, PROMPT_GMLP_TILE, PROMPT_GLA_TILE))
    assert seq_s < SUBLANES and SUBLANES % seq_s == 0 and (n_seq_s * seq_s) % SAMPLE_TILE == 0
    row = lambda a: a.reshape(1, -1)

    mod_p, mod_s = _ada_call(c_prompt, c_sample, ada_w, ada_b)

    causal = np.tril(np.ones((CHUNK_A, CHUNK_A), dtype=bool))
    ws = jnp.where(causal[None], gmlp_w_s, jnp.zeros_like(gmlp_w_s))
    rows = np.arange(CHUNK_A)
    pick = (rows[:, None] % seq_s == np.arange(seq_s)[None, :]).astype(np.float32)
    same_seq = (rows[:, None] // seq_s) == (rows[None, :] // seq_s)
    exact = lax.Precision.HIGHEST
    ws_s = jnp.where(same_seq[None], jnp.einsum("ri,gij,cj->grc", pick, ws[:, :seq_s, :seq_s], pick,
                                                precision=exact), 0.0)
    bias_p = jnp.repeat(gmlp_b_s.T, GMLP_GROUP_W, axis=1)
    bias_s = jnp.dot(pick, bias_p[:seq_s], precision=exact)

    gmlp_w = dict(norm_g=row(norm_mix_g[0]), w_in=gmlp_w_in, b_in=row(gmlp_b_in),
                  ln_g=row(gmlp_ln_g), ln_b=row(gmlp_ln_b), w_out=gmlp_w_out,
                  b_out=row(gmlp_b_out))
    gmlp_w = dict(gmlp_w, mix_p=ws.astype(BF16), bias_p=bias_p, mix_s=ws_s.astype(BF16),
                  bias_s=bias_s)

    depth = ffn_w1.shape[0]
    ffn_w = dict(norm_g=norm_ffn_g.reshape(depth, 1, d), w1=ffn_w1,
                 b1=ffn_b1.reshape(depth, 1, D_FF), w2=ffn_w2,
                 b2=ffn_b2.reshape(depth, 1, d), final_g=row(final_norm_g))

    w_in_pad = jnp.pad(gla_w_in, ((0, 0), (0, GLA_IN_PAD - gla_w_in.shape[1]))).astype(BF16)
    wg2_pad = jnp.pad(gla_w_gate2, ((0, GLA_GATE_PAD - GLA_GATE_RANK), (0, 0))).astype(BF16)
    blk_ones = np.kron(np.eye(2, dtype=np.float32),
                       np.ones((GLA_DK_HEAD, GLA_DK_HEAD), np.float32))
    gla_w = dict(norm_g=row(norm_mix_g[1]), w_in=w_in_pad, w_gate2=wg2_pad, b_gate=row(gla_b_gate),
                 out_g=row(gla_norm_g), w_out=gla_w_out.astype(BF16),
                 tril=jnp.asarray(np.tril(np.ones((GLA_TRIL, GLA_TRIL), np.float32)), BF16),
                 ones_bd=jnp.asarray(blk_ones, BF16))

    xp = x_prompt.reshape(n_seq_p * seq_p, d)
    grid_for = lambda tm_p: _TwoGroupGrid(xp.shape[0], n_seq_s * seq_s, tm_p, SAMPLE_TILE, seq_p,
                                          seq_s)
    xp, xs, chunk_v = _gmlp_call(xp, x_sample, mod_p, mod_s, grid_for(PROMPT_GMLP_TILE), gmlp_w)
    xp, xs = _ffn_call(xp, xs, mod_p, mod_s, 0, grid_for(PROMPT_FFN_TILE), ffn_w, final_norm=False)
    xp, state_p = _gla_prompt_call(xp, mod_p, 1, n_seq_p, seq_p, PROMPT_GLA_TILE, gla_w)
    proj, log_a = _gla_proj_call(xs, mod_s, 1, SAMPLE_TILE, seq_s, gla_w)
    o, state_s = _gla_step_call(proj, log_a, state_gla, seq_s, SAMPLE_SEQS_PER_STEP)
    xs = _gla_out_call(o, proj, xs, mod_s, 1, SAMPLE_TILE, seq_s, gla_w)
    xp, xs = _ffn_call(xp, xs, mod_p, mod_s, 1, grid_for(PROMPT_FFN_TILE), ffn_w, final_norm=True)

    return (xp.reshape(x_prompt.shape), xs, state_p, state_s, chunk_v)
```

```python
import functools

import jax
import jax.numpy as jnp
import numpy as np
from jax import lax
from jax.experimental import pallas as pl
from jax.experimental.pallas import tpu as pltpu

F32 = jnp.float32
BF16 = jnp.bfloat16

D_MODEL = 1024
N_MOD = 6
CHUNK_A = 128
GMLP_WIDTH = D_MODEL
GMLP_GROUPS = 4
GMLP_GROUP_W = GMLP_WIDTH // GMLP_GROUPS
GLA_HEADS = 4
GLA_DK = D_MODEL // 2
GLA_DV = D_MODEL
GLA_DK_HEAD = GLA_DK // GLA_HEADS
GLA_DV_HEAD = GLA_DV // GLA_HEADS
GLA_GATE_RANK = 16
GLA_TAU = 16.0
GLA_CHUNK = 64
GLA_FAST_CHUNK = 128
GLA_TRIL = max(GLA_CHUNK, GLA_FAST_CHUNK)
GLA_SAFE_EXPONENT = 80.0
GLA_QKVG = 2 * GLA_DK + 2 * GLA_DV
GLA_GATE_PAD = GLA_GATE_RANK
GLA_IN_PAD = GLA_QKVG + GLA_GATE_PAD
D_FF = 4 * D_MODEL
EPS = 1e-6

LANES = 128
SUBLANES = 8
BF16_SUBLANES = 16
ADA_ROWS = 256
FFN_CHUNK = 1024
FFN_SUBTILE = 1024
GMLP_SUBTILE = 512
GLA_SUBTILE = 512
VMEM_LIMIT = 56 * 1024 * 1024
WEIGHT_LOAD_CHUNKS = 32
WEIGHT_STAGE_SLOTS = 4


def _rms(x, g):
    return x * lax.rsqrt(jnp.mean(x * x, axis=-1, keepdims=True) + EPS) * g


def _modulated_norm(x, g, scale, shift):
    inv = lax.rsqrt(jnp.mean(x * x, axis=-1, keepdims=True) + EPS)
    return (x * inv) * (g * (1.0 + scale)) + shift


def _dot(a, b):
    return jnp.dot(a.astype(BF16), b.astype(BF16), preferred_element_type=F32)


def _dot_nt(a, b):
    return lax.dot_general(a.astype(BF16), b.astype(BF16), (((1,), (1,)), ((), ())),
                           preferred_element_type=F32)


def _dot_tn(a, b):
    return lax.dot_general(a.astype(BF16), b.astype(BF16), (((0,), (0,)), ((), ())),
                           preferred_element_type=F32)


def _gelu_tanh(x):
    c1 = -2.0 * 0.7978845608028654 * 1.4426950408889634
    c2 = c1 * 0.044715
    return x / (1.0 + jnp.exp2(x * (c1 + c2 * (x * x))))


def _silu(x):
    return x * jax.nn.sigmoid(x)


def _log_sigmoid(x):
    return -(jnp.maximum(-x, 0.0) + jnp.log1p(jnp.exp(-jnp.abs(x))))


def _bf16_terms(x, n):
    terms = []
    for _ in range(n - 1):
        t = x.astype(BF16)
        terms.append(t)
        x = x - t.astype(F32)
    terms.append(x.astype(BF16))
    return terms


def _ada_kernel(c_ref, wa_ref, wb_ref, b_ref, op_ref, os_ref, s_scr, acc_scr, *, n_prompt, tk):
    k = pl.program_id(1)

    @pl.when((pl.program_id(0) == 0) & (k == 0))
    def _():
        for kk in range(s_scr.shape[0]):
            s_scr[kk] = _silu(c_ref[:, kk * tk:(kk + 1) * tk]).astype(BF16)

    @pl.when(k == 0)
    def _():
        acc_scr[...] = jnp.broadcast_to(b_ref[...], acc_scr.shape)

    s = s_scr[k]
    half = N_MOD // 2
    for j in range(N_MOD):
        cols = slice(j * D_MODEL, (j + 1) * D_MODEL)
        w_ref, jj = (wa_ref, j) if j < half else (wb_ref, j - half)
        w = w_ref[:, jj * D_MODEL:(jj + 1) * D_MODEL].astype(BF16)
        acc_scr[:, cols] += jnp.dot(s, w, preferred_element_type=F32)

    @pl.when(k == pl.num_programs(1) - 1)
    def _():
        op_ref[...] = acc_scr[0:n_prompt, :]
        os_ref[...] = acc_scr[n_prompt:n_prompt + os_ref.shape[0], :]


def _ada_call(c_prompt, c_sample, ada_w, ada_b):
    depth = ada_w.shape[0]
    n_prompt, n_seq = c_prompt.shape[0], c_sample.shape[0]
    d = D_MODEL
    pad = -(n_prompt + n_seq) % BF16_SUBLANES
    c_all = jnp.concatenate([c_prompt, c_sample, jnp.zeros((pad, d), F32)], axis=0)
    rows = c_all.shape[0]
    tk = ADA_ROWS
    width = N_MOD * d
    return pl.pallas_call(
        functools.partial(_ada_kernel, n_prompt=n_prompt, tk=tk),
        grid=(depth, d // tk),
        in_specs=[
            pl.BlockSpec((rows, d), lambda l, k: (0, 0)),
            pl.BlockSpec((None, tk, width // 2), lambda l, k: (l, k, 0)),
            pl.BlockSpec((None, tk, width // 2), lambda l, k: (l, k, 1)),
            pl.BlockSpec((None, 1, width), lambda l, k: (l, 0, 0)),
        ],
        out_specs=[
            pl.BlockSpec((None, n_prompt, width), lambda l, k: (l, 0, 0)),
            pl.BlockSpec((None, n_seq, width), lambda l, k: (l, 0, 0)),
        ],
        out_shape=[
            jax.ShapeDtypeStruct((depth, n_prompt, width), F32),
            jax.ShapeDtypeStruct((depth, n_seq, width), F32),
        ],
        scratch_shapes=[pltpu.VMEM((d // tk, rows, tk), BF16),
                        pltpu.VMEM((rows, width), F32)],
        compiler_params=pltpu.CompilerParams(
            dimension_semantics=("arbitrary", "arbitrary"), vmem_limit_bytes=VMEM_LIMIT),
        name="adaln_mod",
    )(c_all, ada_w, ada_w, ada_b.reshape(depth, 1, N_MOD * d))


def _const_spec(shape):
    zeros = (0,) * len(shape)
    return pl.BlockSpec(shape, lambda i: zeros, pipeline_mode=pl.Buffered(1))


def _layer_spec(shape, layer):
    zeros = (0,) * len(shape)
    return pl.BlockSpec((None,) + shape, lambda i: (layer,) + zeros,
                        pipeline_mode=pl.Buffered(1))


def _mod_specs(mod, layer, first_chunk, tm, rows_per_seq):
    d = D_MODEL
    specs = []
    for j in range(first_chunk, first_chunk + 3):
        if rows_per_seq < tm:
            specs.append(pl.BlockSpec((None, tm // rows_per_seq, d), lambda i, j=j: (layer, i, j)))
        else:
            specs.append(pl.BlockSpec((None, mod.shape[1], d), lambda i, j=j: (layer, 0, j)))
    return specs


def _token_copy_matrix(tm, rows_per_seq):
    copy = np.repeat(np.eye(tm // rows_per_seq, dtype=np.float32), rows_per_seq, axis=0)
    return jnp.asarray(np.tile(copy, (1, 2)), BF16)


def _rows_to_tokens(e_ref, mod_ref):
    terms = jnp.concatenate(_bf16_terms(mod_ref[...], 2), axis=0)
    return jnp.dot(e_ref[...], terms, preferred_element_type=F32)


def _mod_rows(ref, rows):
    return ref[...] if ref.shape[0] == 1 else ref[rows, :]


def _row_spec(tm, width):
    return pl.BlockSpec((tm, width), lambda i: (i, 0))


class _TwoGroupGrid:
    def __init__(self, n_prompt_rows, n_sample_rows, tm_p, tm_s, seq_len, sample_seq_len):
        self.tm_p, self.tm_s = tm_p, tm_s
        self.sample_seq_len = sample_seq_len
        self.p_steps = n_prompt_rows // tm_p
        self.s_steps = n_sample_rows // tm_s
        self.tiles_per_seq = seq_len // tm_p
        self.n_prompt_seqs = n_prompt_rows // seq_len
        self.steps = self.p_steps + self.s_steps

    def p_idx(self, i):
        return jnp.minimum(i, self.p_steps - 1)

    def s_idx(self, i):
        return jnp.maximum(i - self.p_steps, 0)

    def rows_p(self, width):
        return pl.BlockSpec((self.tm_p, width), lambda i: (self.p_idx(i), 0))

    def rows_s(self, width, col_block=0):
        return pl.BlockSpec((self.tm_s, width), lambda i: (self.s_idx(i), col_block))

    def mods_p(self, layer, first_chunk):
        return [pl.BlockSpec((None, self.n_prompt_seqs, D_MODEL), lambda i, j=j: (layer, 0, j))
                for j in range(first_chunk, first_chunk + 3)]

    def prompt_row(self, ref, step):
        return ref[pl.ds(step // self.tiles_per_seq, 1), :]

    def mods_s(self, layer, first_chunk):
        return [pl.BlockSpec((None, self.tm_s // self.sample_seq_len, D_MODEL),
                             lambda i, j=j: (layer, self.s_idx(i), j))
                for j in range(first_chunk, first_chunk + 3)]

    def seqs_s(self, width):
        seq = self.sample_seq_len
        return pl.BlockSpec((self.tm_s // seq, seq, width), lambda i: (self.s_idx(i), 0, 0))

    def token_copy_spec(self):
        return _const_spec((self.tm_s, 2 * self.tm_s // self.sample_seq_len))

    def token_copy_matrix(self):
        return _token_copy_matrix(self.tm_s, self.sample_seq_len)


def _gmlp_rows(x_ref, sh_ref, sc_ref, gt_ref, mix_ref, bs_ref, xo_ref, v_ref, z_scr, w, tm, sub):
    ng_ref, win_ref, bin_ref, lng_ref, lnb_ref, wout_ref, bout_ref = w
    sub = min(sub, tm)
    groups = [slice(r0, r0 + sub) for r0 in range(0, tm, sub)]

    def project(gi):
        rows = groups[gi]
        h = _modulated_norm(x_ref[rows, :], ng_ref[...], _mod_rows(sc_ref, rows),
                            _mod_rows(sh_ref, rows))
        z_scr[gi % 2, 0:sub, :] = _dot(h, win_ref[...])

    project(0)
    for gi, rows in enumerate(groups):
        if gi + 1 < len(groups):
            project(gi + 1)
        z = _gelu_tanh(z_scr[gi % 2, 0:sub, :] + bin_ref[...])
        u = z[:, :GMLP_WIDTH]
        v = z[:, GMLP_WIDTH:]
        mu = jnp.mean(v, axis=-1, keepdims=True)
        vc = v - mu
        var = jnp.mean(vc * vc, axis=-1, keepdims=True)
        v = vc * lax.rsqrt(var + EPS) * lng_ref[...] + lnb_ref[...]
        if v_ref is not None:
            v_ref[rows, :] = v
        vb = v.astype(BF16)
        mixed = []
        for c in range(sub // CHUNK_A):
            cols = []
            for g in range(GMLP_GROUPS):
                blk = vb[c * CHUNK_A:(c + 1) * CHUNK_A, g * GMLP_GROUP_W:(g + 1) * GMLP_GROUP_W]
                cols.append(jnp.dot(mix_ref[g], blk, preferred_element_type=F32))
            mixed.append(jnp.concatenate(cols, axis=1) + bs_ref[...])
        s = jnp.concatenate(mixed, axis=0)
        y = _dot(u * s, wout_ref[...]) + bout_ref[...]
        xo_ref[rows, :] = x_ref[rows, :] + _mod_rows(gt_ref, rows) * y


def _rows_to_seq_layout(rows_ref, seq_ref, slab_scr):
    n_seq, seq, width = seq_ref.shape
    for j in range(width // LANES):
        lanes = slice(j * LANES, (j + 1) * LANES)
        slab_scr[...] = rows_ref[:, lanes]
        for t in range(seq):
            seq_ref[:, t, lanes] = slab_scr[pl.ds(t, n_seq, stride=seq), :]


def _seq_layout_to_rows(seq_ref, rows_ref, slab_scr):
    n_seq, seq, width = seq_ref.shape
    for j in range(width // LANES):
        lanes = slice(j * LANES, (j + 1) * LANES)
        for t in range(seq):
            slab_scr[pl.ds(t, n_seq, stride=seq), :] = seq_ref[:, t, lanes]
        rows_ref[:, lanes] = slab_scr[...]


def _gmlp_kernel(xp_ref, shp_ref, scp_ref, gtp_ref, xs_ref, shs_ref, scs_ref, gts_ref, e_ref,
                 ng_ref, win_hbm, bin_ref, lng_ref, lnb_ref, wout_hbm, bout_ref,
                 mixp_ref, bsp_ref, mixs_ref, bss_ref, op_ref, os_ref, vs_ref, z_scr, v_scr,
                 x_scr, slab_scr,
                 win_ref, wout_ref, stage_in, stage_out, sems_in, sems_out, *, grid):
    w = (ng_ref, win_ref, bin_ref, lng_ref, lnb_ref, wout_ref, bout_ref)
    step = pl.program_id(0)

    @pl.when(step == 0)
    def _():
        _load_as_bf16([(win_hbm, win_ref, stage_in, sems_in),
                       (wout_hbm, wout_ref, stage_out, sems_out)])

    @pl.when(step < grid.p_steps)
    def _():
        sh, sc, gt = (grid.prompt_row(r, step) for r in (shp_ref, scp_ref, gtp_ref))
        _gmlp_rows(xp_ref, sh, sc, gt, mixp_ref, bsp_ref, op_ref, None, z_scr, w,
                   grid.tm_p, GMLP_SUBTILE)

    @pl.when(step >= grid.p_steps)
    def _():
        sh, sc, gt = (_rows_to_tokens(e_ref, r) for r in (shs_ref, scs_ref, gts_ref))
        _seq_layout_to_rows(xs_ref, x_scr, slab_scr)
        _gmlp_rows(x_scr, sh, sc, gt, mixs_ref, bss_ref, os_ref, v_scr, z_scr, w,
                   grid.tm_s, GMLP_SUBTILE)
        _rows_to_seq_layout(v_scr, vs_ref, slab_scr)


def _gmlp_call(xp, xs_seq, mod_p, mod_s, grid, w):
    d = D_MODEL
    n_sample = xs_seq.shape[0] * xs_seq.shape[1]
    mix_spec = _const_spec((GMLP_GROUPS, CHUNK_A, CHUNK_A))
    bias_spec = _const_spec((CHUNK_A, GMLP_WIDTH))
    in_specs = ([grid.rows_p(d)] + grid.mods_p(0, 0) + [grid.seqs_s(d)] + grid.mods_s(0, 0) + [
        grid.token_copy_spec(),
        _const_spec((1, d)),
        pl.BlockSpec(memory_space=pl.ANY),
        _const_spec((1, 2 * GMLP_WIDTH)),
        _const_spec((1, GMLP_WIDTH)),
        _const_spec((1, GMLP_WIDTH)),
        pl.BlockSpec(memory_space=pl.ANY),
        _const_spec((1, d)),
        mix_spec, bias_spec, mix_spec, bias_spec,
    ])
    return pl.pallas_call(
        functools.partial(_gmlp_kernel, grid=grid),
        grid=(grid.steps,),
        in_specs=in_specs,
        out_specs=[grid.rows_p(d), grid.rows_s(d), grid.seqs_s(GMLP_WIDTH)],
        out_shape=[jax.ShapeDtypeStruct(xp.shape, F32),
                   jax.ShapeDtypeStruct((n_sample, d), F32),
                   jax.ShapeDtypeStruct(xs_seq.shape[:2] + (GMLP_WIDTH,), F32)],
        scratch_shapes=[pltpu.VMEM((2, GMLP_SUBTILE, 2 * GMLP_WIDTH), F32),
                        pltpu.VMEM((grid.tm_s, GMLP_WIDTH), F32),
                        pltpu.VMEM((grid.tm_s, d), F32),
                        pltpu.VMEM((grid.tm_s, LANES), F32),
                        pltpu.VMEM((d, 2 * GMLP_WIDTH), BF16),
                        pltpu.VMEM((GMLP_WIDTH, d), BF16),
                        pltpu.VMEM((WEIGHT_STAGE_SLOTS, d // WEIGHT_LOAD_CHUNKS, 2 * GMLP_WIDTH),
                                   F32),
                        pltpu.VMEM((WEIGHT_STAGE_SLOTS, GMLP_WIDTH // WEIGHT_LOAD_CHUNKS, d), F32),
                        pltpu.SemaphoreType.DMA((WEIGHT_STAGE_SLOTS,)),
                        pltpu.SemaphoreType.DMA((WEIGHT_STAGE_SLOTS,))],
        compiler_params=pltpu.CompilerParams(
            dimension_semantics=("arbitrary",), vmem_limit_bytes=VMEM_LIMIT),
        name="gmlp_mixer",
    )(xp, mod_p, mod_p, mod_p, xs_seq, mod_s, mod_s, mod_s, grid.token_copy_matrix(),
      w["norm_g"], w["w_in"], w["b_in"],
      w["ln_g"], w["ln_b"], w["w_out"], w["b_out"], w["mix_p"], w["bias_p"], w["mix_s"],
      w["bias_s"])


def _ffn_rows(x_ref, sh_ref, sc_ref, gt_ref, o_ref, h_scr, w, final_norm, tm):
    ng_ref, w1_ref, b1_ref, w2_ref, b2_ref, fg_ref = w
    sub = min(FFN_SUBTILE, tm)
    groups = [slice(r0, r0 + sub) for r0 in range(0, tm, sub)]

    def normalise(gi):
        rows = groups[gi]
        h_scr[gi % 2, 0:sub, :] = _modulated_norm(
            x_ref[rows, :], ng_ref[...], _mod_rows(sc_ref, rows),
            _mod_rows(sh_ref, rows)).astype(BF16)

    normalise(0)
    for gi, rows in enumerate(groups):
        if gi + 1 < len(groups):
            normalise(gi + 1)
        h = h_scr[gi % 2, 0:sub, :]
        acc = jnp.zeros((sub, D_MODEL), F32)
        for j in range(D_FF // FFN_CHUNK):
            cols = slice(j * FFN_CHUNK, (j + 1) * FFN_CHUNK)
            a = jnp.dot(h, w1_ref[:, cols], preferred_element_type=F32) + b1_ref[:, cols]
            r = jnp.square(jnp.maximum(a, 0.0)).astype(BF16)
            acc = acc + jnp.dot(r, w2_ref[cols, :], preferred_element_type=F32)
        y = x_ref[rows, :] + _mod_rows(gt_ref, rows) * (acc + b2_ref[...])
        if final_norm:
            y = _rms(y, fg_ref[...])
        o_ref[rows, :] = y


def _load_as_bf16(jobs):
    def chunk_copy(job, c):
        src_hbm, _, stage_ref, sem_ref = job
        slot = c % stage_ref.shape[0]
        rows = pl.ds(c * stage_ref.shape[1], stage_ref.shape[1])
        return pltpu.make_async_copy(src_hbm.at[rows, :], stage_ref.at[slot], sem_ref.at[slot])

    n_chunks = {job[1].shape[0] // job[2].shape[1] for job in jobs}
    n_slots = {job[2].shape[0] for job in jobs}
    assert len(n_chunks) == 1 and len(n_slots) == 1
    n_chunks, n_slots = n_chunks.pop(), n_slots.pop()
    for c in range(min(n_slots - 1, n_chunks)):
        for job in jobs:
            chunk_copy(job, c).start()
    for c in range(n_chunks):
        for job in jobs:
            if c + n_slots - 1 < n_chunks:
                chunk_copy(job, c + n_slots - 1).start()
        for job in jobs:
            _, dst_ref, stage_ref, _ = job
            chunk_copy(job, c).wait()
            rows = stage_ref.shape[1]
            dst_ref[c * rows:(c + 1) * rows, :] = stage_ref[c % n_slots].astype(BF16)


def _ffn_kernel(xp_ref, shp_ref, scp_ref, gtp_ref, xs_ref, shs_ref, scs_ref, gts_ref, e_ref,
                ng_ref, w1_hbm, b1_ref, w2_hbm, b2_ref, fg_ref, op_ref, os_ref, h_scr,
                w1_ref, w2_ref, stage1, stage2, sems1, sems2, ys_scr, slab_scr,
                *, final_norm, grid, layer):
    w = (ng_ref, w1_ref, b1_ref, w2_ref, b2_ref, fg_ref)
    step = pl.program_id(0)

    @pl.when(step == 0)
    def _():
        _load_as_bf16([(w1_hbm.at[layer], w1_ref, stage1, sems1),
                       (w2_hbm.at[layer], w2_ref, stage2, sems2)])

    @pl.when(step < grid.p_steps)
    def _():
        sh, sc, gt = (grid.prompt_row(r, step) for r in (shp_ref, scp_ref, gtp_ref))
        _ffn_rows(xp_ref, sh, sc, gt, op_ref, h_scr, w, final_norm, grid.tm_p)

    @pl.when(step >= grid.p_steps)
    def _():
        sh, sc, gt = (_rows_to_tokens(e_ref, r) for r in (shs_ref, scs_ref, gts_ref))
        if final_norm:
            _ffn_rows(xs_ref, sh, sc, gt, ys_scr, h_scr, w, final_norm, grid.tm_s)
            _rows_to_seq_layout(ys_scr, os_ref, slab_scr)
        else:
            _ffn_rows(xs_ref, sh, sc, gt, os_ref, h_scr, w, final_norm, grid.tm_s)


def _ffn_call(xp, xs, mod_p, mod_s, layer, grid, w, final_norm):
    d = D_MODEL
    seq_shape = (xs.shape[0] // grid.sample_seq_len, grid.sample_seq_len, d)
    in_specs = ([grid.rows_p(d)] + grid.mods_p(layer, 3) + [grid.rows_s(d)]
                + grid.mods_s(layer, 3) + [
        grid.token_copy_spec(),
        _layer_spec((1, d), layer),
        pl.BlockSpec(memory_space=pl.ANY),
        _layer_spec((1, D_FF), layer),
        pl.BlockSpec(memory_space=pl.ANY),
        _layer_spec((1, d), layer),
        _const_spec((1, d)),
    ])
    return pl.pallas_call(
        functools.partial(_ffn_kernel, final_norm=final_norm, grid=grid, layer=layer),
        grid=(grid.steps,),
        in_specs=in_specs,
        out_specs=[grid.rows_p(d), grid.seqs_s(d) if final_norm else grid.rows_s(d)],
        out_shape=[jax.ShapeDtypeStruct(xp.shape, F32),
                   jax.ShapeDtypeStruct(seq_shape if final_norm else xs.shape, F32)],
        scratch_shapes=[pltpu.VMEM((2, FFN_SUBTILE, d), BF16),
                        pltpu.VMEM((d, D_FF), BF16),
                        pltpu.VMEM((D_FF, d), BF16),
                        pltpu.VMEM((WEIGHT_STAGE_SLOTS, d // WEIGHT_LOAD_CHUNKS, D_FF), F32),
                        pltpu.VMEM((WEIGHT_STAGE_SLOTS, D_FF // WEIGHT_LOAD_CHUNKS, d), F32),
                        pltpu.SemaphoreType.DMA((WEIGHT_STAGE_SLOTS,)),
                        pltpu.SemaphoreType.DMA((WEIGHT_STAGE_SLOTS,)),
                        pltpu.VMEM((grid.tm_s, d), F32),
                        pltpu.VMEM((grid.tm_s, LANES), F32)],
        compiler_params=pltpu.CompilerParams(
            dimension_semantics=("arbitrary",), vmem_limit_bytes=VMEM_LIMIT),
        name="sqrelu_mlp",
    )(xp, mod_p, mod_p, mod_p, xs, mod_s, mod_s, mod_s, grid.token_copy_matrix(),
      w["norm_g"], w["w1"], w["b1"], w["w2"],
      w["b2"], w["final_g"])


def _gla_project(x, ng, scale, shift, win, wg2, bg):
    h = _modulated_norm(x, ng, scale, shift)
    proj = _dot(h, win)
    gate = _dot(proj[:, GLA_QKVG:], wg2) + bg
    log_a = _log_sigmoid(gate) * (1.0 / GLA_TAU)
    return proj[:, :GLA_QKVG], log_a


def _gla_output(o, gate_in, x, gt, og, wout):
    parts = []
    for h in range(GLA_HEADS):
        cols = slice(h * GLA_DV_HEAD, (h + 1) * GLA_DV_HEAD)
        parts.append(_rms(o[:, cols], og))
    on = jnp.concatenate(parts, axis=1)
    y = _dot(on * _silu(gate_in), wout)
    return x + gt * y


def _as_column(row):
    return jnp.broadcast_to(row, (SUBLANES, row.shape[1])).T[:, 0:1]


def _chunk_cumsum(tril, log_a):
    head = log_a.astype(BF16)
    rest = (log_a - head.astype(F32)).astype(BF16)
    return (jnp.dot(tril, head, preferred_element_type=F32)
            + jnp.dot(tril, rest, preferred_element_type=F32))


def _level_masks(c):
    rr = lax.broadcasted_iota(jnp.int32, (c, c), 0)
    cc = lax.broadcasted_iota(jnp.int32, (c, c), 1)
    masks = []
    s = c
    while s > SUBLANES:
        half = s // 2
        same = (rr ^ cc) < s
        masks.append(same & ((rr & half) != 0) & ((cc & half) == 0))
        s = half
    return masks, rr, cc


def _block_rows(x, s, r):
    c, w = x.shape
    return jnp.concatenate(
        [jnp.broadcast_to(x[p * s + r:p * s + r + 1, :], (s, w)) for p in range(c // s)], axis=0)


def _diag_terms(q, k, a):
    c, w = q.shape
    ii = lax.broadcasted_iota(jnp.int32, (c, w), 0) & (SUBLANES - 1)
    decay = jnp.zeros((c, w), F32)
    terms = [None] * SUBLANES
    for j in range(SUBLANES - 1, -1, -1):
        if j < SUBLANES - 1:
            decay = decay * _block_rows(a, SUBLANES, j + 1)
        decay = jnp.where(ii == j, 1.0, decay)
        terms[j] = q * _block_rows(k, SUBLANES, j) * decay
    return jnp.concatenate(terms, axis=0)


def _gla_chunk_pair(q2, k2, v2, la2, b2, st_refs, ones_bd, masks, rr, cc):
    c = q2[0].shape[0]
    zs = []
    for q, k, la in zip(q2, k2, la2):
        zs.append(_diag_terms(q, k, jnp.exp(la)))
    sums = jnp.dot(jnp.concatenate(zs, axis=1).astype(BF16), ones_bd,
                   preferred_element_type=F32)
    outs = []
    for hh, (q, k, v, b, st_ref) in enumerate(zip(q2, k2, v2, b2, st_refs)):
        dk = q.shape[1]
        attn = jnp.zeros((c, c), F32)
        base = rr & ~(SUBLANES - 1)
        for j in range(SUBLANES):
            rj = sums[j * c:(j + 1) * c, hh * dk:hh * dk + c]
            attn = jnp.where(cc == base + j, rj, attn)
        s = c
        for mask in masks:
            half = s // 2
            e = jnp.exp(-jnp.abs(b - _block_rows(b, s, half)))
            attn = jnp.where(mask, _dot_nt(q * e, k * e), attn)
            s = half
        b_last = b[c - 1:c, :]
        st = st_ref[...]
        o = _dot(q * jnp.exp(b), st) + _dot(attn, v)
        kd = k * jnp.exp(b_last - b)
        st_ref[...] = _as_column(jnp.exp(b_last)) * st + _dot_tn(kd, v)
        outs.append(o)
    return outs


def _gla_prompt_kernel(x_ref, sh_ref, sc_ref, gt_ref, ng_ref, win_ref, wg2_ref, bg_ref,
                       og_ref, wout_ref, tril_ref, ones_ref, xo_ref, s_ref,
                       proj_scr, ga_scr, la_scr, b_scr, o_scr, st_scr, *, tm, sub, tiles_per_seq):
    tile = pl.program_id(0) % tiles_per_seq
    seq_row = pl.ds(pl.program_id(0) // tiles_per_seq, 1)
    shift, scale, gate_mod = (r[seq_row, :] for r in (sh_ref, sc_ref, gt_ref))

    @pl.when(tile == 0)
    def _():
        st_scr[...] = jnp.zeros(st_scr.shape, F32)

    dk, dv = GLA_DK_HEAD, GLA_DV_HEAD
    qscale = GLA_DK_HEAD ** -0.5
    cf = GLA_FAST_CHUNK
    c = GLA_CHUNK
    groups = [slice(r0, r0 + sub) for r0 in range(0, tm, sub)]

    def project(rows):
        h = _modulated_norm(x_ref[rows, :], ng_ref[...], scale, shift)
        proj = _dot(h, win_ref[...])
        proj_scr[rows, :] = proj[:, :GLA_QKVG]
        ga_scr[rows, :] = proj[:, GLA_QKVG:]

    def decays(rows):
        gate = _dot(ga_scr[rows, :], wg2_ref[...]) + bg_ref[...]
        log_a = _log_sigmoid(gate) * (1.0 / GLA_TAU)
        la_scr[rows, :] = log_a
        for ci in range(sub // cf):
            b_scr[rows.start + ci * cf:rows.start + (ci + 1) * cf, :] = _chunk_cumsum(
                tril_ref[0:cf, 0:cf], log_a[ci * cf:(ci + 1) * cf, :])
        return jnp.min(b_scr[rows, :]) >= -GLA_SAFE_EXPONENT

    def head_slices(rows, h):
        q = proj_scr[rows, h * dk:(h + 1) * dk] * qscale
        k = proj_scr[rows, GLA_DK + h * dk:GLA_DK + (h + 1) * dk]
        v = proj_scr[rows, 2 * GLA_DK + h * dv:2 * GLA_DK + (h + 1) * dv]
        return q, k, v

    def single_ref_rows(group):
        rr = lax.broadcasted_iota(jnp.int32, (cf, cf), 0)
        cc = lax.broadcasted_iota(jnp.int32, (cf, cf), 1)
        causal = rr >= cc
        chunks = [slice(r0, r0 + cf) for r0 in range(group.start, group.stop, cf)]
        for h in range(GLA_HEADS):
            qes, vbs, scores, kvs, decays = [], [], [], [], []
            for rows in chunks:
                q, k, v = head_slices(rows, h)
                b = b_scr[rows, h * dk:(h + 1) * dk]
                decay_last = jnp.exp(b[cf - 1:cf, :])
                qe = (q * jnp.exp(b)).astype(BF16)
                kt = k * jnp.exp(-b)
                vb = v.astype(BF16)
                scores.append(_dot_nt(qe, kt))
                kvs.append(_dot_tn(kt * decay_last, vb))
                qes.append(qe)
                vbs.append(vb)
                decays.append(_as_column(decay_last))
            st = st_scr[h]
            states = []
            for kv, decay_col in zip(kvs, decays):
                states.append(st)
                st = decay_col * st + kv
            st_scr[h] = st
            for rows, qe, vb, sc, st_in in zip(chunks, qes, vbs, scores, states):
                attn = jnp.where(causal, sc, 0.0).astype(BF16)
                lhs = jnp.concatenate([qe, attn], axis=1)
                rhs = jnp.concatenate([st_in.astype(BF16), vb], axis=0)
                o_scr[rows, h * dv:(h + 1) * dv] = jnp.dot(lhs, rhs, preferred_element_type=F32)

    def robust_rows(group):
        def robust_body(ci, carry):
            rows = pl.ds(pl.multiple_of(group.start + ci * c, c), c)
            masks, rr, cc = _level_masks(c)
            b_c = _chunk_cumsum(tril_ref[0:c, 0:c], la_scr[rows, :])
            outs = []
            for h0 in range(0, GLA_HEADS, 2):
                heads = (h0, h0 + 1)
                qkv = [head_slices(rows, h) for h in heads]
                la2 = [la_scr[rows, h * dk:(h + 1) * dk] for h in heads]
                b2 = [b_c[:, h * dk:(h + 1) * dk] for h in heads]
                st_refs = [st_scr.at[h] for h in heads]
                outs += _gla_chunk_pair([t[0] for t in qkv], [t[1] for t in qkv],
                                        [t[2] for t in qkv], la2, b2, st_refs, ones_ref[...],
                                        masks, rr, cc)
            o_scr[rows, :] = jnp.concatenate(outs, axis=1)
            return carry

        lax.fori_loop(0, sub // c, robust_body, 0)

    def output(rows):
        gate_in = proj_scr[rows, 2 * GLA_DK + GLA_DV:]
        xo_ref[rows, :] = _gla_output(o_scr[rows, :], gate_in, x_ref[rows, :], gate_mod,
                                      og_ref[...], wout_ref[...])

    project(groups[0])
    for gi, rows in enumerate(groups):
        single_ref_safe = decays(rows)
        following = groups[gi + 1] if gi + 1 < len(groups) else None

        @pl.when(single_ref_safe)
        def _():
            if following is not None:
                project(following)
            single_ref_rows(rows)
            output(rows)

        @pl.when(jnp.logical_not(single_ref_safe))
        def _():
            if following is not None:
                project(following)
            robust_rows(rows)
            output(rows)

    @pl.when(tile == tiles_per_seq - 1)
    def _():
        s_ref[...] = st_scr[...]


def _gla_prompt_call(x, mod, layer, n_seq, seq_len, tm, w):
    n, d = x.shape
    tiles_per_seq = seq_len // tm
    dk, dv = GLA_DK_HEAD, GLA_DV_HEAD
    in_specs = [_row_spec(tm, d)] + _mod_specs(mod, layer, 0, tm, seq_len) + [
        _const_spec((1, d)),
        _const_spec((d, GLA_IN_PAD)),
        _const_spec((GLA_GATE_PAD, GLA_DK)),
        _const_spec((1, GLA_DK)),
        _const_spec((1, dv)),
        _const_spec((GLA_DV, d)),
        _const_spec((GLA_TRIL, GLA_TRIL)),
        _const_spec((2 * dk, 2 * dk)),
    ]
    out_specs = [
        _row_spec(tm, d),
        pl.BlockSpec((None, GLA_HEADS, dk, dv), lambda i: (i // tiles_per_seq, 0, 0, 0)),
    ]
    out_shape = [
        jax.ShapeDtypeStruct((n, d), F32),
        jax.ShapeDtypeStruct((n_seq, GLA_HEADS, dk, dv), F32),
    ]
    return pl.pallas_call(
        functools.partial(_gla_prompt_kernel, tm=tm, sub=GLA_SUBTILE,
                          tiles_per_seq=tiles_per_seq),
        grid=(n // tm,),
        in_specs=in_specs,
        out_specs=out_specs,
        out_shape=out_shape,
        scratch_shapes=[
            pltpu.VMEM((tm, GLA_QKVG), F32),
            pltpu.VMEM((tm, GLA_GATE_PAD), F32),
            pltpu.VMEM((tm, GLA_DK), F32),
            pltpu.VMEM((tm, GLA_DK), F32),
            pltpu.VMEM((tm, GLA_DV), F32),
            pltpu.VMEM((GLA_HEADS, dk, dv), F32),
        ],
        compiler_params=pltpu.CompilerParams(
            dimension_semantics=("arbitrary",), vmem_limit_bytes=VMEM_LIMIT),
        name="gla_mixer_prompt",
    )(x, mod, mod, mod, w["norm_g"], w["w_in"], w["w_gate2"], w["b_gate"], w["out_g"],
      w["w_out"], w["tril"], w["ones_bd"])


def _gla_proj_kernel(x_ref, sh_ref, sc_ref, e_ref, ng_ref, win_ref, wg2_ref, bg_ref, p_ref,
                     la_ref):
    proj, log_a = _gla_project(x_ref[...], ng_ref[...], _rows_to_tokens(e_ref, sc_ref),
                               _rows_to_tokens(e_ref, sh_ref), win_ref[...], wg2_ref[...],
                               bg_ref[...])
    p_ref[...] = proj
    la_ref[...] = log_a


def _gla_proj_call(x, mod, layer, tm, seq_len, w):
    n, d = x.shape
    specs = _mod_specs(mod, layer, 0, tm, seq_len)
    in_specs = [_row_spec(tm, d), specs[0], specs[1],
                _const_spec((tm, 2 * tm // seq_len)),
                _const_spec((1, d)),
                _const_spec((d, GLA_IN_PAD)),
                _const_spec((GLA_GATE_PAD, GLA_DK)),
                _const_spec((1, GLA_DK))]
    return pl.pallas_call(
        _gla_proj_kernel,
        grid=(n // tm,),
        in_specs=in_specs,
        out_specs=[_row_spec(tm, GLA_QKVG), _row_spec(tm, GLA_DK)],
        out_shape=[jax.ShapeDtypeStruct((n, GLA_QKVG), F32),
                   jax.ShapeDtypeStruct((n, GLA_DK), F32)],
        compiler_params=pltpu.CompilerParams(
            dimension_semantics=("arbitrary",), vmem_limit_bytes=VMEM_LIMIT),
        name="gla_proj_sample",
    )(x, mod, mod, _token_copy_matrix(tm, seq_len), w["norm_g"], w["w_in"], w["w_gate2"],
      w["b_gate"])


def _gla_step_kernel(p_ref, la_ref, s0_ref, o_ref, s1_ref, *, seqs, steps):
    dk, dv = GLA_DK_HEAD, GLA_DV_HEAD
    qscale = GLA_DK_HEAD ** -0.5
    per_tile = SUBLANES // steps
    row = lax.broadcasted_iota(jnp.int32, (SUBLANES, dk), 0)

    def seq_rows(tile, s):
        if s:
            tile = pltpu.roll(tile, SUBLANES - s * steps, 0)
        keep = lax.broadcasted_iota(jnp.int32, tile.shape, 0) < steps
        return jnp.where(keep, tile, 0.0)

    def tile_body(ti, carry):
        rows = pl.ds(pl.multiple_of(ti * SUBLANES, SUBLANES), SUBLANES)
        p_tile = p_ref[rows, :]
        la_tile = la_ref[rows, :]
        o_tile = jnp.zeros((SUBLANES, GLA_DV), F32)
        for s in range(per_tile):
            bi = ti * per_tile + s
            p = seq_rows(p_tile, s)
            la_all = seq_rows(la_tile, s)
            outs = []
            for h in range(GLA_HEADS):
                q = p[:, h * dk:(h + 1) * dk] * qscale
                k = p[:, GLA_DK + h * dk:GLA_DK + (h + 1) * dk]
                v = p[:, 2 * GLA_DK + h * dv:2 * GLA_DK + (h + 1) * dv]
                la = la_all[:, h * dk:(h + 1) * dk]
                b = jnp.zeros((SUBLANES, dk), F32)
                for t in range(steps):
                    b = b + jnp.where(row >= t, la[t:t + 1, :], 0.0)
                b_last = b[steps - 1:steps, :]
                s0 = s0_ref[bi, h]
                o = _dot(q * jnp.exp(b), s0)
                for j in range(steps):
                    e = jnp.exp(jnp.minimum(b - b[j:j + 1, :], 0.0))
                    z = jnp.where(row >= j, q * k[j:j + 1, :] * e, 0.0)
                    o = o + jnp.sum(z, axis=-1, keepdims=True) * v[j:j + 1, :]
                outs.append(o)
                m = jnp.where(row == steps, jnp.exp(b_last), k * jnp.exp(b_last - b))
                m_t = m.T
                s1_ref[bi, h] = m_t[:, steps:steps + 1] * s0 + _dot(m_t, v)
            o_seq = jnp.concatenate(outs, axis=1)
            o_tile = o_tile + (pltpu.roll(o_seq, s * steps, 0) if s else o_seq)
        o_ref[rows, :] = o_tile
        return carry

    lax.fori_loop(0, seqs // per_tile, tile_body, 0)


def _gla_step_call(proj, log_a, state, steps, seqs):
    n_seq = state.shape[0]
    dk, dv = GLA_DK_HEAD, GLA_DV_HEAD
    rows = lambda width: pl.BlockSpec((seqs * steps, width), lambda i: (i, 0))
    st_spec = pl.BlockSpec((seqs, GLA_HEADS, dk, dv), lambda i: (i, 0, 0, 0))
    return pl.pallas_call(
        functools.partial(_gla_step_kernel, seqs=seqs, steps=steps),
        grid=(n_seq // seqs,),
        in_specs=[rows(GLA_QKVG), rows(GLA_DK), st_spec],
        out_specs=[rows(GLA_DV), st_spec],
        out_shape=[jax.ShapeDtypeStruct((n_seq * steps, GLA_DV), F32),
                   jax.ShapeDtypeStruct(state.shape, F32)],
        compiler_params=pltpu.CompilerParams(
            dimension_semantics=("arbitrary",), vmem_limit_bytes=VMEM_LIMIT),
        name="gla_step_sample",
    )(proj, log_a, state)


def _gla_out_kernel(o_ref, p_ref, x_ref, gt_ref, e_ref, og_ref, wout_ref, xo_ref):
    xo_ref[...] = _gla_output(o_ref[...], p_ref[...], x_ref[...], _rows_to_tokens(e_ref, gt_ref),
                              og_ref[...], wout_ref[...])


def _gla_out_call(o, proj, x, mod, layer, tm, seq_len, w):
    n, d = x.shape
    gate_spec = _mod_specs(mod, layer, 0, tm, seq_len)[2]
    g_block = (2 * GLA_DK + GLA_DV) // GLA_DV
    in_specs = [_row_spec(tm, GLA_DV),
                pl.BlockSpec((tm, GLA_DV), lambda i: (i, g_block)),
                _row_spec(tm, d), gate_spec,
                _const_spec((tm, 2 * tm // seq_len)),
                _const_spec((1, GLA_DV_HEAD)),
                _const_spec((GLA_DV, d))]
    return pl.pallas_call(
        _gla_out_kernel,
        grid=(n // tm,),
        in_specs=in_specs,
        out_specs=_row_spec(tm, d),
        out_shape=jax.ShapeDtypeStruct((n, d), F32),
        compiler_params=pltpu.CompilerParams(
            dimension_semantics=("arbitrary",), vmem_limit_bytes=VMEM_LIMIT),
        name="gla_out_sample",
    )(o, proj, x, mod, _token_copy_matrix(tm, seq_len), w["out_g"], w["w_out"])


PROMPT_FFN_TILE = 1024
PROMPT_GMLP_TILE = 1024
PROMPT_GLA_TILE = 1024
SAMPLE_TILE = 256
SAMPLE_SEQS_PER_STEP = 16


def kernel(x_prompt, x_sample, c_prompt, c_sample, state_gla, ada_w, ada_b, norm_mix_g, norm_ffn_g,
           ffn_w1, ffn_b1, ffn_w2, ffn_b2, gmlp_w_in, gmlp_b_in, gmlp_ln_g, gmlp_ln_b, gmlp_w_s,
           gmlp_b_s, gmlp_w_out, gmlp_b_out, gla_w_in, gla_w_gate2, gla_b_gate, gla_norm_g,
           gla_w_out, final_norm_g):
    n_seq_p, seq_p, d = x_prompt.shape
    n_seq_s, seq_s, _ = x_sample.shape
    assert d == D_MODEL and GMLP_SUBTILE % CHUNK_A == 0 and GLA_SUBTILE % GLA_FAST_CHUNK == 0
    assert all(seq_p % t == 0 for t in (PROMPT_FFN_TILE, PROMPT_GMLP_TILE, PROMPT_GLA_TILE))
    assert seq_s < SUBLANES and SUBLANES % seq_s == 0 and (n_seq_s * seq_s) % SAMPLE_TILE == 0
    row = lambda a: a.reshape(1, -1)

    mod_p, mod_s = _ada_call(c_prompt, c_sample, ada_w, ada_b)

    causal = np.tril(np.ones((CHUNK_A, CHUNK_A), dtype=bool))
    ws = jnp.where(causal[None], gmlp_w_s, jnp.zeros_like(gmlp_w_s))
    rows = np.arange(CHUNK_A)
    pick = (rows[:, None] % seq_s == np.arange(seq_s)[None, :]).astype(np.float32)
    same_seq = (rows[:, None] // seq_s) == (rows[None, :] // seq_s)
    exact = lax.Precision.HIGHEST
    ws_s = jnp.where(same_seq[None], jnp.einsum("ri,gij,cj->grc", pick, ws[:, :seq_s, :seq_s], pick,
                                                precision=exact), 0.0)
    bias_p = jnp.repeat(gmlp_b_s.T, GMLP_GROUP_W, axis=1)
    bias_s = jnp.dot(pick, bias_p[:seq_s], precision=exact)

    gmlp_w = dict(norm_g=row(norm_mix_g[0]), w_in=gmlp_w_in, b_in=row(gmlp_b_in),
                  ln_g=row(gmlp_ln_g), ln_b=row(gmlp_ln_b), w_out=gmlp_w_out,
                  b_out=row(gmlp_b_out))
    gmlp_w = dict(gmlp_w, mix_p=ws.astype(BF16), bias_p=bias_p, mix_s=ws_s.astype(BF16),
                  bias_s=bias_s)

    depth = ffn_w1.shape[0]
    ffn_w = dict(norm_g=norm_ffn_g.reshape(depth, 1, d), w1=ffn_w1,
                 b1=ffn_b1.reshape(depth, 1, D_FF), w2=ffn_w2,
                 b2=ffn_b2.reshape(depth, 1, d), final_g=row(final_norm_g))

    w_in_pad = gla_w_in.astype(BF16)
    wg2_pad = gla_w_gate2.astype(BF16)
    blk_ones = np.kron(np.eye(2, dtype=np.float32),
                       np.ones((GLA_DK_HEAD, GLA_DK_HEAD), np.float32))
    gla_w = dict(norm_g=row(norm_mix_g[1]), w_in=w_in_pad, w_gate2=wg2_pad, b_gate=row(gla_b_gate),
                 out_g=row(gla_norm_g), w_out=gla_w_out.astype(BF16),
                 tril=jnp.asarray(np.tril(np.ones((GLA_TRIL, GLA_TRIL), np.float32)), BF16),
                 ones_bd=jnp.asarray(blk_ones, BF16))

    xp = x_prompt.reshape(n_seq_p * seq_p, d)
    grid_for = lambda tm_p: _TwoGroupGrid(xp.shape[0], n_seq_s * seq_s, tm_p, SAMPLE_TILE, seq_p,
                                          seq_s)
    xp, xs, chunk_v = _gmlp_call(xp, x_sample, mod_p, mod_s, grid_for(PROMPT_GMLP_TILE), gmlp_w)
    xp, xs = _ffn_call(xp, xs, mod_p, mod_s, 0, grid_for(PROMPT_FFN_TILE), ffn_w, final_norm=False)
    xp, state_p = _gla_prompt_call(xp, mod_p, 1, n_seq_p, seq_p, PROMPT_GLA_TILE, gla_w)
    proj, log_a = _gla_proj_call(xs, mod_s, 1, SAMPLE_TILE, seq_s, gla_w)
    o, state_s = _gla_step_call(proj, log_a, state_gla, seq_s, SAMPLE_SEQS_PER_STEP)
    xs = _gla_out_call(o, proj, xs, mod_s, 1, SAMPLE_TILE, seq_s, gla_w)
    xp, xs = _ffn_call(xp, xs, mod_p, mod_s, 1, grid_for(PROMPT_FFN_TILE), ffn_w, final_norm=True)

    return (xp.reshape(x_prompt.shape), xs, state_p, state_s, chunk_v)
```

```python
import functools

import jax
import jax.numpy as jnp
import numpy as np
from jax import lax
from jax.experimental import pallas as pl
from jax.experimental.pallas import tpu as pltpu

F32 = jnp.float32
BF16 = jnp.bfloat16

D_MODEL = 1024
N_MOD = 6
CHUNK_A = 128
GMLP_WIDTH = D_MODEL
GMLP_GROUPS = 4
GMLP_GROUP_W = GMLP_WIDTH // GMLP_GROUPS
GLA_HEADS = 4
GLA_DK = D_MODEL // 2
GLA_DV = D_MODEL
GLA_DK_HEAD = GLA_DK // GLA_HEADS
GLA_DV_HEAD = GLA_DV // GLA_HEADS
GLA_GATE_RANK = 16
GLA_TAU = 16.0
GLA_CHUNK = 64
GLA_FAST_CHUNK = 128
GLA_TRIL = max(GLA_CHUNK, GLA_FAST_CHUNK)
GLA_SAFE_EXPONENT = 80.0
GLA_QKVG = 2 * GLA_DK + 2 * GLA_DV
GLA_GATE_PAD = GLA_GATE_RANK
GLA_IN_PAD = GLA_QKVG + GLA_GATE_PAD
D_FF = 4 * D_MODEL
EPS = 1e-6

LANES = 128
SUBLANES = 8
BF16_SUBLANES = 16
ADA_ROWS = 256
FFN_CHUNK = 1024
FFN_SUBTILE = 1024
GMLP_SUBTILE = 512
GLA_SUBTILE = 512
VMEM_LIMIT = 56 * 1024 * 1024
WEIGHT_LOAD_CHUNKS = 64
WEIGHT_STAGE_SLOTS = 4


def _rms(x, g):
    return x * lax.rsqrt(jnp.mean(x * x, axis=-1, keepdims=True) + EPS) * g


def _modulated_norm(x, g, scale, shift):
    inv = lax.rsqrt(jnp.mean(x * x, axis=-1, keepdims=True) + EPS)
    return (x * inv) * (g * (1.0 + scale)) + shift


def _dot(a, b):
    return jnp.dot(a.astype(BF16), b.astype(BF16), preferred_element_type=F32)


def _dot_nt(a, b):
    return lax.dot_general(a.astype(BF16), b.astype(BF16), (((1,), (1,)), ((), ())),
                           preferred_element_type=F32)


def _dot_tn(a, b):
    return lax.dot_general(a.astype(BF16), b.astype(BF16), (((0,), (0,)), ((), ())),
                           preferred_element_type=F32)


def _gelu_tanh(x):
    c1 = -2.0 * 0.7978845608028654 * 1.4426950408889634
    c2 = c1 * 0.044715
    return x / (1.0 + jnp.exp2(x * (c1 + c2 * (x * x))))


def _silu(x):
    return x * jax.nn.sigmoid(x)


def _log_sigmoid(x):
    return -(jnp.maximum(-x, 0.0) + jnp.log1p(jnp.exp(-jnp.abs(x))))


def _bf16_terms(x, n):
    terms = []
    for _ in range(n - 1):
        t = x.astype(BF16)
        terms.append(t)
        x = x - t.astype(F32)
    terms.append(x.astype(BF16))
    return terms


def _ada_kernel(c_ref, wa_ref, wb_ref, b_ref, op_ref, os_ref, s_scr, acc_scr, *, n_prompt, tk):
    k = pl.program_id(1)

    @pl.when((pl.program_id(0) == 0) & (k == 0))
    def _():
        for kk in range(s_scr.shape[0]):
            s_scr[kk] = _silu(c_ref[:, kk * tk:(kk + 1) * tk]).astype(BF16)

    @pl.when(k == 0)
    def _():
        acc_scr[...] = jnp.broadcast_to(b_ref[...], acc_scr.shape)

    s = s_scr[k]
    half = N_MOD // 2
    for j in range(N_MOD):
        cols = slice(j * D_MODEL, (j + 1) * D_MODEL)
        w_ref, jj = (wa_ref, j) if j < half else (wb_ref, j - half)
        w = w_ref[:, jj * D_MODEL:(jj + 1) * D_MODEL].astype(BF16)
        acc_scr[:, cols] += jnp.dot(s, w, preferred_element_type=F32)

    @pl.when(k == pl.num_programs(1) - 1)
    def _():
        op_ref[...] = acc_scr[0:n_prompt, :]
        os_ref[...] = acc_scr[n_prompt:n_prompt + os_ref.shape[0], :]


def _ada_call(c_prompt, c_sample, ada_w, ada_b):
    depth = ada_w.shape[0]
    n_prompt, n_seq = c_prompt.shape[0], c_sample.shape[0]
    d = D_MODEL
    pad = -(n_prompt + n_seq) % BF16_SUBLANES
    c_all = jnp.concatenate([c_prompt, c_sample, jnp.zeros((pad, d), F32)], axis=0)
    rows = c_all.shape[0]
    tk = ADA_ROWS
    width = N_MOD * d
    return pl.pallas_call(
        functools.partial(_ada_kernel, n_prompt=n_prompt, tk=tk),
        grid=(depth, d // tk),
        in_specs=[
            pl.BlockSpec((rows, d), lambda l, k: (0, 0)),
            pl.BlockSpec((None, tk, width // 2), lambda l, k: (l, k, 0)),
            pl.BlockSpec((None, tk, width // 2), lambda l, k: (l, k, 1)),
            pl.BlockSpec((None, 1, width), lambda l, k: (l, 0, 0)),
        ],
        out_specs=[
            pl.BlockSpec((None, n_prompt, width), lambda l, k: (l, 0, 0)),
            pl.BlockSpec((None, n_seq, width), lambda l, k: (l, 0, 0)),
        ],
        out_shape=[
            jax.ShapeDtypeStruct((depth, n_prompt, width), F32),
            jax.ShapeDtypeStruct((depth, n_seq, width), F32),
        ],
        scratch_shapes=[pltpu.VMEM((d // tk, rows, tk), BF16),
                        pltpu.VMEM((rows, width), F32)],
        compiler_params=pltpu.CompilerParams(
            dimension_semantics=("arbitrary", "arbitrary"), vmem_limit_bytes=VMEM_LIMIT),
        name="adaln_mod",
    )(c_all, ada_w, ada_w, ada_b.reshape(depth, 1, N_MOD * d))


def _const_spec(shape):
    zeros = (0,) * len(shape)
    return pl.BlockSpec(shape, lambda i: zeros, pipeline_mode=pl.Buffered(1))


def _layer_spec(shape, layer):
    zeros = (0,) * len(shape)
    return pl.BlockSpec((None,) + shape, lambda i: (layer,) + zeros,
                        pipeline_mode=pl.Buffered(1))


def _mod_specs(mod, layer, first_chunk, tm, rows_per_seq):
    d = D_MODEL
    specs = []
    for j in range(first_chunk, first_chunk + 3):
        if rows_per_seq < tm:
            specs.append(pl.BlockSpec((None, tm // rows_per_seq, d), lambda i, j=j: (layer, i, j)))
        else:
            specs.append(pl.BlockSpec((None, mod.shape[1], d), lambda i, j=j: (layer, 0, j)))
    return specs


def _token_copy_matrix(tm, rows_per_seq):
    copy = np.repeat(np.eye(tm // rows_per_seq, dtype=np.float32), rows_per_seq, axis=0)
    return jnp.asarray(np.tile(copy, (1, 2)), BF16)


def _rows_to_tokens(e_ref, mod_ref):
    terms = jnp.concatenate(_bf16_terms(mod_ref[...], 2), axis=0)
    return jnp.dot(e_ref[...], terms, preferred_element_type=F32)


def _mod_rows(ref, rows):
    return ref[...] if ref.shape[0] == 1 else ref[rows, :]


def _row_spec(tm, width):
    return pl.BlockSpec((tm, width), lambda i: (i, 0))


class _TwoGroupGrid:
    def __init__(self, n_prompt_rows, n_sample_rows, tm_p, tm_s, seq_len, sample_seq_len):
        self.tm_p, self.tm_s = tm_p, tm_s
        self.sample_seq_len = sample_seq_len
        self.p_steps = n_prompt_rows // tm_p
        self.s_steps = n_sample_rows // tm_s
        self.tiles_per_seq = seq_len // tm_p
        self.n_prompt_seqs = n_prompt_rows // seq_len
        self.steps = self.p_steps + self.s_steps

    def p_idx(self, i):
        return jnp.minimum(i, self.p_steps - 1)

    def s_idx(self, i):
        return jnp.maximum(i - self.p_steps, 0)

    def rows_p(self, width):
        return pl.BlockSpec((self.tm_p, width), lambda i: (self.p_idx(i), 0))

    def rows_s(self, width, col_block=0):
        return pl.BlockSpec((self.tm_s, width), lambda i: (self.s_idx(i), col_block))

    def mods_p(self, layer, first_chunk):
        return [pl.BlockSpec((None, self.n_prompt_seqs, D_MODEL), lambda i, j=j: (layer, 0, j))
                for j in range(first_chunk, first_chunk + 3)]

    def prompt_row(self, ref, step):
        return ref[pl.ds(step // self.tiles_per_seq, 1), :]

    def mods_s(self, layer, first_chunk):
        return [pl.BlockSpec((None, self.tm_s // self.sample_seq_len, D_MODEL),
                             lambda i, j=j: (layer, self.s_idx(i), j))
                for j in range(first_chunk, first_chunk + 3)]

    def seqs_s(self, width):
        seq = self.sample_seq_len
        return pl.BlockSpec((self.tm_s // seq, seq, width), lambda i: (self.s_idx(i), 0, 0))

    def token_copy_spec(self):
        return _const_spec((self.tm_s, 2 * self.tm_s // self.sample_seq_len))

    def token_copy_matrix(self):
        return _token_copy_matrix(self.tm_s, self.sample_seq_len)


def _gmlp_rows(x_ref, sh_ref, sc_ref, gt_ref, mix_ref, bs_ref, xo_ref, v_ref, z_scr, w, tm, sub):
    ng_ref, win_ref, bin_ref, lng_ref, lnb_ref, wout_ref, bout_ref = w
    sub = min(sub, tm)
    groups = [slice(r0, r0 + sub) for r0 in range(0, tm, sub)]

    def project(gi):
        rows = groups[gi]
        h = _modulated_norm(x_ref[rows, :], ng_ref[...], _mod_rows(sc_ref, rows),
                            _mod_rows(sh_ref, rows))
        z_scr[gi % 2, 0:sub, :] = _dot(h, win_ref[...])

    project(0)
    for gi, rows in enumerate(groups):
        if gi + 1 < len(groups):
            project(gi + 1)
        z = _gelu_tanh(z_scr[gi % 2, 0:sub, :] + bin_ref[...])
        u = z[:, :GMLP_WIDTH]
        v = z[:, GMLP_WIDTH:]
        mu = jnp.mean(v, axis=-1, keepdims=True)
        vc = v - mu
        var = jnp.mean(vc * vc, axis=-1, keepdims=True)
        v = vc * lax.rsqrt(var + EPS) * lng_ref[...] + lnb_ref[...]
        if v_ref is not None:
            v_ref[rows, :] = v
        vb = v.astype(BF16)
        mixed = []
        for c in range(sub // CHUNK_A):
            cols = []
            for g in range(GMLP_GROUPS):
                blk = vb[c * CHUNK_A:(c + 1) * CHUNK_A, g * GMLP_GROUP_W:(g + 1) * GMLP_GROUP_W]
                cols.append(jnp.dot(mix_ref[g], blk, preferred_element_type=F32))
            mixed.append(jnp.concatenate(cols, axis=1) + bs_ref[...])
        s = jnp.concatenate(mixed, axis=0)
        y = _dot(u * s, wout_ref[...]) + bout_ref[...]
        xo_ref[rows, :] = x_ref[rows, :] + _mod_rows(gt_ref, rows) * y


def _rows_to_seq_layout(rows_ref, seq_ref, slab_scr):
    n_seq, seq, width = seq_ref.shape
    for j in range(width // LANES):
        lanes = slice(j * LANES, (j + 1) * LANES)
        slab_scr[...] = rows_ref[:, lanes]
        for t in range(seq):
            seq_ref[:, t, lanes] = slab_scr[pl.ds(t, n_seq, stride=seq), :]


def _seq_layout_to_rows(seq_ref, rows_ref, slab_scr):
    n_seq, seq, width = seq_ref.shape
    for j in range(width // LANES):
        lanes = slice(j * LANES, (j + 1) * LANES)
        for t in range(seq):
            slab_scr[pl.ds(t, n_seq, stride=seq), :] = seq_ref[:, t, lanes]
        rows_ref[:, lanes] = slab_scr[...]


def _gmlp_kernel(xp_ref, shp_ref, scp_ref, gtp_ref, xs_ref, shs_ref, scs_ref, gts_ref, e_ref,
                 ng_ref, win_hbm, bin_ref, lng_ref, lnb_ref, wout_hbm, bout_ref,
                 mixp_ref, bsp_ref, mixs_ref, bss_ref, op_ref, os_ref, vs_ref, z_scr, v_scr,
                 x_scr, slab_scr,
                 win_ref, wout_ref, stage_in, stage_out, sems_in, sems_out, *, grid):
    w = (ng_ref, win_ref, bin_ref, lng_ref, lnb_ref, wout_ref, bout_ref)
    step = pl.program_id(0)

    @pl.when(step == 0)
    def _():
        _load_as_bf16([(win_hbm, win_ref, stage_in, sems_in),
                       (wout_hbm, wout_ref, stage_out, sems_out)])

    @pl.when(step < grid.p_steps)
    def _():
        sh, sc, gt = (grid.prompt_row(r, step) for r in (shp_ref, scp_ref, gtp_ref))
        _gmlp_rows(xp_ref, sh, sc, gt, mixp_ref, bsp_ref, op_ref, None, z_scr, w,
                   grid.tm_p, GMLP_SUBTILE)

    @pl.when(step >= grid.p_steps)
    def _():
        sh, sc, gt = (_rows_to_tokens(e_ref, r) for r in (shs_ref, scs_ref, gts_ref))
        _seq_layout_to_rows(xs_ref, x_scr, slab_scr)
        _gmlp_rows(x_scr, sh, sc, gt, mixs_ref, bss_ref, os_ref, v_scr, z_scr, w,
                   grid.tm_s, GMLP_SUBTILE)
        _rows_to_seq_layout(v_scr, vs_ref, slab_scr)


def _gmlp_call(xp, xs_seq, mod_p, mod_s, grid, w):
    d = D_MODEL
    n_sample = xs_seq.shape[0] * xs_seq.shape[1]
    mix_spec = _const_spec((GMLP_GROUPS, CHUNK_A, CHUNK_A))
    bias_spec = _const_spec((CHUNK_A, GMLP_WIDTH))
    in_specs = ([grid.rows_p(d)] + grid.mods_p(0, 0) + [grid.seqs_s(d)] + grid.mods_s(0, 0) + [
        grid.token_copy_spec(),
        _const_spec((1, d)),
        pl.BlockSpec(memory_space=pl.ANY),
        _const_spec((1, 2 * GMLP_WIDTH)),
        _const_spec((1, GMLP_WIDTH)),
        _const_spec((1, GMLP_WIDTH)),
        pl.BlockSpec(memory_space=pl.ANY),
        _const_spec((1, d)),
        mix_spec, bias_spec, mix_spec, bias_spec,
    ])
    return pl.pallas_call(
        functools.partial(_gmlp_kernel, grid=grid),
        grid=(grid.steps,),
        in_specs=in_specs,
        out_specs=[grid.rows_p(d), grid.rows_s(d), grid.seqs_s(GMLP_WIDTH)],
        out_shape=[jax.ShapeDtypeStruct(xp.shape, F32),
                   jax.ShapeDtypeStruct((n_sample, d), F32),
                   jax.ShapeDtypeStruct(xs_seq.shape[:2] + (GMLP_WIDTH,), F32)],
        scratch_shapes=[pltpu.VMEM((2, GMLP_SUBTILE, 2 * GMLP_WIDTH), F32),
                        pltpu.VMEM((grid.tm_s, GMLP_WIDTH), F32),
                        pltpu.VMEM((grid.tm_s, d), F32),
                        pltpu.VMEM((grid.tm_s, LANES), F32),
                        pltpu.VMEM((d, 2 * GMLP_WIDTH), BF16),
                        pltpu.VMEM((GMLP_WIDTH, d), BF16),
                        pltpu.VMEM((WEIGHT_STAGE_SLOTS, d // WEIGHT_LOAD_CHUNKS, 2 * GMLP_WIDTH),
                                   F32),
                        pltpu.VMEM((WEIGHT_STAGE_SLOTS, GMLP_WIDTH // WEIGHT_LOAD_CHUNKS, d), F32),
                        pltpu.SemaphoreType.DMA((WEIGHT_STAGE_SLOTS,)),
                        pltpu.SemaphoreType.DMA((WEIGHT_STAGE_SLOTS,))],
        compiler_params=pltpu.CompilerParams(
            dimension_semantics=("arbitrary",), vmem_limit_bytes=VMEM_LIMIT),
        name="gmlp_mixer",
    )(xp, mod_p, mod_p, mod_p, xs_seq, mod_s, mod_s, mod_s, grid.token_copy_matrix(),
      w["norm_g"], w["w_in"], w["b_in"],
      w["ln_g"], w["ln_b"], w["w_out"], w["b_out"], w["mix_p"], w["bias_p"], w["mix_s"],
      w["bias_s"])


def _ffn_rows(x_ref, sh_ref, sc_ref, gt_ref, o_ref, h_scr, w, final_norm, tm):
    ng_ref, w1_ref, b1_ref, w2_ref, b2_ref, fg_ref = w
    sub = min(FFN_SUBTILE, tm)
    groups = [slice(r0, r0 + sub) for r0 in range(0, tm, sub)]

    def normalise(gi):
        rows = groups[gi]
        h_scr[gi % 2, 0:sub, :] = _modulated_norm(
            x_ref[rows, :], ng_ref[...], _mod_rows(sc_ref, rows),
            _mod_rows(sh_ref, rows)).astype(BF16)

    normalise(0)
    for gi, rows in enumerate(groups):
        if gi + 1 < len(groups):
            normalise(gi + 1)
        h = h_scr[gi % 2, 0:sub, :]
        acc = jnp.zeros((sub, D_MODEL), F32)
        for j in range(D_FF // FFN_CHUNK):
            cols = slice(j * FFN_CHUNK, (j + 1) * FFN_CHUNK)
            a = jnp.dot(h, w1_ref[:, cols], preferred_element_type=F32) + b1_ref[:, cols]
            r = jnp.square(jnp.maximum(a, 0.0)).astype(BF16)
            acc = acc + jnp.dot(r, w2_ref[cols, :], preferred_element_type=F32)
        y = x_ref[rows, :] + _mod_rows(gt_ref, rows) * (acc + b2_ref[...])
        if final_norm:
            y = _rms(y, fg_ref[...])
        o_ref[rows, :] = y


def _load_as_bf16(jobs):
    def chunk_copy(job, c):
        src_hbm, _, stage_ref, sem_ref = job
        slot = c % stage_ref.shape[0]
        rows = pl.ds(c * stage_ref.shape[1], stage_ref.shape[1])
        return pltpu.make_async_copy(src_hbm.at[rows, :], stage_ref.at[slot], sem_ref.at[slot])

    n_chunks = {job[1].shape[0] // job[2].shape[1] for job in jobs}
    n_slots = {job[2].shape[0] for job in jobs}
    assert len(n_chunks) == 1 and len(n_slots) == 1
    n_chunks, n_slots = n_chunks.pop(), n_slots.pop()
    for c in range(min(n_slots - 1, n_chunks)):
        for job in jobs:
            chunk_copy(job, c).start()
    for c in range(n_chunks):
        for job in jobs:
            if c + n_slots - 1 < n_chunks:
                chunk_copy(job, c + n_slots - 1).start()
        for job in jobs:
            _, dst_ref, stage_ref, _ = job
            chunk_copy(job, c).wait()
            rows = stage_ref.shape[1]
            dst_ref[c * rows:(c + 1) * rows, :] = stage_ref[c % n_slots].astype(BF16)


def _ffn_kernel(xp_ref, shp_ref, scp_ref, gtp_ref, xs_ref, shs_ref, scs_ref, gts_ref, e_ref,
                ng_ref, w1_hbm, b1_ref, w2_hbm, b2_ref, fg_ref, op_ref, os_ref, h_scr,
                w1_ref, w2_ref, stage1, stage2, sems1, sems2, ys_scr, slab_scr,
                *, final_norm, grid, layer):
    w = (ng_ref, w1_ref, b1_ref, w2_ref, b2_ref, fg_ref)
    step = pl.program_id(0)

    @pl.when(step == 0)
    def _():
        _load_as_bf16([(w1_hbm.at[layer], w1_ref, stage1, sems1),
                       (w2_hbm.at[layer], w2_ref, stage2, sems2)])

    @pl.when(step < grid.p_steps)
    def _():
        sh, sc, gt = (grid.prompt_row(r, step) for r in (shp_ref, scp_ref, gtp_ref))
        _ffn_rows(xp_ref, sh, sc, gt, op_ref, h_scr, w, final_norm, grid.tm_p)

    @pl.when(step >= grid.p_steps)
    def _():
        sh, sc, gt = (_rows_to_tokens(e_ref, r) for r in (shs_ref, scs_ref, gts_ref))
        if final_norm:
            _ffn_rows(xs_ref, sh, sc, gt, ys_scr, h_scr, w, final_norm, grid.tm_s)
            _rows_to_seq_layout(ys_scr, os_ref, slab_scr)
        else:
            _ffn_rows(xs_ref, sh, sc, gt, os_ref, h_scr, w, final_norm, grid.tm_s)


def _ffn_call(xp, xs, mod_p, mod_s, layer, grid, w, final_norm):
    d = D_MODEL
    seq_shape = (xs.shape[0] // grid.sample_seq_len, grid.sample_seq_len, d)
    in_specs = ([grid.rows_p(d)] + grid.mods_p(layer, 3) + [grid.rows_s(d)]
                + grid.mods_s(layer, 3) + [
        grid.token_copy_spec(),
        _layer_spec((1, d), layer),
        pl.BlockSpec(memory_space=pl.ANY),
        _layer_spec((1, D_FF), layer),
        pl.BlockSpec(memory_space=pl.ANY),
        _layer_spec((1, d), layer),
        _const_spec((1, d)),
    ])
    return pl.pallas_call(
        functools.partial(_ffn_kernel, final_norm=final_norm, grid=grid, layer=layer),
        grid=(grid.steps,),
        in_specs=in_specs,
        out_specs=[grid.rows_p(d), grid.seqs_s(d) if final_norm else grid.rows_s(d)],
        out_shape=[jax.ShapeDtypeStruct(xp.shape, F32),
                   jax.ShapeDtypeStruct(seq_shape if final_norm else xs.shape, F32)],
        scratch_shapes=[pltpu.VMEM((2, FFN_SUBTILE, d), BF16),
                        pltpu.VMEM((d, D_FF), BF16),
                        pltpu.VMEM((D_FF, d), BF16),
                        pltpu.VMEM((WEIGHT_STAGE_SLOTS, d // WEIGHT_LOAD_CHUNKS, D_FF), F32),
                        pltpu.VMEM((WEIGHT_STAGE_SLOTS, D_FF // WEIGHT_LOAD_CHUNKS, d), F32),
                        pltpu.SemaphoreType.DMA((WEIGHT_STAGE_SLOTS,)),
                        pltpu.SemaphoreType.DMA((WEIGHT_STAGE_SLOTS,)),
                        pltpu.VMEM((grid.tm_s, d), F32),
                        pltpu.VMEM((grid.tm_s, LANES), F32)],
        compiler_params=pltpu.CompilerParams(
            dimension_semantics=("arbitrary",), vmem_limit_bytes=VMEM_LIMIT),
        name="sqrelu_mlp",
    )(xp, mod_p, mod_p, mod_p, xs, mod_s, mod_s, mod_s, grid.token_copy_matrix(),
      w["norm_g"], w["w1"], w["b1"], w["w2"],
      w["b2"], w["final_g"])


def _gla_project(x, ng, scale, shift, win, wg2, bg):
    h = _modulated_norm(x, ng, scale, shift)
    proj = _dot(h, win)
    gate = _dot(proj[:, GLA_QKVG:], wg2) + bg
    log_a = _log_sigmoid(gate) * (1.0 / GLA_TAU)
    return proj[:, :GLA_QKVG], log_a


def _gla_output(o, gate_in, x, gt, og, wout):
    parts = []
    for h in range(GLA_HEADS):
        cols = slice(h * GLA_DV_HEAD, (h + 1) * GLA_DV_HEAD)
        parts.append(_rms(o[:, cols], og))
    on = jnp.concatenate(parts, axis=1)
    y = _dot(on * _silu(gate_in), wout)
    return x + gt * y


def _as_column(row):
    return jnp.broadcast_to(row, (SUBLANES, row.shape[1])).T[:, 0:1]


def _chunk_cumsum(tril, log_a):
    head = log_a.astype(BF16)
    rest = (log_a - head.astype(F32)).astype(BF16)
    return (jnp.dot(tril, head, preferred_element_type=F32)
            + jnp.dot(tril, rest, preferred_element_type=F32))


def _level_masks(c):
    rr = lax.broadcasted_iota(jnp.int32, (c, c), 0)
    cc = lax.broadcasted_iota(jnp.int32, (c, c), 1)
    masks = []
    s = c
    while s > SUBLANES:
        half = s // 2
        same = (rr ^ cc) < s
        masks.append(same & ((rr & half) != 0) & ((cc & half) == 0))
        s = half
    return masks, rr, cc


def _block_rows(x, s, r):
    c, w = x.shape
    return jnp.concatenate(
        [jnp.broadcast_to(x[p * s + r:p * s + r + 1, :], (s, w)) for p in range(c // s)], axis=0)


def _diag_terms(q, k, a):
    c, w = q.shape
    ii = lax.broadcasted_iota(jnp.int32, (c, w), 0) & (SUBLANES - 1)
    decay = jnp.zeros((c, w), F32)
    terms = [None] * SUBLANES
    for j in range(SUBLANES - 1, -1, -1):
        if j < SUBLANES - 1:
            decay = decay * _block_rows(a, SUBLANES, j + 1)
        decay = jnp.where(ii == j, 1.0, decay)
        terms[j] = q * _block_rows(k, SUBLANES, j) * decay
    return jnp.concatenate(terms, axis=0)


def _gla_chunk_pair(q2, k2, v2, la2, b2, st_refs, ones_bd, masks, rr, cc):
    c = q2[0].shape[0]
    zs = []
    for q, k, la in zip(q2, k2, la2):
        zs.append(_diag_terms(q, k, jnp.exp(la)))
    sums = jnp.dot(jnp.concatenate(zs, axis=1).astype(BF16), ones_bd,
                   preferred_element_type=F32)
    outs = []
    for hh, (q, k, v, b, st_ref) in enumerate(zip(q2, k2, v2, b2, st_refs)):
        dk = q.shape[1]
        attn = jnp.zeros((c, c), F32)
        base = rr & ~(SUBLANES - 1)
        for j in range(SUBLANES):
            rj = sums[j * c:(j + 1) * c, hh * dk:hh * dk + c]
            attn = jnp.where(cc == base + j, rj, attn)
        s = c
        for mask in masks:
            half = s // 2
            e = jnp.exp(-jnp.abs(b - _block_rows(b, s, half)))
            attn = jnp.where(mask, _dot_nt(q * e, k * e), attn)
            s = half
        b_last = b[c - 1:c, :]
        st = st_ref[...]
        o = _dot(q * jnp.exp(b), st) + _dot(attn, v)
        kd = k * jnp.exp(b_last - b)
        st_ref[...] = _as_column(jnp.exp(b_last)) * st + _dot_tn(kd, v)
        outs.append(o)
    return outs


def _gla_prompt_kernel(x_ref, sh_ref, sc_ref, gt_ref, ng_ref, win_ref, wg2_ref, bg_ref,
                       og_ref, wout_ref, tril_ref, ones_ref, xo_ref, s_ref,
                       proj_scr, ga_scr, la_scr, b_scr, o_scr, st_scr, *, tm, sub, tiles_per_seq):
    tile = pl.program_id(0) % tiles_per_seq
    seq_row = pl.ds(pl.program_id(0) // tiles_per_seq, 1)
    shift, scale, gate_mod = (r[seq_row, :] for r in (sh_ref, sc_ref, gt_ref))

    @pl.when(tile == 0)
    def _():
        st_scr[...] = jnp.zeros(st_scr.shape, F32)

    dk, dv = GLA_DK_HEAD, GLA_DV_HEAD
    qscale = GLA_DK_HEAD ** -0.5
    cf = GLA_FAST_CHUNK
    c = GLA_CHUNK
    groups = [slice(r0, r0 + sub) for r0 in range(0, tm, sub)]

    def project(rows):
        h = _modulated_norm(x_ref[rows, :], ng_ref[...], scale, shift)
        proj = _dot(h, win_ref[...])
        proj_scr[rows, :] = proj[:, :GLA_QKVG]
        ga_scr[rows, :] = proj[:, GLA_QKVG:]

    def decays(rows):
        gate = _dot(ga_scr[rows, :], wg2_ref[...]) + bg_ref[...]
        log_a = _log_sigmoid(gate) * (1.0 / GLA_TAU)
        la_scr[rows, :] = log_a
        for ci in range(sub // cf):
            b_scr[rows.start + ci * cf:rows.start + (ci + 1) * cf, :] = _chunk_cumsum(
                tril_ref[0:cf, 0:cf], log_a[ci * cf:(ci + 1) * cf, :])
        return jnp.min(b_scr[rows, :]) >= -GLA_SAFE_EXPONENT

    def head_slices(rows, h):
        q = proj_scr[rows, h * dk:(h + 1) * dk] * qscale
        k = proj_scr[rows, GLA_DK + h * dk:GLA_DK + (h + 1) * dk]
        v = proj_scr[rows, 2 * GLA_DK + h * dv:2 * GLA_DK + (h + 1) * dv]
        return q, k, v

    def single_ref_rows(group):
        rr = lax.broadcasted_iota(jnp.int32, (cf, cf), 0)
        cc = lax.broadcasted_iota(jnp.int32, (cf, cf), 1)
        causal = rr >= cc
        chunks = [slice(r0, r0 + cf) for r0 in range(group.start, group.stop, cf)]
        for h in range(GLA_HEADS):
            qes, vbs, scores, kvs, decays = [], [], [], [], []
            for rows in chunks:
                q, k, v = head_slices(rows, h)
                b = b_scr[rows, h * dk:(h + 1) * dk]
                decay_last = jnp.exp(b[cf - 1:cf, :])
                qe = (q * jnp.exp(b)).astype(BF16)
                kt = k * jnp.exp(-b)
                vb = v.astype(BF16)
                scores.append(_dot_nt(qe, kt))
                kvs.append(_dot_tn(kt * decay_last, vb))
                qes.append(qe)
                vbs.append(vb)
                decays.append(_as_column(decay_last))
            st = st_scr[h]
            states = []
            for kv, decay_col in zip(kvs, decays):
                states.append(st)
                st = decay_col * st + kv
            st_scr[h] = st
            for rows, qe, vb, sc, st_in in zip(chunks, qes, vbs, scores, states):
                attn = jnp.where(causal, sc, 0.0).astype(BF16)
                lhs = jnp.concatenate([qe, attn], axis=1)
                rhs = jnp.concatenate([st_in.astype(BF16), vb], axis=0)
                o_scr[rows, h * dv:(h + 1) * dv] = jnp.dot(lhs, rhs, preferred_element_type=F32)

    def robust_rows(group):
        def robust_body(ci, carry):
            rows = pl.ds(pl.multiple_of(group.start + ci * c, c), c)
            masks, rr, cc = _level_masks(c)
            b_c = _chunk_cumsum(tril_ref[0:c, 0:c], la_scr[rows, :])
            outs = []
            for h0 in range(0, GLA_HEADS, 2):
                heads = (h0, h0 + 1)
                qkv = [head_slices(rows, h) for h in heads]
                la2 = [la_scr[rows, h * dk:(h + 1) * dk] for h in heads]
                b2 = [b_c[:, h * dk:(h + 1) * dk] for h in heads]
                st_refs = [st_scr.at[h] for h in heads]
                outs += _gla_chunk_pair([t[0] for t in qkv], [t[1] for t in qkv],
                                        [t[2] for t in qkv], la2, b2, st_refs, ones_ref[...],
                                        masks, rr, cc)
            o_scr[rows, :] = jnp.concatenate(outs, axis=1)
            return carry

        lax.fori_loop(0, sub // c, robust_body, 0)

    def output(rows):
        gate_in = proj_scr[rows, 2 * GLA_DK + GLA_DV:]
        xo_ref[rows, :] = _gla_output(o_scr[rows, :], gate_in, x_ref[rows, :], gate_mod,
                                      og_ref[...], wout_ref[...])

    project(groups[0])
    for gi, rows in enumerate(groups):
        single_ref_safe = decays(rows)
        following = groups[gi + 1] if gi + 1 < len(groups) else None

        @pl.when(single_ref_safe)
        def _():
            if following is not None:
                project(following)
            single_ref_rows(rows)
            output(rows)

        @pl.when(jnp.logical_not(single_ref_safe))
        def _():
            if following is not None:
                project(following)
            robust_rows(rows)
            output(rows)

    @pl.when(tile == tiles_per_seq - 1)
    def _():
        s_ref[...] = st_scr[...]


def _gla_prompt_call(x, mod, layer, n_seq, seq_len, tm, w):
    n, d = x.shape
    tiles_per_seq = seq_len // tm
    dk, dv = GLA_DK_HEAD, GLA_DV_HEAD
    in_specs = [_row_spec(tm, d)] + _mod_specs(mod, layer, 0, tm, seq_len) + [
        _const_spec((1, d)),
        _const_spec((d, GLA_IN_PAD)),
        _const_spec((GLA_GATE_PAD, GLA_DK)),
        _const_spec((1, GLA_DK)),
        _const_spec((1, dv)),
        _const_spec((GLA_DV, d)),
        _const_spec((GLA_TRIL, GLA_TRIL)),
        _const_spec((2 * dk, 2 * dk)),
    ]
    out_specs = [
        _row_spec(tm, d),
        pl.BlockSpec((None, GLA_HEADS, dk, dv), lambda i: (i // tiles_per_seq, 0, 0, 0)),
    ]
    out_shape = [
        jax.ShapeDtypeStruct((n, d), F32),
        jax.ShapeDtypeStruct((n_seq, GLA_HEADS, dk, dv), F32),
    ]
    return pl.pallas_call(
        functools.partial(_gla_prompt_kernel, tm=tm, sub=GLA_SUBTILE,
                          tiles_per_seq=tiles_per_seq),
        grid=(n // tm,),
        in_specs=in_specs,
        out_specs=out_specs,
        out_shape=out_shape,
        scratch_shapes=[
            pltpu.VMEM((tm, GLA_QKVG), F32),
            pltpu.VMEM((tm, GLA_GATE_PAD), F32),
            pltpu.VMEM((tm, GLA_DK), F32),
            pltpu.VMEM((tm, GLA_DK), F32),
            pltpu.VMEM((tm, GLA_DV), F32),
            pltpu.VMEM((GLA_HEADS, dk, dv), F32),
        ],
        compiler_params=pltpu.CompilerParams(
            dimension_semantics=("arbitrary",), vmem_limit_bytes=VMEM_LIMIT),
        name="gla_mixer_prompt",
    )(x, mod, mod, mod, w["norm_g"], w["w_in"], w["w_gate2"], w["b_gate"], w["out_g"],
      w["w_out"], w["tril"], w["ones_bd"])


def _gla_proj_kernel(x_ref, sh_ref, sc_ref, e_ref, ng_ref, win_ref, wg2_ref, bg_ref, p_ref,
                     la_ref):
    proj, log_a = _gla_project(x_ref[...], ng_ref[...], _rows_to_tokens(e_ref, sc_ref),
                               _rows_to_tokens(e_ref, sh_ref), win_ref[...], wg2_ref[...],
                               bg_ref[...])
    p_ref[...] = proj
    la_ref[...] = log_a


def _gla_proj_call(x, mod, layer, tm, seq_len, w):
    n, d = x.shape
    specs = _mod_specs(mod, layer, 0, tm, seq_len)
    in_specs = [_row_spec(tm, d), specs[0], specs[1],
                _const_spec((tm, 2 * tm // seq_len)),
                _const_spec((1, d)),
                _const_spec((d, GLA_IN_PAD)),
                _const_spec((GLA_GATE_PAD, GLA_DK)),
                _const_spec((1, GLA_DK))]
    return pl.pallas_call(
        _gla_proj_kernel,
        grid=(n // tm,),
        in_specs=in_specs,
        out_specs=[_row_spec(tm, GLA_QKVG), _row_spec(tm, GLA_DK)],
        out_shape=[jax.ShapeDtypeStruct((n, GLA_QKVG), F32),
                   jax.ShapeDtypeStruct((n, GLA_DK), F32)],
        compiler_params=pltpu.CompilerParams(
            dimension_semantics=("arbitrary",), vmem_limit_bytes=VMEM_LIMIT),
        name="gla_proj_sample",
    )(x, mod, mod, _token_copy_matrix(tm, seq_len), w["norm_g"], w["w_in"], w["w_gate2"],
      w["b_gate"])


def _gla_step_kernel(p_ref, la_ref, s0_ref, o_ref, s1_ref, *, seqs, steps):
    dk, dv = GLA_DK_HEAD, GLA_DV_HEAD
    qscale = GLA_DK_HEAD ** -0.5
    per_tile = SUBLANES // steps
    row = lax.broadcasted_iota(jnp.int32, (SUBLANES, dk), 0)

    def seq_rows(tile, s):
        if s:
            tile = pltpu.roll(tile, SUBLANES - s * steps, 0)
        keep = lax.broadcasted_iota(jnp.int32, tile.shape, 0) < steps
        return jnp.where(keep, tile, 0.0)

    def tile_body(ti, carry):
        rows = pl.ds(pl.multiple_of(ti * SUBLANES, SUBLANES), SUBLANES)
        p_tile = p_ref[rows, :]
        la_tile = la_ref[rows, :]
        o_tile = jnp.zeros((SUBLANES, GLA_DV), F32)
        for s in range(per_tile):
            bi = ti * per_tile + s
            p = seq_rows(p_tile, s)
            la_all = seq_rows(la_tile, s)
            outs = []
            for h in range(GLA_HEADS):
                q = p[:, h * dk:(h + 1) * dk] * qscale
                k = p[:, GLA_DK + h * dk:GLA_DK + (h + 1) * dk]
                v = p[:, 2 * GLA_DK + h * dv:2 * GLA_DK + (h + 1) * dv]
                la = la_all[:, h * dk:(h + 1) * dk]
                b = jnp.zeros((SUBLANES, dk), F32)
                for t in range(steps):
                    b = b + jnp.where(row >= t, la[t:t + 1, :], 0.0)
                b_last = b[steps - 1:steps, :]
                s0 = s0_ref[bi, h]
                o = _dot(q * jnp.exp(b), s0)
                for j in range(steps):
                    e = jnp.exp(jnp.minimum(b - b[j:j + 1, :], 0.0))
                    z = jnp.where(row >= j, q * k[j:j + 1, :] * e, 0.0)
                    o = o + jnp.sum(z, axis=-1, keepdims=True) * v[j:j + 1, :]
                outs.append(o)
                m = jnp.where(row == steps, jnp.exp(b_last), k * jnp.exp(b_last - b))
                m_t = m.T
                s1_ref[bi, h] = m_t[:, steps:steps + 1] * s0 + _dot(m_t, v)
            o_seq = jnp.concatenate(outs, axis=1)
            o_tile = o_tile + (pltpu.roll(o_seq, s * steps, 0) if s else o_seq)
        o_ref[rows, :] = o_tile
        return carry

    lax.fori_loop(0, seqs // per_tile, tile_body, 0)


def _gla_step_call(proj, log_a, state, steps, seqs):
    n_seq = state.shape[0]
    dk, dv = GLA_DK_HEAD, GLA_DV_HEAD
    rows = lambda width: pl.BlockSpec((seqs * steps, width), lambda i: (i, 0))
    st_spec = pl.BlockSpec((seqs, GLA_HEADS, dk, dv), lambda i: (i, 0, 0, 0))
    return pl.pallas_call(
        functools.partial(_gla_step_kernel, seqs=seqs, steps=steps),
        grid=(n_seq // seqs,),
        in_specs=[rows(GLA_QKVG), rows(GLA_DK), st_spec],
        out_specs=[rows(GLA_DV), st_spec],
        out_shape=[jax.ShapeDtypeStruct((n_seq * steps, GLA_DV), F32),
                   jax.ShapeDtypeStruct(state.shape, F32)],
        compiler_params=pltpu.CompilerParams(
            dimension_semantics=("arbitrary",), vmem_limit_bytes=VMEM_LIMIT),
        name="gla_step_sample",
    )(proj, log_a, state)


def _gla_out_kernel(o_ref, p_ref, x_ref, gt_ref, e_ref, og_ref, wout_ref, xo_ref):
    xo_ref[...] = _gla_output(o_ref[...], p_ref[...], x_ref[...], _rows_to_tokens(e_ref, gt_ref),
                              og_ref[...], wout_ref[...])


def _gla_out_call(o, proj, x, mod, layer, tm, seq_len, w):
    n, d = x.shape
    gate_spec = _mod_specs(mod, layer, 0, tm, seq_len)[2]
    g_block = (2 * GLA_DK + GLA_DV) // GLA_DV
    in_specs = [_row_spec(tm, GLA_DV),
                pl.BlockSpec((tm, GLA_DV), lambda i: (i, g_block)),
                _row_spec(tm, d), gate_spec,
                _const_spec((tm, 2 * tm // seq_len)),
                _const_spec((1, GLA_DV_HEAD)),
                _const_spec((GLA_DV, d))]
    return pl.pallas_call(
        _gla_out_kernel,
        grid=(n // tm,),
        in_specs=in_specs,
        out_specs=_row_spec(tm, d),
        out_shape=jax.ShapeDtypeStruct((n, d), F32),
        compiler_params=pltpu.CompilerParams(
            dimension_semantics=("arbitrary",), vmem_limit_bytes=VMEM_LIMIT),
        name="gla_out_sample",
    )(o, proj, x, mod, _token_copy_matrix(tm, seq_len), w["out_g"], w["w_out"])


PROMPT_FFN_TILE = 1024
PROMPT_GMLP_TILE = 1024
PROMPT_GLA_TILE = 1024
SAMPLE_TILE = 256
SAMPLE_SEQS_PER_STEP = 16


def kernel(x_prompt, x_sample, c_prompt, c_sample, state_gla, ada_w, ada_b, norm_mix_g, norm_ffn_g,
           ffn_w1, ffn_b1, ffn_w2, ffn_b2, gmlp_w_in, gmlp_b_in, gmlp_ln_g, gmlp_ln_b, gmlp_w_s,
           gmlp_b_s, gmlp_w_out, gmlp_b_out, gla_w_in, gla_w_gate2, gla_b_gate, gla_norm_g,
           gla_w_out, final_norm_g):
    n_seq_p, seq_p, d = x_prompt.shape
    n_seq_s, seq_s, _ = x_sample.shape
    assert d == D_MODEL and GMLP_SUBTILE % CHUNK_A == 0 and GLA_SUBTILE % GLA_FAST_CHUNK == 0
    assert all(seq_p % t == 0 for t in (PROMPT_FFN_TILE, PROMPT_GMLP_TILE, PROMPT_GLA_TILE))
    assert seq_s < SUBLANES and SUBLANES % seq_s == 0 and (n_seq_s * seq_s) % SAMPLE_TILE == 0
    row = lambda a: a.reshape(1, -1)

    mod_p, mod_s = _ada_call(c_prompt, c_sample, ada_w, ada_b)

    causal = np.tril(np.ones((CHUNK_A, CHUNK_A), dtype=bool))
    ws = jnp.where(causal[None], gmlp_w_s, jnp.zeros_like(gmlp_w_s))
    rows = np.arange(CHUNK_A)
    pick = (rows[:, None] % seq_s == np.arange(seq_s)[None, :]).astype(np.float32)
    same_seq = (rows[:, None] // seq_s) == (rows[None, :] // seq_s)
    exact = lax.Precision.HIGHEST
    ws_s = jnp.where(same_seq[None], jnp.einsum("ri,gij,cj->grc", pick, ws[:, :seq_s, :seq_s], pick,
                                                precision=exact), 0.0)
    bias_p = jnp.repeat(gmlp_b_s.T, GMLP_GROUP_W, axis=1)
    bias_s = jnp.dot(pick, bias_p[:seq_s], precision=exact)

    gmlp_w = dict(norm_g=row(norm_mix_g[0]), w_in=gmlp_w_in, b_in=row(gmlp_b_in),
                  ln_g=row(gmlp_ln_g), ln_b=row(gmlp_ln_b), w_out=gmlp_w_out,
                  b_out=row(gmlp_b_out))
    gmlp_w = dict(gmlp_w, mix_p=ws.astype(BF16), bias_p=bias_p, mix_s=ws_s.astype(BF16),
                  bias_s=bias_s)

    depth = ffn_w1.shape[0]
    ffn_w = dict(norm_g=norm_ffn_g.reshape(depth, 1, d), w1=ffn_w1,
                 b1=ffn_b1.reshape(depth, 1, D_FF), w2=ffn_w2,
                 b2=ffn_b2.reshape(depth, 1, d), final_g=row(final_norm_g))

    w_in_pad = gla_w_in.astype(BF16)
    wg2_pad = gla_w_gate2.astype(BF16)
    blk_ones = np.kron(np.eye(2, dtype=np.float32),
                       np.ones((GLA_DK_HEAD, GLA_DK_HEAD), np.float32))
    gla_w = dict(norm_g=row(norm_mix_g[1]), w_in=w_in_pad, w_gate2=wg2_pad, b_gate=row(gla_b_gate),
                 out_g=row(gla_norm_g), w_out=gla_w_out.astype(BF16),
                 tril=jnp.asarray(np.tril(np.ones((GLA_TRIL, GLA_TRIL), np.float32)), BF16),
                 ones_bd=jnp.asarray(blk_ones, BF16))

    xp = x_prompt.reshape(n_seq_p * seq_p, d)
    grid_for = lambda tm_p: _TwoGroupGrid(xp.shape[0], n_seq_s * seq_s, tm_p, SAMPLE_TILE, seq_p,
                                          seq_s)
    xp, xs, chunk_v = _gmlp_call(xp, x_sample, mod_p, mod_s, grid_for(PROMPT_GMLP_TILE), gmlp_w)
    xp, xs = _ffn_call(xp, xs, mod_p, mod_s, 0, grid_for(PROMPT_FFN_TILE), ffn_w, final_norm=False)
    xp, state_p = _gla_prompt_call(xp, mod_p, 1, n_seq_p, seq_p, PROMPT_GLA_TILE, gla_w)
    proj, log_a = _gla_proj_call(xs, mod_s, 1, SAMPLE_TILE, seq_s, gla_w)
    o, state_s = _gla_step_call(proj, log_a, state_gla, seq_s, SAMPLE_SEQS_PER_STEP)
    xs = _gla_out_call(o, proj, xs, mod_s, 1, SAMPLE_TILE, seq_s, gla_w)
    xp, xs = _ffn_call(xp, xs, mod_p, mod_s, 1, grid_for(PROMPT_FFN_TILE), ffn_w, final_norm=True)

    return (xp.reshape(x_prompt.shape), xs, state_p, state_s, chunk_v)
```
